```python
import jax, jax.numpy as jnp
from jax import lax
import numpy as np

D_MODEL = 2048
BATCH = 4
SEQ = 4096
DEPTH = 1

A_HEADS = 8
A_HEAD_DIM = 128
A_WIDTH = A_HEADS * A_HEAD_DIM
Q_RANK = 512
IDX_HEADS = 16
IDX_DIM = 128
TOPK_MAX = 256
Q_BLOCK = 128
ROPE_THETA = 10000.0
G_HEADS = 8
G_KDIM = 128
G_VDIM = 128
G_WIDTH_K = G_HEADS * G_KDIM
G_WIDTH_V = G_HEADS * G_VDIM
CHUNK = 64
N_GROUPS = 4
EXP_PER_GROUP = 8
N_EXPERTS = N_GROUPS * EXP_PER_GROUP
TOPK_EXP = 2
D_EXPERT = 512
EPS = 1e-6
SEG_SIZES = (Q_RANK, A_HEAD_DIM, A_HEAD_DIM, IDX_DIM, IDX_HEADS, G_WIDTH_K, G_WIDTH_K, G_WIDTH_V, G_WIDTH_V, D_MODEL, D_MODEL)
D_IN = Q_RANK + 2 * A_HEAD_DIM + IDX_DIM + IDX_HEADS + 2 * G_WIDTH_K + 2 * G_WIDTH_V + 2 * D_MODEL

kernel_name = 'hybrid_dsa_hgrn2_hmoe'


def rmsnorm(x, g):
    x32 = x.astype(jnp.float32)
    y = x32 * lax.rsqrt(jnp.mean(x32 * x32, axis=-1, keepdims=True) + EPS)
    return (y * g.astype(jnp.float32)).astype(x.dtype)


def rope_tables(seq, dim):
    inv = 1.0 / (ROPE_THETA ** (jnp.arange(0, dim, 2, dtype=jnp.float32) / dim))
    ang = jnp.arange(seq, dtype=jnp.float32)[:, None] * inv[None, :]
    return jnp.cos(ang), jnp.sin(ang)


def apply_rope(x, cos, sin):
    x32 = x.astype(jnp.float32)
    x1, x2 = jnp.split(x32, 2, axis=-1)
    c = cos[None, :, None, :]
    s = sin[None, :, None, :]
    return jnp.concatenate([x1 * c - x2 * s, x2 * c + x1 * s], axis=-1).astype(x.dtype)


def split_cols(p):
    bounds = np.cumsum(SEG_SIZES)[:-1].tolist()
    return jnp.split(p, bounds, axis=-1)


def dsa_attention(c_q, k_att, v_att, k_idx, w_idx, g_cq, w_uq, w_qidx, cos, sin):
    b, s, _ = c_q.shape
    topk = min(TOPK_MAX, s // 4)
    nb = s // Q_BLOCK
    cqn = rmsnorm(c_q, g_cq)
    q = apply_rope((cqn @ w_uq).reshape(b, s, A_HEADS, A_HEAD_DIM), cos, sin)
    qi = apply_rope((cqn @ w_qidx).reshape(b, s, IDX_HEADS, IDX_DIM), cos, sin)
    k = apply_rope(k_att[:, :, None, :], cos, sin)[:, :, 0, :]
    ki = apply_rope(k_idx[:, :, None, :], cos, sin)[:, :, 0, :]
    wi = w_idx.astype(jnp.float32) * (IDX_HEADS ** -0.5 * IDX_DIM ** -0.5)
    key_pos = jnp.arange(s)
    scale = A_HEAD_DIM ** -0.5

    def blocks(t):
        return jnp.swapaxes(t.reshape((b, nb, Q_BLOCK) + t.shape[2:]), 0, 1)

    def one_block(args):
        qb, qib, wb, pos = args
        logits = jnp.einsum('bthd,bsd->bths', qib, ki, preferred_element_type=jnp.float32)
        score = jnp.einsum('bths,bth->bts', jax.nn.relu(logits), wb)
        causal = key_pos[None, :] <= pos[:, None]
        score = jnp.where(causal[None], score, -jnp.inf)
        _, idx = lax.top_k(score, topk)
        valid = idx <= pos[None, :, None]
        kg = jax.vmap(lambda kk, ii: kk[ii])(k, idx)
        vg = jax.vmap(lambda vv, ii: vv[ii])(v_att, idx)
        att = jnp.einsum('bthd,btkd->bthk', qb, kg, preferred_element_type=jnp.float32) * scale
        att = jnp.where(valid[:, :, None, :], att, -jnp.inf)
        p = jax.nn.softmax(att, axis=-1)
        return jnp.einsum('bthk,btkd->bthd', p.astype(vg.dtype), vg)

    pos_blocks = key_pos.reshape(nb, Q_BLOCK)
    out = lax.map(one_block, (blocks(q), blocks(qi), blocks(wi), pos_blocks))
    return jnp.swapaxes(out, 0, 1).reshape(b, s, A_WIDTH)


def hgrn2(hq, hf, hi, hg, lb, g_onorm):
    b, s, _ = hq.shape
    n = s // CHUNK
    f = lb + (1.0 - lb) * jax.nn.sigmoid(hf.astype(jnp.float32))
    log_f = jnp.log(f)
    k = 1.0 - f
    q = jax.nn.silu(hq.astype(jnp.float32))
    v = hi.astype(jnp.float32)

    def chunks(t, d):
        return t.reshape(b, n, CHUNK, G_HEADS, d).transpose(1, 0, 3, 2, 4)

    tri = jnp.tril(jnp.ones((CHUNK, CHUNK), dtype=bool))

    def step(state, inp):
        qc, kc, vc, gc = inp
        cum = jnp.cumsum(gc, axis=2)
        diff = cum[:, :, :, None, :] - cum[:, :, None, :, :]
        decay = jnp.exp(jnp.where(tri[None, None, :, :, None], diff, -jnp.inf))
        scores = jnp.einsum('bhtsd,bhsd->bhts', qc[:, :, :, None, :] * decay, kc)
        o = jnp.einsum('bhts,bhsv->bhtv', scores, vc) + jnp.einsum('bhtd,bhdv->bhtv', qc * jnp.exp(cum), state)
        last = cum[:, :, -1:, :]
        new_state = jnp.exp(last[:, :, 0, :])[..., None] * state + jnp.einsum('bhsd,bhsv->bhdv', kc * jnp.exp(last - cum), vc)
        return new_state, o

    s0 = jnp.zeros((b, G_HEADS, G_KDIM, G_VDIM), jnp.float32)
    _, o = lax.scan(step, s0, (chunks(q, G_KDIM), chunks(k, G_KDIM), chunks(v, G_VDIM), chunks(log_f, G_KDIM)))
    o = o.transpose(1, 0, 3, 2, 4).reshape(b, s, G_WIDTH_V).astype(hq.dtype)
    return rmsnorm(o, g_onorm) * jax.nn.silu(hg)


def hier_moe(h, w_grp, b_grp, w_exp, b_exp, w_gate, w_up, w_down):
    b, s, d = h.shape
    n = b * s
    t = h.reshape(n, d)
    p_grp = jax.nn.softmax((t @ w_grp).astype(jnp.float32) + b_grp.astype(jnp.float32), axis=-1)
    grp = jnp.argmax(p_grp, axis=-1)
    p_g = jnp.max(p_grp, axis=-1, keepdims=True)
    logit_e = ((t @ w_exp).astype(jnp.float32) + b_exp.astype(jnp.float32)).reshape(n, N_GROUPS, EXP_PER_GROUP)
    logit_e = logit_e[jnp.arange(n), grp]
    p_e, e_idx = lax.top_k(jax.nn.softmax(logit_e, axis=-1), TOPK_EXP)
    gate = p_g * p_e / jnp.sum(p_e, axis=-1, keepdims=True)
    eid = (grp[:, None] * EXP_PER_GROUP + e_idx).reshape(-1)
    order = jnp.argsort(eid)
    tok = order // TOPK_EXP
    xs = t[tok]
    sizes = jnp.bincount(eid, length=N_EXPERTS).astype(jnp.int32)
    a = lax.ragged_dot(xs, w_gate, sizes)
    u = lax.ragged_dot(xs, w_up, sizes)
    y = lax.ragged_dot(jax.nn.silu(a) * u, w_down, sizes)
    y = y * gate.reshape(-1)[order][:, None].astype(y.dtype)
    return jax.ops.segment_sum(y, tok, num_segments=n).reshape(b, s, d)


def setup_inputs(seed: int = 0) -> dict:
    key = jax.random.key(seed)
    ks = jax.random.split(key, 20)
    f32 = jnp.float32

    def nrm(k, shape, fan_in):
        return jax.random.normal(k, shape, f32) * (fan_in ** -0.5)

    def gain(k, shape):
        return 1.0 + 0.02 * jax.random.normal(k, shape, f32)

    return {
        'x': jax.random.normal(ks[0], (BATCH, SEQ, D_MODEL), f32),
        'g_norm1': gain(ks[1], (DEPTH, D_MODEL)),
        'w_in': nrm(ks[2], (DEPTH, D_MODEL, D_IN), D_MODEL),
        'g_cq': gain(ks[3], (DEPTH, Q_RANK)),
        'w_uq': nrm(ks[4], (DEPTH, Q_RANK, A_WIDTH), Q_RANK),
        'w_qidx': nrm(ks[5], (DEPTH, Q_RANK, IDX_HEADS * IDX_DIM), Q_RANK),
        'lb_logits': 0.5 * jax.random.normal(ks[6], (DEPTH + 1, G_WIDTH_K), f32),
        'g_hnorm': gain(ks[7], (DEPTH, G_WIDTH_V)),
        'w_pa': nrm(ks[8], (DEPTH, A_WIDTH, D_MODEL), A_WIDTH),
        'w_pb': nrm(ks[9], (DEPTH, G_WIDTH_V, D_MODEL), G_WIDTH_V),
        'w_o': nrm(ks[10], (DEPTH, D_MODEL, D_MODEL), D_MODEL),
        'g_norm2': gain(ks[11], (DEPTH, D_MODEL)),
        'w_grp': nrm(ks[12], (DEPTH, D_MODEL, N_GROUPS), D_MODEL),
        'b_grp': 0.01 * jax.random.normal(ks[13], (DEPTH, N_GROUPS), f32),
        'w_exp': nrm(ks[14], (DEPTH, D_MODEL, N_EXPERTS), D_MODEL),
        'b_exp': 0.01 * jax.random.normal(ks[15], (DEPTH, N_EXPERTS), f32),
        'w_gate': nrm(ks[16], (DEPTH, N_EXPERTS, D_MODEL, D_EXPERT), D_MODEL),
        'w_up': nrm(ks[17], (DEPTH, N_EXPERTS, D_MODEL, D_EXPERT), D_MODEL),
        'w_down': nrm(ks[18], (DEPTH, N_EXPERTS, D_EXPERT, D_MODEL), D_EXPERT),
        'g_final': gain(ks[19], (D_MODEL,)),
    }


def reference(x, g_norm1, w_in, g_cq, w_uq, w_qidx, lb_logits, g_hnorm, w_pa, w_pb, w_o, g_norm2, w_grp, b_grp, w_exp, b_exp, w_gate, w_up, w_down, g_final):
    s = x.shape[1]
    cos, sin = rope_tables(s, A_HEAD_DIM)
    lb_all = jnp.cumsum(jax.nn.softmax(lb_logits.astype(jnp.float32), axis=0), axis=0)
    for l in range(DEPTH):
        h = rmsnorm(x, g_norm1[l])
        c_q, k_att, v_att, k_idx, w_idx, hq, hf, hi, hg, ga, gb = split_cols(h @ w_in[l])
        y_att = dsa_attention(c_q, k_att, v_att, k_idx, w_idx, g_cq[l], w_uq[l], w_qidx[l], cos, sin)
        y_rec = hgrn2(hq, hf, hi, hg, lb_all[l], g_hnorm[l])
        merged = jax.nn.sigmoid(ga) * (y_att @ w_pa[l]) + jax.nn.sigmoid(gb) * (y_rec @ w_pb[l])
        x = x + merged @ w_o[l]
        x = x + hier_moe(rmsnorm(x, g_norm2[l]), w_grp[l], b_grp[l], w_exp[l], b_exp[l], w_gate[l], w_up[l], w_down[l])
    return rmsnorm(x, g_final)
```

```python
import functools

import jax
import jax.numpy as jnp
from jax import lax
from jax.experimental import pallas as pl
from jax.experimental.pallas import tpu as pltpu

F32 = jnp.float32
BF16 = jnp.bfloat16
I32 = jnp.int32

EPS = 1e-6
ROPE_THETA = 10000.0
LANES = 128

A_HEADS = 8
HEAD_DIM = 128
IDX_HEADS = 16
Q_RANK = 512
TOPK_MAX = 256
G_HEADS = 8
N_GROUPS = 4
EXP_PER_GROUP = 8
N_EXPERTS = N_GROUPS * EXP_PER_GROUP
D_EXPERT = 512

INT_MIN = -2147483648
EXP_CLAMP = 80.0


def _cparams(sem, vmem_mb):
    return pltpu.CompilerParams(dimension_semantics=sem, vmem_limit_bytes=vmem_mb << 20)


def _sigmoid(x):
    return 1.0 / (1.0 + jnp.exp(-x))


def _inproj_kernel(x_ref, g_ref, w_ref, o_ref, xn_ref):
    @pl.when(pl.program_id(1) == 0)
    def _():
        x = x_ref[...]
        ms = jnp.mean(x * x, axis=-1, keepdims=True)
        xn_ref[...] = (x * lax.rsqrt(ms + EPS) * g_ref[...]).astype(BF16)

    o_ref[...] = jnp.dot(xn_ref[...], w_ref[...], preferred_element_type=F32)


def _in_proj(x2, g, w, tm, tn):
    n, d = x2.shape
    dout = w.shape[1]
    return pl.pallas_call(
        _inproj_kernel,
        grid=(n // tm, dout // tn),
        in_specs=[
            pl.BlockSpec((tm, d), lambda i, j: (i, 0)),
            pl.BlockSpec((1, d), lambda i, j: (0, 0)),
            pl.BlockSpec((d, tn), lambda i, j: (0, j)),
        ],
        out_specs=pl.BlockSpec((tm, tn), lambda i, j: (i, j)),
        out_shape=jax.ShapeDtypeStruct((n, dout), F32),
        scratch_shapes=[pltpu.VMEM((tm, d), BF16)],
        compiler_params=_cparams(("parallel", "arbitrary"), 48),
        name="in_proj",
    )(x2, g.reshape(1, d), w)


def _rope(x, c2, s2):
    return x * c2 + pltpu.roll(x, HEAD_DIM // 2, 1) * s2


def _prep_kernel(cq_ref, ka_ref, va_ref, kx_ref, cos_ref, sin_ref, g_ref, wuq_ref, wqi_ref,
                 q_ref, qi_ref, k_ref, ki_ref, v_ref, *, scale):
    c2 = cos_ref[...]
    s2 = sin_ref[...]
    cq = cq_ref[...]
    ms = jnp.mean(cq * cq, axis=-1, keepdims=True)
    cqn = (cq * lax.rsqrt(ms + EPS) * g_ref[...]).astype(BF16)
    q = jnp.dot(cqn, wuq_ref[...], preferred_element_type=F32)
    for h in range(A_HEADS):
        sl = slice(h * HEAD_DIM, (h + 1) * HEAD_DIM)
        q_ref[:, sl] = (_rope(q[:, sl], c2, s2) * scale).astype(BF16)
    qi = jnp.dot(cqn, wqi_ref[...], preferred_element_type=F32)
    for h in range(IDX_HEADS):
        sl = slice(h * HEAD_DIM, (h + 1) * HEAD_DIM)
        qi_ref[:, sl] = _rope(qi[:, sl], c2, s2).astype(BF16)
    k_ref[...] = _rope(ka_ref[...], c2, s2).astype(BF16)
    ki_ref[...] = _rope(kx_ref[...], c2, s2).astype(BF16)
    v_ref[...] = va_ref[...].astype(BF16)


def _dsa_prep(p, cos2, sin2, g_cq, w_uq, w_qidx, seq, tm):
    n = p.shape[0]
    nsb = seq // tm
    aw = A_HEADS * HEAD_DIM
    iw = IDX_HEADS * HEAD_DIM
    row = lambda i: (i, 0)
    return pl.pallas_call(
        functools.partial(_prep_kernel, scale=HEAD_DIM ** -0.5),
        grid=(n // tm,),
        in_specs=[
            pl.BlockSpec((tm, Q_RANK), lambda i: (i, 0)),
            pl.BlockSpec((tm, LANES), lambda i: (i, 4)),
            pl.BlockSpec((tm, LANES), lambda i: (i, 5)),
            pl.BlockSpec((tm, LANES), lambda i: (i, 6)),
            pl.BlockSpec((tm, LANES), lambda i: (i % nsb, 0)),
            pl.BlockSpec((tm, LANES), lambda i: (i % nsb, 0)),
            pl.BlockSpec((1, Q_RANK), lambda i: (0, 0)),
            pl.BlockSpec((Q_RANK, aw), lambda i: (0, 0)),
            pl.BlockSpec((Q_RANK, iw), lambda i: (0, 0)),
        ],
        out_specs=[
            pl.BlockSpec((tm, aw), row),
            pl.BlockSpec((tm, iw), row),
            pl.BlockSpec((tm, LANES), row),
            pl.BlockSpec((tm, LANES), row),
            pl.BlockSpec((tm, LANES), row),
        ],
        out_shape=[
            jax.ShapeDtypeStruct((n, aw), BF16),
            jax.ShapeDtypeStruct((n, iw), BF16),
            jax.ShapeDtypeStruct((n, LANES), BF16),
            jax.ShapeDtypeStruct((n, LANES), BF16),
            jax.ShapeDtypeStruct((n, LANES), BF16),
        ],
        compiler_params=_cparams(("parallel",), 40),
        name="dsa_prep",
    )(p, p, p, p, cos2, sin2, g_cq.reshape(1, Q_RANK), w_uq, w_qidx)


def _dot_nt(a, b):
    return lax.dot_general(a, b, (((1,), (1,)), ((), ())), preferred_element_type=F32)


def _dsa_kernel(qi_ref, q_ref, w_ref, ki_ref, k_ref, v_ref, o_ref,
                key_ref, qs_ref, m_ref, l_ref, acc_ref, *, tq, tk, topk, wscale):
    i = pl.program_id(1)
    nkc = ((i + 1) * tq + tk - 1) // tk
    nlb = tk // LANES
    q_pos = i * tq + lax.broadcasted_iota(I32, (tq, tk), 0)
    lane_pos = lax.broadcasted_iota(I32, (tq, tk), 1)

    w = w_ref[...] * wscale
    wcols = [w[:, h:h + 1] for h in range(IDX_HEADS)]

    def score_chunk(j, carry):
        kic = ki_ref[pl.ds(pl.multiple_of(j * tk, tk), tk), :]
        acc = jnp.zeros((tq, tk), F32)
        for h in range(IDX_HEADS):
            lg = _dot_nt(qi_ref[:, h * HEAD_DIM:(h + 1) * HEAD_DIM], kic)
            acc = acc + jnp.maximum(lg, 0.0) * wcols[h]
        bits = lax.bitcast_convert_type(acc, I32)
        key = jnp.where(bits < 0, bits ^ jnp.int32(0x7FFFFFFF), bits)
        causal = (j * tk + lane_pos) <= q_pos
        key_ref[j] = jnp.where(causal, key, jnp.int32(INT_MIN))
        return carry

    lax.fori_loop(0, nkc, score_chunk, 0)

    def bit_step(b, t_u):
        bit = jnp.left_shift(jnp.int32(1), 31 - b)
        cand_s = (t_u | bit) ^ jnp.int32(INT_MIN)

        def cnt_chunk(j, part):
            ge = (key_ref[j] >= cand_s).astype(I32)
            for c in range(nlb):
                part = part + ge[:, c * LANES:(c + 1) * LANES]
            return part

        part = lax.fori_loop(0, nkc, cnt_chunk, jnp.zeros((tq, LANES), I32))
        cnt = jnp.sum(part, axis=1, keepdims=True)
        return jnp.where(cnt >= topk, t_u | bit, t_u)

    t_u = lax.fori_loop(0, 32, bit_step, jnp.zeros((tq, 1), I32))
    thr = jnp.maximum(t_u ^ jnp.int32(INT_MIN), jnp.int32(INT_MIN + 1))

    for h in range(A_HEADS):
        qs_ref[h * tq:(h + 1) * tq, :] = q_ref[:, h * HEAD_DIM:(h + 1) * HEAD_DIM]
    m_ref[...] = jnp.full(m_ref.shape, -1e30, F32)
    l_ref[...] = jnp.zeros(l_ref.shape, F32)
    acc_ref[...] = jnp.zeros(acc_ref.shape, F32)

    def att_chunk(j, carry):
        row0 = pl.multiple_of(j * tk, tk)
        kc = k_ref[pl.ds(row0, tk), :]
        vc = v_ref[pl.ds(row0, tk), :]
        sel = key_ref[j] >= thr
        s = _dot_nt(qs_ref[...], kc).reshape(A_HEADS, tq, tk)
        s = jnp.where(sel[None], s, -1e30)
        m_old = m_ref[...]
        m_new = jnp.maximum(m_old, jnp.max(s, axis=-1, keepdims=True))
        p = jnp.where(sel[None], jnp.exp(s - m_new), 0.0)
        alpha = jnp.exp(m_old - m_new)
        l_ref[...] = alpha * l_ref[...] + jnp.sum(p, axis=-1, keepdims=True)
        pv = jnp.dot(p.reshape(A_HEADS * tq, tk).astype(BF16), vc, preferred_element_type=F32)
        acc_ref[...] = alpha * acc_ref[...] + pv.reshape(A_HEADS, tq, HEAD_DIM)
        m_ref[...] = m_new
        return carry

    lax.fori_loop(0, nkc, att_chunk, 0)
    out = acc_ref[...] / l_ref[...]
    for h in range(A_HEADS):
        o_ref[:, h * HEAD_DIM:(h + 1) * HEAD_DIM] = out[h].astype(o_ref.dtype)


def _dsa(qi, q, p, ki, k, v, batch, seq, tq, tk):
    n = batch * seq
    nqb = seq // tq
    topk = min(TOPK_MAX, seq // 4)
    aw = A_HEADS * HEAD_DIM
    iw = IDX_HEADS * HEAD_DIM
    kern = functools.partial(_dsa_kernel, tq=tq, tk=tk, topk=topk,
                             wscale=IDX_HEADS ** -0.5 * HEAD_DIM ** -0.5)
    return pl.pallas_call(
        kern,
        grid=(batch, nqb),
        in_specs=[
            pl.BlockSpec((tq, iw), lambda b, i: (b * nqb + i, 0)),
            pl.BlockSpec((tq, aw), lambda b, i: (b * nqb + i, 0)),
            pl.BlockSpec((tq, LANES), lambda b, i: (b * nqb + i, 7)),
            pl.BlockSpec((seq, LANES), lambda b, i: (b, 0)),
            pl.BlockSpec((seq, LANES), lambda b, i: (b, 0)),
            pl.BlockSpec((seq, LANES), lambda b, i: (b, 0)),
        ],
        out_specs=pl.BlockSpec((tq, aw), lambda b, i: (b * nqb + i, 0)),
        out_shape=jax.ShapeDtypeStruct((n, aw), BF16),
        scratch_shapes=[
            pltpu.VMEM((seq // tk, tq, tk), I32),
            pltpu.VMEM((A_HEADS * tq, HEAD_DIM), BF16),
            pltpu.VMEM((A_HEADS, tq, 1), F32),
            pltpu.VMEM((A_HEADS, tq, 1), F32),
            pltpu.VMEM((A_HEADS, tq, HEAD_DIM), F32),
        ],
        compiler_params=_cparams(("parallel", "arbitrary"), 48),
        name="dsa",
    )(qi, q, p, ki, k, v)


HG_SUB = 16


def _hgrn_kernel(hq_ref, hf_ref, hi_ref, hg_ref, lb_ref, gn_ref, o_ref, st_ref, *, chunk):
    nsub = chunk // HG_SUB
    gw = hq_ref.shape[1]

    @pl.when(pl.program_id(1) == 0)
    def _():
        st_ref[...] = jnp.zeros(st_ref.shape, F32)

    lb = lb_ref[...]
    f = lb + (1.0 - lb) * _sigmoid(hf_ref[...])
    g = jnp.log(f)
    kk = 1.0 - f
    hq = hq_ref[...]
    q = hq * _sigmoid(hq)
    v = hi_ref[...]

    row = lax.broadcasted_iota(I32, (chunk, gw), 0)
    rsub = row & (HG_SUB - 1)
    sub = row // HG_SUB
    lc = g
    sh = 1
    while sh < HG_SUB:
        lc = lc + jnp.where(rsub >= sh, pltpu.roll(lc, sh, 0), 0.0)
        sh *= 2
    tots = [lc[(a + 1) * HG_SUB - 1:(a + 1) * HG_SUB, :] for a in range(nsub)]
    ends = []
    run = jnp.zeros((1, gw), F32)
    for a in range(nsub):
        run = run + tots[a]
        ends.append(run)
    cum_l = ends[-1]
    start_full = jnp.zeros((chunk, gw), F32)
    tot_full = jnp.zeros((chunk, gw), F32)
    end_full = jnp.zeros((chunk, gw), F32)
    for a in range(nsub):
        ina = sub == a
        if a > 0:
            start_full = jnp.where(ina, ends[a - 1], start_full)
        tot_full = jnp.where(ina, tots[a], tot_full)
        end_full = jnp.where(ina, ends[a], end_full)
    cum = lc + start_full

    qd = (q * jnp.exp(lc)).astype(BF16)
    kd = (kk * jnp.exp(jnp.minimum(-lc, EXP_CLAMP))).astype(BF16)
    qs = (q * jnp.exp(cum)).astype(BF16)
    koff = kk * jnp.exp(tot_full - lc)
    ke = (koff * jnp.exp(cum_l - end_full)).astype(BF16)
    koff_b = koff.astype(BF16)
    zero_b = jnp.zeros((chunk, gw), BF16)
    qoff = []
    kmask = []
    for b in range(nsub - 1):
        qb = q * jnp.exp(jnp.minimum(cum - ends[b], 0.0))
        qoff.append(jnp.where(sub > b, qb, 0.0).astype(BF16))
        kmask.append(jnp.where(sub == b, koff_b, zero_b))
    vb = v.astype(BF16)
    decay_l = jnp.exp(cum_l)

    ti = lax.broadcasted_iota(I32, (chunk, chunk), 0)
    si = lax.broadcasted_iota(I32, (chunk, chunk), 1)
    diag_ok = (ti // HG_SUB == si // HG_SUB) & (si <= ti)

    outs = []
    for h in range(G_HEADS):
        sl = slice(h * HEAD_DIM, (h + 1) * HEAD_DIM)
        a_diag = jnp.where(diag_ok, _dot_nt(qd[:, sl], kd[:, sl]), 0.0)
        qcat = jnp.concatenate([qo[:, sl] for qo in qoff], axis=1)
        kcat = jnp.concatenate([km[:, sl] for km in kmask], axis=1)
        a = (a_diag + _dot_nt(qcat, kcat)).astype(BF16)
        st = st_ref[h]
        o_h = jnp.dot(a, vb[:, sl], preferred_element_type=F32) + _dot_nt(qs[:, sl], st.astype(BF16))
        outs.append(o_h)
        upd = jnp.dot(v[:, sl].T.astype(BF16), ke[:, sl], preferred_element_type=F32)
        st_ref[h] = st * decay_l[:, sl] + upd
    o = jnp.concatenate(outs, axis=1)
    ms = jnp.mean(o * o, axis=-1, keepdims=True)
    hg = hg_ref[...]
    y = o * lax.rsqrt(ms + EPS) * gn_ref[...] * (hg * _sigmoid(hg))
    o_ref[...] = y.astype(o_ref.dtype)


def _hgrn(p, lb, g_hnorm, batch, seq, chunk):
    n = batch * seq
    gw = G_HEADS * HEAD_DIM
    ncb = seq // chunk
    blk = lambda c: pl.BlockSpec((chunk, gw), lambda b, i: (b * ncb + i, c))
    return pl.pallas_call(
        functools.partial(_hgrn_kernel, chunk=chunk),
        grid=(batch, ncb),
        in_specs=[blk(1), blk(2), blk(3), blk(4),
                  pl.BlockSpec((1, gw), lambda b, i: (0, 0)),
                  pl.BlockSpec((1, gw), lambda b, i: (0, 0))],
        out_specs=pl.BlockSpec((chunk, gw), lambda b, i: (b * ncb + i, 0)),
        out_shape=jax.ShapeDtypeStruct((n, gw), BF16),
        scratch_shapes=[pltpu.VMEM((G_HEADS, HEAD_DIM, HEAD_DIM), F32)],
        compiler_params=_cparams(("parallel", "arbitrary"), 32),
        name="hgrn",
    )(p, p, p, p, lb.reshape(1, gw), g_hnorm.reshape(1, gw))


def _merge_kernel(ya_ref, yr_ref, ga_ref, gb_ref, x_ref, wpa_ref, wpb_ref, wo_ref, o_ref, acc_ref):
    c = pl.program_id(1)

    @pl.when(c == 0)
    def _():
        acc_ref[...] = x_ref[...]

    ma = jnp.dot(ya_ref[...], wpa_ref[...], preferred_element_type=F32)
    mb = jnp.dot(yr_ref[...], wpb_ref[...], preferred_element_type=F32)
    merged = _sigmoid(ga_ref[...]) * ma + _sigmoid(gb_ref[...]) * mb
    acc_ref[...] += jnp.dot(merged.astype(BF16), wo_ref[...], preferred_element_type=F32)

    @pl.when(c == pl.num_programs(1) - 1)
    def _():
        o_ref[...] = acc_ref[...]


def _merge(y_att, y_rec, p, x2, w_pa, w_pb, w_o, tm, tc):
    n, d = x2.shape
    aw = y_att.shape[1]
    gw = y_rec.shape[1]
    ga0 = 5120 // tc
    gb0 = 7168 // tc
    return pl.pallas_call(
        _merge_kernel,
        grid=(n // tm, d // tc),
        in_specs=[
            pl.BlockSpec((tm, aw), lambda i, c: (i, 0)),
            pl.BlockSpec((tm, gw), lambda i, c: (i, 0)),
            pl.BlockSpec((tm, tc), lambda i, c: (i, ga0 + c)),
            pl.BlockSpec((tm, tc), lambda i, c: (i, gb0 + c)),
            pl.BlockSpec((tm, d), lambda i, c: (i, 0)),
            pl.BlockSpec((aw, tc), lambda i, c: (0, c)),
            pl.BlockSpec((gw, tc), lambda i, c: (0, c)),
            pl.BlockSpec((tc, d), lambda i, c: (c, 0)),
        ],
        out_specs=pl.BlockSpec((tm, d), lambda i, c: (i, 0)),
        out_shape=jax.ShapeDtypeStruct((n, d), F32),
        scratch_shapes=[pltpu.VMEM((tm, d), F32)],
        compiler_params=_cparams(("parallel", "arbitrary"), 48),
        name="merge",
    )(y_att, y_rec, p, p, x2, w_pa, w_pb, w_o)


def _router_kernel(x_ref, g_ref, wr_ref, br_ref, h_ref, r_ref):
    x = x_ref[...]
    ms = jnp.mean(x * x, axis=-1, keepdims=True)
    h = x * lax.rsqrt(ms + EPS) * g_ref[...]
    h_ref[...] = h
    lg = jnp.dot(h.astype(BF16), wr_ref[...], preferred_element_type=F32) + br_ref[...]
    tm = lg.shape[0]
    lane = lax.broadcasted_iota(I32, (tm, LANES), 1)
    neg = jnp.float32(-1e30)
    big = jnp.float32(LANES)
    is_g = lane < N_GROUPS
    lgg = jnp.where(is_g, lg, neg)
    mg = jnp.max(lgg, axis=1, keepdims=True)
    zg = jnp.sum(jnp.where(is_g, jnp.exp(lgg - mg), 0.0), axis=1, keepdims=True)
    p_g = 1.0 / zg
    grp = jnp.min(jnp.where(is_g & (lgg == mg), lane.astype(F32), big), axis=1, keepdims=True)
    e_id = lane - N_GROUPS
    e_grp = lax.shift_right_arithmetic(e_id, jnp.int32(EXP_PER_GROUP.bit_length() - 1)).astype(F32)
    e_idf = e_id.astype(F32)
    is_e = (e_id >= 0) & (e_id < N_EXPERTS) & (e_grp == grp)
    lge = jnp.where(is_e, lg, neg)
    m1 = jnp.max(lge, axis=1, keepdims=True)
    i1 = jnp.min(jnp.where(is_e & (lge == m1), e_idf, big), axis=1, keepdims=True)
    is_e2 = is_e & (e_idf != i1)
    lge2 = jnp.where(is_e2, lg, neg)
    m2 = jnp.max(lge2, axis=1, keepdims=True)
    i2 = jnp.min(jnp.where(is_e2 & (lge2 == m2), e_idf, big), axis=1, keepdims=True)
    ex2 = jnp.exp(m2 - m1)
    den = 1.0 + ex2
    g1 = p_g / den
    g2 = p_g * ex2 / den
    r = jnp.where(lane == 0, i1, 0.0)
    r = jnp.where(lane == 1, i2, r)
    r = jnp.where(lane == 2, g1, r)
    r = jnp.where(lane == 3, g2, r)
    r_ref[...] = r


def _router(x1, g, w_r, b_r, tm):
    n, d = x1.shape
    return pl.pallas_call(
        _router_kernel,
        grid=(n // tm,),
        in_specs=[
            pl.BlockSpec((tm, d), lambda i: (i, 0)),
            pl.BlockSpec((1, d), lambda i: (0, 0)),
            pl.BlockSpec((d, LANES), lambda i: (0, 0)),
            pl.BlockSpec((1, LANES), lambda i: (0, 0)),
        ],
        out_specs=[pl.BlockSpec((tm, d), lambda i: (i, 0)),
                   pl.BlockSpec((tm, LANES), lambda i: (i, 0))],
        out_shape=[jax.ShapeDtypeStruct((n, d), F32),
                   jax.ShapeDtypeStruct((n, LANES), F32)],
        compiler_params=_cparams(("parallel",), 40),
        name="router",
    )(x1, g.reshape(1, d), w_r, b_r)


def _moe_kernel(te_ref, src_ref, dst_ref, nt_ref, h_hbm, wg_ref, wu_ref, wd_ref, y_hbm,
                xbuf, ybuf, gsem, ssem, *, tm):
    m = pl.program_id(0)

    @pl.when(m < nt_ref[0])
    def _():
        base = m * tm

        def gather_row(r, carry):
            pltpu.make_async_copy(h_hbm.at[pl.ds(src_ref[base + r], 1), :],
                                  xbuf.at[pl.ds(r, 1), :], gsem).start()
            return carry

        lax.fori_loop(0, tm, gather_row, 0)
        pltpu.make_async_copy(h_hbm.at[pl.ds(0, tm), :], xbuf, gsem).wait()

        xb = xbuf[...].astype(BF16)
        a = jnp.dot(xb, wg_ref[0], preferred_element_type=F32)
        u = jnp.dot(xb, wu_ref[0], preferred_element_type=F32)
        hmid = (a * _sigmoid(a) * u).astype(BF16)
        ybuf[...] = jnp.dot(hmid, wd_ref[0], preferred_element_type=F32)

        def scatter_row(r, carry):
            pltpu.make_async_copy(ybuf.at[pl.ds(r, 1), :],
                                  y_hbm.at[pl.ds(dst_ref[base + r], 1), :], ssem).start()
            return carry

        lax.fori_loop(0, tm, scatter_row, 0)
        pltpu.make_async_copy(ybuf, y_hbm.at[pl.ds(0, tm), :], ssem).wait()

    @pl.when(m >= nt_ref[0])
    def _():
        ybuf[...] = jnp.zeros(ybuf.shape, F32)
        row0 = pl.multiple_of(dst_ref[m * tm], tm)
        cp = pltpu.make_async_copy(ybuf, y_hbm.at[pl.ds(row0, tm), :], ssem)
        cp.start()
        cp.wait()


def _moe(h2, tile_expert, row_src, row_dst, n_tiles, w_gate, w_up, w_down, tm, n_rows_pad):
    n, d = h2.shape
    de = w_gate.shape[2]
    max_tiles = n_rows_pad // tm
    grid_spec = pltpu.PrefetchScalarGridSpec(
        num_scalar_prefetch=4,
        grid=(max_tiles,),
        in_specs=[
            pl.BlockSpec(memory_space=pl.ANY),
            pl.BlockSpec((1, d, de), lambda m, te, *_: (te[m], 0, 0)),
            pl.BlockSpec((1, d, de), lambda m, te, *_: (te[m], 0, 0)),
            pl.BlockSpec((1, de, d), lambda m, te, *_: (te[m], 0, 0)),
        ],
        out_specs=pl.BlockSpec(memory_space=pl.ANY),
        scratch_shapes=[
            pltpu.VMEM((tm, d), F32),
            pltpu.VMEM((tm, d), F32),
            pltpu.SemaphoreType.DMA(()),
            pltpu.SemaphoreType.DMA(()),
        ],
    )
    return pl.pallas_call(
        functools.partial(_moe_kernel, tm=tm),
        grid_spec=grid_spec,
        out_shape=jax.ShapeDtypeStruct((n_rows_pad, d), F32),
        compiler_params=_cparams(("arbitrary",), 48),
        name="moe",
    )(tile_expert, row_src, row_dst, n_tiles, h2, w_gate, w_up, w_down)


def _moe_plan(eid, n_tok, tm, n_rows_pad):
    na = eid.shape[0]
    order = jnp.argsort(eid).astype(I32)
    sizes = jnp.bincount(eid, length=N_EXPERTS).astype(I32)
    starts = jnp.cumsum(sizes) - sizes
    psizes = ((sizes + tm - 1) // tm) * tm
    pends = jnp.cumsum(psizes)
    pstarts = pends - psizes
    e_sorted = eid[order]
    j = jnp.arange(na, dtype=I32)
    prow = pstarts[e_sorted] + (j - starts[e_sorted])
    is_real = jnp.zeros((n_rows_pad,), bool).at[prow].set(True)
    spare = na + jnp.cumsum(~is_real) - 1
    row_src = jnp.zeros((n_rows_pad,), I32).at[prow].set(order // 2)
    row_dst = jnp.where(is_real, jnp.zeros((n_rows_pad,), I32).at[prow].set(order), spare).astype(I32)
    n_tiles = (pends[-1] // tm).astype(I32).reshape(1)
    tile_start = jnp.arange(n_rows_pad // tm, dtype=I32) * tm
    tile_expert = jnp.minimum(jnp.searchsorted(pends, tile_start, side="right"), N_EXPERTS - 1).astype(I32)
    last_e = tile_expert[jnp.maximum(n_tiles[0] - 1, 0)]
    tile_expert = jnp.where(tile_start < pends[-1], tile_expert, last_e)
    return tile_expert, row_src, row_dst, n_tiles


def _final_kernel(x_ref, y_ref, r_ref, g_ref, o_ref):
    d = x_ref.shape[1]
    r = r_ref[...]
    g1 = r[:, 2:3]
    g2 = r[:, 3:4]
    x = x_ref[...] + g1 * y_ref[:, :d] + g2 * y_ref[:, d:]
    ms = jnp.mean(x * x, axis=-1, keepdims=True)
    o_ref[...] = x * lax.rsqrt(ms + EPS) * g_ref[...]


def _final(x1, y2, r, g, tm):
    n, d = x1.shape
    return pl.pallas_call(
        _final_kernel,
        grid=(n // tm,),
        in_specs=[
            pl.BlockSpec((tm, d), lambda i: (i, 0)),
            pl.BlockSpec((tm, 2 * d), lambda i: (i, 0)),
            pl.BlockSpec((tm, LANES), lambda i: (i, 0)),
            pl.BlockSpec((1, d), lambda i: (0, 0)),
        ],
        out_specs=pl.BlockSpec((tm, d), lambda i: (i, 0)),
        out_shape=jax.ShapeDtypeStruct((n, d), F32),
        compiler_params=_cparams(("parallel",), 40),
        name="final",
    )(x1, y2, r, g.reshape(1, d))


def _pack_w_in(w):
    d = w.shape[0]
    head = w[:, :896]
    widx = w[:, 896:912]
    rest = w[:, 912:]
    pad = jnp.zeros((d, LANES - IDX_HEADS), w.dtype)
    return jnp.concatenate([head, widx, pad, rest], axis=1).astype(BF16)


def _rope_tables(seq):
    half = HEAD_DIM // 2
    inv = 1.0 / (ROPE_THETA ** (jnp.arange(0, HEAD_DIM, 2, dtype=F32) / HEAD_DIM))
    ang = jnp.arange(seq, dtype=F32)[:, None] * inv[None, :]
    c, s = jnp.cos(ang), jnp.sin(ang)
    del half
    return jnp.concatenate([c, c], axis=1), jnp.concatenate([-s, s], axis=1)


def _layer(x2, batch, seq, cos2, sin2, lb, g_norm1, w_in, g_cq, w_uq, w_qidx, g_hnorm, w_pa, w_pb, w_o,
           g_norm2, w_grp, b_grp, w_exp, b_exp, w_gate, w_up, w_down, g_out):
    n, d = x2.shape
    p = _in_proj(x2, g_norm1, _pack_w_in(w_in), tm=min(512, n), tn=1024)
    q, qi, k, ki, v = _dsa_prep(p, cos2, sin2, g_cq, w_uq.astype(BF16), w_qidx.astype(BF16), seq, tm=min(256, seq))
    y_att = _dsa(qi, q, p, ki, k, v, batch, seq, tq=128, tk=min(256, seq))
    y_rec = _hgrn(p, lb, g_hnorm, batch, seq, chunk=64)
    x1 = _merge(y_att, y_rec, p, x2, w_pa.astype(BF16), w_pb.astype(BF16), w_o.astype(BF16),
                tm=min(512, n), tc=512)
    w_r = jnp.concatenate([w_grp, w_exp, jnp.zeros((d, LANES - N_GROUPS - N_EXPERTS), F32)], axis=1).astype(BF16)
    b_r = jnp.concatenate([b_grp, b_exp, jnp.zeros((LANES - N_GROUPS - N_EXPERTS,), F32)]).reshape(1, LANES)
    h2, r = _router(x1, g_norm2, w_r, b_r, tm=min(512, n))
    eid = r[:, :2].astype(I32).reshape(-1)
    tm_moe = 256
    assert (2 * n) % tm_moe == 0
    n_rows_pad = 2 * n + N_EXPERTS * tm_moe
    te, row_src, row_dst, n_tiles = _moe_plan(eid, n, tm_moe, n_rows_pad)
    y = _moe(h2, te, row_src, row_dst, n_tiles, w_gate.astype(BF16), w_up.astype(BF16), w_down.astype(BF16),
             tm_moe, n_rows_pad)
    y2 = y.reshape(n_rows_pad // 2, 2 * d)
    return _final(x1, y2, r, g_out, tm=min(512, n))


def kernel(x, g_norm1, w_in, g_cq, w_uq, w_qidx, lb_logits, g_hnorm, w_pa, w_pb, w_o, g_norm2, w_grp, b_grp,
           w_exp, b_exp, w_gate, w_up, w_down, g_final):
    batch, seq, d = x.shape
    depth = g_norm1.shape[0]
    cos2, sin2 = _rope_tables(seq)
    lb_all = jnp.cumsum(jax.nn.softmax(lb_logits.astype(F32), axis=0), axis=0)
    x2 = x.reshape(batch * seq, d)
    for l in range(depth):
        assert depth == 1
        x2 = _layer(x2, batch, seq, cos2, sin2, lb_all[l], g_norm1[l], w_in[l], g_cq[l], w_uq[l], w_qidx[l],
                    g_hnorm[l], w_pa[l], w_pb[l], w_o[l], g_norm2[l], w_grp[l], b_grp[l], w_exp[l], b_exp[l],
                    w_gate[l], w_up[l], w_down[l], g_final)
    return x2.reshape(batch, seq, d)
```

```python
import functools

import jax
import jax.numpy as jnp
from jax import lax
from jax.experimental import pallas as pl
from jax.experimental.pallas import tpu as pltpu

F32 = jnp.float32
BF16 = jnp.bfloat16
I32 = jnp.int32

EPS = 1e-6
ROPE_THETA = 10000.0
LANES = 128

A_HEADS = 8
HEAD_DIM = 128
IDX_HEADS = 16
Q_RANK = 512
TOPK_MAX = 256
G_HEADS = 8
N_GROUPS = 4
EXP_PER_GROUP = 8
N_EXPERTS = N_GROUPS * EXP_PER_GROUP
D_EXPERT = 512

INT_MIN = -2147483648
EXP_CLAMP = 80.0


def _cparams(sem, vmem_mb):
    return pltpu.CompilerParams(dimension_semantics=sem, vmem_limit_bytes=vmem_mb << 20)


def _sigmoid(x):
    return 1.0 / (1.0 + jnp.exp(-x))


def _inproj_kernel(x_ref, g_ref, w_ref, o_ref, xn_ref):
    @pl.when(pl.program_id(1) == 0)
    def _():
        x = x_ref[...]
        ms = jnp.mean(x * x, axis=-1, keepdims=True)
        xn_ref[...] = (x * lax.rsqrt(ms + EPS) * g_ref[...]).astype(BF16)

    o_ref[...] = jnp.dot(xn_ref[...], w_ref[...], preferred_element_type=F32)


def _in_proj(x2, g, w, tm, tn):
    n, d = x2.shape
    dout = w.shape[1]
    return pl.pallas_call(
        _inproj_kernel,
        grid=(n // tm, dout // tn),
        in_specs=[
            pl.BlockSpec((tm, d), lambda i, j: (i, 0)),
            pl.BlockSpec((1, d), lambda i, j: (0, 0)),
            pl.BlockSpec((d, tn), lambda i, j: (0, j)),
        ],
        out_specs=pl.BlockSpec((tm, tn), lambda i, j: (i, j)),
        out_shape=jax.ShapeDtypeStruct((n, dout), F32),
        scratch_shapes=[pltpu.VMEM((tm, d), BF16)],
        compiler_params=_cparams(("parallel", "arbitrary"), 48),
        name="in_proj",
    )(x2, g.reshape(1, d), w)


def _rope(x, c2, s2):
    return x * c2 + pltpu.roll(x, HEAD_DIM // 2, 1) * s2


def _prep_kernel(cq_ref, ka_ref, va_ref, kx_ref, wi_ref, cos_ref, sin_ref, g_ref, wuq_ref, wqi_ref,
                 qt_ref, qit_ref, wt_ref, k_ref, ki_ref, vt_ref, *, scale, wscale, tq):
    nqb = cq_ref.shape[0] // tq
    c2 = cos_ref[...]
    s2 = sin_ref[...]
    cq = cq_ref[...]
    ms = jnp.mean(cq * cq, axis=-1, keepdims=True)
    cqn = (cq * lax.rsqrt(ms + EPS) * g_ref[...]).astype(BF16)
    q = jnp.dot(cqn, wuq_ref[...], preferred_element_type=F32)
    for h in range(A_HEADS):
        sl = slice(h * HEAD_DIM, (h + 1) * HEAD_DIM)
        qh_t = (_rope(q[:, sl], c2, s2) * scale).T
        for b in range(nqb):
            qt_ref[b, :, h * tq:(h + 1) * tq] = qh_t[:, b * tq:(b + 1) * tq].astype(BF16)
    qi = jnp.dot(cqn, wqi_ref[...], preferred_element_type=F32)
    for h in range(IDX_HEADS):
        sl = slice(h * HEAD_DIM, (h + 1) * HEAD_DIM)
        qh_t = _rope(qi[:, sl], c2, s2).T
        for b in range(nqb):
            qit_ref[b, :, h * tq:(h + 1) * tq] = qh_t[:, b * tq:(b + 1) * tq].astype(BF16)
    w_t = (wi_ref[...] * wscale).T
    for h in range(IDX_HEADS):
        for b in range(nqb):
            wt_ref[b, :, h * tq:(h + 1) * tq] = w_t[h:h + 1, b * tq:(b + 1) * tq]
    k_ref[...] = _rope(ka_ref[...], c2, s2).astype(BF16)
    ki_ref[...] = _rope(kx_ref[...], c2, s2).astype(BF16)
    vt_ref[0] = va_ref[...].T.astype(BF16)


def _dsa_prep(p, cos2, sin2, g_cq, w_uq, w_qidx, seq, tm, tq):
    n = p.shape[0]
    nsb = seq // tm
    nqb = tm // tq
    aw = A_HEADS * HEAD_DIM
    iw = IDX_HEADS * HEAD_DIM
    row = lambda i: (i, 0)
    blk3 = lambda i: (i, 0, 0)
    kern = functools.partial(_prep_kernel, scale=HEAD_DIM ** -0.5,
                             wscale=IDX_HEADS ** -0.5 * HEAD_DIM ** -0.5, tq=tq)
    return pl.pallas_call(
        kern,
        grid=(n // tm,),
        in_specs=[
            pl.BlockSpec((tm, Q_RANK), lambda i: (i, 0)),
            pl.BlockSpec((tm, LANES), lambda i: (i, 4)),
            pl.BlockSpec((tm, LANES), lambda i: (i, 5)),
            pl.BlockSpec((tm, LANES), lambda i: (i, 6)),
            pl.BlockSpec((tm, LANES), lambda i: (i, 7)),
            pl.BlockSpec((tm, LANES), lambda i: (i % nsb, 0)),
            pl.BlockSpec((tm, LANES), lambda i: (i % nsb, 0)),
            pl.BlockSpec((1, Q_RANK), lambda i: (0, 0)),
            pl.BlockSpec((Q_RANK, aw), lambda i: (0, 0)),
            pl.BlockSpec((Q_RANK, iw), lambda i: (0, 0)),
        ],
        out_specs=[
            pl.BlockSpec((nqb, HEAD_DIM, A_HEADS * tq), blk3),
            pl.BlockSpec((nqb, HEAD_DIM, IDX_HEADS * tq), blk3),
            pl.BlockSpec((nqb, 1, IDX_HEADS * tq), blk3),
            pl.BlockSpec((tm, LANES), row),
            pl.BlockSpec((tm, LANES), row),
            pl.BlockSpec((1, HEAD_DIM, tm), blk3),
        ],
        out_shape=[
            jax.ShapeDtypeStruct((n // tq, HEAD_DIM, A_HEADS * tq), BF16),
            jax.ShapeDtypeStruct((n // tq, HEAD_DIM, IDX_HEADS * tq), BF16),
            jax.ShapeDtypeStruct((n // tq, 1, IDX_HEADS * tq), F32),
            jax.ShapeDtypeStruct((n, LANES), BF16),
            jax.ShapeDtypeStruct((n, LANES), BF16),
            jax.ShapeDtypeStruct((n // tm, HEAD_DIM, tm), BF16),
        ],
        compiler_params=_cparams(("parallel",), 40),
        name="dsa_prep",
    )(p, p, p, p, p, cos2, sin2, g_cq.reshape(1, Q_RANK), w_uq, w_qidx)


def _dot_nt(a, b):
    return lax.dot_general(a, b, (((1,), (1,)), ((), ())), preferred_element_type=F32)


def _dsa_kernel(qit_ref, qt_ref, wt_ref, ki_ref, k_ref, vt_ref, o_ref,
                key_ref, p_ref, m_ref, l_ref, al_ref, acc_ref, *, tq, tk, topk):
    i = pl.program_id(1)
    nkc = ((i + 1) * tq + tk - 1) // tk
    q_pos = i * tq + lax.broadcasted_iota(I32, (tk, tq), 1)
    k_off = lax.broadcasted_iota(I32, (tk, tq), 0)

    def score_chunk(j, carry):
        kic = ki_ref[pl.ds(pl.multiple_of(j * tk, tk), tk), :]
        acc = jnp.zeros((tk, tq), F32)
        for h2 in range(IDX_HEADS // 2):
            sl = slice(2 * h2 * tq, (2 * h2 + 2) * tq)
            lg = jnp.dot(kic, qit_ref[0, :, sl], preferred_element_type=F32)
            x = jnp.maximum(lg, 0.0) * wt_ref[0, :, sl]
            acc = acc + x[:, :tq] + x[:, tq:]
        bits = lax.bitcast_convert_type(acc, I32)
        key = jnp.where(bits < 0, bits ^ jnp.int32(0x7FFFFFFF), bits)
        causal = (j * tk + k_off) <= q_pos
        key_ref[j] = jnp.where(causal, key, jnp.int32(INT_MIN))
        return carry

    lax.fori_loop(0, nkc, score_chunk, 0)

    def bit_step(b, t_u):
        bit = jnp.left_shift(jnp.int32(1), 31 - b)
        cand_s = (t_u | bit) ^ jnp.int32(INT_MIN)

        def cnt_chunk(j, part):
            ge = (key_ref[j] >= cand_s).astype(I32)
            return part + jnp.sum(ge.reshape(tk // 8, 8, tq), axis=0)

        part = lax.fori_loop(0, nkc, cnt_chunk, jnp.zeros((8, tq), I32))
        cnt = jnp.sum(part, axis=0, keepdims=True)
        return jnp.where(cnt >= topk, t_u | bit, t_u)

    t_u = lax.fori_loop(0, 32, bit_step, jnp.zeros((1, tq), I32))
    thr = jnp.maximum(t_u ^ jnp.int32(INT_MIN), jnp.int32(INT_MIN + 1))

    m_ref[...] = jnp.full(m_ref.shape, -1e30, F32)
    l_ref[...] = jnp.zeros(l_ref.shape, F32)
    acc_ref[...] = jnp.zeros(acc_ref.shape, F32)

    def att_chunk(j, carry):
        kc = k_ref[pl.ds(pl.multiple_of(j * tk, tk), tk), :]
        sel = key_ref[j] >= thr
        s = jnp.dot(kc, qt_ref[0], preferred_element_type=F32)
        for h in range(A_HEADS):
            sl = slice(h * tq, (h + 1) * tq)
            s_h = jnp.where(sel, s[:, sl], -1e30)
            m_old = m_ref[:, sl]
            m_new = jnp.maximum(m_old, jnp.max(s_h, axis=0, keepdims=True))
            p = jnp.where(sel, jnp.exp(s_h - m_new), 0.0)
            alpha = jnp.exp(m_old - m_new)
            l_ref[:, sl] = alpha * l_ref[:, sl] + jnp.sum(p, axis=0, keepdims=True)
            m_ref[:, sl] = m_new
            al_ref[:, sl] = alpha
            p_ref[:, sl] = p.astype(BF16)
        pv = jnp.dot(vt_ref[j], p_ref[...], preferred_element_type=F32)
        acc_ref[...] = al_ref[...] * acc_ref[...] + pv
        return carry

    lax.fori_loop(0, nkc, att_chunk, 0)
    out_t = acc_ref[...] / l_ref[...]
    for h in range(A_HEADS):
        o_ref[:, h * HEAD_DIM:(h + 1) * HEAD_DIM] = out_t[:, h * tq:(h + 1) * tq].T.astype(o_ref.dtype)


def _dsa(qit, qt, wt, ki, k, vt, batch, seq, tq, tk):
    n = batch * seq
    nqb = seq // tq
    nkb = seq // tk
    topk = min(TOPK_MAX, seq // 4)
    aw = A_HEADS * HEAD_DIM
    kern = functools.partial(_dsa_kernel, tq=tq, tk=tk, topk=topk)
    qblk = lambda b, i: (b * nqb + i, 0, 0)
    return pl.pallas_call(
        kern,
        grid=(batch, nqb),
        in_specs=[
            pl.BlockSpec((1, HEAD_DIM, IDX_HEADS * tq), qblk),
            pl.BlockSpec((1, HEAD_DIM, A_HEADS * tq), qblk),
            pl.BlockSpec((1, 1, IDX_HEADS * tq), qblk),
            pl.BlockSpec((seq, LANES), lambda b, i: (b, 0)),
            pl.BlockSpec((seq, LANES), lambda b, i: (b, 0)),
            pl.BlockSpec((nkb, HEAD_DIM, tk), lambda b, i: (b, 0, 0)),
        ],
        out_specs=pl.BlockSpec((tq, aw), lambda b, i: (b * nqb + i, 0)),
        out_shape=jax.ShapeDtypeStruct((n, aw), BF16),
        scratch_shapes=[
            pltpu.VMEM((nkb, tk, tq), I32),
            pltpu.VMEM((tk, A_HEADS * tq), BF16),
            pltpu.VMEM((1, A_HEADS * tq), F32),
            pltpu.VMEM((1, A_HEADS * tq), F32),
            pltpu.VMEM((1, A_HEADS * tq), F32),
            pltpu.VMEM((HEAD_DIM, A_HEADS * tq), F32),
        ],
        compiler_params=_cparams(("parallel", "arbitrary"), 48),
        name="dsa",
    )(qit, qt, wt, ki, k, vt)


HG_SUB = 16


def _hgrn_kernel(hq_ref, hf_ref, hi_ref, hg_ref, lb_ref, gn_ref, o_ref, st_ref, *, chunk):
    nsub = chunk // HG_SUB
    gw = hq_ref.shape[1]

    @pl.when(pl.program_id(1) == 0)
    def _():
        st_ref[...] = jnp.zeros(st_ref.shape, F32)

    lb = lb_ref[...]
    f = lb + (1.0 - lb) * _sigmoid(hf_ref[...])
    g = jnp.log(f)
    kk = 1.0 - f
    hq = hq_ref[...]
    q = hq * _sigmoid(hq)
    v = hi_ref[...]

    row = lax.broadcasted_iota(I32, (chunk, gw), 0)
    rsub = row & (HG_SUB - 1)
    sub = row // HG_SUB
    lc = g
    sh = 1
    while sh < HG_SUB:
        lc = lc + jnp.where(rsub >= sh, pltpu.roll(lc, sh, 0), 0.0)
        sh *= 2
    tots = [lc[(a + 1) * HG_SUB - 1:(a + 1) * HG_SUB, :] for a in range(nsub)]
    ends = []
    run = jnp.zeros((1, gw), F32)
    for a in range(nsub):
        run = run + tots[a]
        ends.append(run)
    cum_l = ends[-1]
    start_full = jnp.zeros((chunk, gw), F32)
    tot_full = jnp.zeros((chunk, gw), F32)
    end_full = jnp.zeros((chunk, gw), F32)
    for a in range(nsub):
        ina = sub == a
        if a > 0:
            start_full = jnp.where(ina, ends[a - 1], start_full)
        tot_full = jnp.where(ina, tots[a], tot_full)
        end_full = jnp.where(ina, ends[a], end_full)
    cum = lc + start_full

    qd = (q * jnp.exp(lc)).astype(BF16)
    kd = (kk * jnp.exp(jnp.minimum(-lc, EXP_CLAMP))).astype(BF16)
    qs = (q * jnp.exp(cum)).astype(BF16)
    koff = kk * jnp.exp(tot_full - lc)
    ke = (koff * jnp.exp(cum_l - end_full)).astype(BF16)
    koff_b = koff.astype(BF16)
    zero_b = jnp.zeros((chunk, gw), BF16)
    qoff = []
    kmask = []
    for b in range(nsub - 1):
        qb = q * jnp.exp(jnp.minimum(cum - ends[b], 0.0))
        qoff.append(jnp.where(sub > b, qb, 0.0).astype(BF16))
        kmask.append(jnp.where(sub == b, koff_b, zero_b))
    vb = v.astype(BF16)
    decay_l = jnp.exp(cum_l)

    ti = lax.broadcasted_iota(I32, (chunk, chunk), 0)
    si = lax.broadcasted_iota(I32, (chunk, chunk), 1)
    diag_ok = (ti // HG_SUB == si // HG_SUB) & (si <= ti)

    outs = []
    for h in range(G_HEADS):
        sl = slice(h * HEAD_DIM, (h + 1) * HEAD_DIM)
        a_diag = jnp.where(diag_ok, _dot_nt(qd[:, sl], kd[:, sl]), 0.0)
        qcat = jnp.concatenate([qo[:, sl] for qo in qoff], axis=1)
        kcat = jnp.concatenate([km[:, sl] for km in kmask], axis=1)
        a = (a_diag + _dot_nt(qcat, kcat)).astype(BF16)
        st = st_ref[h]
        o_h = jnp.dot(a, vb[:, sl], preferred_element_type=F32) + _dot_nt(qs[:, sl], st.astype(BF16))
        outs.append(o_h)
        upd = jnp.dot(v[:, sl].T.astype(BF16), ke[:, sl], preferred_element_type=F32)
        st_ref[h] = st * decay_l[:, sl] + upd
    o = jnp.concatenate(outs, axis=1)
    ms = jnp.mean(o * o, axis=-1, keepdims=True)
    hg = hg_ref[...]
    y = o * lax.rsqrt(ms + EPS) * gn_ref[...] * (hg * _sigmoid(hg))
    o_ref[...] = y.astype(o_ref.dtype)


def _hgrn(p, lb, g_hnorm, batch, seq, chunk):
    n = batch * seq
    gw = G_HEADS * HEAD_DIM
    ncb = seq // chunk
    blk = lambda c: pl.BlockSpec((chunk, gw), lambda b, i: (b * ncb + i, c))
    return pl.pallas_call(
        functools.partial(_hgrn_kernel, chunk=chunk),
        grid=(batch, ncb),
        in_specs=[blk(1), blk(2), blk(3), blk(4),
                  pl.BlockSpec((1, gw), lambda b, i: (0, 0)),
                  pl.BlockSpec((1, gw), lambda b, i: (0, 0))],
        out_specs=pl.BlockSpec((chunk, gw), lambda b, i: (b * ncb + i, 0)),
        out_shape=jax.ShapeDtypeStruct((n, gw), BF16),
        scratch_shapes=[pltpu.VMEM((G_HEADS, HEAD_DIM, HEAD_DIM), F32)],
        compiler_params=_cparams(("parallel", "arbitrary"), 32),
        name="hgrn",
    )(p, p, p, p, lb.reshape(1, gw), g_hnorm.reshape(1, gw))


def _merge_kernel(ya_ref, yr_ref, ga_ref, gb_ref, x_ref, wpa_ref, wpb_ref, wo_ref, o_ref, acc_ref):
    c = pl.program_id(1)

    @pl.when(c == 0)
    def _():
        acc_ref[...] = x_ref[...]

    ma = jnp.dot(ya_ref[...], wpa_ref[...], preferred_element_type=F32)
    mb = jnp.dot(yr_ref[...], wpb_ref[...], preferred_element_type=F32)
    merged = _sigmoid(ga_ref[...]) * ma + _sigmoid(gb_ref[...]) * mb
    acc_ref[...] += jnp.dot(merged.astype(BF16), wo_ref[...], preferred_element_type=F32)

    @pl.when(c == pl.num_programs(1) - 1)
    def _():
        o_ref[...] = acc_ref[...]


def _merge(y_att, y_rec, p, x2, w_pa, w_pb, w_o, tm, tc):
    n, d = x2.shape
    aw = y_att.shape[1]
    gw = y_rec.shape[1]
    ga0 = 5120 // tc
    gb0 = 7168 // tc
    return pl.pallas_call(
        _merge_kernel,
        grid=(n // tm, d // tc),
        in_specs=[
            pl.BlockSpec((tm, aw), lambda i, c: (i, 0)),
            pl.BlockSpec((tm, gw), lambda i, c: (i, 0)),
            pl.BlockSpec((tm, tc), lambda i, c: (i, ga0 + c)),
            pl.BlockSpec((tm, tc), lambda i, c: (i, gb0 + c)),
            pl.BlockSpec((tm, d), lambda i, c: (i, 0)),
            pl.BlockSpec((aw, tc), lambda i, c: (0, c)),
            pl.BlockSpec((gw, tc), lambda i, c: (0, c)),
            pl.BlockSpec((tc, d), lambda i, c: (c, 0)),
        ],
        out_specs=pl.BlockSpec((tm, d), lambda i, c: (i, 0)),
        out_shape=jax.ShapeDtypeStruct((n, d), F32),
        scratch_shapes=[pltpu.VMEM((tm, d), F32)],
        compiler_params=_cparams(("parallel", "arbitrary"), 48),
        name="merge",
    )(y_att, y_rec, p, p, x2, w_pa, w_pb, w_o)


def _router_kernel(x_ref, g_ref, wr_ref, br_ref, h_ref, r_ref):
    x = x_ref[...]
    ms = jnp.mean(x * x, axis=-1, keepdims=True)
    h = x * lax.rsqrt(ms + EPS) * g_ref[...]
    h_ref[...] = h
    lg = jnp.dot(h.astype(BF16), wr_ref[...], preferred_element_type=F32) + br_ref[...]
    tm = lg.shape[0]
    lane = lax.broadcasted_iota(I32, (tm, LANES), 1)
    neg = jnp.float32(-1e30)
    big = jnp.float32(LANES)
    is_g = lane < N_GROUPS
    lgg = jnp.where(is_g, lg, neg)
    mg = jnp.max(lgg, axis=1, keepdims=True)
    zg = jnp.sum(jnp.where(is_g, jnp.exp(lgg - mg), 0.0), axis=1, keepdims=True)
    p_g = 1.0 / zg
    grp = jnp.min(jnp.where(is_g & (lgg == mg), lane.astype(F32), big), axis=1, keepdims=True)
    e_id = lane - N_GROUPS
    e_grp = lax.shift_right_arithmetic(e_id, jnp.int32(EXP_PER_GROUP.bit_length() - 1)).astype(F32)
    e_idf = e_id.astype(F32)
    is_e = (e_id >= 0) & (e_id < N_EXPERTS) & (e_grp == grp)
    lge = jnp.where(is_e, lg, neg)
    m1 = jnp.max(lge, axis=1, keepdims=True)
    i1 = jnp.min(jnp.where(is_e & (lge == m1), e_idf, big), axis=1, keepdims=True)
    is_e2 = is_e & (e_idf != i1)
    lge2 = jnp.where(is_e2, lg, neg)
    m2 = jnp.max(lge2, axis=1, keepdims=True)
    i2 = jnp.min(jnp.where(is_e2 & (lge2 == m2), e_idf, big), axis=1, keepdims=True)
    ex2 = jnp.exp(m2 - m1)
    den = 1.0 + ex2
    g1 = p_g / den
    g2 = p_g * ex2 / den
    r = jnp.where(lane == 0, i1, 0.0)
    r = jnp.where(lane == 1, i2, r)
    r = jnp.where(lane == 2, g1, r)
    r = jnp.where(lane == 3, g2, r)
    r_ref[...] = r


def _router(x1, g, w_r, b_r, tm):
    n, d = x1.shape
    return pl.pallas_call(
        _router_kernel,
        grid=(n // tm,),
        in_specs=[
            pl.BlockSpec((tm, d), lambda i: (i, 0)),
            pl.BlockSpec((1, d), lambda i: (0, 0)),
            pl.BlockSpec((d, LANES), lambda i: (0, 0)),
            pl.BlockSpec((1, LANES), lambda i: (0, 0)),
        ],
        out_specs=[pl.BlockSpec((tm, d), lambda i: (i, 0)),
                   pl.BlockSpec((tm, LANES), lambda i: (i, 0))],
        out_shape=[jax.ShapeDtypeStruct((n, d), F32),
                   jax.ShapeDtypeStruct((n, LANES), F32)],
        compiler_params=_cparams(("parallel",), 40),
        name="router",
    )(x1, g.reshape(1, d), w_r, b_r)


def _moe_kernel(te_ref, src_ref, dst_ref, nt_ref, h_hbm, wg_ref, wu_ref, wd_ref, y_hbm,
                xbuf, ybuf, gsem, ssem, *, tm):
    m = pl.program_id(0)

    @pl.when(m < nt_ref[0])
    def _():
        base = m * tm

        def gather_row(r, carry):
            pltpu.make_async_copy(h_hbm.at[pl.ds(src_ref[base + r], 1), :],
                                  xbuf.at[pl.ds(r, 1), :], gsem).start()
            return carry

        lax.fori_loop(0, tm, gather_row, 0)
        pltpu.make_async_copy(h_hbm.at[pl.ds(0, tm), :], xbuf, gsem).wait()

        xb = xbuf[...].astype(BF16)
        a = jnp.dot(xb, wg_ref[0], preferred_element_type=F32)
        u = jnp.dot(xb, wu_ref[0], preferred_element_type=F32)
        hmid = (a * _sigmoid(a) * u).astype(BF16)
        ybuf[...] = jnp.dot(hmid, wd_ref[0], preferred_element_type=F32)

        def scatter_row(r, carry):
            pltpu.make_async_copy(ybuf.at[pl.ds(r, 1), :],
                                  y_hbm.at[pl.ds(dst_ref[base + r], 1), :], ssem).start()
            return carry

        lax.fori_loop(0, tm, scatter_row, 0)
        pltpu.make_async_copy(ybuf, y_hbm.at[pl.ds(0, tm), :], ssem).wait()

    @pl.when(m >= nt_ref[0])
    def _():
        ybuf[...] = jnp.zeros(ybuf.shape, F32)
        row0 = pl.multiple_of(dst_ref[m * tm], tm)
        cp = pltpu.make_async_copy(ybuf, y_hbm.at[pl.ds(row0, tm), :], ssem)
        cp.start()
        cp.wait()


def _moe(h2, tile_expert, row_src, row_dst, n_tiles, w_gate, w_up, w_down, tm, n_rows_pad):
    n, d = h2.shape
    de = w_gate.shape[2]
    max_tiles = n_rows_pad // tm
    grid_spec = pltpu.PrefetchScalarGridSpec(
        num_scalar_prefetch=4,
        grid=(max_tiles,),
        in_specs=[
            pl.BlockSpec(memory_space=pl.ANY),
            pl.BlockSpec((1, d, de), lambda m, te, *_: (te[m], 0, 0)),
            pl.BlockSpec((1, d, de), lambda m, te, *_: (te[m], 0, 0)),
            pl.BlockSpec((1, de, d), lambda m, te, *_: (te[m], 0, 0)),
        ],
        out_specs=pl.BlockSpec(memory_space=pl.ANY),
        scratch_shapes=[
            pltpu.VMEM((tm, d), F32),
            pltpu.VMEM((tm, d), F32),
            pltpu.SemaphoreType.DMA(()),
            pltpu.SemaphoreType.DMA(()),
        ],
    )
    return pl.pallas_call(
        functools.partial(_moe_kernel, tm=tm),
        grid_spec=grid_spec,
        out_shape=jax.ShapeDtypeStruct((n_rows_pad, d), F32),
        compiler_params=_cparams(("arbitrary",), 48),
        name="moe",
    )(tile_expert, row_src, row_dst, n_tiles, h2, w_gate, w_up, w_down)


def _moe_plan(eid, n_tok, tm, n_rows_pad):
    na = eid.shape[0]
    order = jnp.argsort(eid).astype(I32)
    sizes = jnp.bincount(eid, length=N_EXPERTS).astype(I32)
    starts = jnp.cumsum(sizes) - sizes
    psizes = ((sizes + tm - 1) // tm) * tm
    pends = jnp.cumsum(psizes)
    pstarts = pends - psizes
    e_sorted = eid[order]
    j = jnp.arange(na, dtype=I32)
    prow = pstarts[e_sorted] + (j - starts[e_sorted])
    is_real = jnp.zeros((n_rows_pad,), bool).at[prow].set(True)
    spare = na + jnp.cumsum(~is_real) - 1
    row_src = jnp.zeros((n_rows_pad,), I32).at[prow].set(order // 2)
    dst = (order % 2) * n_tok + order // 2
    row_dst = jnp.where(is_real, jnp.zeros((n_rows_pad,), I32).at[prow].set(dst), spare).astype(I32)
    n_tiles = (pends[-1] // tm).astype(I32).reshape(1)
    tile_start = jnp.arange(n_rows_pad // tm, dtype=I32) * tm
    tile_expert = jnp.minimum(jnp.searchsorted(pends, tile_start, side="right"), N_EXPERTS - 1).astype(I32)
    last_e = tile_expert[jnp.maximum(n_tiles[0] - 1, 0)]
    tile_expert = jnp.where(tile_start < pends[-1], tile_expert, last_e)
    return tile_expert, row_src, row_dst, n_tiles


def _final_kernel(x_ref, y0_ref, y1_ref, r_ref, g_ref, o_ref):
    r = r_ref[...]
    g1 = r[:, 2:3]
    g2 = r[:, 3:4]
    x = x_ref[...] + g1 * y0_ref[...] + g2 * y1_ref[...]
    ms = jnp.mean(x * x, axis=-1, keepdims=True)
    o_ref[...] = x * lax.rsqrt(ms + EPS) * g_ref[...]


def _final(x1, y, r, g, tm):
    n, d = x1.shape
    nb = n // tm
    return pl.pallas_call(
        _final_kernel,
        grid=(nb,),
        in_specs=[
            pl.BlockSpec((tm, d), lambda i: (i, 0)),
            pl.BlockSpec((tm, d), lambda i: (i, 0)),
            pl.BlockSpec((tm, d), lambda i: (nb + i, 0)),
            pl.BlockSpec((tm, LANES), lambda i: (i, 0)),
            pl.BlockSpec((1, d), lambda i: (0, 0)),
        ],
        out_specs=pl.BlockSpec((tm, d), lambda i: (i, 0)),
        out_shape=jax.ShapeDtypeStruct((n, d), F32),
        compiler_params=_cparams(("parallel",), 40),
        name="final",
    )(x1, y, y, r, g.reshape(1, d))


def _pack_w_in(w):
    d = w.shape[0]
    head = w[:, :896]
    widx = w[:, 896:912]
    rest = w[:, 912:]
    pad = jnp.zeros((d, LANES - IDX_HEADS), w.dtype)
    return jnp.concatenate([head, widx, pad, rest], axis=1).astype(BF16)


def _rope_tables(seq):
    half = HEAD_DIM // 2
    inv = 1.0 / (ROPE_THETA ** (jnp.arange(0, HEAD_DIM, 2, dtype=F32) / HEAD_DIM))
    ang = jnp.arange(seq, dtype=F32)[:, None] * inv[None, :]
    c, s = jnp.cos(ang), jnp.sin(ang)
    del half
    return jnp.concatenate([c, c], axis=1), jnp.concatenate([-s, s], axis=1)


def _layer(x2, batch, seq, cos2, sin2, lb, g_norm1, w_in, g_cq, w_uq, w_qidx, g_hnorm, w_pa, w_pb, w_o,
           g_norm2, w_grp, b_grp, w_exp, b_exp, w_gate, w_up, w_down, g_out):
    n, d = x2.shape
    p = _in_proj(x2, g_norm1, _pack_w_in(w_in), tm=min(512, n), tn=1024)
    tq, tk = 128, min(256, seq)
    qt, qit, wt, k, ki, vt = _dsa_prep(p, cos2, sin2, g_cq, w_uq.astype(BF16), w_qidx.astype(BF16), seq, tm=tk, tq=tq)
    y_att = _dsa(qit, qt, wt, ki, k, vt, batch, seq, tq=tq, tk=tk)
    y_rec = _hgrn(p, lb, g_hnorm, batch, seq, chunk=64)
    x1 = _merge(y_att, y_rec, p, x2, w_pa.astype(BF16), w_pb.astype(BF16), w_o.astype(BF16),
                tm=min(512, n), tc=512)
    w_r = jnp.concatenate([w_grp, w_exp, jnp.zeros((d, LANES - N_GROUPS - N_EXPERTS), F32)], axis=1).astype(BF16)
    b_r = jnp.concatenate([b_grp, b_exp, jnp.zeros((LANES - N_GROUPS - N_EXPERTS,), F32)]).reshape(1, LANES)
    h2, r = _router(x1, g_norm2, w_r, b_r, tm=min(512, n))
    eid = r[:, :2].astype(I32).reshape(-1)
    tm_moe = 256
    assert (2 * n) % tm_moe == 0
    n_rows_pad = 2 * n + N_EXPERTS * tm_moe
    te, row_src, row_dst, n_tiles = _moe_plan(eid, n, tm_moe, n_rows_pad)
    y = _moe(h2, te, row_src, row_dst, n_tiles, w_gate.astype(BF16), w_up.astype(BF16), w_down.astype(BF16),
             tm_moe, n_rows_pad)
    return _final(x1, y, r, g_out, tm=min(512, n))


def kernel(x, g_norm1, w_in, g_cq, w_uq, w_qidx, lb_logits, g_hnorm, w_pa, w_pb, w_o, g_norm2, w_grp, b_grp,
           w_exp, b_exp, w_gate, w_up, w_down, g_final):
    batch, seq, d = x.shape
    depth = g_norm1.shape[0]
    cos2, sin2 = _rope_tables(seq)
    lb_all = jnp.cumsum(jax.nn.softmax(lb_logits.astype(F32), axis=0), axis=0)
    x2 = x.reshape(batch * seq, d)
    for l in range(depth):
        assert depth == 1
        x2 = _layer(x2, batch, seq, cos2, sin2, lb_all[l], g_norm1[l], w_in[l], g_cq[l], w_uq[l], w_qidx[l],
                    g_hnorm[l], w_pa[l], w_pb[l], w_o[l], g_norm2[l], w_grp[l], b_grp[l], w_exp[l], b_exp[l],
                    w_gate[l], w_up[l], w_down[l], g_final)
    return x2.reshape(batch, seq, d)
```

```python
import functools
import math

import jax
import jax.numpy as jnp
from jax import lax
from jax.experimental import pallas as pl
from jax.experimental.pallas import tpu as pltpu

F32 = jnp.float32
BF16 = jnp.bfloat16
I32 = jnp.int32

EPS = 1e-6
ROPE_THETA = 10000.0
LANES = 128

A_HEADS = 8
HEAD_DIM = 128
IDX_HEADS = 16
Q_RANK = 512
TOPK_MAX = 256
G_HEADS = 8
N_GROUPS = 4
EXP_PER_GROUP = 8
N_EXPERTS = N_GROUPS * EXP_PER_GROUP
D_EXPERT = 512

INT_MIN = -2147483648
EXP_CLAMP = 80.0


def _cparams(sem, vmem_mb):
    return pltpu.CompilerParams(dimension_semantics=sem, vmem_limit_bytes=vmem_mb << 20)


def _sigmoid(x):
    return 1.0 / (1.0 + jnp.exp(-x))


def _inproj_kernel(x_ref, g_ref, w_ref, o_ref, xn_ref):
    @pl.when(pl.program_id(1) == 0)
    def _():
        x = x_ref[...]
        ms = jnp.mean(x * x, axis=-1, keepdims=True)
        xn_ref[...] = (x * lax.rsqrt(ms + EPS) * g_ref[...]).astype(BF16)

    o_ref[...] = jnp.dot(xn_ref[...], w_ref[...], preferred_element_type=F32)


def _in_proj(x2, g, w, tm, tn):
    n, d = x2.shape
    dout = w.shape[1]
    return pl.pallas_call(
        _inproj_kernel,
        grid=(n // tm, dout // tn),
        in_specs=[
            pl.BlockSpec((tm, d), lambda i, j: (i, 0)),
            pl.BlockSpec((1, d), lambda i, j: (0, 0)),
            pl.BlockSpec((d, tn), lambda i, j: (0, j)),
        ],
        out_specs=pl.BlockSpec((tm, tn), lambda i, j: (i, j)),
        out_shape=jax.ShapeDtypeStruct((n, dout), F32),
        scratch_shapes=[pltpu.VMEM((tm, d), BF16)],
        compiler_params=_cparams(("parallel", "arbitrary"), 48),
        name="in_proj",
    )(x2, g.reshape(1, d), w)


def _rope(x, c2, s2):
    return x * c2 + pltpu.roll(x, HEAD_DIM // 2, 1) * s2


def _prep_kernel(cq_ref, ka_ref, va_ref, kx_ref, wi_ref, cos_ref, sin_ref, g_ref, wuq_ref, wqi_ref,
                 qt_ref, qit_ref, wt_ref, k_ref, ki_ref, vt_ref, *, scale, wscale, tq):
    nqb = cq_ref.shape[0] // tq
    c2 = cos_ref[...]
    s2 = sin_ref[...]
    cq = cq_ref[...]
    ms = jnp.mean(cq * cq, axis=-1, keepdims=True)
    cqn = (cq * lax.rsqrt(ms + EPS) * g_ref[...]).astype(BF16)
    q = jnp.dot(cqn, wuq_ref[...], preferred_element_type=F32)
    for h in range(A_HEADS):
        sl = slice(h * HEAD_DIM, (h + 1) * HEAD_DIM)
        qh_t = (_rope(q[:, sl], c2, s2) * scale).T
        for b in range(nqb):
            qt_ref[b, :, h * tq:(h + 1) * tq] = qh_t[:, b * tq:(b + 1) * tq].astype(BF16)
    qi = jnp.dot(cqn, wqi_ref[...], preferred_element_type=F32)
    for h in range(IDX_HEADS):
        sl = slice(h * HEAD_DIM, (h + 1) * HEAD_DIM)
        qh_t = _rope(qi[:, sl], c2, s2).T
        for b in range(nqb):
            qit_ref[b, :, h * tq:(h + 1) * tq] = qh_t[:, b * tq:(b + 1) * tq].astype(BF16)
    w_t = (wi_ref[...] * wscale).T
    for h in range(IDX_HEADS):
        for b in range(nqb):
            wt_ref[b, :, h * tq:(h + 1) * tq] = w_t[h:h + 1, b * tq:(b + 1) * tq]
    k_ref[...] = _rope(ka_ref[...], c2, s2).astype(BF16)
    ki_ref[...] = _rope(kx_ref[...], c2, s2).astype(BF16)
    vt_ref[0] = va_ref[...].T.astype(BF16)


def _dsa_prep(p, cos2, sin2, g_cq, w_uq, w_qidx, seq, tm, tq):
    n = p.shape[0]
    nsb = seq // tm
    nqb = tm // tq
    aw = A_HEADS * HEAD_DIM
    iw = IDX_HEADS * HEAD_DIM
    row = lambda i: (i, 0)
    blk3 = lambda i: (i, 0, 0)
    kern = functools.partial(_prep_kernel, scale=HEAD_DIM ** -0.5 * math.log2(math.e),
                             wscale=IDX_HEADS ** -0.5 * HEAD_DIM ** -0.5, tq=tq)
    return pl.pallas_call(
        kern,
        grid=(n // tm,),
        in_specs=[
            pl.BlockSpec((tm, Q_RANK), lambda i: (i, 0)),
            pl.BlockSpec((tm, LANES), lambda i: (i, 4)),
            pl.BlockSpec((tm, LANES), lambda i: (i, 5)),
            pl.BlockSpec((tm, LANES), lambda i: (i, 6)),
            pl.BlockSpec((tm, LANES), lambda i: (i, 7)),
            pl.BlockSpec((tm, LANES), lambda i: (i % nsb, 0)),
            pl.BlockSpec((tm, LANES), lambda i: (i % nsb, 0)),
            pl.BlockSpec((1, Q_RANK), lambda i: (0, 0)),
            pl.BlockSpec((Q_RANK, aw), lambda i: (0, 0)),
            pl.BlockSpec((Q_RANK, iw), lambda i: (0, 0)),
        ],
        out_specs=[
            pl.BlockSpec((nqb, HEAD_DIM, A_HEADS * tq), blk3),
            pl.BlockSpec((nqb, HEAD_DIM, IDX_HEADS * tq), blk3),
            pl.BlockSpec((nqb, 1, IDX_HEADS * tq), blk3),
            pl.BlockSpec((tm, LANES), row),
            pl.BlockSpec((tm, LANES), row),
            pl.BlockSpec((1, HEAD_DIM, tm), blk3),
        ],
        out_shape=[
            jax.ShapeDtypeStruct((n // tq, HEAD_DIM, A_HEADS * tq), BF16),
            jax.ShapeDtypeStruct((n // tq, HEAD_DIM, IDX_HEADS * tq), BF16),
            jax.ShapeDtypeStruct((n // tq, 1, IDX_HEADS * tq), F32),
            jax.ShapeDtypeStruct((n, LANES), BF16),
            jax.ShapeDtypeStruct((n, LANES), BF16),
            jax.ShapeDtypeStruct((n // tm, HEAD_DIM, tm), BF16),
        ],
        compiler_params=_cparams(("parallel",), 40),
        name="dsa_prep",
    )(p, p, p, p, p, cos2, sin2, g_cq.reshape(1, Q_RANK), w_uq, w_qidx)


def _dot_nt(a, b):
    return lax.dot_general(a, b, (((1,), (1,)), ((), ())), preferred_element_type=F32)


def _dsa_kernel(qit_ref, qt_ref, wt_ref, ki_ref, k_ref, vt_ref, o_ref,
                key_ref, s_ref, acc_ref, *, tq, tk, topk, idx_bits):
    i = pl.program_id(1)
    nkc = ((i + 1) * tq + tk - 1) // tk
    q_pos = i * tq + lax.broadcasted_iota(I32, (tk, tq), 1)
    k_off = lax.broadcasted_iota(I32, (tk, tq), 0)

    def score_chunk(j, carry):
        kic = ki_ref[pl.ds(pl.multiple_of(j * tk, tk), tk), :]
        acc = jnp.zeros((tk, tq), F32)
        for h2 in range(IDX_HEADS // 2):
            sl = slice(2 * h2 * tq, (2 * h2 + 2) * tq)
            lg = jnp.dot(kic, qit_ref[0, :, sl], preferred_element_type=F32)
            x = jnp.maximum(lg, 0.0) * wt_ref[0, :, sl]
            acc = acc + x[:, :tq] + x[:, tq:]
        bits = lax.bitcast_convert_type(acc, I32)
        key = jnp.where(bits < 0, bits ^ jnp.int32(0x7FFFFFFF), bits)
        causal = (j * tk + k_off) <= q_pos
        key_ref[j] = jnp.where(causal, key, jnp.int32(INT_MIN))
        return carry

    lax.fori_loop(0, nkc, score_chunk, 0)

    def count_rows(pred):
        def cnt_chunk(j, part):
            hit = pred(j, key_ref[j]).astype(I32)
            return part + jnp.sum(hit.reshape(tk // 8, 8, tq), axis=0)

        part = lax.fori_loop(0, nkc, cnt_chunk, jnp.zeros((8, tq), I32))
        return jnp.sum(part, axis=0, keepdims=True)

    def bit_step(b, c):
        t_u, cnt_t = c
        bit = jnp.left_shift(jnp.int32(1), 31 - b)
        cand_s = (t_u | bit) ^ jnp.int32(INT_MIN)
        cnt = count_rows(lambda j, key: key >= cand_s)
        take = cnt >= topk
        return jnp.where(take, t_u | bit, t_u), jnp.where(take, cnt, cnt_t)

    n_causal = i * tq + lax.broadcasted_iota(I32, (1, tq), 1) + 1
    t_u, cnt_t = lax.fori_loop(0, 32, bit_step, (jnp.zeros((1, tq), I32), n_causal))
    thr = jnp.maximum(t_u ^ jnp.int32(INT_MIN), jnp.int32(INT_MIN + 1))

    @pl.when(jnp.max(cnt_t) > topk)
    def _():
        need = topk - count_rows(lambda j, key: key > thr)

        def idx_step(b, cut):
            cand = cut | jnp.left_shift(jnp.int32(1), idx_bits - 1 - b)
            cnt = count_rows(lambda j, key: (key == thr) & ((j * tk + k_off) < cand))
            return jnp.where(cnt <= need, cand, cut)

        cut = lax.fori_loop(0, idx_bits, idx_step, jnp.zeros((1, tq), I32))

        def demote(j, carry):
            key = key_ref[j]
            drop = (key == thr) & ((j * tk + k_off) >= cut)
            key_ref[j] = jnp.where(drop, key - 1, key)
            return carry

        lax.fori_loop(0, nkc, demote, 0)

    acc_ref[...] = jnp.zeros(acc_ref.shape, F32)

    def att_chunk(j, carry):
        m_old, l_old = carry
        kc = k_ref[pl.ds(pl.multiple_of(j * tk, tk), tk), :]
        sel = key_ref[j] >= thr
        cmax = []
        for h in range(A_HEADS):
            sl = slice(h * tq, (h + 1) * tq)
            s_h = jnp.dot(kc, qt_ref[0, :, sl], preferred_element_type=F32)
            s_h = jnp.where(sel, s_h, -jnp.inf)
            s_ref[:, sl] = s_h
            cmax.append(jnp.max(s_h, axis=0, keepdims=True))
        m_new = jnp.maximum(m_old, jnp.concatenate(cmax, axis=1))
        alpha = jnp.exp2(m_old - m_new)
        p = jnp.exp2(s_ref[...] - m_new)
        l_new = alpha * l_old + jnp.sum(p, axis=0, keepdims=True)
        pv = jnp.dot(vt_ref[j], p.astype(BF16), preferred_element_type=F32)
        acc_ref[...] = alpha * acc_ref[...] + pv
        return m_new, l_new

    stat0 = (jnp.full((1, A_HEADS * tq), -1e30, F32), jnp.zeros((1, A_HEADS * tq), F32))
    _, l_fin = lax.fori_loop(0, nkc, att_chunk, stat0)
    out_t = acc_ref[...] / l_fin
    for h in range(A_HEADS):
        o_ref[:, h * HEAD_DIM:(h + 1) * HEAD_DIM] = out_t[:, h * tq:(h + 1) * tq].T.astype(o_ref.dtype)


def _dsa(qit, qt, wt, ki, k, vt, batch, seq, tq, tk):
    n = batch * seq
    nqb = seq // tq
    nkb = seq // tk
    topk = min(TOPK_MAX, seq // 4)
    aw = A_HEADS * HEAD_DIM
    kern = functools.partial(_dsa_kernel, tq=tq, tk=tk, topk=topk, idx_bits=seq.bit_length())
    qblk = lambda b, i: (b * nqb + i, 0, 0)
    return pl.pallas_call(
        kern,
        grid=(batch, nqb),
        in_specs=[
            pl.BlockSpec((1, HEAD_DIM, IDX_HEADS * tq), qblk),
            pl.BlockSpec((1, HEAD_DIM, A_HEADS * tq), qblk),
            pl.BlockSpec((1, 1, IDX_HEADS * tq), qblk),
            pl.BlockSpec((seq, LANES), lambda b, i: (b, 0)),
            pl.BlockSpec((seq, LANES), lambda b, i: (b, 0)),
            pl.BlockSpec((nkb, HEAD_DIM, tk), lambda b, i: (b, 0, 0)),
        ],
        out_specs=pl.BlockSpec((tq, aw), lambda b, i: (b * nqb + i, 0)),
        out_shape=jax.ShapeDtypeStruct((n, aw), BF16),
        scratch_shapes=[
            pltpu.VMEM((nkb, tk, tq), I32),
            pltpu.VMEM((tk, A_HEADS * tq), F32),
            pltpu.VMEM((HEAD_DIM, A_HEADS * tq), F32),
        ],
        compiler_params=_cparams(("parallel", "arbitrary"), 48),
        name="dsa",
    )(qit, qt, wt, ki, k, vt)


HG_SUB = 16


def _hgrn_kernel(hq_ref, hf_ref, hi_ref, hg_ref, lb_ref, gn_ref, o_ref, st_ref, *, chunk):
    nsub = chunk // HG_SUB
    gw = hq_ref.shape[1]

    @pl.when(pl.program_id(1) == 0)
    def _():
        st_ref[...] = jnp.zeros(st_ref.shape, F32)

    lb = lb_ref[...]
    f = lb + (1.0 - lb) * _sigmoid(hf_ref[...])
    g = jnp.log(f)
    kk = 1.0 - f
    hq = hq_ref[...]
    q = hq * _sigmoid(hq)
    v = hi_ref[...]

    row = lax.broadcasted_iota(I32, (chunk, gw), 0)
    rsub = row & (HG_SUB - 1)
    sub = row // HG_SUB
    lc = g
    sh = 1
    while sh < HG_SUB:
        lc = lc + jnp.where(rsub >= sh, pltpu.roll(lc, sh, 0), 0.0)
        sh *= 2
    tots = [lc[(a + 1) * HG_SUB - 1:(a + 1) * HG_SUB, :] for a in range(nsub)]
    ends = []
    run = jnp.zeros((1, gw), F32)
    for a in range(nsub):
        run = run + tots[a]
        ends.append(run)
    cum_l = ends[-1]
    start_full = jnp.zeros((chunk, gw), F32)
    tot_full = jnp.zeros((chunk, gw), F32)
    end_full = jnp.zeros((chunk, gw), F32)
    for a in range(nsub):
        ina = sub == a
        if a > 0:
            start_full = jnp.where(ina, ends[a - 1], start_full)
        tot_full = jnp.where(ina, tots[a], tot_full)
        end_full = jnp.where(ina, ends[a], end_full)
    cum = lc + start_full

    qd = (q * jnp.exp(lc)).astype(BF16)
    kd = (kk * jnp.exp(jnp.minimum(-lc, EXP_CLAMP))).astype(BF16)
    qs = (q * jnp.exp(cum)).astype(BF16)
    koff = kk * jnp.exp(tot_full - lc)
    ke = (koff * jnp.exp(cum_l - end_full)).astype(BF16)
    koff_b = koff.astype(BF16)
    zero_b = jnp.zeros((chunk, gw), BF16)
    qoff = []
    kmask = []
    for b in range(nsub - 1):
        qb = q * jnp.exp(jnp.minimum(cum - ends[b], 0.0))
        qoff.append(jnp.where(sub > b, qb, 0.0).astype(BF16))
        kmask.append(jnp.where(sub == b, koff_b, zero_b))
    vb = v.astype(BF16)
    decay_l = jnp.exp(cum_l)

    ti = lax.broadcasted_iota(I32, (chunk, chunk), 0)
    si = lax.broadcasted_iota(I32, (chunk, chunk), 1)
    diag_ok = (ti // HG_SUB == si // HG_SUB) & (si <= ti)

    outs = []
    for h in range(G_HEADS):
        sl = slice(h * HEAD_DIM, (h + 1) * HEAD_DIM)
        a_diag = jnp.where(diag_ok, _dot_nt(qd[:, sl], kd[:, sl]), 0.0)
        qcat = jnp.concatenate([qo[:, sl] for qo in qoff], axis=1)
        kcat = jnp.concatenate([km[:, sl] for km in kmask], axis=1)
        a = (a_diag + _dot_nt(qcat, kcat)).astype(BF16)
        st = st_ref[h]
        o_h = jnp.dot(a, vb[:, sl], preferred_element_type=F32) + _dot_nt(qs[:, sl], st.astype(BF16))
        outs.append(o_h)
        upd = jnp.dot(v[:, sl].T.astype(BF16), ke[:, sl], preferred_element_type=F32)
        st_ref[h] = st * decay_l[:, sl] + upd
    o = jnp.concatenate(outs, axis=1)
    ms = jnp.mean(o * o, axis=-1, keepdims=True)
    hg = hg_ref[...]
    y = o * lax.rsqrt(ms + EPS) * gn_ref[...] * (hg * _sigmoid(hg))
    o_ref[...] = y.astype(o_ref.dtype)


def _hgrn(p, lb, g_hnorm, batch, seq, chunk):
    n = batch * seq
    gw = G_HEADS * HEAD_DIM
    ncb = seq // chunk
    blk = lambda c: pl.BlockSpec((chunk, gw), lambda b, i: (b * ncb + i, c))
    return pl.pallas_call(
        functools.partial(_hgrn_kernel, chunk=chunk),
        grid=(batch, ncb),
        in_specs=[blk(1), blk(2), blk(3), blk(4),
                  pl.BlockSpec((1, gw), lambda b, i: (0, 0)),
                  pl.BlockSpec((1, gw), lambda b, i: (0, 0))],
        out_specs=pl.BlockSpec((chunk, gw), lambda b, i: (b * ncb + i, 0)),
        out_shape=jax.ShapeDtypeStruct((n, gw), BF16),
        scratch_shapes=[pltpu.VMEM((G_HEADS, HEAD_DIM, HEAD_DIM), F32)],
        compiler_params=_cparams(("parallel", "arbitrary"), 32),
        name="hgrn",
    )(p, p, p, p, lb.reshape(1, gw), g_hnorm.reshape(1, gw))


def _merge_kernel(ya_ref, yr_ref, ga_ref, gb_ref, x_ref, wpa_ref, wpb_ref, wo_ref, o_ref, acc_ref):
    c = pl.program_id(1)

    @pl.when(c == 0)
    def _():
        acc_ref[...] = x_ref[...]

    ma = jnp.dot(ya_ref[...], wpa_ref[...], preferred_element_type=F32)
    mb = jnp.dot(yr_ref[...], wpb_ref[...], preferred_element_type=F32)
    merged = _sigmoid(ga_ref[...]) * ma + _sigmoid(gb_ref[...]) * mb
    acc_ref[...] += jnp.dot(merged.astype(BF16), wo_ref[...], preferred_element_type=F32)

    @pl.when(c == pl.num_programs(1) - 1)
    def _():
        o_ref[...] = acc_ref[...]


def _merge(y_att, y_rec, p, x2, w_pa, w_pb, w_o, tm, tc):
    n, d = x2.shape
    aw = y_att.shape[1]
    gw = y_rec.shape[1]
    ga0 = 5120 // tc
    gb0 = 7168 // tc
    return pl.pallas_call(
        _merge_kernel,
        grid=(n // tm, d // tc),
        in_specs=[
            pl.BlockSpec((tm, aw), lambda i, c: (i, 0)),
            pl.BlockSpec((tm, gw), lambda i, c: (i, 0)),
            pl.BlockSpec((tm, tc), lambda i, c: (i, ga0 + c)),
            pl.BlockSpec((tm, tc), lambda i, c: (i, gb0 + c)),
            pl.BlockSpec((tm, d), lambda i, c: (i, 0)),
            pl.BlockSpec((aw, tc), lambda i, c: (0, c)),
            pl.BlockSpec((gw, tc), lambda i, c: (0, c)),
            pl.BlockSpec((tc, d), lambda i, c: (c, 0)),
        ],
        out_specs=pl.BlockSpec((tm, d), lambda i, c: (i, 0)),
        out_shape=jax.ShapeDtypeStruct((n, d), F32),
        scratch_shapes=[pltpu.VMEM((tm, d), F32)],
        compiler_params=_cparams(("parallel", "arbitrary"), 48),
        name="merge",
    )(y_att, y_rec, p, p, x2, w_pa, w_pb, w_o)


def _router_kernel(x_ref, g_ref, wr_ref, br_ref, h_ref, r_ref, cnt_ref):
    x = x_ref[...]
    ms = jnp.mean(x * x, axis=-1, keepdims=True)
    h = x * lax.rsqrt(ms + EPS) * g_ref[...]
    h_ref[...] = h
    lg = jnp.dot(h.astype(BF16), wr_ref[...], preferred_element_type=F32) + br_ref[...]
    tm = lg.shape[0]
    lane = lax.broadcasted_iota(I32, (tm, LANES), 1)
    neg = jnp.float32(-1e30)
    big = jnp.float32(LANES)
    is_g = lane < N_GROUPS
    lgg = jnp.where(is_g, lg, neg)
    mg = jnp.max(lgg, axis=1, keepdims=True)
    zg = jnp.sum(jnp.where(is_g, jnp.exp(lgg - mg), 0.0), axis=1, keepdims=True)
    p_g = 1.0 / zg
    grp = jnp.min(jnp.where(is_g & (lgg == mg), lane.astype(F32), big), axis=1, keepdims=True)
    e_id = lane - N_GROUPS
    e_grp = lax.shift_right_arithmetic(e_id, jnp.int32(EXP_PER_GROUP.bit_length() - 1)).astype(F32)
    e_idf = e_id.astype(F32)
    is_e = (e_id >= 0) & (e_id < N_EXPERTS) & (e_grp == grp)
    lge = jnp.where(is_e, lg, neg)
    m1 = jnp.max(lge, axis=1, keepdims=True)
    i1 = jnp.min(jnp.where(is_e & (lge == m1), e_idf, big), axis=1, keepdims=True)
    is_e2 = is_e & (e_idf != i1)
    lge2 = jnp.where(is_e2, lg, neg)
    m2 = jnp.max(lge2, axis=1, keepdims=True)
    i2 = jnp.min(jnp.where(is_e2 & (lge2 == m2), e_idf, big), axis=1, keepdims=True)
    ex2 = jnp.exp(m2 - m1)
    den = 1.0 + ex2
    g1 = p_g / den
    g2 = p_g * ex2 / den
    @pl.when(pl.program_id(0) == 0)
    def _():
        cnt_ref[...] = jnp.zeros(cnt_ref.shape, F32)

    lanef = lane.astype(F32)
    oh1 = lanef == i1
    oh2 = lanef == i2
    oh = jnp.where(oh1 | oh2, 1.0, 0.0)
    ti = lax.broadcasted_iota(I32, (tm, tm), 0)
    si = lax.broadcasted_iota(I32, (tm, tm), 1)
    tri = jnp.where(si < ti, 1.0, 0.0).astype(BF16)
    before = jnp.dot(tri, oh.astype(BF16), preferred_element_type=F32) + cnt_ref[...]
    rank1 = jnp.sum(jnp.where(oh1, before, 0.0), axis=1, keepdims=True)
    rank2 = jnp.sum(jnp.where(oh2, before, 0.0), axis=1, keepdims=True)
    cnt_ref[...] += jnp.sum(oh, axis=0, keepdims=True)

    r = jnp.where(lane == 0, i1, 0.0)
    r = jnp.where(lane == 1, i2, r)
    r = jnp.where(lane == 2, g1, r)
    r = jnp.where(lane == 3, g2, r)
    r = jnp.where(lane == 4, rank1, r)
    r = jnp.where(lane == 5, rank2, r)
    r_ref[...] = r


def _router(x1, g, w_r, b_r, tm):
    n, d = x1.shape
    return pl.pallas_call(
        _router_kernel,
        grid=(n // tm,),
        in_specs=[
            pl.BlockSpec((tm, d), lambda i: (i, 0)),
            pl.BlockSpec((1, d), lambda i: (0, 0)),
            pl.BlockSpec((d, LANES), lambda i: (0, 0)),
            pl.BlockSpec((1, LANES), lambda i: (0, 0)),
        ],
        out_specs=[pl.BlockSpec((tm, d), lambda i: (i, 0)),
                   pl.BlockSpec((tm, LANES), lambda i: (i, 0)),
                   pl.BlockSpec((1, LANES), lambda i: (0, 0))],
        out_shape=[jax.ShapeDtypeStruct((n, d), F32),
                   jax.ShapeDtypeStruct((n, LANES), F32),
                   jax.ShapeDtypeStruct((1, LANES), F32)],
        compiler_params=_cparams(("arbitrary",), 40),
        name="router",
    )(x1, g.reshape(1, d), w_r, b_r)


def _dispatch_kernel(prow_ref, h_ref, xs_hbm, sem, *, tm):
    base = 2 * tm * pl.program_id(0)

    def scatter_row(r, carry):
        for s in range(2):
            pltpu.make_async_copy(h_ref.at[pl.ds(r, 1), :],
                                  xs_hbm.at[pl.ds(prow_ref[base + 2 * r + s], 1), :], sem).start()
        return carry

    lax.fori_loop(0, tm, scatter_row, 0, unroll=4)
    for _ in range(2):
        pltpu.make_async_copy(h_ref, xs_hbm.at[pl.ds(0, tm), :], sem).wait()


def _dispatch(h2, prow, tm):
    n, d = h2.shape
    grid_spec = pltpu.PrefetchScalarGridSpec(
        num_scalar_prefetch=1,
        grid=(n // tm,),
        in_specs=[pl.BlockSpec((tm, d), lambda i, pr: (i, 0))],
        out_specs=pl.BlockSpec(memory_space=pl.ANY),
        scratch_shapes=[pltpu.SemaphoreType.DMA(())],
    )
    return pl.pallas_call(
        functools.partial(_dispatch_kernel, tm=tm),
        grid_spec=grid_spec,
        out_shape=jax.ShapeDtypeStruct((2 * n, d), F32),
        compiler_params=_cparams(("arbitrary",), 32),
        name="dispatch",
    )(prow, h2)


def _moe_kernel(pt_ref, pe_ref, plo_ref, phi_ref, np_ref, x_ref, wg_ref, wu_ref, wd_ref, y_ref,
                wgb_ref, wub_ref, wdb_ref):
    p = pl.program_id(0)
    active = p < np_ref[0]
    prev = jnp.maximum(p - 1, 0)
    new_expert = (p == 0) | (pe_ref[p] != pe_ref[prev])
    new_tile = (p == 0) | (pt_ref[p] != pt_ref[prev])

    @pl.when(active & new_expert)
    def _():
        wgb_ref[...] = wg_ref[0].astype(BF16)
        wub_ref[...] = wu_ref[0].astype(BF16)
        wdb_ref[...] = wd_ref[0].astype(BF16)

    @pl.when(active)
    def _():
        tm = x_ref.shape[0]
        xb = x_ref[...].astype(BF16)
        a = jnp.dot(xb, wgb_ref[...], preferred_element_type=F32)
        u = jnp.dot(xb, wub_ref[...], preferred_element_type=F32)
        hmid = (a * _sigmoid(a) * u).astype(BF16)
        yv = jnp.dot(hmid, wdb_ref[...], preferred_element_type=F32)
        row = lax.broadcasted_iota(I32, (tm, 1), 0)
        mine = (row >= plo_ref[p]) & (row < phi_ref[p])
        yv = jnp.where(mine, yv, 0.0)

        @pl.when(new_tile)
        def _():
            y_ref[...] = yv

        @pl.when(jnp.logical_not(new_tile))
        def _():
            y_ref[...] += yv


def _moe(xs, pair_tile, pair_expert, pair_lo, pair_hi, n_pairs, w_gate, w_up, w_down, tm):
    na, d = xs.shape
    de = w_gate.shape[2]
    max_pairs = pair_tile.shape[0]
    wmap = lambda p, pt, pe, *_: (pe[p], 0, 0)
    grid_spec = pltpu.PrefetchScalarGridSpec(
        num_scalar_prefetch=5,
        grid=(max_pairs,),
        in_specs=[
            pl.BlockSpec((tm, d), lambda p, pt, *_: (pt[p], 0)),
            pl.BlockSpec((1, d, de), wmap),
            pl.BlockSpec((1, d, de), wmap),
            pl.BlockSpec((1, de, d), wmap),
        ],
        out_specs=pl.BlockSpec((tm, d), lambda p, pt, *_: (pt[p], 0)),
        scratch_shapes=[
            pltpu.VMEM((d, de), BF16),
            pltpu.VMEM((d, de), BF16),
            pltpu.VMEM((de, d), BF16),
        ],
    )
    return pl.pallas_call(
        _moe_kernel,
        grid_spec=grid_spec,
        out_shape=jax.ShapeDtypeStruct((na, d), F32),
        compiler_params=_cparams(("arbitrary",), 56),
        name="moe",
    )(pair_tile, pair_expert, pair_lo, pair_hi, n_pairs, xs, w_gate, w_up, w_down)


def _moe_rows(r, cnt):
    sizes = cnt[0, :N_EXPERTS].astype(I32)
    starts = jnp.cumsum(sizes) - sizes
    e12 = r[:, 0:2].astype(I32)
    rank = r[:, 4:6].astype(I32)
    onehot = e12[:, :, None] == jnp.arange(N_EXPERTS, dtype=I32)[None, None, :]
    start_a = jnp.sum(jnp.where(onehot, starts[None, None, :], 0), axis=-1)
    return (start_a + rank).reshape(-1).astype(I32), sizes


def _moe_pairs(sizes, na, tm):
    ntiles = na // tm
    max_pairs = ntiles + N_EXPERTS - 1
    ends = jnp.cumsum(sizes)
    starts = ends - sizes
    t0 = jnp.arange(ntiles, dtype=I32) * tm
    e_lo = jnp.searchsorted(ends, t0, side="right").astype(I32)
    e_hi = jnp.searchsorted(ends, t0 + (tm - 1), side="right").astype(I32)
    per_tile = e_hi - e_lo + 1
    pend = jnp.cumsum(per_tile)
    pstart = pend - per_tile
    n_pairs = pend[-1]
    pc = jnp.minimum(jnp.arange(max_pairs, dtype=I32), n_pairs - 1)
    tile = jnp.searchsorted(pend, pc, side="right").astype(I32)
    expert = e_lo[tile] + (pc - pstart[tile])
    lo = jnp.clip(starts[expert] - tile * tm, 0, tm).astype(I32)
    hi = jnp.clip(ends[expert] - tile * tm, 0, tm).astype(I32)
    return tile, expert.astype(I32), lo, hi, n_pairs.astype(I32).reshape(1)


def _final_kernel(prow_ref, x_ref, r_ref, g_ref, y_hbm, o_ref, ybuf, sem, *, tm):
    base = 2 * tm * pl.program_id(0)

    def gather_row(t, carry):
        for s in range(2):
            pltpu.make_async_copy(y_hbm.at[pl.ds(prow_ref[base + 2 * t + s], 1), :],
                                  ybuf.at[s, pl.ds(t, 1), :], sem).start()
        return carry

    lax.fori_loop(0, tm, gather_row, 0, unroll=4)
    for s in range(2):
        pltpu.make_async_copy(y_hbm.at[pl.ds(0, tm), :], ybuf.at[s], sem).wait()
    r = r_ref[...]
    g1 = r[:, 2:3]
    g2 = r[:, 3:4]
    x = x_ref[...] + g1 * ybuf[0] + g2 * ybuf[1]
    ms = jnp.mean(x * x, axis=-1, keepdims=True)
    o_ref[...] = x * lax.rsqrt(ms + EPS) * g_ref[...]


def _final(x1, y, r, prow, g, tm):
    n, d = x1.shape
    grid_spec = pltpu.PrefetchScalarGridSpec(
        num_scalar_prefetch=1,
        grid=(n // tm,),
        in_specs=[
            pl.BlockSpec((tm, d), lambda i, pr: (i, 0)),
            pl.BlockSpec((tm, LANES), lambda i, pr: (i, 0)),
            pl.BlockSpec((1, d), lambda i, pr: (0, 0)),
            pl.BlockSpec(memory_space=pl.ANY),
        ],
        out_specs=pl.BlockSpec((tm, d), lambda i, pr: (i, 0)),
        scratch_shapes=[pltpu.VMEM((2, tm, d), F32), pltpu.SemaphoreType.DMA(())],
    )
    return pl.pallas_call(
        functools.partial(_final_kernel, tm=tm),
        grid_spec=grid_spec,
        out_shape=jax.ShapeDtypeStruct((n, d), F32),
        compiler_params=_cparams(("arbitrary",), 40),
        name="final",
    )(prow, x1, r, g.reshape(1, d), y)


def _pack_w_in(w):
    d = w.shape[0]
    head = w[:, :896]
    widx = w[:, 896:912]
    rest = w[:, 912:]
    pad = jnp.zeros((d, LANES - IDX_HEADS), w.dtype)
    return jnp.concatenate([head, widx, pad, rest], axis=1).astype(BF16)


def _rope_tables(seq):
    half = HEAD_DIM // 2
    inv = 1.0 / (ROPE_THETA ** (jnp.arange(0, HEAD_DIM, 2, dtype=F32) / HEAD_DIM))
    ang = jnp.arange(seq, dtype=F32)[:, None] * inv[None, :]
    c, s = jnp.cos(ang), jnp.sin(ang)
    del half
    return jnp.concatenate([c, c], axis=1), jnp.concatenate([-s, s], axis=1)


def _layer(x2, batch, seq, cos2, sin2, lb, g_norm1, w_in, g_cq, w_uq, w_qidx, g_hnorm, w_pa, w_pb, w_o,
           g_norm2, w_grp, b_grp, w_exp, b_exp, w_gate, w_up, w_down, g_out):
    n, d = x2.shape
    p = _in_proj(x2, g_norm1, _pack_w_in(w_in), tm=min(512, n), tn=1024)
    tq, tk = 128, min(256, seq)
    qt, qit, wt, k, ki, vt = _dsa_prep(p, cos2, sin2, g_cq, w_uq.astype(BF16), w_qidx.astype(BF16), seq, tm=tk, tq=tq)
    y_att = _dsa(qit, qt, wt, ki, k, vt, batch, seq, tq=tq, tk=tk)
    y_rec = _hgrn(p, lb, g_hnorm, batch, seq, chunk=64)
    x1 = _merge(y_att, y_rec, p, x2, w_pa.astype(BF16), w_pb.astype(BF16), w_o.astype(BF16),
                tm=min(512, n), tc=512)
    w_r = jnp.concatenate([w_grp, w_exp, jnp.zeros((d, LANES - N_GROUPS - N_EXPERTS), F32)], axis=1).astype(BF16)
    b_r = jnp.concatenate([b_grp, b_exp, jnp.zeros((LANES - N_GROUPS - N_EXPERTS,), F32)]).reshape(1, LANES)
    h2, r, cnt = _router(x1, g_norm2, w_r, b_r, tm=min(512, n))
    prow, sizes = _moe_rows(r, cnt)
    xs = _dispatch(h2, prow, tm=min(512, n))
    tm_moe = 256
    assert (2 * n) % tm_moe == 0
    y = _moe(xs, *_moe_pairs(sizes, 2 * n, tm_moe), w_gate, w_up, w_down, tm_moe)
    return _final(x1, y, r, prow, g_out, tm=min(512, n))


def kernel(x, g_norm1, w_in, g_cq, w_uq, w_qidx, lb_logits, g_hnorm, w_pa, w_pb, w_o, g_norm2, w_grp, b_grp,
           w_exp, b_exp, w_gate, w_up, w_down, g_final):
    batch, seq, d = x.shape
    depth = g_norm1.shape[0]
    cos2, sin2 = _rope_tables(seq)
    lb_all = jnp.cumsum(jax.nn.softmax(lb_logits.astype(F32), axis=0), axis=0)
    x2 = x.reshape(batch * seq, d)
    for l in range(depth):
        assert depth == 1
        x2 = _layer(x2, batch, seq, cos2, sin2, lb_all[l], g_norm1[l], w_in[l], g_cq[l], w_uq[l], w_qidx[l],
                    g_hnorm[l], w_pa[l], w_pb[l], w_o[l], g_norm2[l], w_grp[l], b_grp[l], w_exp[l], b_exp[l],
                    w_gate[l], w_up[l], w_down[l], g_final)
    return x2.reshape(batch, seq, d)
```

```python
import functools
import math

import jax
import jax.numpy as jnp
from jax import lax
from jax.experimental import pallas as pl
from jax.experimental.pallas import tpu as pltpu

F32 = jnp.float32
BF16 = jnp.bfloat16
I32 = jnp.int32

EPS = 1e-6
ROPE_THETA = 10000.0
LANES = 128

A_HEADS = 8
HEAD_DIM = 128
IDX_HEADS = 16
Q_RANK = 512
TOPK_MAX = 256
G_HEADS = 8
N_GROUPS = 4
EXP_PER_GROUP = 8
N_EXPERTS = N_GROUPS * EXP_PER_GROUP
D_EXPERT = 512

INT_MIN = -2147483648
EXP_CLAMP = 80.0


def _cparams(sem, vmem_mb):
    return pltpu.CompilerParams(dimension_semantics=sem, vmem_limit_bytes=vmem_mb << 20)


def _sigmoid(x):
    return 1.0 / (1.0 + jnp.exp(-x))


PH_COLS = 2048
PT_COLS = 7168


def _inproj_kernel(x_ref, g_ref, w_ref, oh_ref, ot_ref, xn_ref, *, nh):
    j = pl.program_id(1)

    @pl.when(j == 0)
    def _():
        x = x_ref[...]
        ms = jnp.mean(x * x, axis=-1, keepdims=True)
        xn_ref[...] = (x * lax.rsqrt(ms + EPS) * g_ref[...]).astype(BF16)

    y = jnp.dot(xn_ref[...], w_ref[...], preferred_element_type=F32)

    @pl.when(j < nh)
    def _():
        oh_ref[...] = y

    @pl.when(j >= nh)
    def _():
        ot_ref[...] = y.astype(ot_ref.dtype)


def _in_proj(x2, g, w, tm, tn):
    n, d = x2.shape
    nh = PH_COLS // tn
    nt = PT_COLS // tn
    return pl.pallas_call(
        functools.partial(_inproj_kernel, nh=nh),
        grid=(n // tm, nh + nt),
        in_specs=[
            pl.BlockSpec((tm, d), lambda i, j: (i, 0)),
            pl.BlockSpec((1, d), lambda i, j: (0, 0)),
            pl.BlockSpec((d, tn), lambda i, j: (0, j)),
        ],
        out_specs=[pl.BlockSpec((tm, tn), lambda i, j: (i, jnp.minimum(j, nh - 1))),
                   pl.BlockSpec((tm, tn), lambda i, j: (i, jnp.maximum(j - nh, 0)))],
        out_shape=[jax.ShapeDtypeStruct((n, PH_COLS), F32),
                   jax.ShapeDtypeStruct((n, PT_COLS), BF16)],
        scratch_shapes=[pltpu.VMEM((tm, d), BF16)],
        compiler_params=_cparams(("parallel", "arbitrary"), 48),
        name="in_proj",
    )(x2, g.reshape(1, d), w)


def _rope(x, c2, s2):
    return x * c2 + pltpu.roll(x, HEAD_DIM // 2, 1) * s2


def _prep_kernel(cq_ref, ka_ref, va_ref, kx_ref, wi_ref, cos_ref, sin_ref, g_ref, wuq_ref, wqi_ref,
                 qt_ref, qit_ref, wt_ref, k_ref, ki_ref, vt_ref, *, scale, wscale, tq):
    nqb = cq_ref.shape[0] // tq
    c2 = cos_ref[...]
    s2 = sin_ref[...]
    cq = cq_ref[...]
    ms = jnp.mean(cq * cq, axis=-1, keepdims=True)
    cqn = (cq * lax.rsqrt(ms + EPS) * g_ref[...]).astype(BF16)
    q = jnp.dot(cqn, wuq_ref[...], preferred_element_type=F32)
    for h in range(A_HEADS):
        sl = slice(h * HEAD_DIM, (h + 1) * HEAD_DIM)
        qh_t = (_rope(q[:, sl], c2, s2) * scale).T
        for b in range(nqb):
            qt_ref[b, :, h * tq:(h + 1) * tq] = qh_t[:, b * tq:(b + 1) * tq].astype(BF16)
    qi = jnp.dot(cqn, wqi_ref[...], preferred_element_type=F32)
    for h in range(IDX_HEADS):
        sl = slice(h * HEAD_DIM, (h + 1) * HEAD_DIM)
        qh_t = _rope(qi[:, sl], c2, s2).T
        for b in range(nqb):
            qit_ref[b, :, h * tq:(h + 1) * tq] = qh_t[:, b * tq:(b + 1) * tq].astype(BF16)
    w_t = (wi_ref[...] * wscale).T
    for h in range(IDX_HEADS):
        for b in range(nqb):
            wt_ref[b, :, h * tq:(h + 1) * tq] = w_t[h:h + 1, b * tq:(b + 1) * tq]
    k_ref[...] = _rope(ka_ref[...], c2, s2).astype(BF16)
    ki_ref[...] = _rope(kx_ref[...], c2, s2).astype(BF16)
    vt_ref[0] = va_ref[...].T.astype(BF16)


def _dsa_prep(p, cos2, sin2, g_cq, w_uq, w_qidx, seq, tm, tq):
    n = p.shape[0]
    nsb = seq // tm
    nqb = tm // tq
    aw = A_HEADS * HEAD_DIM
    iw = IDX_HEADS * HEAD_DIM
    row = lambda i: (i, 0)
    blk3 = lambda i: (i, 0, 0)
    kern = functools.partial(_prep_kernel, scale=HEAD_DIM ** -0.5 * math.log2(math.e),
                             wscale=IDX_HEADS ** -0.5 * HEAD_DIM ** -0.5, tq=tq)
    return pl.pallas_call(
        kern,
        grid=(n // tm,),
        in_specs=[
            pl.BlockSpec((tm, Q_RANK), lambda i: (i, 0)),
            pl.BlockSpec((tm, LANES), lambda i: (i, 4)),
            pl.BlockSpec((tm, LANES), lambda i: (i, 5)),
            pl.BlockSpec((tm, LANES), lambda i: (i, 6)),
            pl.BlockSpec((tm, LANES), lambda i: (i, 7)),
            pl.BlockSpec((tm, LANES), lambda i: (i % nsb, 0)),
            pl.BlockSpec((tm, LANES), lambda i: (i % nsb, 0)),
            pl.BlockSpec((1, Q_RANK), lambda i: (0, 0)),
            pl.BlockSpec((Q_RANK, aw), lambda i: (0, 0)),
            pl.BlockSpec((Q_RANK, iw), lambda i: (0, 0)),
        ],
        out_specs=[
            pl.BlockSpec((nqb, HEAD_DIM, A_HEADS * tq), blk3),
            pl.BlockSpec((nqb, HEAD_DIM, IDX_HEADS * tq), blk3),
            pl.BlockSpec((nqb, 1, IDX_HEADS * tq), blk3),
            pl.BlockSpec((tm, LANES), row),
            pl.BlockSpec((tm, LANES), row),
            pl.BlockSpec((1, HEAD_DIM, tm), blk3),
        ],
        out_shape=[
            jax.ShapeDtypeStruct((n // tq, HEAD_DIM, A_HEADS * tq), BF16),
            jax.ShapeDtypeStruct((n // tq, HEAD_DIM, IDX_HEADS * tq), BF16),
            jax.ShapeDtypeStruct((n // tq, 1, IDX_HEADS * tq), F32),
            jax.ShapeDtypeStruct((n, LANES), BF16),
            jax.ShapeDtypeStruct((n, LANES), BF16),
            jax.ShapeDtypeStruct((n // tm, HEAD_DIM, tm), BF16),
        ],
        compiler_params=_cparams(("parallel",), 40),
        name="dsa_prep",
    )(p, p, p, p, p, cos2, sin2, g_cq.reshape(1, Q_RANK), w_uq, w_qidx)


def _dot_nt(a, b):
    return lax.dot_general(a, b, (((1,), (1,)), ((), ())), preferred_element_type=F32)


def _dsa_kernel(qit_ref, qt_ref, wt_ref, ki_ref, k_ref, vt_ref, o_ref,
                key_ref, s_ref, acc_ref, *, tq, tk, topk, idx_bits):
    i = pl.program_id(1)
    nkc = ((i + 1) * tq + tk - 1) // tk
    q_pos = i * tq + lax.broadcasted_iota(I32, (tk, tq), 1)
    k_off = lax.broadcasted_iota(I32, (tk, tq), 0)

    def score_chunk(j, carry):
        kic = ki_ref[pl.ds(pl.multiple_of(j * tk, tk), tk), :]
        acc = jnp.zeros((tk, tq), F32)
        for h2 in range(IDX_HEADS // 2):
            sl = slice(2 * h2 * tq, (2 * h2 + 2) * tq)
            lg = jnp.dot(kic, qit_ref[0, :, sl], preferred_element_type=F32)
            x = jnp.maximum(lg, 0.0) * wt_ref[0, :, sl]
            acc = acc + x[:, :tq] + x[:, tq:]
        acc = jnp.where(acc == 0.0, 0.0, acc)
        bits = lax.bitcast_convert_type(acc, I32)
        key = jnp.where(bits < 0, bits ^ jnp.int32(0x7FFFFFFF), bits)
        causal = (j * tk + k_off) <= q_pos
        key_ref[j] = jnp.where(causal, key, jnp.int32(INT_MIN))
        return carry

    lax.fori_loop(0, nkc, score_chunk, 0)

    def count_rows(pred):
        def cnt_chunk(j, part):
            hit = pred(j, key_ref[j]).astype(I32)
            return part + jnp.sum(hit.reshape(tk // 8, 8, tq), axis=0)

        part = lax.fori_loop(0, nkc, cnt_chunk, jnp.zeros((8, tq), I32))
        return jnp.sum(part, axis=0, keepdims=True)

    def bit_step(b, c):
        t_u, cnt_t = c
        bit = jnp.left_shift(jnp.int32(1), 31 - b)
        cand_s = (t_u | bit) ^ jnp.int32(INT_MIN)
        cnt = count_rows(lambda j, key: key >= cand_s)
        take = cnt >= topk
        return jnp.where(take, t_u | bit, t_u), jnp.where(take, cnt, cnt_t)

    n_causal = i * tq + lax.broadcasted_iota(I32, (1, tq), 1) + 1
    t_u, cnt_t = lax.fori_loop(0, 32, bit_step, (jnp.zeros((1, tq), I32), n_causal))
    thr = jnp.maximum(t_u ^ jnp.int32(INT_MIN), jnp.int32(INT_MIN + 1))

    @pl.when(jnp.max(cnt_t) > topk)
    def _():
        need = topk - count_rows(lambda j, key: key > thr)

        def idx_step(b, cut):
            cand = cut | jnp.left_shift(jnp.int32(1), idx_bits - 1 - b)
            cnt = count_rows(lambda j, key: (key == thr) & ((j * tk + k_off) < cand))
            return jnp.where(cnt <= need, cand, cut)

        cut = lax.fori_loop(0, idx_bits, idx_step, jnp.zeros((1, tq), I32))

        def demote(j, carry):
            key = key_ref[j]
            drop = (key == thr) & ((j * tk + k_off) >= cut)
            key_ref[j] = jnp.where(drop, key - 1, key)
            return carry

        lax.fori_loop(0, nkc, demote, 0)

    acc_ref[...] = jnp.zeros(acc_ref.shape, F32)

    def att_chunk(j, carry):
        m_old, l_old = carry
        kc = k_ref[pl.ds(pl.multiple_of(j * tk, tk), tk), :]
        sel = key_ref[j] >= thr
        cmax = []
        for h in range(A_HEADS):
            sl = slice(h * tq, (h + 1) * tq)
            s_h = jnp.dot(kc, qt_ref[0, :, sl], preferred_element_type=F32)
            s_h = jnp.where(sel, s_h, -jnp.inf)
            s_ref[:, sl] = s_h
            cmax.append(jnp.max(s_h, axis=0, keepdims=True))
        m_new = jnp.maximum(m_old, jnp.concatenate(cmax, axis=1))
        alpha = jnp.exp2(m_old - m_new)
        p = jnp.exp2(s_ref[...] - m_new)
        l_new = alpha * l_old + jnp.sum(p, axis=0, keepdims=True)
        pv = jnp.dot(vt_ref[j], p.astype(BF16), preferred_element_type=F32)
        acc_ref[...] = alpha * acc_ref[...] + pv
        return m_new, l_new

    stat0 = (jnp.full((1, A_HEADS * tq), -1e30, F32), jnp.zeros((1, A_HEADS * tq), F32))
    _, l_fin = lax.fori_loop(0, nkc, att_chunk, stat0)
    out_t = acc_ref[...] / l_fin
    for h in range(A_HEADS):
        o_ref[:, h * HEAD_DIM:(h + 1) * HEAD_DIM] = out_t[:, h * tq:(h + 1) * tq].T.astype(o_ref.dtype)


def _dsa(qit, qt, wt, ki, k, vt, batch, seq, tq, tk):
    n = batch * seq
    nqb = seq // tq
    nkb = seq // tk
    topk = min(TOPK_MAX, seq // 4)
    aw = A_HEADS * HEAD_DIM
    kern = functools.partial(_dsa_kernel, tq=tq, tk=tk, topk=topk, idx_bits=seq.bit_length())
    qblk = lambda b, i: (b * nqb + i, 0, 0)
    return pl.pallas_call(
        kern,
        grid=(batch, nqb),
        in_specs=[
            pl.BlockSpec((1, HEAD_DIM, IDX_HEADS * tq), qblk),
            pl.BlockSpec((1, HEAD_DIM, A_HEADS * tq), qblk),
            pl.BlockSpec((1, 1, IDX_HEADS * tq), qblk),
            pl.BlockSpec((seq, LANES), lambda b, i: (b, 0)),
            pl.BlockSpec((seq, LANES), lambda b, i: (b, 0)),
            pl.BlockSpec((nkb, HEAD_DIM, tk), lambda b, i: (b, 0, 0)),
        ],
        out_specs=pl.BlockSpec((tq, aw), lambda b, i: (b * nqb + i, 0)),
        out_shape=jax.ShapeDtypeStruct((n, aw), BF16),
        scratch_shapes=[
            pltpu.VMEM((nkb, tk, tq), I32),
            pltpu.VMEM((tk, A_HEADS * tq), F32),
            pltpu.VMEM((HEAD_DIM, A_HEADS * tq), F32),
        ],
        compiler_params=_cparams(("parallel", "arbitrary"), 48),
        name="dsa",
    )(qit, qt, wt, ki, k, vt)


HG_SUB = 16


def _hgrn_diag_exact(q, kk, v, lc, rsub, nsub):
    chunk, gw = q.shape

    def bcast(x, s):
        return jnp.concatenate([jnp.broadcast_to(x[a * HG_SUB + s:a * HG_SUB + s + 1, :], (HG_SUB, gw))
                                for a in range(nsub)], axis=0)

    acc = [jnp.zeros((chunk, HEAD_DIM), F32) for _ in range(G_HEADS)]
    for s in range(HG_SUB):
        term = q * bcast(kk, s) * jnp.exp(jnp.minimum(lc - bcast(lc, s), 0.0))
        term = jnp.where(rsub >= s, term, 0.0)
        vrow = bcast(v, s)
        for h in range(G_HEADS):
            sl = slice(h * HEAD_DIM, (h + 1) * HEAD_DIM)
            acc[h] = acc[h] + jnp.sum(term[:, sl], axis=1, keepdims=True) * vrow[:, sl]
    return jnp.concatenate(acc, axis=1)


def _hgrn_gates(hq_ref, hf_ref, hi_ref, lb_ref):
    chunk, gw = hq_ref.shape
    lb = lb_ref[...]
    f = lb + (1.0 - lb) * _sigmoid(hf_ref[...])
    kk = 1.0 - f
    hq = hq_ref[...].astype(F32)
    q = hq * _sigmoid(hq)
    v = hi_ref[...].astype(F32)
    rsub = lax.broadcasted_iota(I32, (chunk, gw), 0) & (HG_SUB - 1)
    lc = jnp.log(f)
    sh = 1
    while sh < HG_SUB:
        lc = lc + jnp.where(rsub >= sh, pltpu.roll(lc, sh, 0), 0.0)
        sh *= 2
    return q, kk, v, lc, rsub


def _hgrn_diag_scores(q, kk, lc, h):
    chunk = q.shape[0]
    sl = slice(h * HEAD_DIM, (h + 1) * HEAD_DIM)
    qd = (q[:, sl] * jnp.exp(lc[:, sl])).astype(BF16)
    kd = (kk[:, sl] * jnp.exp(jnp.minimum(-lc[:, sl], EXP_CLAMP))).astype(BF16)
    ti = lax.broadcasted_iota(I32, (chunk, chunk), 0)
    si = lax.broadcasted_iota(I32, (chunk, chunk), 1)
    diag_ok = (ti // HG_SUB == si // HG_SUB) & (si <= ti)
    return jnp.where(diag_ok, _dot_nt(qd, kd), 0.0).astype(BF16)


def _hgrn_kernel(need_ref, hq_ref, hf_ref, hi_ref, hg_ref, lb_ref, gn_ref, o_ref, st_ref, oa_ref, *, chunk):
    nsub = chunk // HG_SUB
    gw = hq_ref.shape[1]

    @pl.when(pl.program_id(1) == 0)
    def _():
        st_ref[...] = jnp.zeros(st_ref.shape, F32)

    q, kk, v, lc, rsub = _hgrn_gates(hq_ref, hf_ref, hi_ref, lb_ref)
    sub = lax.broadcasted_iota(I32, (chunk, gw), 0) // HG_SUB
    tots = [lc[(a + 1) * HG_SUB - 1:(a + 1) * HG_SUB, :] for a in range(nsub)]
    ends = []
    run = jnp.zeros((1, gw), F32)
    for a in range(nsub):
        run = run + tots[a]
        ends.append(run)
    cum_l = ends[-1]
    start_full = jnp.zeros((chunk, gw), F32)
    tot_full = jnp.zeros((chunk, gw), F32)
    end_full = jnp.zeros((chunk, gw), F32)
    for a in range(nsub):
        ina = sub == a
        if a > 0:
            start_full = jnp.where(ina, ends[a - 1], start_full)
        tot_full = jnp.where(ina, tots[a], tot_full)
        end_full = jnp.where(ina, ends[a], end_full)
    cum = lc + start_full

    qs = (q * jnp.exp(cum)).astype(BF16)
    koff = kk * jnp.exp(tot_full - lc)
    ke = (koff * jnp.exp(cum_l - end_full)).astype(BF16)
    koff_b = koff.astype(BF16)
    zero_b = jnp.zeros((chunk, gw), BF16)
    qoff = []
    kmask = []
    for b in range(nsub - 1):
        qb = q * jnp.exp(jnp.minimum(cum - ends[b], 0.0))
        qoff.append(jnp.where(sub > b, qb, 0.0).astype(BF16))
        kmask.append(jnp.where(sub == b, koff_b, zero_b))
    vb = v.astype(BF16)
    decay_l = jnp.exp(cum_l)

    for h in range(G_HEADS):
        sl = slice(h * HEAD_DIM, (h + 1) * HEAD_DIM)
        qcat = jnp.concatenate([qo[:, sl] for qo in qoff], axis=1)
        kcat = jnp.concatenate([km[:, sl] for km in kmask], axis=1)
        a = _hgrn_diag_scores(q, kk, lc, h) + _dot_nt(qcat, kcat).astype(BF16)
        st = st_ref[h]
        oa_ref[:, sl] = jnp.dot(a, vb[:, sl], preferred_element_type=F32) + _dot_nt(qs[:, sl], st.astype(BF16))
        upd = jnp.dot(v[:, sl].T.astype(BF16), ke[:, sl], preferred_element_type=F32)
        st_ref[h] = st * decay_l[:, sl] + upd

    @pl.when(need_ref[0] != 0)
    def _():
        q2, kk2, v2, lc2, rsub2 = _hgrn_gates(hq_ref, hf_ref, hi_ref, lb_ref)

        @pl.when(jnp.max(-lc2) > EXP_CLAMP)
        def _():
            v2b = v2.astype(BF16)
            fast = [jnp.dot(_hgrn_diag_scores(q2, kk2, lc2, h), v2b[:, h * HEAD_DIM:(h + 1) * HEAD_DIM],
                            preferred_element_type=F32) for h in range(G_HEADS)]
            oa_ref[...] += _hgrn_diag_exact(q2, kk2, v2, lc2, rsub2, nsub) - jnp.concatenate(fast, axis=1)

    o = oa_ref[...]
    ms = jnp.mean(o * o, axis=-1, keepdims=True)
    hg = hg_ref[...].astype(F32)
    y = o * lax.rsqrt(ms + EPS) * gn_ref[...] * (hg * _sigmoid(hg))
    o_ref[...] = y.astype(o_ref.dtype)


def _hgrn(ph, pt, lb, g_hnorm, batch, seq, chunk):
    n = batch * seq
    gw = G_HEADS * HEAD_DIM
    ncb = seq // chunk
    blk = lambda c: pl.BlockSpec((chunk, gw), lambda b, i, nd: (b * ncb + i, c))
    need = (HG_SUB * jnp.log(jnp.min(lb)) < -EXP_CLAMP).astype(I32).reshape(1)
    grid_spec = pltpu.PrefetchScalarGridSpec(
        num_scalar_prefetch=1,
        grid=(batch, ncb),
        in_specs=[blk(0), blk(1), blk(1), blk(2),
                  pl.BlockSpec((1, gw), lambda b, i, nd: (0, 0)),
                  pl.BlockSpec((1, gw), lambda b, i, nd: (0, 0))],
        out_specs=pl.BlockSpec((chunk, gw), lambda b, i, nd: (b * ncb + i, 0)),
        scratch_shapes=[pltpu.VMEM((G_HEADS, HEAD_DIM, HEAD_DIM), F32),
                        pltpu.VMEM((chunk, gw), F32)],
    )
    return pl.pallas_call(
        functools.partial(_hgrn_kernel, chunk=chunk),
        grid_spec=grid_spec,
        out_shape=jax.ShapeDtypeStruct((n, gw), BF16),
        compiler_params=_cparams(("parallel", "arbitrary"), 32),
        name="hgrn",
    )(need, pt, ph, pt, pt, lb.reshape(1, gw), g_hnorm.reshape(1, gw))


def _merge_kernel(ya_ref, yr_ref, ga_ref, gb_ref, x_ref, wpa_ref, wpb_ref, wo_ref, o_ref, acc_ref):
    c = pl.program_id(1)

    @pl.when(c == 0)
    def _():
        acc_ref[...] = x_ref[...]

    ma = jnp.dot(ya_ref[...], wpa_ref[...], preferred_element_type=F32)
    mb = jnp.dot(yr_ref[...], wpb_ref[...], preferred_element_type=F32)
    merged = _sigmoid(ga_ref[...].astype(F32)) * ma + _sigmoid(gb_ref[...].astype(F32)) * mb
    acc_ref[...] += jnp.dot(merged.astype(BF16), wo_ref[...], preferred_element_type=F32)

    @pl.when(c == pl.num_programs(1) - 1)
    def _():
        o_ref[...] = acc_ref[...]


def _merge(y_att, y_rec, pt, x2, w_pa, w_pb, w_o, tm, tc):
    n, d = x2.shape
    aw = y_att.shape[1]
    gw = y_rec.shape[1]
    ga0 = 3072 // tc
    gb0 = 5120 // tc
    return pl.pallas_call(
        _merge_kernel,
        grid=(n // tm, d // tc),
        in_specs=[
            pl.BlockSpec((tm, aw), lambda i, c: (i, 0)),
            pl.BlockSpec((tm, gw), lambda i, c: (i, 0)),
            pl.BlockSpec((tm, tc), lambda i, c: (i, ga0 + c)),
            pl.BlockSpec((tm, tc), lambda i, c: (i, gb0 + c)),
            pl.BlockSpec((tm, d), lambda i, c: (i, 0)),
            pl.BlockSpec((aw, tc), lambda i, c: (0, c)),
            pl.BlockSpec((gw, tc), lambda i, c: (0, c)),
            pl.BlockSpec((tc, d), lambda i, c: (c, 0)),
        ],
        out_specs=pl.BlockSpec((tm, d), lambda i, c: (i, 0)),
        out_shape=jax.ShapeDtypeStruct((n, d), F32),
        scratch_shapes=[pltpu.VMEM((tm, d), F32)],
        compiler_params=_cparams(("parallel", "arbitrary"), 48),
        name="merge",
    )(y_att, y_rec, pt, pt, x2, w_pa, w_pb, w_o)


def _router_kernel(x_ref, g_ref, wr_ref, br_ref, h_ref, r_ref, cnt_ref):
    x = x_ref[...]
    ms = jnp.mean(x * x, axis=-1, keepdims=True)
    h = x * lax.rsqrt(ms + EPS) * g_ref[...]
    h_ref[...] = h
    lg = jnp.dot(h.astype(BF16), wr_ref[...], preferred_element_type=F32) + br_ref[...]
    tm = lg.shape[0]
    lane = lax.broadcasted_iota(I32, (tm, LANES), 1)
    neg = jnp.float32(-1e30)
    big = jnp.float32(LANES)
    is_g = lane < N_GROUPS
    lgg = jnp.where(is_g, lg, neg)
    mg = jnp.max(lgg, axis=1, keepdims=True)
    zg = jnp.sum(jnp.where(is_g, jnp.exp(lgg - mg), 0.0), axis=1, keepdims=True)
    p_g = 1.0 / zg
    grp = jnp.min(jnp.where(is_g & (lgg == mg), lane.astype(F32), big), axis=1, keepdims=True)
    e_id = lane - N_GROUPS
    e_grp = lax.shift_right_arithmetic(e_id, jnp.int32(EXP_PER_GROUP.bit_length() - 1)).astype(F32)
    e_idf = e_id.astype(F32)
    is_e = (e_id >= 0) & (e_id < N_EXPERTS) & (e_grp == grp)
    lge = jnp.where(is_e, lg, neg)
    m1 = jnp.max(lge, axis=1, keepdims=True)
    i1 = jnp.min(jnp.where(is_e & (lge == m1), e_idf, big), axis=1, keepdims=True)
    is_e2 = is_e & (e_idf != i1)
    lge2 = jnp.where(is_e2, lg, neg)
    m2 = jnp.max(lge2, axis=1, keepdims=True)
    i2 = jnp.min(jnp.where(is_e2 & (lge2 == m2), e_idf, big), axis=1, keepdims=True)
    ex2 = jnp.exp(m2 - m1)
    den = 1.0 + ex2
    g1 = p_g / den
    g2 = p_g * ex2 / den
    @pl.when(pl.program_id(0) == 0)
    def _():
        cnt_ref[...] = jnp.zeros(cnt_ref.shape, F32)

    lanef = lane.astype(F32)
    oh1 = lanef == i1
    oh2 = lanef == i2
    oh = jnp.where(oh1 | oh2, 1.0, 0.0)
    ti = lax.broadcasted_iota(I32, (tm, tm), 0)
    si = lax.broadcasted_iota(I32, (tm, tm), 1)
    tri = jnp.where(si < ti, 1.0, 0.0).astype(BF16)
    before = jnp.dot(tri, oh.astype(BF16), preferred_element_type=F32) + cnt_ref[...]
    rank1 = jnp.sum(jnp.where(oh1, before, 0.0), axis=1, keepdims=True)
    rank2 = jnp.sum(jnp.where(oh2, before, 0.0), axis=1, keepdims=True)
    cnt_ref[...] += jnp.sum(oh, axis=0, keepdims=True)

    r = jnp.where(lane == 0, i1, 0.0)
    r = jnp.where(lane == 1, i2, r)
    r = jnp.where(lane == 2, g1, r)
    r = jnp.where(lane == 3, g2, r)
    r = jnp.where(lane == 4, rank1, r)
    r = jnp.where(lane == 5, rank2, r)
    r_ref[...] = r


def _router(x1, g, w_r, b_r, tm):
    n, d = x1.shape
    return pl.pallas_call(
        _router_kernel,
        grid=(n // tm,),
        in_specs=[
            pl.BlockSpec((tm, d), lambda i: (i, 0)),
            pl.BlockSpec((1, d), lambda i: (0, 0)),
            pl.BlockSpec((d, LANES), lambda i: (0, 0)),
            pl.BlockSpec((1, LANES), lambda i: (0, 0)),
        ],
        out_specs=[pl.BlockSpec((tm, d), lambda i: (i, 0)),
                   pl.BlockSpec((tm, LANES), lambda i: (i, 0)),
                   pl.BlockSpec((1, LANES), lambda i: (0, 0))],
        out_shape=[jax.ShapeDtypeStruct((n, d), F32),
                   jax.ShapeDtypeStruct((n, LANES), F32),
                   jax.ShapeDtypeStruct((1, LANES), F32)],
        compiler_params=_cparams(("arbitrary",), 40),
        name="router",
    )(x1, g.reshape(1, d), w_r, b_r)


def _dispatch_kernel(prow_ref, h_ref, xs_hbm, sem, *, tm):
    base = 2 * tm * pl.program_id(0)

    def scatter_row(r, carry):
        for s in range(2):
            pltpu.make_async_copy(h_ref.at[pl.ds(r, 1), :],
                                  xs_hbm.at[pl.ds(prow_ref[base + 2 * r + s], 1), :], sem).start()
        return carry

    lax.fori_loop(0, tm, scatter_row, 0, unroll=4)
    for _ in range(2):
        pltpu.make_async_copy(h_ref, xs_hbm.at[pl.ds(0, tm), :], sem).wait()


def _dispatch(h2, prow, tm):
    n, d = h2.shape
    grid_spec = pltpu.PrefetchScalarGridSpec(
        num_scalar_prefetch=1,
        grid=(n // tm,),
        in_specs=[pl.BlockSpec((tm, d), lambda i, pr: (i, 0))],
        out_specs=pl.BlockSpec(memory_space=pl.ANY),
        scratch_shapes=[pltpu.SemaphoreType.DMA(())],
    )
    return pl.pallas_call(
        functools.partial(_dispatch_kernel, tm=tm),
        grid_spec=grid_spec,
        out_shape=jax.ShapeDtypeStruct((2 * n, d), F32),
        compiler_params=_cparams(("arbitrary",), 32),
        name="dispatch",
    )(prow, h2)


def _moe_kernel(pt_ref, pe_ref, plo_ref, phi_ref, np_ref, x_ref, wg_ref, wu_ref, wd_ref, y_ref,
                wgb_ref, wub_ref, wdb_ref):
    p = pl.program_id(0)
    active = p < np_ref[0]
    prev = jnp.maximum(p - 1, 0)
    new_expert = (p == 0) | (pe_ref[p] != pe_ref[prev])
    new_tile = (p == 0) | (pt_ref[p] != pt_ref[prev])

    @pl.when(active & new_expert)
    def _():
        wgb_ref[...] = wg_ref[0].astype(BF16)
        wub_ref[...] = wu_ref[0].astype(BF16)
        wdb_ref[...] = wd_ref[0].astype(BF16)

    @pl.when(active)
    def _():
        tm = x_ref.shape[0]
        xb = x_ref[...].astype(BF16)
        a = jnp.dot(xb, wgb_ref[...], preferred_element_type=F32)
        u = jnp.dot(xb, wub_ref[...], preferred_element_type=F32)
        hmid = (a * _sigmoid(a) * u).astype(BF16)
        yv = jnp.dot(hmid, wdb_ref[...], preferred_element_type=F32)
        row = lax.broadcasted_iota(I32, (tm, 1), 0)
        mine = (row >= plo_ref[p]) & (row < phi_ref[p])
        yv = jnp.where(mine, yv, 0.0)

        @pl.when(new_tile)
        def _():
            y_ref[...] = yv

        @pl.when(jnp.logical_not(new_tile))
        def _():
            y_ref[...] += yv


def _moe(xs, pair_tile, pair_expert, pair_lo, pair_hi, n_pairs, w_gate, w_up, w_down, tm):
    na, d = xs.shape
    de = w_gate.shape[2]
    max_pairs = pair_tile.shape[0]
    wmap = lambda p, pt, pe, *_: (pe[p], 0, 0)
    grid_spec = pltpu.PrefetchScalarGridSpec(
        num_scalar_prefetch=5,
        grid=(max_pairs,),
        in_specs=[
            pl.BlockSpec((tm, d), lambda p, pt, *_: (pt[p], 0)),
            pl.BlockSpec((1, d, de), wmap),
            pl.BlockSpec((1, d, de), wmap),
            pl.BlockSpec((1, de, d), wmap),
        ],
        out_specs=pl.BlockSpec((tm, d), lambda p, pt, *_: (pt[p], 0)),
        scratch_shapes=[
            pltpu.VMEM((d, de), BF16),
            pltpu.VMEM((d, de), BF16),
            pltpu.VMEM((de, d), BF16),
        ],
    )
    return pl.pallas_call(
        _moe_kernel,
        grid_spec=grid_spec,
        out_shape=jax.ShapeDtypeStruct((na, d), F32),
        compiler_params=_cparams(("arbitrary",), 56),
        name="moe",
    )(pair_tile, pair_expert, pair_lo, pair_hi, n_pairs, xs, w_gate, w_up, w_down)


def _moe_rows(r, cnt):
    sizes = cnt[0, :N_EXPERTS].astype(I32)
    starts = jnp.cumsum(sizes) - sizes
    e12 = r[:, 0:2].astype(I32)
    rank = r[:, 4:6].astype(I32)
    onehot = e12[:, :, None] == jnp.arange(N_EXPERTS, dtype=I32)[None, None, :]
    start_a = jnp.sum(jnp.where(onehot, starts[None, None, :], 0), axis=-1)
    return (start_a + rank).reshape(-1).astype(I32), sizes


def _moe_pairs(sizes, na, tm):
    ntiles = na // tm
    max_pairs = ntiles + N_EXPERTS - 1
    ends = jnp.cumsum(sizes)
    starts = ends - sizes
    t0 = jnp.arange(ntiles, dtype=I32) * tm
    e_lo = jnp.searchsorted(ends, t0, side="right").astype(I32)
    e_hi = jnp.searchsorted(ends, t0 + (tm - 1), side="right").astype(I32)
    per_tile = e_hi - e_lo + 1
    pend = jnp.cumsum(per_tile)
    pstart = pend - per_tile
    n_pairs = pend[-1]
    pc = jnp.minimum(jnp.arange(max_pairs, dtype=I32), n_pairs - 1)
    tile = jnp.searchsorted(pend, pc, side="right").astype(I32)
    expert = e_lo[tile] + (pc - pstart[tile])
    lo = jnp.clip(starts[expert] - tile * tm, 0, tm).astype(I32)
    hi = jnp.clip(ends[expert] - tile * tm, 0, tm).astype(I32)
    return tile, expert.astype(I32), lo, hi, n_pairs.astype(I32).reshape(1)


def _final_kernel(prow_ref, x_ref, r_ref, g_ref, y_hbm, o_ref, ybuf, sem, *, tm):
    base = 2 * tm * pl.program_id(0)

    def gather_row(t, carry):
        for s in range(2):
            pltpu.make_async_copy(y_hbm.at[pl.ds(prow_ref[base + 2 * t + s], 1), :],
                                  ybuf.at[s, pl.ds(t, 1), :], sem).start()
        return carry

    lax.fori_loop(0, tm, gather_row, 0, unroll=4)
    for s in range(2):
        pltpu.make_async_copy(y_hbm.at[pl.ds(0, tm), :], ybuf.at[s], sem).wait()
    r = r_ref[...]
    g1 = r[:, 2:3]
    g2 = r[:, 3:4]
    x = x_ref[...] + g1 * ybuf[0] + g2 * ybuf[1]
    ms = jnp.mean(x * x, axis=-1, keepdims=True)
    o_ref[...] = x * lax.rsqrt(ms + EPS) * g_ref[...]


def _final(x1, y, r, prow, g, tm):
    n, d = x1.shape
    grid_spec = pltpu.PrefetchScalarGridSpec(
        num_scalar_prefetch=1,
        grid=(n // tm,),
        in_specs=[
            pl.BlockSpec((tm, d), lambda i, pr: (i, 0)),
            pl.BlockSpec((tm, LANES), lambda i, pr: (i, 0)),
            pl.BlockSpec((1, d), lambda i, pr: (0, 0)),
            pl.BlockSpec(memory_space=pl.ANY),
        ],
        out_specs=pl.BlockSpec((tm, d), lambda i, pr: (i, 0)),
        scratch_shapes=[pltpu.VMEM((2, tm, d), F32), pltpu.SemaphoreType.DMA(())],
    )
    return pl.pallas_call(
        functools.partial(_final_kernel, tm=tm),
        grid_spec=grid_spec,
        out_shape=jax.ShapeDtypeStruct((n, d), F32),
        compiler_params=_cparams(("arbitrary",), 40),
        name="final",
    )(prow, x1, r, g.reshape(1, d), y)


def _pack_w_in(w):
    d = w.shape[0]
    pad = jnp.zeros((d, LANES - IDX_HEADS), w.dtype)
    return jnp.concatenate([w[:, :912], pad, w[:, 1936:2960], w[:, 912:1936], w[:, 2960:]], axis=1).astype(BF16)


def _rope_tables(seq):
    half = HEAD_DIM // 2
    inv = 1.0 / (ROPE_THETA ** (jnp.arange(0, HEAD_DIM, 2, dtype=F32) / HEAD_DIM))
    ang = jnp.arange(seq, dtype=F32)[:, None] * inv[None, :]
    c, s = jnp.cos(ang), jnp.sin(ang)
    del half
    return jnp.concatenate([c, c], axis=1), jnp.concatenate([-s, s], axis=1)


def _layer(x2, batch, seq, cos2, sin2, lb, g_norm1, w_in, g_cq, w_uq, w_qidx, g_hnorm, w_pa, w_pb, w_o,
           g_norm2, w_grp, b_grp, w_exp, b_exp, w_gate, w_up, w_down, g_out):
    n, d = x2.shape
    ph, pt = _in_proj(x2, g_norm1, _pack_w_in(w_in), tm=min(1024, n), tn=512)
    tq, tk = 128, min(256, seq)
    qt, qit, wt, k, ki, vt = _dsa_prep(ph, cos2, sin2, g_cq, w_uq.astype(BF16), w_qidx.astype(BF16), seq, tm=tk, tq=tq)
    y_att = _dsa(qit, qt, wt, ki, k, vt, batch, seq, tq=tq, tk=tk)
    y_rec = _hgrn(ph, pt, lb, g_hnorm, batch, seq, chunk=64)
    x1 = _merge(y_att, y_rec, pt, x2, w_pa.astype(BF16), w_pb.astype(BF16), w_o.astype(BF16),
                tm=min(512, n), tc=512)
    w_r = jnp.concatenate([w_grp, w_exp, jnp.zeros((d, LANES - N_GROUPS - N_EXPERTS), F32)], axis=1).astype(BF16)
    b_r = jnp.concatenate([b_grp, b_exp, jnp.zeros((LANES - N_GROUPS - N_EXPERTS,), F32)]).reshape(1, LANES)
    h2, r, cnt = _router(x1, g_norm2, w_r, b_r, tm=min(512, n))
    prow, sizes = _moe_rows(r, cnt)
    xs = _dispatch(h2, prow, tm=min(512, n))
    tm_moe = 256
    assert (2 * n) % tm_moe == 0
    y = _moe(xs, *_moe_pairs(sizes, 2 * n, tm_moe), w_gate, w_up, w_down, tm_moe)
    return _final(x1, y, r, prow, g_out, tm=min(512, n))


def kernel(x, g_norm1, w_in, g_cq, w_uq, w_qidx, lb_logits, g_hnorm, w_pa, w_pb, w_o, g_norm2, w_grp, b_grp,
           w_exp, b_exp, w_gate, w_up, w_down, g_final):
    batch, seq, d = x.shape
    depth = g_norm1.shape[0]
    cos2, sin2 = _rope_tables(seq)
    lb_all = jnp.cumsum(jax.nn.softmax(lb_logits.astype(F32), axis=0), axis=0)
    x2 = x.reshape(batch * seq, d)
    for l in range(depth):
        assert depth == 1
        x2 = _layer(x2, batch, seq, cos2, sin2, lb_all[l], g_norm1[l], w_in[l], g_cq[l], w_uq[l], w_qidx[l],
                    g_hnorm[l], w_pa[l], w_pb[l], w_o[l], g_norm2[l], w_grp[l], b_grp[l], w_exp[l], b_exp[l],
                    w_gate[l], w_up[l], w_down[l], g_final)
    return x2.reshape(batch, seq, d)
```

```python
import functools
import math

import jax
import jax.numpy as jnp
from jax import lax
from jax.experimental import pallas as pl
from jax.experimental.pallas import tpu as pltpu

F32 = jnp.float32
BF16 = jnp.bfloat16
I32 = jnp.int32

EPS = 1e-6
ROPE_THETA = 10000.0
LANES = 128
MXU_LANES = 256

A_HEADS = 8
HEAD_DIM = 128
IDX_HEADS = 16
Q_RANK = 512
TOPK_MAX = 256
G_HEADS = 8
N_GROUPS = 4
EXP_PER_GROUP = 8
N_EXPERTS = N_GROUPS * EXP_PER_GROUP
D_EXPERT = 512

INT_MIN = -2147483648
EXP_CLAMP = 80.0


def _cparams(sem, vmem_mb):
    return pltpu.CompilerParams(dimension_semantics=sem, vmem_limit_bytes=vmem_mb << 20)


def _sigmoid(x):
    return 1.0 / (1.0 + jnp.exp(-x))


PH_COLS = 2048
PT_COLS = 7168


def _inproj_kernel(x_ref, g_ref, w_ref, oh_ref, ot_ref, xn_ref, *, nh):
    j = pl.program_id(1)

    @pl.when(j == 0)
    def _():
        x = x_ref[...]
        ms = jnp.mean(x * x, axis=-1, keepdims=True)
        xn_ref[...] = (x * lax.rsqrt(ms + EPS) * g_ref[...]).astype(BF16)

    y = jnp.dot(xn_ref[...], w_ref[...], preferred_element_type=F32)

    @pl.when(j < nh)
    def _():
        oh_ref[...] = y

    @pl.when(j >= nh)
    def _():
        ot_ref[...] = y.astype(ot_ref.dtype)


def _in_proj(x2, g, w, tm, tn):
    n, d = x2.shape
    nh = PH_COLS // tn
    nt = PT_COLS // tn
    return pl.pallas_call(
        functools.partial(_inproj_kernel, nh=nh),
        grid=(n // tm, nh + nt),
        in_specs=[
            pl.BlockSpec((tm, d), lambda i, j: (i, 0)),
            pl.BlockSpec((1, d), lambda i, j: (0, 0)),
            pl.BlockSpec((d, tn), lambda i, j: (0, j)),
        ],
        out_specs=[pl.BlockSpec((tm, tn), lambda i, j: (i, jnp.minimum(j, nh - 1))),
                   pl.BlockSpec((tm, tn), lambda i, j: (i, jnp.maximum(j - nh, 0)))],
        out_shape=[jax.ShapeDtypeStruct((n, PH_COLS), F32),
                   jax.ShapeDtypeStruct((n, PT_COLS), BF16)],
        scratch_shapes=[pltpu.VMEM((tm, d), BF16)],
        compiler_params=_cparams(("parallel", "arbitrary"), 48),
        name="in_proj",
    )(x2, g.reshape(1, d), w)


def _rope(x, c2, s2):
    return x * c2 + pltpu.roll(x, HEAD_DIM // 2, 1) * s2


def _prep_kernel(cq_ref, ka_ref, va_ref, kx_ref, wi_ref, cos_ref, sin_ref, g_ref, wuq_ref, wqi_ref,
                 qt_ref, qit_ref, wt_ref, k_ref, ki_ref, vt_ref, *, scale, wscale, tq):
    nqb = cq_ref.shape[0] // tq
    c2 = cos_ref[...]
    s2 = sin_ref[...]
    cq = cq_ref[...]
    ms = jnp.mean(cq * cq, axis=-1, keepdims=True)
    cqn = (cq * lax.rsqrt(ms + EPS) * g_ref[...]).astype(BF16)
    q = jnp.dot(cqn, wuq_ref[...], preferred_element_type=F32)
    for h in range(A_HEADS):
        sl = slice(h * HEAD_DIM, (h + 1) * HEAD_DIM)
        qh_t = (_rope(q[:, sl], c2, s2) * scale).T
        for b in range(nqb):
            qt_ref[b, :, h * tq:(h + 1) * tq] = qh_t[:, b * tq:(b + 1) * tq].astype(BF16)
    qi = jnp.dot(cqn, wqi_ref[...], preferred_element_type=F32)
    for h in range(IDX_HEADS):
        sl = slice(h * HEAD_DIM, (h + 1) * HEAD_DIM)
        qh_t = _rope(qi[:, sl], c2, s2).T
        for b in range(nqb):
            qit_ref[b, :, h * tq:(h + 1) * tq] = qh_t[:, b * tq:(b + 1) * tq].astype(BF16)
    w_t = (wi_ref[...] * wscale).T
    for h in range(IDX_HEADS):
        for b in range(nqb):
            wt_ref[b, :, h * tq:(h + 1) * tq] = w_t[h:h + 1, b * tq:(b + 1) * tq]
    k_ref[...] = _rope(ka_ref[...], c2, s2).astype(BF16)
    ki_ref[...] = _rope(kx_ref[...], c2, s2).astype(BF16)
    vt_ref[0] = va_ref[...].T.astype(BF16)


def _dsa_prep(p, cos2, sin2, g_cq, w_uq, w_qidx, seq, tm, tq):
    n = p.shape[0]
    nsb = seq // tm
    nqb = tm // tq
    aw = A_HEADS * HEAD_DIM
    iw = IDX_HEADS * HEAD_DIM
    row = lambda i: (i, 0)
    blk3 = lambda i: (i, 0, 0)
    kern = functools.partial(_prep_kernel, scale=HEAD_DIM ** -0.5 * math.log2(math.e),
                             wscale=IDX_HEADS ** -0.5 * HEAD_DIM ** -0.5, tq=tq)
    return pl.pallas_call(
        kern,
        grid=(n // tm,),
        in_specs=[
            pl.BlockSpec((tm, Q_RANK), lambda i: (i, 0)),
            pl.BlockSpec((tm, LANES), lambda i: (i, 4)),
            pl.BlockSpec((tm, LANES), lambda i: (i, 5)),
            pl.BlockSpec((tm, LANES), lambda i: (i, 6)),
            pl.BlockSpec((tm, LANES), lambda i: (i, 7)),
            pl.BlockSpec((tm, LANES), lambda i: (i % nsb, 0)),
            pl.BlockSpec((tm, LANES), lambda i: (i % nsb, 0)),
            pl.BlockSpec((1, Q_RANK), lambda i: (0, 0)),
            pl.BlockSpec((Q_RANK, aw), lambda i: (0, 0)),
            pl.BlockSpec((Q_RANK, iw), lambda i: (0, 0)),
        ],
        out_specs=[
            pl.BlockSpec((nqb, HEAD_DIM, A_HEADS * tq), blk3),
            pl.BlockSpec((nqb, HEAD_DIM, IDX_HEADS * tq), blk3),
            pl.BlockSpec((nqb, 1, IDX_HEADS * tq), blk3),
            pl.BlockSpec((tm, LANES), row),
            pl.BlockSpec((tm, LANES), row),
            pl.BlockSpec((1, HEAD_DIM, tm), blk3),
        ],
        out_shape=[
            jax.ShapeDtypeStruct((n // tq, HEAD_DIM, A_HEADS * tq), BF16),
            jax.ShapeDtypeStruct((n // tq, HEAD_DIM, IDX_HEADS * tq), BF16),
            jax.ShapeDtypeStruct((n // tq, 1, IDX_HEADS * tq), F32),
            jax.ShapeDtypeStruct((n, LANES), BF16),
            jax.ShapeDtypeStruct((n, LANES), BF16),
            jax.ShapeDtypeStruct((n // tm, HEAD_DIM, tm), BF16),
        ],
        compiler_params=_cparams(("parallel",), 40),
        name="dsa_prep",
    )(p, p, p, p, p, cos2, sin2, g_cq.reshape(1, Q_RANK), w_uq, w_qidx)


def _dot_nt(a, b):
    return lax.dot_general(a, b, (((1,), (1,)), ((), ())), preferred_element_type=F32)


def _dsa_kernel(qit_ref, qt_ref, wt_ref, ki_ref, k_ref, vt_ref, o_ref,
                key_ref, s_ref, acc_ref, *, tq, tk, topk, idx_bits):
    i = pl.program_id(1)
    nkc = ((i + 1) * tq + tk - 1) // tk
    q_pos = i * tq + lax.broadcasted_iota(I32, (tk, tq), 1)
    k_off = lax.broadcasted_iota(I32, (tk, tq), 0)

    def score_chunk(j, carry):
        kic = ki_ref[pl.ds(pl.multiple_of(j * tk, tk), tk), :]
        acc = jnp.zeros((tk, tq), F32)
        hpm = max(1, MXU_LANES // tq)
        for h0 in range(0, IDX_HEADS, hpm):
            sl = slice(h0 * tq, (h0 + hpm) * tq)
            lg = jnp.dot(kic, qit_ref[0, :, sl], preferred_element_type=F32)
            x = jnp.maximum(lg, 0.0) * wt_ref[0, :, sl]
            for u in range(hpm):
                acc = acc + x[:, u * tq:(u + 1) * tq]
        acc = jnp.where(acc == 0.0, 0.0, acc)
        bits = lax.bitcast_convert_type(acc, I32)
        key = jnp.where(bits < 0, bits ^ jnp.int32(0x7FFFFFFF), bits)
        causal = (j * tk + k_off) <= q_pos
        key_ref[j] = jnp.where(causal, key, jnp.int32(INT_MIN))
        return carry

    lax.fori_loop(0, nkc, score_chunk, 0)

    def count_rows(pred):
        def cnt_chunk(j, part):
            hit = pred(j, key_ref[j]).astype(I32)
            return part + jnp.sum(hit.reshape(tk // 8, 8, tq), axis=0)

        part = lax.fori_loop(0, nkc, cnt_chunk, jnp.zeros((8, tq), I32))
        return jnp.sum(part, axis=0, keepdims=True)

    def bit_step(b, c):
        t_u, cnt_t = c
        bit = jnp.left_shift(jnp.int32(1), 31 - b)
        cand_s = (t_u | bit) ^ jnp.int32(INT_MIN)
        cnt = count_rows(lambda j, key: key >= cand_s)
        take = cnt >= topk
        return jnp.where(take, t_u | bit, t_u), jnp.where(take, cnt, cnt_t)

    n_causal = i * tq + lax.broadcasted_iota(I32, (1, tq), 1) + 1
    t_u, cnt_t = lax.fori_loop(0, 32, bit_step, (jnp.zeros((1, tq), I32), n_causal))
    thr = jnp.maximum(t_u ^ jnp.int32(INT_MIN), jnp.int32(INT_MIN + 1))

    @pl.when(jnp.max(cnt_t) > topk)
    def _():
        need = topk - count_rows(lambda j, key: key > thr)

        def idx_step(b, cut):
            cand = cut | jnp.left_shift(jnp.int32(1), idx_bits - 1 - b)
            cnt = count_rows(lambda j, key: (key == thr) & ((j * tk + k_off) < cand))
            return jnp.where(cnt <= need, cand, cut)

        cut = lax.fori_loop(0, idx_bits, idx_step, jnp.zeros((1, tq), I32))

        def demote(j, carry):
            key = key_ref[j]
            drop = (key == thr) & ((j * tk + k_off) >= cut)
            key_ref[j] = jnp.where(drop, key - 1, key)
            return carry

        lax.fori_loop(0, nkc, demote, 0)

    acc_ref[...] = jnp.zeros(acc_ref.shape, F32)

    def att_chunk(j, carry):
        m_old, l_old = carry
        kc = k_ref[pl.ds(pl.multiple_of(j * tk, tk), tk), :]
        sel = key_ref[j] >= thr
        cmax = []
        for h in range(A_HEADS):
            sl = slice(h * tq, (h + 1) * tq)
            s_h = jnp.dot(kc, qt_ref[0, :, sl], preferred_element_type=F32)
            s_h = jnp.where(sel, s_h, -jnp.inf)
            s_ref[:, sl] = s_h
            cmax.append(jnp.max(s_h, axis=0, keepdims=True))
        m_new = jnp.maximum(m_old, jnp.concatenate(cmax, axis=1))
        alpha = jnp.exp2(m_old - m_new)
        p = jnp.exp2(s_ref[...] - m_new)
        l_new = alpha * l_old + jnp.sum(p, axis=0, keepdims=True)
        pv = jnp.dot(vt_ref[j], p.astype(BF16), preferred_element_type=F32)
        acc_ref[...] = alpha * acc_ref[...] + pv
        return m_new, l_new

    stat0 = (jnp.full((1, A_HEADS * tq), -1e30, F32), jnp.zeros((1, A_HEADS * tq), F32))
    _, l_fin = lax.fori_loop(0, nkc, att_chunk, stat0)
    out_t = acc_ref[...] / l_fin
    for h in range(A_HEADS):
        o_ref[:, h * HEAD_DIM:(h + 1) * HEAD_DIM] = out_t[:, h * tq:(h + 1) * tq].T.astype(o_ref.dtype)


def _dsa(qit, qt, wt, ki, k, vt, batch, seq, tq, tk):
    n = batch * seq
    nqb = seq // tq
    nkb = seq // tk
    topk = min(TOPK_MAX, seq // 4)
    aw = A_HEADS * HEAD_DIM
    kern = functools.partial(_dsa_kernel, tq=tq, tk=tk, topk=topk, idx_bits=seq.bit_length())
    qblk = lambda b, i: (b * nqb + i, 0, 0)
    return pl.pallas_call(
        kern,
        grid=(batch, nqb),
        in_specs=[
            pl.BlockSpec((1, HEAD_DIM, IDX_HEADS * tq), qblk),
            pl.BlockSpec((1, HEAD_DIM, A_HEADS * tq), qblk),
            pl.BlockSpec((1, 1, IDX_HEADS * tq), qblk),
            pl.BlockSpec((seq, LANES), lambda b, i: (b, 0)),
            pl.BlockSpec((seq, LANES), lambda b, i: (b, 0)),
            pl.BlockSpec((nkb, HEAD_DIM, tk), lambda b, i: (b, 0, 0)),
        ],
        out_specs=pl.BlockSpec((tq, aw), lambda b, i: (b * nqb + i, 0)),
        out_shape=jax.ShapeDtypeStruct((n, aw), BF16),
        scratch_shapes=[
            pltpu.VMEM((nkb, tk, tq), I32),
            pltpu.VMEM((tk, A_HEADS * tq), F32),
            pltpu.VMEM((HEAD_DIM, A_HEADS * tq), F32),
        ],
        compiler_params=_cparams(("parallel", "arbitrary"), 48),
        name="dsa",
    )(qit, qt, wt, ki, k, vt)


HG_SUB = 16


def _hgrn_diag_exact(q, kk, v, lc, rsub, nsub):
    chunk, gw = q.shape

    def bcast(x, s):
        return jnp.concatenate([jnp.broadcast_to(x[a * HG_SUB + s:a * HG_SUB + s + 1, :], (HG_SUB, gw))
                                for a in range(nsub)], axis=0)

    acc = [jnp.zeros((chunk, HEAD_DIM), F32) for _ in range(G_HEADS)]
    for s in range(HG_SUB):
        term = q * bcast(kk, s) * jnp.exp(jnp.minimum(lc - bcast(lc, s), 0.0))
        term = jnp.where(rsub >= s, term, 0.0)
        vrow = bcast(v, s)
        for h in range(G_HEADS):
            sl = slice(h * HEAD_DIM, (h + 1) * HEAD_DIM)
            acc[h] = acc[h] + jnp.sum(term[:, sl], axis=1, keepdims=True) * vrow[:, sl]
    return jnp.concatenate(acc, axis=1)


def _hgrn_gates(hq_ref, hf_ref, hi_ref, lb_ref):
    chunk, gw = hq_ref.shape
    lb = lb_ref[...]
    f = lb + (1.0 - lb) * _sigmoid(hf_ref[...])
    kk = 1.0 - f
    hq = hq_ref[...].astype(F32)
    q = hq * _sigmoid(hq)
    v = hi_ref[...].astype(F32)
    rsub = lax.broadcasted_iota(I32, (chunk, gw), 0) & (HG_SUB - 1)
    lc = jnp.log(f)
    sh = 1
    while sh < HG_SUB:
        lc = lc + jnp.where(rsub >= sh, pltpu.roll(lc, sh, 0), 0.0)
        sh *= 2
    return q, kk, v, lc, rsub


def _hgrn_diag_scores(q, kk, lc, h):
    chunk = q.shape[0]
    sl = slice(h * HEAD_DIM, (h + 1) * HEAD_DIM)
    qd = (q[:, sl] * jnp.exp(lc[:, sl])).astype(BF16)
    kd = (kk[:, sl] * jnp.exp(jnp.minimum(-lc[:, sl], EXP_CLAMP))).astype(BF16)
    ti = lax.broadcasted_iota(I32, (chunk, chunk), 0)
    si = lax.broadcasted_iota(I32, (chunk, chunk), 1)
    diag_ok = (ti // HG_SUB == si // HG_SUB) & (si <= ti)
    return jnp.where(diag_ok, _dot_nt(qd, kd), 0.0).astype(BF16)


def _hgrn_kernel(need_ref, hq_ref, hf_ref, hi_ref, hg_ref, lb_ref, gn_ref, o_ref, st_ref, oa_ref, *, chunk):
    nsub = chunk // HG_SUB
    gw = hq_ref.shape[1]

    @pl.when(pl.program_id(1) == 0)
    def _():
        st_ref[...] = jnp.zeros(st_ref.shape, F32)

    q, kk, v, lc, rsub = _hgrn_gates(hq_ref, hf_ref, hi_ref, lb_ref)
    sub = lax.broadcasted_iota(I32, (chunk, gw), 0) // HG_SUB
    tots = [lc[(a + 1) * HG_SUB - 1:(a + 1) * HG_SUB, :] for a in range(nsub)]
    ends = []
    run = jnp.zeros((1, gw), F32)
    for a in range(nsub):
        run = run + tots[a]
        ends.append(run)
    cum_l = ends[-1]
    start_full = jnp.zeros((chunk, gw), F32)
    tot_full = jnp.zeros((chunk, gw), F32)
    end_full = jnp.zeros((chunk, gw), F32)
    for a in range(nsub):
        ina = sub == a
        if a > 0:
            start_full = jnp.where(ina, ends[a - 1], start_full)
        tot_full = jnp.where(ina, tots[a], tot_full)
        end_full = jnp.where(ina, ends[a], end_full)
    cum = lc + start_full

    qs = (q * jnp.exp(cum)).astype(BF16)
    koff = kk * jnp.exp(tot_full - lc)
    ke = (koff * jnp.exp(cum_l - end_full)).astype(BF16)
    koff_b = koff.astype(BF16)
    zero_b = jnp.zeros((chunk, gw), BF16)
    qoff = []
    kmask = []
    for b in range(nsub - 1):
        qb = q * jnp.exp(jnp.minimum(cum - ends[b], 0.0))
        qoff.append(jnp.where(sub > b, qb, 0.0).astype(BF16))
        kmask.append(jnp.where(sub == b, koff_b, zero_b))
    vb = v.astype(BF16)
    decay_l = jnp.exp(cum_l)

    for h in range(G_HEADS):
        sl = slice(h * HEAD_DIM, (h + 1) * HEAD_DIM)
        qcat = jnp.concatenate([qo[:, sl] for qo in qoff], axis=1)
        kcat = jnp.concatenate([km[:, sl] for km in kmask], axis=1)
        a = _hgrn_diag_scores(q, kk, lc, h) + _dot_nt(qcat, kcat).astype(BF16)
        st = st_ref[h]
        oa_ref[:, sl] = jnp.dot(a, vb[:, sl], preferred_element_type=F32) + _dot_nt(qs[:, sl], st.astype(BF16))
        upd = jnp.dot(v[:, sl].T.astype(BF16), ke[:, sl], preferred_element_type=F32)
        st_ref[h] = st * decay_l[:, sl] + upd

    @pl.when(need_ref[0] != 0)
    def _():
        q2, kk2, v2, lc2, rsub2 = _hgrn_gates(hq_ref, hf_ref, hi_ref, lb_ref)

        @pl.when(jnp.max(-lc2) > EXP_CLAMP)
        def _():
            v2b = v2.astype(BF16)
            fast = [jnp.dot(_hgrn_diag_scores(q2, kk2, lc2, h), v2b[:, h * HEAD_DIM:(h + 1) * HEAD_DIM],
                            preferred_element_type=F32) for h in range(G_HEADS)]
            oa_ref[...] += _hgrn_diag_exact(q2, kk2, v2, lc2, rsub2, nsub) - jnp.concatenate(fast, axis=1)

    o = oa_ref[...]
    ms = jnp.mean(o * o, axis=-1, keepdims=True)
    hg = hg_ref[...].astype(F32)
    y = o * lax.rsqrt(ms + EPS) * gn_ref[...] * (hg * _sigmoid(hg))
    o_ref[...] = y.astype(o_ref.dtype)


def _hgrn(ph, pt, lb, g_hnorm, batch, seq, chunk):
    n = batch * seq
    gw = G_HEADS * HEAD_DIM
    ncb = seq // chunk
    blk = lambda c: pl.BlockSpec((chunk, gw), lambda b, i, nd: (b * ncb + i, c))
    need = (HG_SUB * jnp.log(jnp.min(lb)) < -EXP_CLAMP).astype(I32).reshape(1)
    grid_spec = pltpu.PrefetchScalarGridSpec(
        num_scalar_prefetch=1,
        grid=(batch, ncb),
        in_specs=[blk(0), blk(1), blk(1), blk(2),
                  pl.BlockSpec((1, gw), lambda b, i, nd: (0, 0)),
                  pl.BlockSpec((1, gw), lambda b, i, nd: (0, 0))],
        out_specs=pl.BlockSpec((chunk, gw), lambda b, i, nd: (b * ncb + i, 0)),
        scratch_shapes=[pltpu.VMEM((G_HEADS, HEAD_DIM, HEAD_DIM), F32),
                        pltpu.VMEM((chunk, gw), F32)],
    )
    return pl.pallas_call(
        functools.partial(_hgrn_kernel, chunk=chunk),
        grid_spec=grid_spec,
        out_shape=jax.ShapeDtypeStruct((n, gw), BF16),
        compiler_params=_cparams(("parallel", "arbitrary"), 32),
        name="hgrn",
    )(need, pt, ph, pt, pt, lb.reshape(1, gw), g_hnorm.reshape(1, gw))


def _merge_kernel(ya_ref, yr_ref, ga_ref, gb_ref, x_ref, wpa_ref, wpb_ref, wo_ref, o_ref, acc_ref):
    c = pl.program_id(1)

    @pl.when(c == 0)
    def _():
        acc_ref[...] = x_ref[...]

    ma = jnp.dot(ya_ref[...], wpa_ref[...], preferred_element_type=F32)
    mb = jnp.dot(yr_ref[...], wpb_ref[...], preferred_element_type=F32)
    merged = _sigmoid(ga_ref[...].astype(F32)) * ma + _sigmoid(gb_ref[...].astype(F32)) * mb
    acc_ref[...] += jnp.dot(merged.astype(BF16), wo_ref[...], preferred_element_type=F32)

    @pl.when(c == pl.num_programs(1) - 1)
    def _():
        o_ref[...] = acc_ref[...]


def _merge(y_att, y_rec, pt, x2, w_pa, w_pb, w_o, tm, tc):
    n, d = x2.shape
    aw = y_att.shape[1]
    gw = y_rec.shape[1]
    ga0 = 3072 // tc
    gb0 = 5120 // tc
    return pl.pallas_call(
        _merge_kernel,
        grid=(n // tm, d // tc),
        in_specs=[
            pl.BlockSpec((tm, aw), lambda i, c: (i, 0)),
            pl.BlockSpec((tm, gw), lambda i, c: (i, 0)),
            pl.BlockSpec((tm, tc), lambda i, c: (i, ga0 + c)),
            pl.BlockSpec((tm, tc), lambda i, c: (i, gb0 + c)),
            pl.BlockSpec((tm, d), lambda i, c: (i, 0)),
            pl.BlockSpec((aw, tc), lambda i, c: (0, c)),
            pl.BlockSpec((gw, tc), lambda i, c: (0, c)),
            pl.BlockSpec((tc, d), lambda i, c: (c, 0)),
        ],
        out_specs=pl.BlockSpec((tm, d), lambda i, c: (i, 0)),
        out_shape=jax.ShapeDtypeStruct((n, d), F32),
        scratch_shapes=[pltpu.VMEM((tm, d), F32)],
        compiler_params=_cparams(("parallel", "arbitrary"), 48),
        name="merge",
    )(y_att, y_rec, pt, pt, x2, w_pa, w_pb, w_o)


def _router_kernel(x_ref, g_ref, wr_ref, br_ref, h_ref, r_ref, cnt_ref):
    x = x_ref[...]
    ms = jnp.mean(x * x, axis=-1, keepdims=True)
    h = x * lax.rsqrt(ms + EPS) * g_ref[...]
    h_ref[...] = h
    lg = jnp.dot(h.astype(BF16), wr_ref[...], preferred_element_type=F32) + br_ref[...]
    tm = lg.shape[0]
    lane = lax.broadcasted_iota(I32, (tm, LANES), 1)
    neg = jnp.float32(-1e30)
    big = jnp.float32(LANES)
    is_g = lane < N_GROUPS
    lgg = jnp.where(is_g, lg, neg)
    mg = jnp.max(lgg, axis=1, keepdims=True)
    zg = jnp.sum(jnp.where(is_g, jnp.exp(lgg - mg), 0.0), axis=1, keepdims=True)
    p_g = 1.0 / zg
    grp = jnp.min(jnp.where(is_g & (lgg == mg), lane.astype(F32), big), axis=1, keepdims=True)
    e_id = lane - N_GROUPS
    e_grp = lax.shift_right_arithmetic(e_id, jnp.int32(EXP_PER_GROUP.bit_length() - 1)).astype(F32)
    e_idf = e_id.astype(F32)
    is_e = (e_id >= 0) & (e_id < N_EXPERTS) & (e_grp == grp)
    lge = jnp.where(is_e, lg, neg)
    m1 = jnp.max(lge, axis=1, keepdims=True)
    i1 = jnp.min(jnp.where(is_e & (lge == m1), e_idf, big), axis=1, keepdims=True)
    is_e2 = is_e & (e_idf != i1)
    lge2 = jnp.where(is_e2, lg, neg)
    m2 = jnp.max(lge2, axis=1, keepdims=True)
    i2 = jnp.min(jnp.where(is_e2 & (lge2 == m2), e_idf, big), axis=1, keepdims=True)
    ex2 = jnp.exp(m2 - m1)
    den = 1.0 + ex2
    g1 = p_g / den
    g2 = p_g * ex2 / den
    @pl.when(pl.program_id(0) == 0)
    def _():
        cnt_ref[...] = jnp.zeros(cnt_ref.shape, F32)

    lanef = lane.astype(F32)
    oh1 = lanef == i1
    oh2 = lanef == i2
    oh = jnp.where(oh1 | oh2, 1.0, 0.0)
    ti = lax.broadcasted_iota(I32, (tm, tm), 0)
    si = lax.broadcasted_iota(I32, (tm, tm), 1)
    tri = jnp.where(si < ti, 1.0, 0.0).astype(BF16)
    before = jnp.dot(tri, oh.astype(BF16), preferred_element_type=F32) + cnt_ref[...]
    rank1 = jnp.sum(jnp.where(oh1, before, 0.0), axis=1, keepdims=True)
    rank2 = jnp.sum(jnp.where(oh2, before, 0.0), axis=1, keepdims=True)
    cnt_ref[...] += jnp.sum(oh, axis=0, keepdims=True)

    r = jnp.where(lane == 0, i1, 0.0)
    r = jnp.where(lane == 1, i2, r)
    r = jnp.where(lane == 2, g1, r)
    r = jnp.where(lane == 3, g2, r)
    r = jnp.where(lane == 4, rank1, r)
    r = jnp.where(lane == 5, rank2, r)
    r_ref[...] = r


def _router(x1, g, w_r, b_r, tm):
    n, d = x1.shape
    return pl.pallas_call(
        _router_kernel,
        grid=(n // tm,),
        in_specs=[
            pl.BlockSpec((tm, d), lambda i: (i, 0)),
            pl.BlockSpec((1, d), lambda i: (0, 0)),
            pl.BlockSpec((d, LANES), lambda i: (0, 0)),
            pl.BlockSpec((1, LANES), lambda i: (0, 0)),
        ],
        out_specs=[pl.BlockSpec((tm, d), lambda i: (i, 0)),
                   pl.BlockSpec((tm, LANES), lambda i: (i, 0)),
                   pl.BlockSpec((1, LANES), lambda i: (0, 0))],
        out_shape=[jax.ShapeDtypeStruct((n, d), F32),
                   jax.ShapeDtypeStruct((n, LANES), F32),
                   jax.ShapeDtypeStruct((1, LANES), F32)],
        compiler_params=_cparams(("arbitrary",), 40),
        name="router",
    )(x1, g.reshape(1, d), w_r, b_r)


def _dispatch_kernel(prow_ref, h_ref, xs_hbm, sem, *, tm):
    base = 2 * tm * pl.program_id(0)

    def scatter_row(r, carry):
        for s in range(2):
            pltpu.make_async_copy(h_ref.at[pl.ds(r, 1), :],
                                  xs_hbm.at[pl.ds(prow_ref[base + 2 * r + s], 1), :], sem).start()
        return carry

    lax.fori_loop(0, tm, scatter_row, 0, unroll=16)
    for _ in range(2):
        pltpu.make_async_copy(h_ref, xs_hbm.at[pl.ds(0, tm), :], sem).wait()


def _dispatch(h2, prow, tm):
    n, d = h2.shape
    grid_spec = pltpu.PrefetchScalarGridSpec(
        num_scalar_prefetch=1,
        grid=(n // tm,),
        in_specs=[pl.BlockSpec((tm, d), lambda i, pr: (i, 0))],
        out_specs=pl.BlockSpec(memory_space=pl.ANY),
        scratch_shapes=[pltpu.SemaphoreType.DMA(())],
    )
    return pl.pallas_call(
        functools.partial(_dispatch_kernel, tm=tm),
        grid_spec=grid_spec,
        out_shape=jax.ShapeDtypeStruct((2 * n, d), F32),
        compiler_params=_cparams(("arbitrary",), 32),
        name="dispatch",
    )(prow, h2)


def _moe_kernel(pt_ref, pe_ref, plo_ref, phi_ref, slot_ref, nxt_ref, np_ref, x_ref, wg_hbm, wu_hbm, wd_hbm,
                y_ref, wgf_ref, wuf_ref, wdf_ref, wgb_ref, wub_ref, wdb_ref, sem):
    p = pl.program_id(0)
    active = p < np_ref[0]
    prev = jnp.maximum(p - 1, 0)
    new_expert = (p == 0) | (pe_ref[p] != pe_ref[prev])
    new_tile = (p == 0) | (pt_ref[p] != pt_ref[prev])
    slot = slot_ref[p]

    def weight_copies(e, s):
        return (pltpu.make_async_copy(wg_hbm.at[e], wgf_ref.at[s], sem.at[s]),
                pltpu.make_async_copy(wu_hbm.at[e], wuf_ref.at[s], sem.at[s]),
                pltpu.make_async_copy(wd_hbm.at[e], wdf_ref.at[s], sem.at[s]))

    @pl.when(p == 0)
    def _():
        for cp in weight_copies(pe_ref[0], 0):
            cp.start()

    @pl.when(active & new_expert)
    def _():
        for cp in weight_copies(pe_ref[p], slot):
            cp.wait()
        wgb_ref[...] = wgf_ref[slot].astype(BF16)
        wub_ref[...] = wuf_ref[slot].astype(BF16)
        wdb_ref[...] = wdf_ref[slot].astype(BF16)

        @pl.when(nxt_ref[p] >= 0)
        def _():
            for cp in weight_copies(nxt_ref[p], 1 - slot):
                cp.start()

    @pl.when(active)
    def _():
        tm = x_ref.shape[0]
        xb = x_ref[...].astype(BF16)
        a = jnp.dot(xb, wgb_ref[...], preferred_element_type=F32)
        u = jnp.dot(xb, wub_ref[...], preferred_element_type=F32)
        hmid = (a * _sigmoid(a) * u).astype(BF16)
        yv = jnp.dot(hmid, wdb_ref[...], preferred_element_type=F32)
        row = lax.broadcasted_iota(I32, (tm, 1), 0)
        mine = (row >= plo_ref[p]) & (row < phi_ref[p])
        yv = jnp.where(mine, yv, 0.0)

        @pl.when(new_tile)
        def _():
            y_ref[...] = yv

        @pl.when(jnp.logical_not(new_tile))
        def _():
            y_ref[...] += yv


def _moe(xs, pairs, w_gate, w_up, w_down, tm):
    pair_tile, pair_expert, pair_lo, pair_hi, pair_slot, pair_next, n_pairs = pairs
    na, d = xs.shape
    de = w_gate.shape[2]
    max_pairs = pair_tile.shape[0]
    grid_spec = pltpu.PrefetchScalarGridSpec(
        num_scalar_prefetch=7,
        grid=(max_pairs,),
        in_specs=[
            pl.BlockSpec((tm, d), lambda p, pt, *_: (pt[p], 0)),
            pl.BlockSpec(memory_space=pl.ANY),
            pl.BlockSpec(memory_space=pl.ANY),
            pl.BlockSpec(memory_space=pl.ANY),
        ],
        out_specs=pl.BlockSpec((tm, d), lambda p, pt, *_: (pt[p], 0)),
        scratch_shapes=[
            pltpu.VMEM((2, d, de), F32),
            pltpu.VMEM((2, d, de), F32),
            pltpu.VMEM((2, de, d), F32),
            pltpu.VMEM((d, de), BF16),
            pltpu.VMEM((d, de), BF16),
            pltpu.VMEM((de, d), BF16),
            pltpu.SemaphoreType.DMA((2,)),
        ],
    )
    return pl.pallas_call(
        _moe_kernel,
        grid_spec=grid_spec,
        out_shape=jax.ShapeDtypeStruct((na, d), F32),
        compiler_params=_cparams(("arbitrary",), 56),
        name="moe",
    )(pair_tile, pair_expert, pair_lo, pair_hi, pair_slot, pair_next, n_pairs, xs, w_gate, w_up, w_down)


def _moe_rows(r, cnt):
    sizes = cnt[0, :N_EXPERTS].astype(I32)
    starts = jnp.cumsum(sizes) - sizes
    e12 = r[:, 0:2].astype(I32)
    rank = r[:, 4:6].astype(I32)
    onehot = e12[:, :, None] == jnp.arange(N_EXPERTS, dtype=I32)[None, None, :]
    start_a = jnp.sum(jnp.where(onehot, starts[None, None, :], 0), axis=-1)
    return (start_a + rank).reshape(-1).astype(I32), sizes


def _moe_pairs(sizes, na, tm):
    ntiles = na // tm
    max_pairs = ntiles + N_EXPERTS - 1
    ends = jnp.cumsum(sizes)
    starts = ends - sizes
    t0 = jnp.arange(ntiles, dtype=I32) * tm
    e_lo = jnp.searchsorted(ends, t0, side="right").astype(I32)
    e_hi = jnp.searchsorted(ends, t0 + (tm - 1), side="right").astype(I32)
    per_tile = e_hi - e_lo + 1
    pend = jnp.cumsum(per_tile)
    pstart = pend - per_tile
    n_pairs = pend[-1]
    pc = jnp.minimum(jnp.arange(max_pairs, dtype=I32), n_pairs - 1)
    tile = jnp.searchsorted(pend, pc, side="right").astype(I32)
    expert = e_lo[tile] + (pc - pstart[tile])
    lo = jnp.clip(starts[expert] - tile * tm, 0, tm).astype(I32)
    hi = jnp.clip(ends[expert] - tile * tm, 0, tm).astype(I32)
    expert = expert.astype(I32)
    first = jnp.concatenate([jnp.ones((1,), bool), expert[1:] != expert[:-1]])
    slot = ((jnp.cumsum(first) - 1) % 2).astype(I32)
    idx = jnp.arange(max_pairs, dtype=I32)
    first_at = jnp.where(first, idx, max_pairs)
    nxt_first = jnp.concatenate([lax.cummin(first_at[::-1])[::-1][1:], jnp.full((1,), max_pairs, I32)])
    nxt = jnp.where(nxt_first < max_pairs, expert[jnp.minimum(nxt_first, max_pairs - 1)], -1).astype(I32)
    return tile, expert, lo, hi, slot, nxt, n_pairs.astype(I32).reshape(1)


def _final_kernel(prow_ref, x_ref, r_ref, g_ref, y_hbm, o_ref, ybuf, sem, *, tm):
    i = pl.program_id(0)
    cur = i % 2

    def issue(step, buf):
        base = 2 * tm * step

        def gather_row(t, carry):
            for s in range(2):
                pltpu.make_async_copy(y_hbm.at[pl.ds(prow_ref[base + 2 * t + s], 1), :],
                                      ybuf.at[buf, s, pl.ds(t, 1), :], sem.at[buf]).start()
            return carry

        lax.fori_loop(0, tm, gather_row, 0, unroll=16)

    @pl.when(i == 0)
    def _():
        issue(0, 0)

    @pl.when(i + 1 < pl.num_programs(0))
    def _():
        issue(i + 1, 1 - cur)

    for s in range(2):
        pltpu.make_async_copy(y_hbm.at[pl.ds(0, tm), :], ybuf.at[cur, s], sem.at[cur]).wait()
    r = r_ref[...]
    g1 = r[:, 2:3]
    g2 = r[:, 3:4]
    x = x_ref[...] + g1 * ybuf[cur, 0] + g2 * ybuf[cur, 1]
    ms = jnp.mean(x * x, axis=-1, keepdims=True)
    o_ref[...] = x * lax.rsqrt(ms + EPS) * g_ref[...]


def _final(x1, y, r, prow, g, tm):
    n, d = x1.shape
    grid_spec = pltpu.PrefetchScalarGridSpec(
        num_scalar_prefetch=1,
        grid=(n // tm,),
        in_specs=[
            pl.BlockSpec((tm, d), lambda i, pr: (i, 0)),
            pl.BlockSpec((tm, LANES), lambda i, pr: (i, 0)),
            pl.BlockSpec((1, d), lambda i, pr: (0, 0)),
            pl.BlockSpec(memory_space=pl.ANY),
        ],
        out_specs=pl.BlockSpec((tm, d), lambda i, pr: (i, 0)),
        scratch_shapes=[pltpu.VMEM((2, 2, tm, d), F32), pltpu.SemaphoreType.DMA((2,))],
    )
    return pl.pallas_call(
        functools.partial(_final_kernel, tm=tm),
        grid_spec=grid_spec,
        out_shape=jax.ShapeDtypeStruct((n, d), F32),
        compiler_params=_cparams(("arbitrary",), 48),
        name="final",
    )(prow, x1, r, g.reshape(1, d), y)


def _pack_w_in(w):
    d, din = w.shape
    tr = 256

    def pack_kernel(w_ref, o_ref):
        o_ref[:, 0:912] = w_ref[:, 0:912].astype(BF16)
        o_ref[:, 912:1024] = jnp.zeros((tr, LANES - IDX_HEADS), BF16)
        o_ref[:, 1024:2048] = w_ref[:, 1936:2960].astype(BF16)
        o_ref[:, 2048:3072] = w_ref[:, 912:1936].astype(BF16)
        o_ref[:, 3072:] = w_ref[:, 2960:].astype(BF16)

    return pl.pallas_call(
        pack_kernel,
        grid=(d // tr,),
        in_specs=[pl.BlockSpec((tr, din), lambda i: (i, 0))],
        out_specs=pl.BlockSpec((tr, PH_COLS + PT_COLS), lambda i: (i, 0)),
        out_shape=jax.ShapeDtypeStruct((d, PH_COLS + PT_COLS), BF16),
        compiler_params=_cparams(("parallel",), 48),
        name="pack_w_in",
    )(w)


def _rope_tables(seq):
    half = HEAD_DIM // 2
    inv = 1.0 / (ROPE_THETA ** (jnp.arange(0, HEAD_DIM, 2, dtype=F32) / HEAD_DIM))
    ang = jnp.arange(seq, dtype=F32)[:, None] * inv[None, :]
    c, s = jnp.cos(ang), jnp.sin(ang)
    del half
    return jnp.concatenate([c, c], axis=1), jnp.concatenate([-s, s], axis=1)


def _layer(x2, batch, seq, cos2, sin2, lb, g_norm1, w_in, g_cq, w_uq, w_qidx, g_hnorm, w_pa, w_pb, w_o,
           g_norm2, w_grp, b_grp, w_exp, b_exp, w_gate, w_up, w_down, g_out):
    n, d = x2.shape
    ph, pt = _in_proj(x2, g_norm1, _pack_w_in(w_in), tm=min(1024, n), tn=512)
    tq, tk = min(256, seq), min(256, seq)
    qt, qit, wt, k, ki, vt = _dsa_prep(ph, cos2, sin2, g_cq, w_uq.astype(BF16), w_qidx.astype(BF16), seq, tm=tk, tq=tq)
    y_att = _dsa(qit, qt, wt, ki, k, vt, batch, seq, tq=tq, tk=tk)
    y_rec = _hgrn(ph, pt, lb, g_hnorm, batch, seq, chunk=64)
    x1 = _merge(y_att, y_rec, pt, x2, w_pa.astype(BF16), w_pb.astype(BF16), w_o.astype(BF16),
                tm=min(512, n), tc=512)
    w_r = jnp.concatenate([w_grp, w_exp, jnp.zeros((d, LANES - N_GROUPS - N_EXPERTS), F32)], axis=1).astype(BF16)
    b_r = jnp.concatenate([b_grp, b_exp, jnp.zeros((LANES - N_GROUPS - N_EXPERTS,), F32)]).reshape(1, LANES)
    h2, r, cnt = _router(x1, g_norm2, w_r, b_r, tm=min(512, n))
    prow, sizes = _moe_rows(r, cnt)
    xs = _dispatch(h2, prow, tm=min(512, n))
    tm_moe = 256
    assert (2 * n) % tm_moe == 0
    y = _moe(xs, _moe_pairs(sizes, 2 * n, tm_moe), w_gate, w_up, w_down, tm_moe)
    return _final(x1, y, r, prow, g_out, tm=min(512, n))


def kernel(x, g_norm1, w_in, g_cq, w_uq, w_qidx, lb_logits, g_hnorm, w_pa, w_pb, w_o, g_norm2, w_grp, b_grp,
           w_exp, b_exp, w_gate, w_up, w_down, g_final):
    batch, seq, d = x.shape
    depth = g_norm1.shape[0]
    cos2, sin2 = _rope_tables(seq)
    lb_all = jnp.cumsum(jax.nn.softmax(lb_logits.astype(F32), axis=0), axis=0)
    x2 = x.reshape(batch * seq, d)
    for l in range(depth):
        assert depth == 1
        x2 = _layer(x2, batch, seq, cos2, sin2, lb_all[l], g_norm1[l], w_in[l], g_cq[l], w_uq[l], w_qidx[l],
                    g_hnorm[l], w_pa[l], w_pb[l], w_o[l], g_norm2[l], w_grp[l], b_grp[l], w_exp[l], b_exp[l],
                    w_gate[l], w_up[l], w_down[l], g_final)
    return x2.reshape(batch, seq, d)
```

```python
import functools
import math

import jax
import jax.numpy as jnp
from jax import lax
from jax.experimental import pallas as pl
from jax.experimental.pallas import tpu as pltpu

F32 = jnp.float32
BF16 = jnp.bfloat16
I32 = jnp.int32

EPS = 1e-6
ROPE_THETA = 10000.0
LANES = 128
MXU_LANES = 256

A_HEADS = 8
HEAD_DIM = 128
IDX_HEADS = 16
Q_RANK = 512
TOPK_MAX = 256
G_HEADS = 8
N_GROUPS = 4
EXP_PER_GROUP = 8
N_EXPERTS = N_GROUPS * EXP_PER_GROUP
D_EXPERT = 512

INT_MIN = -2147483648
EXP_CLAMP = 80.0


def _cparams(sem, vmem_mb):
    return pltpu.CompilerParams(dimension_semantics=sem, vmem_limit_bytes=vmem_mb << 20)


def _sigmoid(x):
    return 1.0 / (1.0 + jnp.exp(-x))


PH_COLS = 2048
PT_COLS = 7168


def _inproj_kernel(x_ref, g_ref, w_ref, oh_ref, ot_ref, xn_ref, *, nh):
    j = pl.program_id(1)

    @pl.when(j == 0)
    def _():
        x = x_ref[...]
        ms = jnp.mean(x * x, axis=-1, keepdims=True)
        xn_ref[...] = (x * lax.rsqrt(ms + EPS) * g_ref[...]).astype(BF16)

    y = _dot_nt(xn_ref[...], w_ref[...])

    @pl.when(j < nh)
    def _():
        oh_ref[...] = y

    @pl.when(j >= nh)
    def _():
        ot_ref[...] = y.astype(ot_ref.dtype)


def _in_proj(x2, g, w, tm, tn):
    n, d = x2.shape
    nh = PH_COLS // tn
    nt = PT_COLS // tn
    return pl.pallas_call(
        functools.partial(_inproj_kernel, nh=nh),
        grid=(n // tm, nh + nt),
        in_specs=[
            pl.BlockSpec((tm, d), lambda i, j: (i, 0)),
            pl.BlockSpec((1, d), lambda i, j: (0, 0)),
            pl.BlockSpec((tn, d), lambda i, j: (j, 0)),
        ],
        out_specs=[pl.BlockSpec((tm, tn), lambda i, j: (i, jnp.minimum(j, nh - 1))),
                   pl.BlockSpec((tm, tn), lambda i, j: (i, jnp.maximum(j - nh, 0)))],
        out_shape=[jax.ShapeDtypeStruct((n, PH_COLS), F32),
                   jax.ShapeDtypeStruct((n, PT_COLS), BF16)],
        scratch_shapes=[pltpu.VMEM((tm, d), BF16)],
        compiler_params=_cparams(("parallel", "arbitrary"), 48),
        name="in_proj",
    )(x2, g.reshape(1, d), w)


def _rope(x, c2, s2):
    return x * c2 + pltpu.roll(x, HEAD_DIM // 2, 1) * s2


def _prep_kernel(cq_ref, ka_ref, va_ref, kx_ref, wi_ref, cos_ref, sin_ref, g_ref, wuq_ref, wqi_ref,
                 qt_ref, qit_ref, wt_ref, k_ref, ki_ref, vt_ref, *, scale, wscale, tq):
    nqb = cq_ref.shape[0] // tq
    c2 = cos_ref[...]
    s2 = sin_ref[...]
    cq = cq_ref[...]
    ms = jnp.mean(cq * cq, axis=-1, keepdims=True)
    cqn = (cq * lax.rsqrt(ms + EPS) * g_ref[...]).astype(BF16)
    q = jnp.dot(cqn, wuq_ref[...], preferred_element_type=F32)
    for h in range(A_HEADS):
        sl = slice(h * HEAD_DIM, (h + 1) * HEAD_DIM)
        qh_t = (_rope(q[:, sl], c2, s2) * scale).T
        for b in range(nqb):
            qt_ref[b, :, h * tq:(h + 1) * tq] = qh_t[:, b * tq:(b + 1) * tq].astype(BF16)
    qi = jnp.dot(cqn, wqi_ref[...], preferred_element_type=F32)
    for h in range(IDX_HEADS):
        sl = slice(h * HEAD_DIM, (h + 1) * HEAD_DIM)
        qh_t = _rope(qi[:, sl], c2, s2).T
        for b in range(nqb):
            qit_ref[b, :, h * tq:(h + 1) * tq] = qh_t[:, b * tq:(b + 1) * tq].astype(BF16)
    w_t = (wi_ref[...] * wscale).T
    for h in range(IDX_HEADS):
        for b in range(nqb):
            wt_ref[b, :, h * tq:(h + 1) * tq] = w_t[h:h + 1, b * tq:(b + 1) * tq]
    k_ref[...] = _rope(ka_ref[...], c2, s2).astype(BF16)
    ki_ref[...] = _rope(kx_ref[...], c2, s2).astype(BF16)
    vt_ref[0] = va_ref[...].T.astype(BF16)


def _dsa_prep(p, cos2, sin2, g_cq, w_uq, w_qidx, seq, tm, tq):
    n = p.shape[0]
    nsb = seq // tm
    nqb = tm // tq
    aw = A_HEADS * HEAD_DIM
    iw = IDX_HEADS * HEAD_DIM
    row = lambda i: (i, 0)
    blk3 = lambda i: (i, 0, 0)
    kern = functools.partial(_prep_kernel, scale=HEAD_DIM ** -0.5 * math.log2(math.e),
                             wscale=IDX_HEADS ** -0.5 * HEAD_DIM ** -0.5, tq=tq)
    return pl.pallas_call(
        kern,
        grid=(n // tm,),
        in_specs=[
            pl.BlockSpec((tm, Q_RANK), lambda i: (i, 0)),
            pl.BlockSpec((tm, LANES), lambda i: (i, 4)),
            pl.BlockSpec((tm, LANES), lambda i: (i, 5)),
            pl.BlockSpec((tm, LANES), lambda i: (i, 6)),
            pl.BlockSpec((tm, LANES), lambda i: (i, 7)),
            pl.BlockSpec((tm, LANES), lambda i: (i % nsb, 0)),
            pl.BlockSpec((tm, LANES), lambda i: (i % nsb, 0)),
            pl.BlockSpec((1, Q_RANK), lambda i: (0, 0)),
            pl.BlockSpec((Q_RANK, aw), lambda i: (0, 0)),
            pl.BlockSpec((Q_RANK, iw), lambda i: (0, 0)),
        ],
        out_specs=[
            pl.BlockSpec((nqb, HEAD_DIM, A_HEADS * tq), blk3),
            pl.BlockSpec((nqb, HEAD_DIM, IDX_HEADS * tq), blk3),
            pl.BlockSpec((nqb, 1, IDX_HEADS * tq), blk3),
            pl.BlockSpec((tm, LANES), row),
            pl.BlockSpec((tm, LANES), row),
            pl.BlockSpec((1, HEAD_DIM, tm), blk3),
        ],
        out_shape=[
            jax.ShapeDtypeStruct((n // tq, HEAD_DIM, A_HEADS * tq), BF16),
            jax.ShapeDtypeStruct((n // tq, HEAD_DIM, IDX_HEADS * tq), BF16),
            jax.ShapeDtypeStruct((n // tq, 1, IDX_HEADS * tq), F32),
            jax.ShapeDtypeStruct((n, LANES), BF16),
            jax.ShapeDtypeStruct((n, LANES), BF16),
            jax.ShapeDtypeStruct((n // tm, HEAD_DIM, tm), BF16),
        ],
        compiler_params=_cparams(("parallel",), 40),
        name="dsa_prep",
    )(p, p, p, p, p, cos2, sin2, g_cq.reshape(1, Q_RANK), w_uq, w_qidx)


def _dot_nt(a, b):
    return lax.dot_general(a, b, (((1,), (1,)), ((), ())), preferred_element_type=F32)


def _dsa_kernel(qit_ref, qt_ref, wt_ref, ki_ref, k_ref, vt_ref, o_ref,
                key_ref, s_ref, acc_ref, *, tq, tk, topk, idx_bits):
    i = pl.program_id(1)
    nkc = ((i + 1) * tq + tk - 1) // tk
    q_pos = i * tq + lax.broadcasted_iota(I32, (tk, tq), 1)
    k_off = lax.broadcasted_iota(I32, (tk, tq), 0)

    def score_chunk(j, carry):
        kic = ki_ref[pl.ds(pl.multiple_of(j * tk, tk), tk), :]
        acc = jnp.zeros((tk, tq), F32)
        hpm = max(1, MXU_LANES // tq)
        for h0 in range(0, IDX_HEADS, hpm):
            sl = slice(h0 * tq, (h0 + hpm) * tq)
            lg = jnp.dot(kic, qit_ref[0, :, sl], preferred_element_type=F32)
            x = jnp.maximum(lg, 0.0) * wt_ref[0, :, sl]
            for u in range(hpm):
                acc = acc + x[:, u * tq:(u + 1) * tq]
        acc = jnp.where(acc == 0.0, 0.0, acc)
        bits = lax.bitcast_convert_type(acc, I32)
        key = jnp.where(bits < 0, bits ^ jnp.int32(0x7FFFFFFF), bits)
        causal = (j * tk + k_off) <= q_pos
        key_ref[j] = jnp.where(causal, key, jnp.int32(INT_MIN))
        return carry

    lax.fori_loop(0, nkc, score_chunk, 0)

    def count_rows(pred):
        def cnt_chunk(j, part):
            hit = pred(j, key_ref[j]).astype(I32)
            return part + jnp.sum(hit.reshape(tk // 8, 8, tq), axis=0)

        part = lax.fori_loop(0, nkc, cnt_chunk, jnp.zeros((8, tq), I32))
        return jnp.sum(part, axis=0, keepdims=True)

    def bit_step(b, c):
        t_u, cnt_t = c
        bit = jnp.left_shift(jnp.int32(1), 31 - b)
        cand_s = (t_u | bit) ^ jnp.int32(INT_MIN)
        cnt = count_rows(lambda j, key: key >= cand_s)
        take = cnt >= topk
        return jnp.where(take, t_u | bit, t_u), jnp.where(take, cnt, cnt_t)

    n_causal = i * tq + lax.broadcasted_iota(I32, (1, tq), 1) + 1
    t_u, cnt_t = lax.fori_loop(0, 32, bit_step, (jnp.zeros((1, tq), I32), n_causal))
    thr = jnp.maximum(t_u ^ jnp.int32(INT_MIN), jnp.int32(INT_MIN + 1))

    @pl.when(jnp.max(cnt_t) > topk)
    def _():
        need = topk - count_rows(lambda j, key: key > thr)

        def idx_step(b, cut):
            cand = cut | jnp.left_shift(jnp.int32(1), idx_bits - 1 - b)
            cnt = count_rows(lambda j, key: (key == thr) & ((j * tk + k_off) < cand))
            return jnp.where(cnt <= need, cand, cut)

        cut = lax.fori_loop(0, idx_bits, idx_step, jnp.zeros((1, tq), I32))

        def demote(j, carry):
            key = key_ref[j]
            drop = (key == thr) & ((j * tk + k_off) >= cut)
            key_ref[j] = jnp.where(drop, key - 1, key)
            return carry

        lax.fori_loop(0, nkc, demote, 0)

    acc_ref[...] = jnp.zeros(acc_ref.shape, F32)

    def att_chunk(j, carry):
        m_old, l_old = carry
        kc = k_ref[pl.ds(pl.multiple_of(j * tk, tk), tk), :]
        sel = key_ref[j] >= thr
        cmax = []
        for h in range(A_HEADS):
            sl = slice(h * tq, (h + 1) * tq)
            s_h = jnp.dot(kc, qt_ref[0, :, sl], preferred_element_type=F32)
            s_h = jnp.where(sel, s_h, -jnp.inf)
            s_ref[:, sl] = s_h
            cmax.append(jnp.max(s_h, axis=0, keepdims=True))
        m_new = jnp.maximum(m_old, jnp.concatenate(cmax, axis=1))
        alpha = jnp.exp2(m_old - m_new)
        p = jnp.exp2(s_ref[...] - m_new)
        l_new = alpha * l_old + jnp.sum(p, axis=0, keepdims=True)
        pv = jnp.dot(vt_ref[j], p.astype(BF16), preferred_element_type=F32)
        acc_ref[...] = alpha * acc_ref[...] + pv
        return m_new, l_new

    stat0 = (jnp.full((1, A_HEADS * tq), -1e30, F32), jnp.zeros((1, A_HEADS * tq), F32))
    _, l_fin = lax.fori_loop(0, nkc, att_chunk, stat0)
    out_t = acc_ref[...] / l_fin
    for h in range(A_HEADS):
        o_ref[:, h * HEAD_DIM:(h + 1) * HEAD_DIM] = out_t[:, h * tq:(h + 1) * tq].T.astype(o_ref.dtype)


def _dsa(qit, qt, wt, ki, k, vt, batch, seq, tq, tk):
    n = batch * seq
    nqb = seq // tq
    nkb = seq // tk
    topk = min(TOPK_MAX, seq // 4)
    aw = A_HEADS * HEAD_DIM
    kern = functools.partial(_dsa_kernel, tq=tq, tk=tk, topk=topk, idx_bits=seq.bit_length())
    qblk = lambda b, i: (b * nqb + i, 0, 0)
    return pl.pallas_call(
        kern,
        grid=(batch, nqb),
        in_specs=[
            pl.BlockSpec((1, HEAD_DIM, IDX_HEADS * tq), qblk),
            pl.BlockSpec((1, HEAD_DIM, A_HEADS * tq), qblk),
            pl.BlockSpec((1, 1, IDX_HEADS * tq), qblk),
            pl.BlockSpec((seq, LANES), lambda b, i: (b, 0)),
            pl.BlockSpec((seq, LANES), lambda b, i: (b, 0)),
            pl.BlockSpec((nkb, HEAD_DIM, tk), lambda b, i: (b, 0, 0)),
        ],
        out_specs=pl.BlockSpec((tq, aw), lambda b, i: (b * nqb + i, 0)),
        out_shape=jax.ShapeDtypeStruct((n, aw), BF16),
        scratch_shapes=[
            pltpu.VMEM((nkb, tk, tq), I32),
            pltpu.VMEM((tk, A_HEADS * tq), F32),
            pltpu.VMEM((HEAD_DIM, A_HEADS * tq), F32),
        ],
        compiler_params=_cparams(("parallel", "arbitrary"), 48),
        name="dsa",
    )(qit, qt, wt, ki, k, vt)


HG_SUB = 16


def _hgrn_diag_exact(q, kk, v, lc, rsub, nsub):
    chunk, gw = q.shape

    def bcast(x, s):
        return jnp.concatenate([jnp.broadcast_to(x[a * HG_SUB + s:a * HG_SUB + s + 1, :], (HG_SUB, gw))
                                for a in range(nsub)], axis=0)

    acc = [jnp.zeros((chunk, HEAD_DIM), F32) for _ in range(G_HEADS)]
    for s in range(HG_SUB):
        term = q * bcast(kk, s) * jnp.exp(jnp.minimum(lc - bcast(lc, s), 0.0))
        term = jnp.where(rsub >= s, term, 0.0)
        vrow = bcast(v, s)
        for h in range(G_HEADS):
            sl = slice(h * HEAD_DIM, (h + 1) * HEAD_DIM)
            acc[h] = acc[h] + jnp.sum(term[:, sl], axis=1, keepdims=True) * vrow[:, sl]
    return jnp.concatenate(acc, axis=1)


def _hgrn_gates(hq_ref, hf_ref, hi_ref, lb_ref):
    chunk, gw = hq_ref.shape
    lb = lb_ref[...]
    f = lb + (1.0 - lb) * _sigmoid(hf_ref[...])
    kk = 1.0 - f
    hq = hq_ref[...].astype(F32)
    q = hq * _sigmoid(hq)
    v = hi_ref[...].astype(F32)
    rsub = lax.broadcasted_iota(I32, (chunk, gw), 0) & (HG_SUB - 1)
    lc = jnp.log(f)
    sh = 1
    while sh < HG_SUB:
        lc = lc + jnp.where(rsub >= sh, pltpu.roll(lc, sh, 0), 0.0)
        sh *= 2
    return q, kk, v, lc, rsub


def _hgrn_diag_scores(q, kk, lc, h):
    chunk = q.shape[0]
    sl = slice(h * HEAD_DIM, (h + 1) * HEAD_DIM)
    qd = (q[:, sl] * jnp.exp(lc[:, sl])).astype(BF16)
    kd = (kk[:, sl] * jnp.exp(jnp.minimum(-lc[:, sl], EXP_CLAMP))).astype(BF16)
    ti = lax.broadcasted_iota(I32, (chunk, chunk), 0)
    si = lax.broadcasted_iota(I32, (chunk, chunk), 1)
    diag_ok = (ti // HG_SUB == si // HG_SUB) & (si <= ti)
    return jnp.where(diag_ok, _dot_nt(qd, kd), 0.0).astype(BF16)


def _hgrn_kernel(need_ref, hq_ref, hf_ref, hi_ref, hg_ref, lb_ref, gn_ref, o_ref, st_ref, oa_ref, *, chunk):
    nsub = chunk // HG_SUB
    gw = hq_ref.shape[1]

    @pl.when(pl.program_id(1) == 0)
    def _():
        st_ref[...] = jnp.zeros(st_ref.shape, F32)

    q, kk, v, lc, rsub = _hgrn_gates(hq_ref, hf_ref, hi_ref, lb_ref)
    def per_sub(vals):
        return jnp.concatenate([jnp.broadcast_to(x, (HG_SUB, gw)) for x in vals], axis=0)

    tots = [lc[(a + 1) * HG_SUB - 1:(a + 1) * HG_SUB, :] for a in range(nsub)]
    ends = []
    run = jnp.zeros((1, gw), F32)
    for a in range(nsub):
        run = run + tots[a]
        ends.append(run)
    cum_l = ends[-1]
    cum = lc + per_sub([jnp.zeros((1, gw), F32)] + ends[:-1])

    qs = (q * jnp.exp(cum)).astype(BF16)
    koff = kk * jnp.exp(per_sub(tots) - lc)
    ke = (koff * per_sub([jnp.exp(cum_l - e) for e in ends])).astype(BF16)
    koff_b = koff.astype(BF16)
    qoff = []
    kmask = []
    for b in range(nsub - 1):
        r0 = (b + 1) * HG_SUB
        qb = (q[r0:, :] * jnp.exp(cum[r0:, :] - ends[b])).astype(BF16)
        qoff.append(jnp.concatenate([jnp.zeros((r0, gw), BF16), qb], axis=0))
        parts = [koff_b[b * HG_SUB:r0, :], jnp.zeros((chunk - r0, gw), BF16)]
        if b > 0:
            parts.insert(0, jnp.zeros((b * HG_SUB, gw), BF16))
        kmask.append(jnp.concatenate(parts, axis=0))
    vb = v.astype(BF16)
    decay_l = jnp.exp(cum_l)

    for h in range(G_HEADS):
        sl = slice(h * HEAD_DIM, (h + 1) * HEAD_DIM)
        qcat = jnp.concatenate([qo[:, sl] for qo in qoff], axis=1)
        kcat = jnp.concatenate([km[:, sl] for km in kmask], axis=1)
        a = _hgrn_diag_scores(q, kk, lc, h) + _dot_nt(qcat, kcat).astype(BF16)
        st = st_ref[h]
        oa_ref[:, sl] = jnp.dot(a, vb[:, sl], preferred_element_type=F32) + _dot_nt(qs[:, sl], st.astype(BF16))
        upd = jnp.dot(v[:, sl].T.astype(BF16), ke[:, sl], preferred_element_type=F32)
        st_ref[h] = st * decay_l[:, sl] + upd

    @pl.when(need_ref[0] != 0)
    def _():
        q2, kk2, v2, lc2, rsub2 = _hgrn_gates(hq_ref, hf_ref, hi_ref, lb_ref)

        @pl.when(jnp.max(-lc2) > EXP_CLAMP)
        def _():
            v2b = v2.astype(BF16)
            fast = [jnp.dot(_hgrn_diag_scores(q2, kk2, lc2, h), v2b[:, h * HEAD_DIM:(h + 1) * HEAD_DIM],
                            preferred_element_type=F32) for h in range(G_HEADS)]
            oa_ref[...] += _hgrn_diag_exact(q2, kk2, v2, lc2, rsub2, nsub) - jnp.concatenate(fast, axis=1)

    o = oa_ref[...]
    ms = jnp.mean(o * o, axis=-1, keepdims=True)
    hg = hg_ref[...].astype(F32)
    y = o * lax.rsqrt(ms + EPS) * gn_ref[...] * (hg * _sigmoid(hg))
    o_ref[...] = y.astype(o_ref.dtype)


def _hgrn(ph, pt, lb, g_hnorm, batch, seq, chunk):
    n = batch * seq
    gw = G_HEADS * HEAD_DIM
    ncb = seq // chunk
    blk = lambda c: pl.BlockSpec((chunk, gw), lambda b, i, nd: (b * ncb + i, c))
    need = (HG_SUB * jnp.log(jnp.min(lb)) < -EXP_CLAMP).astype(I32).reshape(1)
    grid_spec = pltpu.PrefetchScalarGridSpec(
        num_scalar_prefetch=1,
        grid=(batch, ncb),
        in_specs=[blk(0), blk(1), blk(1), blk(2),
                  pl.BlockSpec((1, gw), lambda b, i, nd: (0, 0)),
                  pl.BlockSpec((1, gw), lambda b, i, nd: (0, 0))],
        out_specs=pl.BlockSpec((chunk, gw), lambda b, i, nd: (b * ncb + i, 0)),
        scratch_shapes=[pltpu.VMEM((G_HEADS, HEAD_DIM, HEAD_DIM), F32),
                        pltpu.VMEM((chunk, gw), F32)],
    )
    return pl.pallas_call(
        functools.partial(_hgrn_kernel, chunk=chunk),
        grid_spec=grid_spec,
        out_shape=jax.ShapeDtypeStruct((n, gw), BF16),
        compiler_params=_cparams(("parallel", "arbitrary"), 32),
        name="hgrn",
    )(need, pt, ph, pt, pt, lb.reshape(1, gw), g_hnorm.reshape(1, gw))


def _merge_kernel(ya_ref, yr_ref, ga_ref, gb_ref, x_ref, wpa_ref, wpb_ref, wo_ref, o_ref, mg_ref):
    c = pl.program_id(1)
    nc = pl.num_programs(1)
    tc = ga_ref.shape[1]

    ma = jnp.dot(ya_ref[...], wpa_ref[...], preferred_element_type=F32)
    mb = jnp.dot(yr_ref[...], wpb_ref[...], preferred_element_type=F32)
    merged = _sigmoid(ga_ref[...].astype(F32)) * ma + _sigmoid(gb_ref[...].astype(F32)) * mb
    merged = merged.astype(BF16)
    for k in range(mg_ref.shape[1] // tc):
        @pl.when(c == k)
        def _():
            mg_ref[:, k * tc:(k + 1) * tc] = merged

    @pl.when(c == nc - 1)
    def _():
        o_ref[...] = x_ref[...] + jnp.dot(mg_ref[...], wo_ref[...], preferred_element_type=F32)


def _merge(y_att, y_rec, pt, x2, w_pa, w_pb, w_o, tm, tc):
    n, d = x2.shape
    aw = y_att.shape[1]
    gw = y_rec.shape[1]
    ga0 = 3072 // tc
    gb0 = 5120 // tc
    return pl.pallas_call(
        _merge_kernel,
        grid=(n // tm, d // tc),
        in_specs=[
            pl.BlockSpec((tm, aw), lambda i, c: (i, 0)),
            pl.BlockSpec((tm, gw), lambda i, c: (i, 0)),
            pl.BlockSpec((tm, tc), lambda i, c: (i, ga0 + c)),
            pl.BlockSpec((tm, tc), lambda i, c: (i, gb0 + c)),
            pl.BlockSpec((tm, d), lambda i, c: (i, 0)),
            pl.BlockSpec((aw, tc), lambda i, c: (0, c)),
            pl.BlockSpec((gw, tc), lambda i, c: (0, c)),
            pl.BlockSpec((d, d), lambda i, c: (0, 0)),
        ],
        out_specs=pl.BlockSpec((tm, d), lambda i, c: (i, 0)),
        out_shape=jax.ShapeDtypeStruct((n, d), F32),
        scratch_shapes=[pltpu.VMEM((tm, d), BF16)],
        compiler_params=_cparams(("parallel", "arbitrary"), 56),
        name="merge",
    )(y_att, y_rec, pt, pt, x2, w_pa, w_pb, w_o)


def _router_kernel(x_ref, g_ref, wr_ref, br_ref, h_ref, r_ref, cnt_ref):
    x = x_ref[...]
    ms = jnp.mean(x * x, axis=-1, keepdims=True)
    h = x * lax.rsqrt(ms + EPS) * g_ref[...]
    h_ref[...] = h
    lg = jnp.dot(h.astype(BF16), wr_ref[...], preferred_element_type=F32) + br_ref[...]
    tm = lg.shape[0]
    lane = lax.broadcasted_iota(I32, (tm, LANES), 1)
    neg = jnp.float32(-1e30)
    big = jnp.float32(LANES)
    is_g = lane < N_GROUPS
    lgg = jnp.where(is_g, lg, neg)
    mg = jnp.max(lgg, axis=1, keepdims=True)
    zg = jnp.sum(jnp.where(is_g, jnp.exp(lgg - mg), 0.0), axis=1, keepdims=True)
    p_g = 1.0 / zg
    grp = jnp.min(jnp.where(is_g & (lgg == mg), lane.astype(F32), big), axis=1, keepdims=True)
    e_id = lane - N_GROUPS
    e_grp = lax.shift_right_arithmetic(e_id, jnp.int32(EXP_PER_GROUP.bit_length() - 1)).astype(F32)
    e_idf = e_id.astype(F32)
    is_e = (e_id >= 0) & (e_id < N_EXPERTS) & (e_grp == grp)
    lge = jnp.where(is_e, lg, neg)
    m1 = jnp.max(lge, axis=1, keepdims=True)
    i1 = jnp.min(jnp.where(is_e & (lge == m1), e_idf, big), axis=1, keepdims=True)
    is_e2 = is_e & (e_idf != i1)
    lge2 = jnp.where(is_e2, lg, neg)
    m2 = jnp.max(lge2, axis=1, keepdims=True)
    i2 = jnp.min(jnp.where(is_e2 & (lge2 == m2), e_idf, big), axis=1, keepdims=True)
    ex2 = jnp.exp(m2 - m1)
    den = 1.0 + ex2
    g1 = p_g / den
    g2 = p_g * ex2 / den
    @pl.when(pl.program_id(0) == 0)
    def _():
        cnt_ref[...] = jnp.zeros(cnt_ref.shape, F32)

    lanef = lane.astype(F32)
    oh1 = lanef == i1
    oh2 = lanef == i2
    oh = jnp.where(oh1 | oh2, 1.0, 0.0)
    ti = lax.broadcasted_iota(I32, (tm, tm), 0)
    si = lax.broadcasted_iota(I32, (tm, tm), 1)
    tri = jnp.where(si < ti, 1.0, 0.0).astype(BF16)
    before = jnp.dot(tri, oh.astype(BF16), preferred_element_type=F32) + cnt_ref[...]
    rank1 = jnp.sum(jnp.where(oh1, before, 0.0), axis=1, keepdims=True)
    rank2 = jnp.sum(jnp.where(oh2, before, 0.0), axis=1, keepdims=True)
    cnt_ref[...] += jnp.sum(oh, axis=0, keepdims=True)

    r = jnp.where(lane == 0, i1, 0.0)
    r = jnp.where(lane == 1, i2, r)
    r = jnp.where(lane == 2, g1, r)
    r = jnp.where(lane == 3, g2, r)
    r = jnp.where(lane == 4, rank1, r)
    r = jnp.where(lane == 5, rank2, r)
    r_ref[...] = r


def _router(x1, g, w_r, b_r, tm):
    n, d = x1.shape
    return pl.pallas_call(
        _router_kernel,
        grid=(n // tm,),
        in_specs=[
            pl.BlockSpec((tm, d), lambda i: (i, 0)),
            pl.BlockSpec((1, d), lambda i: (0, 0)),
            pl.BlockSpec((d, LANES), lambda i: (0, 0)),
            pl.BlockSpec((1, LANES), lambda i: (0, 0)),
        ],
        out_specs=[pl.BlockSpec((tm, d), lambda i: (i, 0)),
                   pl.BlockSpec((tm, LANES), lambda i: (i, 0)),
                   pl.BlockSpec((1, LANES), lambda i: (0, 0))],
        out_shape=[jax.ShapeDtypeStruct((n, d), F32),
                   jax.ShapeDtypeStruct((n, LANES), F32),
                   jax.ShapeDtypeStruct((1, LANES), F32)],
        compiler_params=_cparams(("arbitrary",), 40),
        name="router",
    )(x1, g.reshape(1, d), w_r, b_r)


def _dispatch_kernel(prow_ref, h_ref, xs_hbm, sem, *, tm):
    base = 2 * tm * pl.program_id(0)

    def scatter_row(r, carry):
        for s in range(2):
            pltpu.make_async_copy(h_ref.at[pl.ds(r, 1), :],
                                  xs_hbm.at[pl.ds(prow_ref[base + 2 * r + s], 1), :], sem).start()
        return carry

    lax.fori_loop(0, tm, scatter_row, 0, unroll=16)
    for _ in range(2):
        pltpu.make_async_copy(h_ref, xs_hbm.at[pl.ds(0, tm), :], sem).wait()


def _dispatch(h2, prow, tm):
    n, d = h2.shape
    grid_spec = pltpu.PrefetchScalarGridSpec(
        num_scalar_prefetch=1,
        grid=(n // tm,),
        in_specs=[pl.BlockSpec((tm, d), lambda i, pr: (i, 0))],
        out_specs=pl.BlockSpec(memory_space=pl.ANY),
        scratch_shapes=[pltpu.SemaphoreType.DMA(())],
    )
    return pl.pallas_call(
        functools.partial(_dispatch_kernel, tm=tm),
        grid_spec=grid_spec,
        out_shape=jax.ShapeDtypeStruct((2 * n, d), F32),
        compiler_params=_cparams(("arbitrary",), 32),
        name="dispatch",
    )(prow, h2)


def _moe_kernel(pt_ref, pe_ref, plo_ref, phi_ref, slot_ref, nxt_ref, np_ref, x_ref, wg_hbm, wu_hbm, wd_hbm,
                y_ref, wgf_ref, wuf_ref, wdf_ref, wgb_ref, wub_ref, wdb_ref, sem):
    p = pl.program_id(0)
    active = p < np_ref[0]
    prev = jnp.maximum(p - 1, 0)
    new_expert = (p == 0) | (pe_ref[p] != pe_ref[prev])
    new_tile = (p == 0) | (pt_ref[p] != pt_ref[prev])
    slot = slot_ref[p]

    def weight_copies(e, s):
        return (pltpu.make_async_copy(wg_hbm.at[e], wgf_ref.at[s], sem.at[s]),
                pltpu.make_async_copy(wu_hbm.at[e], wuf_ref.at[s], sem.at[s]),
                pltpu.make_async_copy(wd_hbm.at[e], wdf_ref.at[s], sem.at[s]))

    @pl.when(p == 0)
    def _():
        for cp in weight_copies(pe_ref[0], 0):
            cp.start()

    @pl.when(active & new_expert)
    def _():
        for cp in weight_copies(pe_ref[p], slot):
            cp.wait()
        wgb_ref[...] = wgf_ref[slot].astype(BF16)
        wub_ref[...] = wuf_ref[slot].astype(BF16)
        wdb_ref[...] = wdf_ref[slot].astype(BF16)

        @pl.when(nxt_ref[p] >= 0)
        def _():
            for cp in weight_copies(nxt_ref[p], 1 - slot):
                cp.start()

    @pl.when(active)
    def _():
        tm = x_ref.shape[0]
        xb = x_ref[...].astype(BF16)
        a = jnp.dot(xb, wgb_ref[...], preferred_element_type=F32)
        u = jnp.dot(xb, wub_ref[...], preferred_element_type=F32)
        hmid = (a * _sigmoid(a) * u).astype(BF16)
        yv = jnp.dot(hmid, wdb_ref[...], preferred_element_type=F32)
        row = lax.broadcasted_iota(I32, (tm, 1), 0)
        mine = (row >= plo_ref[p]) & (row < phi_ref[p])
        yv = jnp.where(mine, yv, 0.0)

        @pl.when(new_tile)
        def _():
            y_ref[...] = yv

        @pl.when(jnp.logical_not(new_tile))
        def _():
            y_ref[...] += yv


def _moe(xs, pairs, w_gate, w_up, w_down, tm):
    pair_tile, pair_expert, pair_lo, pair_hi, pair_slot, pair_next, n_pairs = pairs
    na, d = xs.shape
    de = w_gate.shape[2]
    max_pairs = pair_tile.shape[0]
    grid_spec = pltpu.PrefetchScalarGridSpec(
        num_scalar_prefetch=7,
        grid=(max_pairs,),
        in_specs=[
            pl.BlockSpec((tm, d), lambda p, pt, *_: (pt[p], 0)),
            pl.BlockSpec(memory_space=pl.ANY),
            pl.BlockSpec(memory_space=pl.ANY),
            pl.BlockSpec(memory_space=pl.ANY),
        ],
        out_specs=pl.BlockSpec((tm, d), lambda p, pt, *_: (pt[p], 0)),
        scratch_shapes=[
            pltpu.VMEM((2, d, de), F32),
            pltpu.VMEM((2, d, de), F32),
            pltpu.VMEM((2, de, d), F32),
            pltpu.VMEM((d, de), BF16),
            pltpu.VMEM((d, de), BF16),
            pltpu.VMEM((de, d), BF16),
            pltpu.SemaphoreType.DMA((2,)),
        ],
    )
    return pl.pallas_call(
        _moe_kernel,
        grid_spec=grid_spec,
        out_shape=jax.ShapeDtypeStruct((na, d), F32),
        compiler_params=_cparams(("arbitrary",), 56),
        name="moe",
    )(pair_tile, pair_expert, pair_lo, pair_hi, pair_slot, pair_next, n_pairs, xs, w_gate, w_up, w_down)


def _moe_rows(r, cnt):
    sizes = cnt[0, :N_EXPERTS].astype(I32)
    starts = jnp.cumsum(sizes) - sizes
    e12 = r[:, 0:2].astype(I32)
    rank = r[:, 4:6].astype(I32)
    onehot = e12[:, :, None] == jnp.arange(N_EXPERTS, dtype=I32)[None, None, :]
    start_a = jnp.sum(jnp.where(onehot, starts[None, None, :], 0), axis=-1)
    return (start_a + rank).reshape(-1).astype(I32), sizes


def _moe_pairs(sizes, na, tm):
    ntiles = na // tm
    max_pairs = ntiles + N_EXPERTS - 1
    ends = jnp.cumsum(sizes)
    starts = ends - sizes
    t0 = jnp.arange(ntiles, dtype=I32) * tm
    e_lo = jnp.searchsorted(ends, t0, side="right").astype(I32)
    e_hi = jnp.searchsorted(ends, t0 + (tm - 1), side="right").astype(I32)
    per_tile = e_hi - e_lo + 1
    pend = jnp.cumsum(per_tile)
    pstart = pend - per_tile
    n_pairs = pend[-1]
    pc = jnp.minimum(jnp.arange(max_pairs, dtype=I32), n_pairs - 1)
    tile = jnp.searchsorted(pend, pc, side="right").astype(I32)
    expert = e_lo[tile] + (pc - pstart[tile])
    lo = jnp.clip(starts[expert] - tile * tm, 0, tm).astype(I32)
    hi = jnp.clip(ends[expert] - tile * tm, 0, tm).astype(I32)
    expert = expert.astype(I32)
    first = jnp.concatenate([jnp.ones((1,), bool), expert[1:] != expert[:-1]])
    slot = ((jnp.cumsum(first) - 1) % 2).astype(I32)
    idx = jnp.arange(max_pairs, dtype=I32)
    first_at = jnp.where(first, idx, max_pairs)
    nxt_first = jnp.concatenate([lax.cummin(first_at[::-1])[::-1][1:], jnp.full((1,), max_pairs, I32)])
    nxt = jnp.where(nxt_first < max_pairs, expert[jnp.minimum(nxt_first, max_pairs - 1)], -1).astype(I32)
    return tile, expert, lo, hi, slot, nxt, n_pairs.astype(I32).reshape(1)


def _final_kernel(prow_ref, x_ref, r_ref, g_ref, y_hbm, o_ref, ybuf, sem, *, tm):
    i = pl.program_id(0)
    cur = i % 2

    def issue(step, buf):
        base = 2 * tm * step

        def gather_row(t, carry):
            for s in range(2):
                pltpu.make_async_copy(y_hbm.at[pl.ds(prow_ref[base + 2 * t + s], 1), :],
                                      ybuf.at[buf, s, pl.ds(t, 1), :], sem.at[buf]).start()
            return carry

        lax.fori_loop(0, tm, gather_row, 0, unroll=16)

    @pl.when(i == 0)
    def _():
        issue(0, 0)

    @pl.when(i + 1 < pl.num_programs(0))
    def _():
        issue(i + 1, 1 - cur)

    for s in range(2):
        pltpu.make_async_copy(y_hbm.at[pl.ds(0, tm), :], ybuf.at[cur, s], sem.at[cur]).wait()
    r = r_ref[...]
    g1 = r[:, 2:3]
    g2 = r[:, 3:4]
    x = x_ref[...] + g1 * ybuf[cur, 0] + g2 * ybuf[cur, 1]
    ms = jnp.mean(x * x, axis=-1, keepdims=True)
    o_ref[...] = x * lax.rsqrt(ms + EPS) * g_ref[...]


def _final(x1, y, r, prow, g, tm):
    n, d = x1.shape
    grid_spec = pltpu.PrefetchScalarGridSpec(
        num_scalar_prefetch=1,
        grid=(n // tm,),
        in_specs=[
            pl.BlockSpec((tm, d), lambda i, pr: (i, 0)),
            pl.BlockSpec((tm, LANES), lambda i, pr: (i, 0)),
            pl.BlockSpec((1, d), lambda i, pr: (0, 0)),
            pl.BlockSpec(memory_space=pl.ANY),
        ],
        out_specs=pl.BlockSpec((tm, d), lambda i, pr: (i, 0)),
        scratch_shapes=[pltpu.VMEM((2, 2, tm, d), F32), pltpu.SemaphoreType.DMA((2,))],
    )
    return pl.pallas_call(
        functools.partial(_final_kernel, tm=tm),
        grid_spec=grid_spec,
        out_shape=jax.ShapeDtypeStruct((n, d), F32),
        compiler_params=_cparams(("arbitrary",), 48),
        name="final",
    )(prow, x1, r, g.reshape(1, d), y)


def _pack_w_in(w):
    d, din = w.shape
    tc = 256
    wt = jnp.swapaxes(w, 0, 1)

    def pack_kernel(w_ref, o_ref):
        o_ref[0:912, :] = w_ref[0:912, :].astype(BF16)
        o_ref[912:1024, :] = jnp.zeros((LANES - IDX_HEADS, tc), BF16)
        o_ref[1024:2048, :] = w_ref[1936:2960, :].astype(BF16)
        o_ref[2048:3072, :] = w_ref[912:1936, :].astype(BF16)
        o_ref[3072:, :] = w_ref[2960:, :].astype(BF16)

    return pl.pallas_call(
        pack_kernel,
        grid=(d // tc,),
        in_specs=[pl.BlockSpec((din, tc), lambda i: (0, i))],
        out_specs=pl.BlockSpec((PH_COLS + PT_COLS, tc), lambda i: (0, i)),
        out_shape=jax.ShapeDtypeStruct((PH_COLS + PT_COLS, d), BF16),
        compiler_params=_cparams(("parallel",), 48),
        name="pack_w_in",
    )(wt)


def _rope_tables(seq):
    half = HEAD_DIM // 2
    inv = 1.0 / (ROPE_THETA ** (jnp.arange(0, HEAD_DIM, 2, dtype=F32) / HEAD_DIM))
    ang = jnp.arange(seq, dtype=F32)[:, None] * inv[None, :]
    c, s = jnp.cos(ang), jnp.sin(ang)
    del half
    return jnp.concatenate([c, c], axis=1), jnp.concatenate([-s, s], axis=1)


def _layer(x2, batch, seq, cos2, sin2, lb, g_norm1, w_in, g_cq, w_uq, w_qidx, g_hnorm, w_pa, w_pb, w_o,
           g_norm2, w_grp, b_grp, w_exp, b_exp, w_gate, w_up, w_down, g_out):
    n, d = x2.shape
    ph, pt = _in_proj(x2, g_norm1, _pack_w_in(w_in), tm=min(1024, n), tn=512)
    tq, tk = min(256, seq), min(256, seq)
    qt, qit, wt, k, ki, vt = _dsa_prep(ph, cos2, sin2, g_cq, w_uq.astype(BF16), w_qidx.astype(BF16), seq, tm=tk, tq=tq)
    y_att = _dsa(qit, qt, wt, ki, k, vt, batch, seq, tq=tq, tk=tk)
    y_rec = _hgrn(ph, pt, lb, g_hnorm, batch, seq, chunk=64)
    x1 = _merge(y_att, y_rec, pt, x2, w_pa.astype(BF16), w_pb.astype(BF16), w_o.astype(BF16),
                tm=min(512, n), tc=512)
    w_r = jnp.concatenate([w_grp, w_exp, jnp.zeros((d, LANES - N_GROUPS - N_EXPERTS), F32)], axis=1).astype(BF16)
    b_r = jnp.concatenate([b_grp, b_exp, jnp.zeros((LANES - N_GROUPS - N_EXPERTS,), F32)]).reshape(1, LANES)
    h2, r, cnt = _router(x1, g_norm2, w_r, b_r, tm=min(512, n))
    prow, sizes = _moe_rows(r, cnt)
    xs = _dispatch(h2, prow, tm=min(512, n))
    tm_moe = 256
    assert (2 * n) % tm_moe == 0
    y = _moe(xs, _moe_pairs(sizes, 2 * n, tm_moe), w_gate, w_up, w_down, tm_moe)
    return _final(x1, y, r, prow, g_out, tm=min(512, n))


def kernel(x, g_norm1, w_in, g_cq, w_uq, w_qidx, lb_logits, g_hnorm, w_pa, w_pb, w_o, g_norm2, w_grp, b_grp,
           w_exp, b_exp, w_gate, w_up, w_down, g_final):
    batch, seq, d = x.shape
    depth = g_norm1.shape[0]
    cos2, sin2 = _rope_tables(seq)
    lb_all = jnp.cumsum(jax.nn.softmax(lb_logits.astype(F32), axis=0), axis=0)
    x2 = x.reshape(batch * seq, d)
    for l in range(depth):
        assert depth == 1
        x2 = _layer(x2, batch, seq, cos2, sin2, lb_all[l], g_norm1[l], w_in[l], g_cq[l], w_uq[l], w_qidx[l],
                    g_hnorm[l], w_pa[l], w_pb[l], w_o[l], g_norm2[l], w_grp[l], b_grp[l], w_exp[l], b_exp[l],
                    w_gate[l], w_up[l], w_down[l], g_final)
    return x2.reshape(batch, seq, d)
```

```python
import functools
import math

import jax
import jax.numpy as jnp
from jax import lax
from jax.experimental import pallas as pl
from jax.experimental.pallas import tpu as pltpu

F32 = jnp.float32
BF16 = jnp.bfloat16
I32 = jnp.int32

EPS = 1e-6
ROPE_THETA = 10000.0
LANES = 128
MXU_LANES = 256
DMA_GROUP = 32
VT_ROWS = 128 + 16

A_HEADS = 8
HEAD_DIM = 128
IDX_HEADS = 16
Q_RANK = 512
TOPK_MAX = 256
G_HEADS = 8
N_GROUPS = 4
EXP_PER_GROUP = 8
N_EXPERTS = N_GROUPS * EXP_PER_GROUP
D_EXPERT = 512

INT_MIN = -2147483648
EXP_CLAMP = 80.0


def _cparams(sem, vmem_mb):
    return pltpu.CompilerParams(dimension_semantics=sem, vmem_limit_bytes=vmem_mb << 20)


def _sigmoid(x):
    return 1.0 / (1.0 + jnp.exp(-x))


PH_COLS = 2048
PT_COLS = 7168


def _inproj_kernel(x_ref, g_ref, w_ref, oh_ref, ot_ref, xn_ref, *, nh):
    j = pl.program_id(1)

    @pl.when(j == 0)
    def _():
        x = x_ref[...]
        ms = jnp.mean(x * x, axis=-1, keepdims=True)
        xn_ref[...] = (x * lax.rsqrt(ms + EPS) * g_ref[...]).astype(BF16)

    y = _dot_nt(xn_ref[...], w_ref[...])

    @pl.when(j < nh)
    def _():
        oh_ref[...] = y

    @pl.when(j >= nh)
    def _():
        ot_ref[...] = y.astype(ot_ref.dtype)


def _in_proj(x2, g, w, tm, tn):
    n, d = x2.shape
    nh = PH_COLS // tn
    nt = PT_COLS // tn
    return pl.pallas_call(
        functools.partial(_inproj_kernel, nh=nh),
        grid=(n // tm, nh + nt),
        in_specs=[
            pl.BlockSpec((tm, d), lambda i, j: (i, 0)),
            pl.BlockSpec((1, d), lambda i, j: (0, 0)),
            pl.BlockSpec((tn, d), lambda i, j: (j, 0)),
        ],
        out_specs=[pl.BlockSpec((tm, tn), lambda i, j: (i, jnp.minimum(j, nh - 1))),
                   pl.BlockSpec((tm, tn), lambda i, j: (i, jnp.maximum(j - nh, 0)))],
        out_shape=[jax.ShapeDtypeStruct((n, PH_COLS), F32),
                   jax.ShapeDtypeStruct((n, PT_COLS), BF16)],
        scratch_shapes=[pltpu.VMEM((tm, d), BF16)],
        compiler_params=_cparams(("parallel", "arbitrary"), 56),
        name="in_proj",
    )(x2, g.reshape(1, d), w)


def _rope(x, c2, s2):
    return x * c2 + pltpu.roll(x, HEAD_DIM // 2, 1) * s2


def _prep_kernel(cq_ref, ka_ref, va_ref, kx_ref, wi_ref, cos_ref, sin_ref, g_ref, wuq_ref, wqi_ref,
                 qt_ref, qit_ref, wt_ref, k_ref, ki_ref, vt_ref, *, scale, wscale, tq):
    nqb = cq_ref.shape[0] // tq
    c2 = cos_ref[...]
    s2 = sin_ref[...]
    cq = cq_ref[...]
    ms = jnp.mean(cq * cq, axis=-1, keepdims=True)
    cqn = (cq * lax.rsqrt(ms + EPS) * g_ref[...]).astype(BF16)
    q = jnp.dot(cqn, wuq_ref[...], preferred_element_type=F32)
    for h in range(A_HEADS):
        sl = slice(h * HEAD_DIM, (h + 1) * HEAD_DIM)
        qh_t = (_rope(q[:, sl], c2, s2) * scale).T
        for b in range(nqb):
            qt_ref[b, :, h * tq:(h + 1) * tq] = qh_t[:, b * tq:(b + 1) * tq].astype(BF16)
    qi = jnp.dot(cqn, wqi_ref[...], preferred_element_type=F32)
    for h in range(IDX_HEADS):
        sl = slice(h * HEAD_DIM, (h + 1) * HEAD_DIM)
        qh_t = _rope(qi[:, sl], c2, s2).T
        for b in range(nqb):
            qit_ref[b, :, h * tq:(h + 1) * tq] = qh_t[:, b * tq:(b + 1) * tq].astype(BF16)
    w_t = (wi_ref[...] * wscale).T
    for h in range(IDX_HEADS):
        for b in range(nqb):
            wt_ref[b, :, h * tq:(h + 1) * tq] = w_t[h:h + 1, b * tq:(b + 1) * tq]
    k_ref[...] = _rope(ka_ref[...], c2, s2).astype(BF16)
    ki_ref[...] = _rope(kx_ref[...], c2, s2).astype(BF16)
    vt_ref[0, 0:HEAD_DIM, :] = va_ref[...].T.astype(BF16)
    ones_row = lax.broadcasted_iota(I32, (VT_ROWS - HEAD_DIM, va_ref.shape[0]), 0) == 0
    vt_ref[0, HEAD_DIM:VT_ROWS, :] = jnp.where(ones_row, 1.0, 0.0).astype(BF16)


def _dsa_prep(p, cos2, sin2, g_cq, w_uq, w_qidx, seq, tm, tq):
    n = p.shape[0]
    nsb = seq // tm
    nqb = tm // tq
    aw = A_HEADS * HEAD_DIM
    iw = IDX_HEADS * HEAD_DIM
    row = lambda i: (i, 0)
    blk3 = lambda i: (i, 0, 0)
    kern = functools.partial(_prep_kernel, scale=HEAD_DIM ** -0.5 * math.log2(math.e),
                             wscale=IDX_HEADS ** -0.5 * HEAD_DIM ** -0.5, tq=tq)
    return pl.pallas_call(
        kern,
        grid=(n // tm,),
        in_specs=[
            pl.BlockSpec((tm, Q_RANK), lambda i: (i, 0)),
            pl.BlockSpec((tm, LANES), lambda i: (i, 4)),
            pl.BlockSpec((tm, LANES), lambda i: (i, 5)),
            pl.BlockSpec((tm, LANES), lambda i: (i, 6)),
            pl.BlockSpec((tm, LANES), lambda i: (i, 7)),
            pl.BlockSpec((tm, LANES), lambda i: (i % nsb, 0)),
            pl.BlockSpec((tm, LANES), lambda i: (i % nsb, 0)),
            pl.BlockSpec((1, Q_RANK), lambda i: (0, 0)),
            pl.BlockSpec((Q_RANK, aw), lambda i: (0, 0)),
            pl.BlockSpec((Q_RANK, iw), lambda i: (0, 0)),
        ],
        out_specs=[
            pl.BlockSpec((nqb, HEAD_DIM, A_HEADS * tq), blk3),
            pl.BlockSpec((nqb, HEAD_DIM, IDX_HEADS * tq), blk3),
            pl.BlockSpec((nqb, 1, IDX_HEADS * tq), blk3),
            pl.BlockSpec((tm, LANES), row),
            pl.BlockSpec((tm, LANES), row),
            pl.BlockSpec((1, VT_ROWS, tm), blk3),
        ],
        out_shape=[
            jax.ShapeDtypeStruct((n // tq, HEAD_DIM, A_HEADS * tq), BF16),
            jax.ShapeDtypeStruct((n // tq, HEAD_DIM, IDX_HEADS * tq), BF16),
            jax.ShapeDtypeStruct((n // tq, 1, IDX_HEADS * tq), F32),
            jax.ShapeDtypeStruct((n, LANES), BF16),
            jax.ShapeDtypeStruct((n, LANES), BF16),
            jax.ShapeDtypeStruct((n // tm, VT_ROWS, tm), BF16),
        ],
        compiler_params=_cparams(("parallel",), 40),
        name="dsa_prep",
    )(p, p, p, p, p, cos2, sin2, g_cq.reshape(1, Q_RANK), w_uq, w_qidx)


def _dot_nt(a, b):
    return lax.dot_general(a, b, (((1,), (1,)), ((), ())), preferred_element_type=F32)


def _dsa_kernel(qit_ref, qt_ref, wt_ref, ki_ref, k_ref, vt_ref, o_ref,
                key_ref, sa_ref, sb_ref, pa_ref, pb_ref, acc_ref, *, tq, tk, topk, idx_bits):
    i = pl.program_id(1)
    nkc = ((i + 1) * tq + tk - 1) // tk
    q_pos = i * tq + lax.broadcasted_iota(I32, (tk, tq), 1)
    k_off = lax.broadcasted_iota(I32, (tk, tq), 0)

    def score_chunk(j, carry):
        kic = ki_ref[pl.ds(pl.multiple_of(j * tk, tk), tk), :]
        acc = jnp.zeros((tk, tq), F32)
        hpm = max(1, MXU_LANES // tq)
        for h0 in range(0, IDX_HEADS, hpm):
            sl = slice(h0 * tq, (h0 + hpm) * tq)
            lg = jnp.dot(kic, qit_ref[0, :, sl], preferred_element_type=F32)
            x = jnp.maximum(lg, 0.0) * wt_ref[0, :, sl]
            for u in range(hpm):
                acc = acc + x[:, u * tq:(u + 1) * tq]
        acc = jnp.where(acc == 0.0, 0.0, acc)
        bits = lax.bitcast_convert_type(acc, I32)
        key = jnp.where(bits < 0, bits ^ jnp.int32(0x7FFFFFFF), bits)
        causal = (j * tk + k_off) <= q_pos
        key_ref[j] = jnp.where(causal, key, jnp.int32(INT_MIN))
        return carry

    lax.fori_loop(0, nkc, score_chunk, 0)

    def count_rows(pred):
        def cnt_chunk(j, part):
            hit = pred(j, key_ref[j]).astype(I32)
            return part + jnp.sum(hit.reshape(tk // 8, 8, tq), axis=0)

        part = lax.fori_loop(0, nkc, cnt_chunk, jnp.zeros((8, tq), I32))
        return jnp.sum(part, axis=0, keepdims=True)

    def bit_step(b, c):
        t_u, cnt_t = c
        bit = jnp.left_shift(jnp.int32(1), 31 - b)
        cand_s = (t_u | bit) ^ jnp.int32(INT_MIN)
        cnt = count_rows(lambda j, key: key >= cand_s)
        take = cnt >= topk
        return jnp.where(take, t_u | bit, t_u), jnp.where(take, cnt, cnt_t)

    n_causal = i * tq + lax.broadcasted_iota(I32, (1, tq), 1) + 1
    t_u, cnt_t = lax.fori_loop(0, 32, bit_step, (jnp.zeros((1, tq), I32), n_causal))
    thr = jnp.maximum(t_u ^ jnp.int32(INT_MIN), jnp.int32(INT_MIN + 1))

    @pl.when(jnp.max(cnt_t) > topk)
    def _():
        need = topk - count_rows(lambda j, key: key > thr)

        def idx_step(b, cut):
            cand = cut | jnp.left_shift(jnp.int32(1), idx_bits - 1 - b)
            cnt = count_rows(lambda j, key: (key == thr) & ((j * tk + k_off) < cand))
            return jnp.where(cnt <= need, cand, cut)

        cut = lax.fori_loop(0, idx_bits, idx_step, jnp.zeros((1, tq), I32))

        def demote(j, carry):
            key = key_ref[j]
            drop = (key == thr) & ((j * tk + k_off) >= cut)
            key_ref[j] = jnp.where(drop, key - 1, key)
            return carry

        lax.fori_loop(0, nkc, demote, 0)

    aw = A_HEADS * tq
    acc_ref[...] = jnp.zeros(acc_ref.shape, F32)

    def scores(j, s_ref):
        kc = k_ref[pl.ds(pl.multiple_of(j * tk, tk), tk), :]
        sel = key_ref[j] >= thr
        cmax = []
        for h in range(A_HEADS):
            sl = slice(h * tq, (h + 1) * tq)
            s_h = jnp.dot(kc, qt_ref[0, :, sl], preferred_element_type=F32)
            s_h = jnp.where(sel, s_h, -jnp.inf)
            s_ref[:, sl] = s_h
            cmax.append(jnp.max(s_h, axis=0, keepdims=True))
        return jnp.concatenate(cmax, axis=1)

    def absorb(j, s_ref, p_ref, cmax, m_old):
        m_new = jnp.maximum(m_old, cmax)
        alpha = jnp.exp2(m_old - m_new)
        for h in range(A_HEADS):
            sl = slice(h * tq, (h + 1) * tq)
            p_ref[:, sl] = jnp.exp2(s_ref[:, sl] - m_new[:, sl]).astype(BF16)
        acc_ref[...] = alpha * acc_ref[...] + jnp.dot(vt_ref[j], p_ref[...], preferred_element_type=F32)
        return m_new

    def pair(t, carry):
        m, cm_a = carry
        a = 2 * t
        cm_b = scores(a + 1, sb_ref)
        m = absorb(a, sa_ref, pa_ref, cm_a, m)
        cm_next = scores(jnp.minimum(a + 2, nkc - 1), sa_ref)
        m = absorb(a + 1, sb_ref, pb_ref, cm_b, m)
        return m, cm_next

    m_fin, cm_last = lax.fori_loop(0, nkc // 2, pair, (jnp.full((1, aw), -1e30, F32), scores(0, sa_ref)))

    @pl.when(nkc % 2 == 1)
    def _():
        absorb(nkc - 1, sa_ref, pa_ref, cm_last, m_fin)

    out_t = acc_ref[0:HEAD_DIM, :] / acc_ref[HEAD_DIM:HEAD_DIM + 1, :]
    for h in range(A_HEADS):
        o_ref[:, h * HEAD_DIM:(h + 1) * HEAD_DIM] = out_t[:, h * tq:(h + 1) * tq].T.astype(o_ref.dtype)


def _dsa(qit, qt, wt, ki, k, vt, batch, seq, tq, tk):
    n = batch * seq
    nqb = seq // tq
    nkb = seq // tk
    topk = min(TOPK_MAX, seq // 4)
    aw = A_HEADS * HEAD_DIM
    kern = functools.partial(_dsa_kernel, tq=tq, tk=tk, topk=topk, idx_bits=seq.bit_length())
    qblk = lambda b, i: (b * nqb + i, 0, 0)
    return pl.pallas_call(
        kern,
        grid=(batch, nqb),
        in_specs=[
            pl.BlockSpec((1, HEAD_DIM, IDX_HEADS * tq), qblk),
            pl.BlockSpec((1, HEAD_DIM, A_HEADS * tq), qblk),
            pl.BlockSpec((1, 1, IDX_HEADS * tq), qblk),
            pl.BlockSpec((seq, LANES), lambda b, i: (b, 0)),
            pl.BlockSpec((seq, LANES), lambda b, i: (b, 0)),
            pl.BlockSpec((nkb, VT_ROWS, tk), lambda b, i: (b, 0, 0)),
        ],
        out_specs=pl.BlockSpec((tq, aw), lambda b, i: (b * nqb + i, 0)),
        out_shape=jax.ShapeDtypeStruct((n, aw), BF16),
        scratch_shapes=[
            pltpu.VMEM((nkb, tk, tq), I32),
            pltpu.VMEM((tk, A_HEADS * tq), F32),
            pltpu.VMEM((tk, A_HEADS * tq), F32),
            pltpu.VMEM((tk, A_HEADS * tq), BF16),
            pltpu.VMEM((tk, A_HEADS * tq), BF16),
            pltpu.VMEM((VT_ROWS, A_HEADS * tq), F32),
        ],
        compiler_params=_cparams(("parallel", "arbitrary"), 48),
        name="dsa",
    )(qit, qt, wt, ki, k, vt)


HG_SUB = 16


def _hgrn_diag_exact(q, kk, v, lc, rsub, nsub):
    chunk, gw = q.shape

    def bcast(x, s):
        return jnp.concatenate([jnp.broadcast_to(x[a * HG_SUB + s:a * HG_SUB + s + 1, :], (HG_SUB, gw))
                                for a in range(nsub)], axis=0)

    acc = [jnp.zeros((chunk, HEAD_DIM), F32) for _ in range(G_HEADS)]
    for s in range(HG_SUB):
        term = q * bcast(kk, s) * jnp.exp(jnp.minimum(lc - bcast(lc, s), 0.0))
        term = jnp.where(rsub >= s, term, 0.0)
        vrow = bcast(v, s)
        for h in range(G_HEADS):
            sl = slice(h * HEAD_DIM, (h + 1) * HEAD_DIM)
            acc[h] = acc[h] + jnp.sum(term[:, sl], axis=1, keepdims=True) * vrow[:, sl]
    return jnp.concatenate(acc, axis=1)


def _hgrn_gates(hq_ref, hf_ref, hi_ref, lb_ref):
    chunk, gw = hq_ref.shape
    lb = lb_ref[...]
    f = lb + (1.0 - lb) * _sigmoid(hf_ref[...])
    kk = 1.0 - f
    hq = hq_ref[...].astype(F32)
    q = hq * _sigmoid(hq)
    v = hi_ref[...].astype(F32)
    rsub = lax.broadcasted_iota(I32, (chunk, gw), 0) & (HG_SUB - 1)
    lc = jnp.log(f)
    sh = 1
    while sh < HG_SUB:
        lc = lc + jnp.where(rsub >= sh, pltpu.roll(lc, sh, 0), 0.0)
        sh *= 2
    return q, kk, v, lc, rsub


def _hgrn_diag_scores(q, kk, lc, h):
    chunk = q.shape[0]
    sl = slice(h * HEAD_DIM, (h + 1) * HEAD_DIM)
    qd = (q[:, sl] * jnp.exp(lc[:, sl])).astype(BF16)
    kd = (kk[:, sl] * jnp.exp(jnp.minimum(-lc[:, sl], EXP_CLAMP))).astype(BF16)
    ti = lax.broadcasted_iota(I32, (chunk, chunk), 0)
    si = lax.broadcasted_iota(I32, (chunk, chunk), 1)
    diag_ok = (ti // HG_SUB == si // HG_SUB) & (si <= ti)
    return jnp.where(diag_ok, _dot_nt(qd, kd), 0.0).astype(BF16)


def _hgrn_kernel(need_ref, hq_ref, hf_ref, hi_ref, hg_ref, lb_ref, gn_ref, o_ref, st_ref, oa_ref, *, chunk):
    nsub = chunk // HG_SUB
    gw = hq_ref.shape[1]

    @pl.when(pl.program_id(1) == 0)
    def _():
        st_ref[...] = jnp.zeros(st_ref.shape, F32)

    q, kk, v, lc, rsub = _hgrn_gates(hq_ref, hf_ref, hi_ref, lb_ref)
    def per_sub(vals):
        return jnp.concatenate([jnp.broadcast_to(x, (HG_SUB, gw)) for x in vals], axis=0)

    tots = [lc[(a + 1) * HG_SUB - 1:(a + 1) * HG_SUB, :] for a in range(nsub)]
    ends = []
    run = jnp.zeros((1, gw), F32)
    for a in range(nsub):
        run = run + tots[a]
        ends.append(run)
    cum_l = ends[-1]
    cum = lc + per_sub([jnp.zeros((1, gw), F32)] + ends[:-1])

    qs = (q * jnp.exp(cum)).astype(BF16)
    koff = kk * jnp.exp(per_sub(tots) - lc)
    ke = (koff * per_sub([jnp.exp(cum_l - e) for e in ends])).astype(BF16)
    koff_b = koff.astype(BF16)
    qoff = []
    kmask = []
    for b in range(nsub - 1):
        r0 = (b + 1) * HG_SUB
        qb = (q[r0:, :] * jnp.exp(cum[r0:, :] - ends[b])).astype(BF16)
        qoff.append(jnp.concatenate([jnp.zeros((r0, gw), BF16), qb], axis=0))
        parts = [koff_b[b * HG_SUB:r0, :], jnp.zeros((chunk - r0, gw), BF16)]
        if b > 0:
            parts.insert(0, jnp.zeros((b * HG_SUB, gw), BF16))
        kmask.append(jnp.concatenate(parts, axis=0))
    vb = v.astype(BF16)
    decay_l = jnp.exp(cum_l)

    for h in range(G_HEADS):
        sl = slice(h * HEAD_DIM, (h + 1) * HEAD_DIM)
        qcat = jnp.concatenate([qo[:, sl] for qo in qoff], axis=1)
        kcat = jnp.concatenate([km[:, sl] for km in kmask], axis=1)
        a = _hgrn_diag_scores(q, kk, lc, h) + _dot_nt(qcat, kcat).astype(BF16)
        st = st_ref[h]
        oa_ref[:, sl] = jnp.dot(a, vb[:, sl], preferred_element_type=F32) + _dot_nt(qs[:, sl], st.astype(BF16))
        upd = jnp.dot(v[:, sl].T.astype(BF16), ke[:, sl], preferred_element_type=F32)
        st_ref[h] = st * decay_l[:, sl] + upd

    @pl.when(need_ref[0] != 0)
    def _():
        q2, kk2, v2, lc2, rsub2 = _hgrn_gates(hq_ref, hf_ref, hi_ref, lb_ref)

        @pl.when(jnp.max(-lc2) > EXP_CLAMP)
        def _():
            v2b = v2.astype(BF16)
            fast = [jnp.dot(_hgrn_diag_scores(q2, kk2, lc2, h), v2b[:, h * HEAD_DIM:(h + 1) * HEAD_DIM],
                            preferred_element_type=F32) for h in range(G_HEADS)]
            oa_ref[...] += _hgrn_diag_exact(q2, kk2, v2, lc2, rsub2, nsub) - jnp.concatenate(fast, axis=1)

    o = oa_ref[...]
    ms = jnp.mean(o * o, axis=-1, keepdims=True)
    hg = hg_ref[...].astype(F32)
    y = o * lax.rsqrt(ms + EPS) * gn_ref[...] * (hg * _sigmoid(hg))
    o_ref[...] = y.astype(o_ref.dtype)


def _hgrn(ph, pt, lb, g_hnorm, batch, seq, chunk):
    n = batch * seq
    gw = G_HEADS * HEAD_DIM
    ncb = seq // chunk
    blk = lambda c: pl.BlockSpec((chunk, gw), lambda b, i, nd: (b * ncb + i, c))
    need = (HG_SUB * jnp.log(jnp.min(lb)) < -EXP_CLAMP).astype(I32).reshape(1)
    grid_spec = pltpu.PrefetchScalarGridSpec(
        num_scalar_prefetch=1,
        grid=(batch, ncb),
        in_specs=[blk(0), blk(1), blk(1), blk(2),
                  pl.BlockSpec((1, gw), lambda b, i, nd: (0, 0)),
                  pl.BlockSpec((1, gw), lambda b, i, nd: (0, 0))],
        out_specs=pl.BlockSpec((chunk, gw), lambda b, i, nd: (b * ncb + i, 0)),
        scratch_shapes=[pltpu.VMEM((G_HEADS, HEAD_DIM, HEAD_DIM), F32),
                        pltpu.VMEM((chunk, gw), F32)],
    )
    return pl.pallas_call(
        functools.partial(_hgrn_kernel, chunk=chunk),
        grid_spec=grid_spec,
        out_shape=jax.ShapeDtypeStruct((n, gw), BF16),
        compiler_params=_cparams(("parallel", "arbitrary"), 32),
        name="hgrn",
    )(need, pt, ph, pt, pt, lb.reshape(1, gw), g_hnorm.reshape(1, gw))


def _merge_kernel(ya_ref, yr_ref, ga_ref, gb_ref, x_ref, wpa_ref, wpb_ref, wo_ref, o_ref, mg_ref):
    c = pl.program_id(1)
    nc = pl.num_programs(1)
    tc = ga_ref.shape[1]

    ma = jnp.dot(ya_ref[...], wpa_ref[...], preferred_element_type=F32)
    mb = jnp.dot(yr_ref[...], wpb_ref[...], preferred_element_type=F32)
    merged = _sigmoid(ga_ref[...].astype(F32)) * ma + _sigmoid(gb_ref[...].astype(F32)) * mb
    merged = merged.astype(BF16)
    for k in range(mg_ref.shape[1] // tc):
        @pl.when(c == k)
        def _():
            mg_ref[:, k * tc:(k + 1) * tc] = merged

    @pl.when(c == nc - 1)
    def _():
        o_ref[...] = x_ref[...] + jnp.dot(mg_ref[...], wo_ref[...], preferred_element_type=F32)


def _merge(y_att, y_rec, pt, x2, w_pa, w_pb, w_o, tm, tc):
    n, d = x2.shape
    aw = y_att.shape[1]
    gw = y_rec.shape[1]
    ga0 = 3072 // tc
    gb0 = 5120 // tc
    return pl.pallas_call(
        _merge_kernel,
        grid=(n // tm, d // tc),
        in_specs=[
            pl.BlockSpec((tm, aw), lambda i, c: (i, 0)),
            pl.BlockSpec((tm, gw), lambda i, c: (i, 0)),
            pl.BlockSpec((tm, tc), lambda i, c: (i, ga0 + c)),
            pl.BlockSpec((tm, tc), lambda i, c: (i, gb0 + c)),
            pl.BlockSpec((tm, d), lambda i, c: (i, 0)),
            pl.BlockSpec((aw, tc), lambda i, c: (0, c)),
            pl.BlockSpec((gw, tc), lambda i, c: (0, c)),
            pl.BlockSpec((d, d), lambda i, c: (0, 0)),
        ],
        out_specs=pl.BlockSpec((tm, d), lambda i, c: (i, 0)),
        out_shape=jax.ShapeDtypeStruct((n, d), F32),
        scratch_shapes=[pltpu.VMEM((tm, d), BF16)],
        compiler_params=_cparams(("parallel", "arbitrary"), 56),
        name="merge",
    )(y_att, y_rec, pt, pt, x2, w_pa, w_pb, w_o)


def _router_kernel(x_ref, g_ref, wr_ref, br_ref, h_ref, r_ref, cnt_ref):
    x = x_ref[...]
    ms = jnp.mean(x * x, axis=-1, keepdims=True)
    h = x * lax.rsqrt(ms + EPS) * g_ref[...]
    h_ref[...] = h
    lg = jnp.dot(h.astype(BF16), wr_ref[...], preferred_element_type=F32) + br_ref[...]
    tm = lg.shape[0]
    lane = lax.broadcasted_iota(I32, (tm, LANES), 1)
    neg = jnp.float32(-1e30)
    big = jnp.float32(LANES)
    is_g = lane < N_GROUPS
    lgg = jnp.where(is_g, lg, neg)
    mg = jnp.max(lgg, axis=1, keepdims=True)
    zg = jnp.sum(jnp.where(is_g, jnp.exp(lgg - mg), 0.0), axis=1, keepdims=True)
    p_g = 1.0 / zg
    grp = jnp.min(jnp.where(is_g & (lgg == mg), lane.astype(F32), big), axis=1, keepdims=True)
    e_id = lane - N_GROUPS
    e_grp = lax.shift_right_arithmetic(e_id, jnp.int32(EXP_PER_GROUP.bit_length() - 1)).astype(F32)
    e_idf = e_id.astype(F32)
    is_e = (e_id >= 0) & (e_id < N_EXPERTS) & (e_grp == grp)
    lge = jnp.where(is_e, lg, neg)
    m1 = jnp.max(lge, axis=1, keepdims=True)
    i1 = jnp.min(jnp.where(is_e & (lge == m1), e_idf, big), axis=1, keepdims=True)
    is_e2 = is_e & (e_idf != i1)
    lge2 = jnp.where(is_e2, lg, neg)
    m2 = jnp.max(lge2, axis=1, keepdims=True)
    i2 = jnp.min(jnp.where(is_e2 & (lge2 == m2), e_idf, big), axis=1, keepdims=True)
    ex2 = jnp.exp(m2 - m1)
    den = 1.0 + ex2
    g1 = p_g / den
    g2 = p_g * ex2 / den
    @pl.when(pl.program_id(0) == 0)
    def _():
        cnt_ref[...] = jnp.zeros(cnt_ref.shape, F32)

    lanef = lane.astype(F32)
    oh1 = lanef == i1
    oh2 = lanef == i2
    oh = jnp.where(oh1 | oh2, 1.0, 0.0)
    ti = lax.broadcasted_iota(I32, (tm, tm), 0)
    si = lax.broadcasted_iota(I32, (tm, tm), 1)
    tri = jnp.where(si < ti, 1.0, 0.0).astype(BF16)
    before = jnp.dot(tri, oh.astype(BF16), preferred_element_type=F32) + cnt_ref[...]
    rank1 = jnp.sum(jnp.where(oh1, before, 0.0), axis=1, keepdims=True)
    rank2 = jnp.sum(jnp.where(oh2, before, 0.0), axis=1, keepdims=True)
    cnt_ref[...] += jnp.sum(oh, axis=0, keepdims=True)

    r = jnp.where(lane == 0, i1, 0.0)
    r = jnp.where(lane == 1, i2, r)
    r = jnp.where(lane == 2, g1, r)
    r = jnp.where(lane == 3, g2, r)
    r = jnp.where(lane == 4, rank1, r)
    r = jnp.where(lane == 5, rank2, r)
    r_ref[...] = r


def _router(x1, g, w_r, b_r, tm):
    n, d = x1.shape
    return pl.pallas_call(
        _router_kernel,
        grid=(n // tm,),
        in_specs=[
            pl.BlockSpec((tm, d), lambda i: (i, 0)),
            pl.BlockSpec((1, d), lambda i: (0, 0)),
            pl.BlockSpec((d, LANES), lambda i: (0, 0)),
            pl.BlockSpec((1, LANES), lambda i: (0, 0)),
        ],
        out_specs=[pl.BlockSpec((tm, d), lambda i: (i, 0)),
                   pl.BlockSpec((tm, LANES), lambda i: (i, 0)),
                   pl.BlockSpec((1, LANES), lambda i: (0, 0))],
        out_shape=[jax.ShapeDtypeStruct((n, d), F32),
                   jax.ShapeDtypeStruct((n, LANES), F32),
                   jax.ShapeDtypeStruct((1, LANES), F32)],
        compiler_params=_cparams(("arbitrary",), 40),
        name="router",
    )(x1, g.reshape(1, d), w_r, b_r)


def _dispatch_kernel(prow_ref, h_ref, xs_hbm, sem, *, tm):
    base = 2 * tm * pl.program_id(0)

    def scatter_group(g, carry):
        r0 = pl.multiple_of(g * DMA_GROUP, DMA_GROUP)
        rows = h_ref.at[pl.ds(r0, DMA_GROUP), :]
        for u in range(DMA_GROUP):
            for s in range(2):
                dst_row = prow_ref[base + 2 * r0 + (2 * u + s)]
                pltpu.make_async_copy(rows.at[pl.ds(u, 1), :], xs_hbm.at[pl.ds(dst_row, 1), :], sem).start()
        return carry

    lax.fori_loop(0, tm // DMA_GROUP, scatter_group, 0)
    for _ in range(2):
        pltpu.make_async_copy(h_ref, xs_hbm.at[pl.ds(0, tm), :], sem).wait()


def _dispatch(h2, prow, tm):
    n, d = h2.shape
    grid_spec = pltpu.PrefetchScalarGridSpec(
        num_scalar_prefetch=1,
        grid=(n // tm,),
        in_specs=[pl.BlockSpec((tm, d), lambda i, pr: (i, 0))],
        out_specs=pl.BlockSpec(memory_space=pl.ANY),
        scratch_shapes=[pltpu.SemaphoreType.DMA(())],
    )
    return pl.pallas_call(
        functools.partial(_dispatch_kernel, tm=tm),
        grid_spec=grid_spec,
        out_shape=jax.ShapeDtypeStruct((2 * n, d), F32),
        compiler_params=_cparams(("arbitrary",), 32),
        name="dispatch",
    )(prow, h2)


def _moe_kernel(pt_ref, pe_ref, plo_ref, phi_ref, slot_ref, nxt_ref, np_ref, x_ref, wg_hbm, wu_hbm, wd_hbm,
                y_ref, wgf_ref, wuf_ref, wdf_ref, wgb_ref, wub_ref, wdb_ref, sem):
    p = pl.program_id(0)
    active = p < np_ref[0]
    prev = jnp.maximum(p - 1, 0)
    new_expert = (p == 0) | (pe_ref[p] != pe_ref[prev])
    new_tile = (p == 0) | (pt_ref[p] != pt_ref[prev])
    slot = slot_ref[p]

    def weight_copies(e, s):
        return (pltpu.make_async_copy(wg_hbm.at[e], wgf_ref.at[s], sem.at[s]),
                pltpu.make_async_copy(wu_hbm.at[e], wuf_ref.at[s], sem.at[s]),
                pltpu.make_async_copy(wd_hbm.at[e], wdf_ref.at[s], sem.at[s]))

    @pl.when(p == 0)
    def _():
        for cp in weight_copies(pe_ref[0], 0):
            cp.start()

    @pl.when(active & new_expert)
    def _():
        for cp in weight_copies(pe_ref[p], slot):
            cp.wait()
        wgb_ref[...] = wgf_ref[slot].astype(BF16)
        wub_ref[...] = wuf_ref[slot].astype(BF16)
        wdb_ref[...] = wdf_ref[slot].astype(BF16)

        @pl.when(nxt_ref[p] >= 0)
        def _():
            for cp in weight_copies(nxt_ref[p], 1 - slot):
                cp.start()

    @pl.when(active)
    def _():
        tm = x_ref.shape[0]
        xb = x_ref[...].astype(BF16)
        a = jnp.dot(xb, wgb_ref[...], preferred_element_type=F32)
        u = jnp.dot(xb, wub_ref[...], preferred_element_type=F32)
        hmid = (a * _sigmoid(a) * u).astype(BF16)
        yv = jnp.dot(hmid, wdb_ref[...], preferred_element_type=F32)
        row = lax.broadcasted_iota(I32, (tm, 1), 0)
        mine = (row >= plo_ref[p]) & (row < phi_ref[p])
        yv = jnp.where(mine, yv, 0.0)

        @pl.when(new_tile)
        def _():
            y_ref[...] = yv

        @pl.when(jnp.logical_not(new_tile))
        def _():
            y_ref[...] += yv


def _moe(xs, pairs, w_gate, w_up, w_down, tm):
    pair_tile, pair_expert, pair_lo, pair_hi, pair_slot, pair_next, n_pairs = pairs
    na, d = xs.shape
    de = w_gate.shape[2]
    max_pairs = pair_tile.shape[0]
    grid_spec = pltpu.PrefetchScalarGridSpec(
        num_scalar_prefetch=7,
        grid=(max_pairs,),
        in_specs=[
            pl.BlockSpec((tm, d), lambda p, pt, *_: (pt[p], 0)),
            pl.BlockSpec(memory_space=pl.ANY),
            pl.BlockSpec(memory_space=pl.ANY),
            pl.BlockSpec(memory_space=pl.ANY),
        ],
        out_specs=pl.BlockSpec((tm, d), lambda p, pt, *_: (pt[p], 0)),
        scratch_shapes=[
            pltpu.VMEM((2, d, de), F32),
            pltpu.VMEM((2, d, de), F32),
            pltpu.VMEM((2, de, d), F32),
            pltpu.VMEM((d, de), BF16),
            pltpu.VMEM((d, de), BF16),
            pltpu.VMEM((de, d), BF16),
            pltpu.SemaphoreType.DMA((2,)),
        ],
    )
    return pl.pallas_call(
        _moe_kernel,
        grid_spec=grid_spec,
        out_shape=jax.ShapeDtypeStruct((na, d), F32),
        compiler_params=_cparams(("arbitrary",), 56),
        name="moe",
    )(pair_tile, pair_expert, pair_lo, pair_hi, pair_slot, pair_next, n_pairs, xs, w_gate, w_up, w_down)


def _moe_rows(r, cnt):
    sizes = cnt[0, :N_EXPERTS].astype(I32)
    starts = jnp.cumsum(sizes) - sizes
    e12 = r[:, 0:2].astype(I32)
    rank = r[:, 4:6].astype(I32)
    onehot = e12[:, :, None] == jnp.arange(N_EXPERTS, dtype=I32)[None, None, :]
    start_a = jnp.sum(jnp.where(onehot, starts[None, None, :], 0), axis=-1)
    return (start_a + rank).reshape(-1).astype(I32), sizes


def _moe_pairs(sizes, na, tm):
    ntiles = na // tm
    max_pairs = ntiles + N_EXPERTS - 1
    ends = jnp.cumsum(sizes)
    starts = ends - sizes
    t0 = jnp.arange(ntiles, dtype=I32) * tm
    e_lo = jnp.searchsorted(ends, t0, side="right").astype(I32)
    e_hi = jnp.searchsorted(ends, t0 + (tm - 1), side="right").astype(I32)
    per_tile = e_hi - e_lo + 1
    pend = jnp.cumsum(per_tile)
    pstart = pend - per_tile
    n_pairs = pend[-1]
    pc = jnp.minimum(jnp.arange(max_pairs, dtype=I32), n_pairs - 1)
    tile = jnp.searchsorted(pend, pc, side="right").astype(I32)
    expert = e_lo[tile] + (pc - pstart[tile])
    lo = jnp.clip(starts[expert] - tile * tm, 0, tm).astype(I32)
    hi = jnp.clip(ends[expert] - tile * tm, 0, tm).astype(I32)
    expert = expert.astype(I32)
    first = jnp.concatenate([jnp.ones((1,), bool), expert[1:] != expert[:-1]])
    slot = ((jnp.cumsum(first) - 1) % 2).astype(I32)
    idx = jnp.arange(max_pairs, dtype=I32)
    first_at = jnp.where(first, idx, max_pairs)
    nxt_first = jnp.concatenate([lax.cummin(first_at[::-1])[::-1][1:], jnp.full((1,), max_pairs, I32)])
    nxt = jnp.where(nxt_first < max_pairs, expert[jnp.minimum(nxt_first, max_pairs - 1)], -1).astype(I32)
    return tile, expert, lo, hi, slot, nxt, n_pairs.astype(I32).reshape(1)


def _final_kernel(prow_ref, x_ref, r_ref, g_ref, y_hbm, o_ref, ybuf, sem, *, tm):
    i = pl.program_id(0)
    cur = i % 2

    def issue(step, buf):
        base = 2 * tm * step

        def gather_group(g, carry):
            r0 = pl.multiple_of(g * DMA_GROUP, DMA_GROUP)
            for s in range(2):
                rows = ybuf.at[buf, s, pl.ds(r0, DMA_GROUP), :]
                for u in range(DMA_GROUP):
                    src_row = prow_ref[base + 2 * r0 + (2 * u + s)]
                    pltpu.make_async_copy(y_hbm.at[pl.ds(src_row, 1), :], rows.at[pl.ds(u, 1), :],
                                          sem.at[buf]).start()
            return carry

        lax.fori_loop(0, tm // DMA_GROUP, gather_group, 0)

    @pl.when(i == 0)
    def _():
        issue(0, 0)

    @pl.when(i + 1 < pl.num_programs(0))
    def _():
        issue(i + 1, 1 - cur)

    for s in range(2):
        pltpu.make_async_copy(y_hbm.at[pl.ds(0, tm), :], ybuf.at[cur, s], sem.at[cur]).wait()
    r = r_ref[...]
    g1 = r[:, 2:3]
    g2 = r[:, 3:4]
    x = x_ref[...] + g1 * ybuf[cur, 0] + g2 * ybuf[cur, 1]
    ms = jnp.mean(x * x, axis=-1, keepdims=True)
    o_ref[...] = x * lax.rsqrt(ms + EPS) * g_ref[...]


def _final(x1, y, r, prow, g, tm):
    n, d = x1.shape
    grid_spec = pltpu.PrefetchScalarGridSpec(
        num_scalar_prefetch=1,
        grid=(n // tm,),
        in_specs=[
            pl.BlockSpec((tm, d), lambda i, pr: (i, 0)),
            pl.BlockSpec((tm, LANES), lambda i, pr: (i, 0)),
            pl.BlockSpec((1, d), lambda i, pr: (0, 0)),
            pl.BlockSpec(memory_space=pl.ANY),
        ],
        out_specs=pl.BlockSpec((tm, d), lambda i, pr: (i, 0)),
        scratch_shapes=[pltpu.VMEM((2, 2, tm, d), F32), pltpu.SemaphoreType.DMA((2,))],
    )
    return pl.pallas_call(
        functools.partial(_final_kernel, tm=tm),
        grid_spec=grid_spec,
        out_shape=jax.ShapeDtypeStruct((n, d), F32),
        compiler_params=_cparams(("arbitrary",), 48),
        name="final",
    )(prow, x1, r, g.reshape(1, d), y)


def _pack_w_in(w):
    d, din = w.shape
    tc = 256
    wt = jnp.swapaxes(w, 0, 1)

    def pack_kernel(w_ref, o_ref):
        o_ref[0:912, :] = w_ref[0:912, :].astype(BF16)
        o_ref[912:1024, :] = jnp.zeros((LANES - IDX_HEADS, tc), BF16)
        o_ref[1024:2048, :] = w_ref[1936:2960, :].astype(BF16)
        o_ref[2048:3072, :] = w_ref[912:1936, :].astype(BF16)
        o_ref[3072:, :] = w_ref[2960:, :].astype(BF16)

    return pl.pallas_call(
        pack_kernel,
        grid=(d // tc,),
        in_specs=[pl.BlockSpec((din, tc), lambda i: (0, i))],
        out_specs=pl.BlockSpec((PH_COLS + PT_COLS, tc), lambda i: (0, i)),
        out_shape=jax.ShapeDtypeStruct((PH_COLS + PT_COLS, d), BF16),
        compiler_params=_cparams(("parallel",), 48),
        name="pack_w_in",
    )(wt)


def _rope_tables(seq):
    half = HEAD_DIM // 2
    inv = 1.0 / (ROPE_THETA ** (jnp.arange(0, HEAD_DIM, 2, dtype=F32) / HEAD_DIM))
    ang = jnp.arange(seq, dtype=F32)[:, None] * inv[None, :]
    c, s = jnp.cos(ang), jnp.sin(ang)
    del half
    return jnp.concatenate([c, c], axis=1), jnp.concatenate([-s, s], axis=1)


def _layer(x2, batch, seq, cos2, sin2, lb, g_norm1, w_in, g_cq, w_uq, w_qidx, g_hnorm, w_pa, w_pb, w_o,
           g_norm2, w_grp, b_grp, w_exp, b_exp, w_gate, w_up, w_down, g_out):
    n, d = x2.shape
    ph, pt = _in_proj(x2, g_norm1, _pack_w_in(w_in), tm=min(1024, n), tn=1024)
    tq, tk = min(256, seq), min(256, seq)
    qt, qit, wt, k, ki, vt = _dsa_prep(ph, cos2, sin2, g_cq, w_uq.astype(BF16), w_qidx.astype(BF16), seq, tm=tk, tq=tq)
    y_att = _dsa(qit, qt, wt, ki, k, vt, batch, seq, tq=tq, tk=tk)
    y_rec = _hgrn(ph, pt, lb, g_hnorm, batch, seq, chunk=64)
    x1 = _merge(y_att, y_rec, pt, x2, w_pa.astype(BF16), w_pb.astype(BF16), w_o.astype(BF16),
                tm=min(512, n), tc=512)
    w_r = jnp.concatenate([w_grp, w_exp, jnp.zeros((d, LANES - N_GROUPS - N_EXPERTS), F32)], axis=1).astype(BF16)
    b_r = jnp.concatenate([b_grp, b_exp, jnp.zeros((LANES - N_GROUPS - N_EXPERTS,), F32)]).reshape(1, LANES)
    h2, r, cnt = _router(x1, g_norm2, w_r, b_r, tm=min(512, n))
    prow, sizes = _moe_rows(r, cnt)
    xs = _dispatch(h2, prow, tm=min(512, n))
    tm_moe = 256
    assert (2 * n) % tm_moe == 0
    y = _moe(xs, _moe_pairs(sizes, 2 * n, tm_moe), w_gate, w_up, w_down, tm_moe)
    return _final(x1, y, r, prow, g_out, tm=min(512, n))


def kernel(x, g_norm1, w_in, g_cq, w_uq, w_qidx, lb_logits, g_hnorm, w_pa, w_pb, w_o, g_norm2, w_grp, b_grp,
           w_exp, b_exp, w_gate, w_up, w_down, g_final):
    batch, seq, d = x.shape
    depth = g_norm1.shape[0]
    cos2, sin2 = _rope_tables(seq)
    lb_all = jnp.cumsum(jax.nn.softmax(lb_logits.astype(F32), axis=0), axis=0)
    x2 = x.reshape(batch * seq, d)
    for l in range(depth):
        assert depth == 1
        x2 = _layer(x2, batch, seq, cos2, sin2, lb_all[l], g_norm1[l], w_in[l], g_cq[l], w_uq[l], w_qidx[l],
                    g_hnorm[l], w_pa[l], w_pb[l], w_o[l], g_norm2[l], w_grp[l], b_grp[l], w_exp[l], b_exp[l],
                    w_gate[l], w_up[l], w_down[l], g_final)
    return x2.reshape(batch, seq, d)
```

```python
import functools
import math

import jax
import jax.numpy as jnp
from jax import lax
from jax.experimental import pallas as pl
from jax.experimental.pallas import tpu as pltpu

F32 = jnp.float32
BF16 = jnp.bfloat16
I32 = jnp.int32
U32 = jnp.uint32

EPS = 1e-6
ROPE_THETA = 10000.0
LANES = 128
MXU_LANES = 256
DMA_GROUP = 32
VT_ROWS = 128 + 16

A_HEADS = 8
HEAD_DIM = 128
IDX_HEADS = 16
Q_RANK = 512
TOPK_MAX = 256
G_HEADS = 8
N_GROUPS = 4
EXP_PER_GROUP = 8
N_EXPERTS = N_GROUPS * EXP_PER_GROUP
D_EXPERT = 512

INT_MIN = -2147483648
EXP_CLAMP = 80.0


def _cparams(sem, vmem_mb):
    return pltpu.CompilerParams(dimension_semantics=sem, vmem_limit_bytes=vmem_mb << 20)


def _pack_halves(x):
    c = x.shape[1] // 2
    bits = lax.bitcast_convert_type(x.astype(BF16).astype(F32), U32)
    return lax.shift_right_logical(bits[:, :c], jnp.uint32(16)) | (bits[:, c:] & jnp.uint32(0xFFFF0000))


def _unpack_halves(w):
    lo = lax.bitcast_convert_type(lax.shift_left(w, jnp.uint32(16)), F32)
    hi = lax.bitcast_convert_type(w & jnp.uint32(0xFFFF0000), F32)
    return lo, hi


def _sigmoid(x):
    return 1.0 / (1.0 + jnp.exp(-x))


PH_COLS = 2048
PT_COLS = 7168


def _inproj_kernel(x_ref, g_ref, w_ref, oh_ref, ot_ref, xn_ref, *, nh):
    j = pl.program_id(1)

    @pl.when(j == 0)
    def _():
        x = x_ref[...]
        ms = jnp.mean(x * x, axis=-1, keepdims=True)
        xn_ref[...] = (x * lax.rsqrt(ms + EPS) * g_ref[...]).astype(BF16)

    y = _dot_nt(xn_ref[...], w_ref[...])

    @pl.when(j < nh)
    def _():
        oh_ref[...] = y

    @pl.when(j >= nh)
    def _():
        ot_ref[...] = y.astype(ot_ref.dtype)


def _in_proj(x2, g, w, tm, tn):
    n, d = x2.shape
    nh = PH_COLS // tn
    nt = PT_COLS // tn
    return pl.pallas_call(
        functools.partial(_inproj_kernel, nh=nh),
        grid=(n // tm, nh + nt),
        in_specs=[
            pl.BlockSpec((tm, d), lambda i, j: (i, 0)),
            pl.BlockSpec((1, d), lambda i, j: (0, 0)),
            pl.BlockSpec((tn, d), lambda i, j: (j, 0)),
        ],
        out_specs=[pl.BlockSpec((tm, tn), lambda i, j: (i, jnp.minimum(j, nh - 1))),
                   pl.BlockSpec((tm, tn), lambda i, j: (i, jnp.maximum(j - nh, 0)))],
        out_shape=[jax.ShapeDtypeStruct((n, PH_COLS), F32),
                   jax.ShapeDtypeStruct((n, PT_COLS), BF16)],
        scratch_shapes=[pltpu.VMEM((tm, d), BF16)],
        compiler_params=_cparams(("parallel", "arbitrary"), 56),
        name="in_proj",
    )(x2, g.reshape(1, d), w)


def _rope(x, c2, s2):
    return x * c2 + pltpu.roll(x, HEAD_DIM // 2, 1) * s2


def _prep_kernel(cq_ref, ka_ref, va_ref, kx_ref, wi_ref, cos_ref, sin_ref, g_ref, wuq_ref, wqi_ref,
                 qt_ref, qit_ref, wt_ref, k_ref, ki_ref, vt_ref, *, scale, wscale, tq):
    nqb = cq_ref.shape[0] // tq
    c2 = cos_ref[...]
    s2 = sin_ref[...]
    cq = cq_ref[...]
    ms = jnp.mean(cq * cq, axis=-1, keepdims=True)
    cqn = (cq * lax.rsqrt(ms + EPS) * g_ref[...]).astype(BF16)
    q = jnp.dot(cqn, wuq_ref[...], preferred_element_type=F32)
    for h in range(A_HEADS):
        sl = slice(h * HEAD_DIM, (h + 1) * HEAD_DIM)
        qh_t = (_rope(q[:, sl], c2, s2) * scale).T
        for b in range(nqb):
            qt_ref[b, :, h * tq:(h + 1) * tq] = qh_t[:, b * tq:(b + 1) * tq].astype(BF16)
    qi = jnp.dot(cqn, wqi_ref[...], preferred_element_type=F32)
    for h in range(IDX_HEADS):
        sl = slice(h * HEAD_DIM, (h + 1) * HEAD_DIM)
        qh_t = _rope(qi[:, sl], c2, s2).T
        for b in range(nqb):
            qit_ref[b, :, h * tq:(h + 1) * tq] = qh_t[:, b * tq:(b + 1) * tq].astype(BF16)
    w_t = (wi_ref[...] * wscale).T
    for h in range(IDX_HEADS):
        for b in range(nqb):
            wt_ref[b, :, h * tq:(h + 1) * tq] = w_t[h:h + 1, b * tq:(b + 1) * tq]
    k_ref[...] = _rope(ka_ref[...], c2, s2).astype(BF16)
    ki_ref[...] = _rope(kx_ref[...], c2, s2).astype(BF16)
    vt_ref[0, 0:HEAD_DIM, :] = va_ref[...].T.astype(BF16)
    ones_row = lax.broadcasted_iota(I32, (VT_ROWS - HEAD_DIM, va_ref.shape[0]), 0) == 0
    vt_ref[0, HEAD_DIM:VT_ROWS, :] = jnp.where(ones_row, 1.0, 0.0).astype(BF16)


def _dsa_prep(p, cos2, sin2, g_cq, w_uq, w_qidx, seq, tm, tq):
    n = p.shape[0]
    nsb = seq // tm
    nqb = tm // tq
    aw = A_HEADS * HEAD_DIM
    iw = IDX_HEADS * HEAD_DIM
    row = lambda i: (i, 0)
    blk3 = lambda i: (i, 0, 0)
    kern = functools.partial(_prep_kernel, scale=HEAD_DIM ** -0.5 * math.log2(math.e),
                             wscale=IDX_HEADS ** -0.5 * HEAD_DIM ** -0.5, tq=tq)
    return pl.pallas_call(
        kern,
        grid=(n // tm,),
        in_specs=[
            pl.BlockSpec((tm, Q_RANK), lambda i: (i, 0)),
            pl.BlockSpec((tm, LANES), lambda i: (i, 4)),
            pl.BlockSpec((tm, LANES), lambda i: (i, 5)),
            pl.BlockSpec((tm, LANES), lambda i: (i, 6)),
            pl.BlockSpec((tm, LANES), lambda i: (i, 7)),
            pl.BlockSpec((tm, LANES), lambda i: (i % nsb, 0)),
            pl.BlockSpec((tm, LANES), lambda i: (i % nsb, 0)),
            pl.BlockSpec((1, Q_RANK), lambda i: (0, 0)),
            pl.BlockSpec((Q_RANK, aw), lambda i: (0, 0)),
            pl.BlockSpec((Q_RANK, iw), lambda i: (0, 0)),
        ],
        out_specs=[
            pl.BlockSpec((nqb, HEAD_DIM, A_HEADS * tq), blk3),
            pl.BlockSpec((nqb, HEAD_DIM, IDX_HEADS * tq), blk3),
            pl.BlockSpec((nqb, 1, IDX_HEADS * tq), blk3),
            pl.BlockSpec((tm, LANES), row),
            pl.BlockSpec((tm, LANES), row),
            pl.BlockSpec((1, VT_ROWS, tm), blk3),
        ],
        out_shape=[
            jax.ShapeDtypeStruct((n // tq, HEAD_DIM, A_HEADS * tq), BF16),
            jax.ShapeDtypeStruct((n // tq, HEAD_DIM, IDX_HEADS * tq), BF16),
            jax.ShapeDtypeStruct((n // tq, 1, IDX_HEADS * tq), F32),
            jax.ShapeDtypeStruct((n, LANES), BF16),
            jax.ShapeDtypeStruct((n, LANES), BF16),
            jax.ShapeDtypeStruct((n // tm, VT_ROWS, tm), BF16),
        ],
        compiler_params=_cparams(("parallel",), 40),
        name="dsa_prep",
    )(p, p, p, p, p, cos2, sin2, g_cq.reshape(1, Q_RANK), w_uq, w_qidx)


def _dot_nt(a, b):
    return lax.dot_general(a, b, (((1,), (1,)), ((), ())), preferred_element_type=F32)


def _dsa_kernel(qit_ref, qt_ref, wt_ref, ki_ref, k_ref, vt_ref, o_ref,
                key_ref, sa_ref, sb_ref, pa_ref, pb_ref, acc_ref, *, tq, tk, topk, idx_bits):
    i = pl.program_id(1)
    nkc = ((i + 1) * tq + tk - 1) // tk
    q_pos = i * tq + lax.broadcasted_iota(I32, (tk, tq), 1)
    k_off = lax.broadcasted_iota(I32, (tk, tq), 0)

    def score_chunk(j, carry):
        kic = ki_ref[pl.ds(pl.multiple_of(j * tk, tk), tk), :]
        acc = jnp.zeros((tk, tq), F32)
        hpm = max(1, MXU_LANES // tq)
        for h0 in range(0, IDX_HEADS, hpm):
            sl = slice(h0 * tq, (h0 + hpm) * tq)
            lg = jnp.dot(kic, qit_ref[0, :, sl], preferred_element_type=F32)
            x = jnp.maximum(lg, 0.0) * wt_ref[0, :, sl]
            for u in range(hpm):
                acc = acc + x[:, u * tq:(u + 1) * tq]
        acc = jnp.where(acc == 0.0, 0.0, acc)
        bits = lax.bitcast_convert_type(acc, I32)
        key = jnp.where(bits < 0, bits ^ jnp.int32(0x7FFFFFFF), bits)
        causal = (j * tk + k_off) <= q_pos
        key_ref[j] = jnp.where(causal, key, jnp.int32(INT_MIN))
        return carry

    lax.fori_loop(0, nkc, score_chunk, 0)

    def count_rows(pred):
        def cnt_chunk(j, part):
            hit = pred(j, key_ref[j]).astype(I32)
            return part + jnp.sum(hit.reshape(tk // 8, 8, tq), axis=0)

        part = lax.fori_loop(0, nkc, cnt_chunk, jnp.zeros((8, tq), I32))
        return jnp.sum(part, axis=0, keepdims=True)

    def bit_step(b, c):
        t_u, cnt_t = c
        bit = jnp.left_shift(jnp.int32(1), 31 - b)
        cand_s = (t_u | bit) ^ jnp.int32(INT_MIN)
        cnt = count_rows(lambda j, key: key >= cand_s)
        take = cnt >= topk
        return jnp.where(take, t_u | bit, t_u), jnp.where(take, cnt, cnt_t)

    n_causal = i * tq + lax.broadcasted_iota(I32, (1, tq), 1) + 1
    t_u, cnt_t = lax.fori_loop(0, 32, bit_step, (jnp.zeros((1, tq), I32), n_causal))
    thr = jnp.maximum(t_u ^ jnp.int32(INT_MIN), jnp.int32(INT_MIN + 1))

    @pl.when(jnp.max(cnt_t) > topk)
    def _():
        need = topk - count_rows(lambda j, key: key > thr)

        def idx_step(b, cut):
            cand = cut | jnp.left_shift(jnp.int32(1), idx_bits - 1 - b)
            cnt = count_rows(lambda j, key: (key == thr) & ((j * tk + k_off) < cand))
            return jnp.where(cnt <= need, cand, cut)

        cut = lax.fori_loop(0, idx_bits, idx_step, jnp.zeros((1, tq), I32))

        def demote(j, carry):
            key = key_ref[j]
            drop = (key == thr) & ((j * tk + k_off) >= cut)
            key_ref[j] = jnp.where(drop, key - 1, key)
            return carry

        lax.fori_loop(0, nkc, demote, 0)

    aw = A_HEADS * tq
    acc_ref[...] = jnp.zeros(acc_ref.shape, F32)

    def scores(j, s_ref):
        kc = k_ref[pl.ds(pl.multiple_of(j * tk, tk), tk), :]
        sel = key_ref[j] >= thr
        cmax = []
        for h in range(A_HEADS):
            sl = slice(h * tq, (h + 1) * tq)
            s_h = jnp.dot(kc, qt_ref[0, :, sl], preferred_element_type=F32)
            s_h = jnp.where(sel, s_h, -jnp.inf)
            s_ref[:, sl] = s_h
            cmax.append(jnp.max(s_h, axis=0, keepdims=True))
        return jnp.concatenate(cmax, axis=1)

    def absorb(j, s_ref, p_ref, cmax, m_old):
        m_new = jnp.maximum(m_old, cmax)
        alpha = jnp.exp2(m_old - m_new)
        for h in range(A_HEADS):
            sl = slice(h * tq, (h + 1) * tq)
            p_ref[:, sl] = jnp.exp2(s_ref[:, sl] - m_new[:, sl]).astype(BF16)
        acc_ref[...] = alpha * acc_ref[...] + jnp.dot(vt_ref[j], p_ref[...], preferred_element_type=F32)
        return m_new

    def pair(t, carry):
        m, cm_a = carry
        a = 2 * t
        cm_b = scores(a + 1, sb_ref)
        m = absorb(a, sa_ref, pa_ref, cm_a, m)
        cm_next = scores(jnp.minimum(a + 2, nkc - 1), sa_ref)
        m = absorb(a + 1, sb_ref, pb_ref, cm_b, m)
        return m, cm_next

    m_fin, cm_last = lax.fori_loop(0, nkc // 2, pair, (jnp.full((1, aw), -1e30, F32), scores(0, sa_ref)))

    @pl.when(nkc % 2 == 1)
    def _():
        absorb(nkc - 1, sa_ref, pa_ref, cm_last, m_fin)

    out_t = acc_ref[0:HEAD_DIM, :] / acc_ref[HEAD_DIM:HEAD_DIM + 1, :]
    for h in range(A_HEADS):
        o_ref[:, h * HEAD_DIM:(h + 1) * HEAD_DIM] = out_t[:, h * tq:(h + 1) * tq].T.astype(o_ref.dtype)


def _dsa(qit, qt, wt, ki, k, vt, batch, seq, tq, tk):
    n = batch * seq
    nqb = seq // tq
    nkb = seq // tk
    topk = min(TOPK_MAX, seq // 4)
    aw = A_HEADS * HEAD_DIM
    kern = functools.partial(_dsa_kernel, tq=tq, tk=tk, topk=topk, idx_bits=seq.bit_length())
    qblk = lambda b, i: (b * nqb + i, 0, 0)
    return pl.pallas_call(
        kern,
        grid=(batch, nqb),
        in_specs=[
            pl.BlockSpec((1, HEAD_DIM, IDX_HEADS * tq), qblk),
            pl.BlockSpec((1, HEAD_DIM, A_HEADS * tq), qblk),
            pl.BlockSpec((1, 1, IDX_HEADS * tq), qblk),
            pl.BlockSpec((seq, LANES), lambda b, i: (b, 0)),
            pl.BlockSpec((seq, LANES), lambda b, i: (b, 0)),
            pl.BlockSpec((nkb, VT_ROWS, tk), lambda b, i: (b, 0, 0)),
        ],
        out_specs=pl.BlockSpec((tq, aw), lambda b, i: (b * nqb + i, 0)),
        out_shape=jax.ShapeDtypeStruct((n, aw), BF16),
        scratch_shapes=[
            pltpu.VMEM((nkb, tk, tq), I32),
            pltpu.VMEM((tk, A_HEADS * tq), F32),
            pltpu.VMEM((tk, A_HEADS * tq), F32),
            pltpu.VMEM((tk, A_HEADS * tq), BF16),
            pltpu.VMEM((tk, A_HEADS * tq), BF16),
            pltpu.VMEM((VT_ROWS, A_HEADS * tq), F32),
        ],
        compiler_params=_cparams(("parallel", "arbitrary"), 48),
        name="dsa",
    )(qit, qt, wt, ki, k, vt)


HG_SUB = 16


def _hgrn_diag_exact(q, kk, v, lc, rsub, nsub):
    chunk, gw = q.shape

    def bcast(x, s):
        return jnp.concatenate([jnp.broadcast_to(x[a * HG_SUB + s:a * HG_SUB + s + 1, :], (HG_SUB, gw))
                                for a in range(nsub)], axis=0)

    acc = [jnp.zeros((chunk, HEAD_DIM), F32) for _ in range(G_HEADS)]
    for s in range(HG_SUB):
        term = q * bcast(kk, s) * jnp.exp(jnp.minimum(lc - bcast(lc, s), 0.0))
        term = jnp.where(rsub >= s, term, 0.0)
        vrow = bcast(v, s)
        for h in range(G_HEADS):
            sl = slice(h * HEAD_DIM, (h + 1) * HEAD_DIM)
            acc[h] = acc[h] + jnp.sum(term[:, sl], axis=1, keepdims=True) * vrow[:, sl]
    return jnp.concatenate(acc, axis=1)


def _hgrn_gates(hq_ref, hf_ref, hi_ref, lb_ref):
    chunk, gw = hq_ref.shape
    lb = lb_ref[...]
    f = lb + (1.0 - lb) * _sigmoid(hf_ref[...])
    kk = 1.0 - f
    hq = hq_ref[...].astype(F32)
    q = hq * _sigmoid(hq)
    v = hi_ref[...].astype(F32)
    rsub = lax.broadcasted_iota(I32, (chunk, gw), 0) & (HG_SUB - 1)
    lc = jnp.log(f)
    sh = 1
    while sh < HG_SUB:
        lc = lc + jnp.where(rsub >= sh, pltpu.roll(lc, sh, 0), 0.0)
        sh *= 2
    return q, kk, v, lc, rsub


def _hgrn_diag_scores(q, kk, lc, h):
    chunk = q.shape[0]
    sl = slice(h * HEAD_DIM, (h + 1) * HEAD_DIM)
    qd = (q[:, sl] * jnp.exp(lc[:, sl])).astype(BF16)
    kd = (kk[:, sl] * jnp.exp(jnp.minimum(-lc[:, sl], EXP_CLAMP))).astype(BF16)
    ti = lax.broadcasted_iota(I32, (chunk, chunk), 0)
    si = lax.broadcasted_iota(I32, (chunk, chunk), 1)
    diag_ok = (ti // HG_SUB == si // HG_SUB) & (si <= ti)
    return jnp.where(diag_ok, _dot_nt(qd, kd), 0.0).astype(BF16)


def _hgrn_kernel(need_ref, hq_ref, hf_ref, hi_ref, hg_ref, lb_ref, gn_ref, o_ref, st_ref, oa_ref, *, chunk):
    nsub = chunk // HG_SUB
    gw = hq_ref.shape[1]

    @pl.when(pl.program_id(1) == 0)
    def _():
        st_ref[...] = jnp.zeros(st_ref.shape, F32)

    q, kk, v, lc, rsub = _hgrn_gates(hq_ref, hf_ref, hi_ref, lb_ref)
    def per_sub(vals):
        return jnp.concatenate([jnp.broadcast_to(x, (HG_SUB, gw)) for x in vals], axis=0)

    tots = [lc[(a + 1) * HG_SUB - 1:(a + 1) * HG_SUB, :] for a in range(nsub)]
    ends = []
    run = jnp.zeros((1, gw), F32)
    for a in range(nsub):
        run = run + tots[a]
        ends.append(run)
    cum_l = ends[-1]
    cum = lc + per_sub([jnp.zeros((1, gw), F32)] + ends[:-1])

    qs = (q * jnp.exp(cum)).astype(BF16)
    koff = kk * jnp.exp(per_sub(tots) - lc)
    ke = (koff * per_sub([jnp.exp(cum_l - e) for e in ends])).astype(BF16)
    koff_b = koff.astype(BF16)
    qoff = []
    kmask = []
    for b in range(nsub - 1):
        r0 = (b + 1) * HG_SUB
        qb = (q[r0:, :] * jnp.exp(cum[r0:, :] - ends[b])).astype(BF16)
        qoff.append(jnp.concatenate([jnp.zeros((r0, gw), BF16), qb], axis=0))
        parts = [koff_b[b * HG_SUB:r0, :], jnp.zeros((chunk - r0, gw), BF16)]
        if b > 0:
            parts.insert(0, jnp.zeros((b * HG_SUB, gw), BF16))
        kmask.append(jnp.concatenate(parts, axis=0))
    vb = v.astype(BF16)
    decay_l = jnp.exp(cum_l)

    for h in range(G_HEADS):
        sl = slice(h * HEAD_DIM, (h + 1) * HEAD_DIM)
        qcat = jnp.concatenate([qo[:, sl] for qo in qoff], axis=1)
        kcat = jnp.concatenate([km[:, sl] for km in kmask], axis=1)
        a = _hgrn_diag_scores(q, kk, lc, h) + _dot_nt(qcat, kcat).astype(BF16)
        st = st_ref[h]
        oa_ref[:, sl] = jnp.dot(a, vb[:, sl], preferred_element_type=F32) + _dot_nt(qs[:, sl], st.astype(BF16))
        upd = jnp.dot(v[:, sl].T.astype(BF16), ke[:, sl], preferred_element_type=F32)
        st_ref[h] = st * decay_l[:, sl] + upd

    @pl.when(need_ref[0] != 0)
    def _():
        q2, kk2, v2, lc2, rsub2 = _hgrn_gates(hq_ref, hf_ref, hi_ref, lb_ref)

        @pl.when(jnp.max(-lc2) > EXP_CLAMP)
        def _():
            v2b = v2.astype(BF16)
            fast = [jnp.dot(_hgrn_diag_scores(q2, kk2, lc2, h), v2b[:, h * HEAD_DIM:(h + 1) * HEAD_DIM],
                            preferred_element_type=F32) for h in range(G_HEADS)]
            oa_ref[...] += _hgrn_diag_exact(q2, kk2, v2, lc2, rsub2, nsub) - jnp.concatenate(fast, axis=1)

    o = oa_ref[...]
    ms = jnp.mean(o * o, axis=-1, keepdims=True)
    hg = hg_ref[...].astype(F32)
    y = o * lax.rsqrt(ms + EPS) * gn_ref[...] * (hg * _sigmoid(hg))
    o_ref[...] = y.astype(o_ref.dtype)


def _hgrn(ph, pt, lb, g_hnorm, batch, seq, chunk):
    n = batch * seq
    gw = G_HEADS * HEAD_DIM
    ncb = seq // chunk
    blk = lambda c: pl.BlockSpec((chunk, gw), lambda b, i, nd: (b * ncb + i, c))
    need = (HG_SUB * jnp.log(jnp.min(lb)) < -EXP_CLAMP).astype(I32).reshape(1)
    grid_spec = pltpu.PrefetchScalarGridSpec(
        num_scalar_prefetch=1,
        grid=(batch, ncb),
        in_specs=[blk(0), blk(1), blk(1), blk(2),
                  pl.BlockSpec((1, gw), lambda b, i, nd: (0, 0)),
                  pl.BlockSpec((1, gw), lambda b, i, nd: (0, 0))],
        out_specs=pl.BlockSpec((chunk, gw), lambda b, i, nd: (b * ncb + i, 0)),
        scratch_shapes=[pltpu.VMEM((G_HEADS, HEAD_DIM, HEAD_DIM), F32),
                        pltpu.VMEM((chunk, gw), F32)],
    )
    return pl.pallas_call(
        functools.partial(_hgrn_kernel, chunk=chunk),
        grid_spec=grid_spec,
        out_shape=jax.ShapeDtypeStruct((n, gw), BF16),
        compiler_params=_cparams(("parallel", "arbitrary"), 32),
        name="hgrn",
    )(need, pt, ph, pt, pt, lb.reshape(1, gw), g_hnorm.reshape(1, gw))


def _merge_kernel(ya_ref, yr_ref, ga_ref, gb_ref, x_ref, wpa_ref, wpb_ref, wo_ref, o_ref, mg_ref):
    c = pl.program_id(1)
    nc = pl.num_programs(1)
    tc = ga_ref.shape[1]

    ma = jnp.dot(ya_ref[...], wpa_ref[...], preferred_element_type=F32)
    mb = jnp.dot(yr_ref[...], wpb_ref[...], preferred_element_type=F32)
    merged = _sigmoid(ga_ref[...].astype(F32)) * ma + _sigmoid(gb_ref[...].astype(F32)) * mb
    merged = merged.astype(BF16)
    for k in range(mg_ref.shape[1] // tc):
        @pl.when(c == k)
        def _():
            mg_ref[:, k * tc:(k + 1) * tc] = merged

    @pl.when(c == nc - 1)
    def _():
        o_ref[...] = x_ref[...] + jnp.dot(mg_ref[...], wo_ref[...], preferred_element_type=F32)


def _merge(y_att, y_rec, pt, x2, w_pa, w_pb, w_o, tm, tc):
    n, d = x2.shape
    aw = y_att.shape[1]
    gw = y_rec.shape[1]
    ga0 = 3072 // tc
    gb0 = 5120 // tc
    return pl.pallas_call(
        _merge_kernel,
        grid=(n // tm, d // tc),
        in_specs=[
            pl.BlockSpec((tm, aw), lambda i, c: (i, 0)),
            pl.BlockSpec((tm, gw), lambda i, c: (i, 0)),
            pl.BlockSpec((tm, tc), lambda i, c: (i, ga0 + c)),
            pl.BlockSpec((tm, tc), lambda i, c: (i, gb0 + c)),
            pl.BlockSpec((tm, d), lambda i, c: (i, 0)),
            pl.BlockSpec((aw, tc), lambda i, c: (0, c)),
            pl.BlockSpec((gw, tc), lambda i, c: (0, c)),
            pl.BlockSpec((d, d), lambda i, c: (0, 0)),
        ],
        out_specs=pl.BlockSpec((tm, d), lambda i, c: (i, 0)),
        out_shape=jax.ShapeDtypeStruct((n, d), F32),
        scratch_shapes=[pltpu.VMEM((tm, d), BF16)],
        compiler_params=_cparams(("parallel", "arbitrary"), 56),
        name="merge",
    )(y_att, y_rec, pt, pt, x2, w_pa, w_pb, w_o)


def _router_kernel(x_ref, g_ref, wr_ref, br_ref, h_ref, r_ref, cnt_ref):
    x = x_ref[...]
    ms = jnp.mean(x * x, axis=-1, keepdims=True)
    h = x * lax.rsqrt(ms + EPS) * g_ref[...]
    h_ref[...] = _pack_halves(h)
    lg = jnp.dot(h.astype(BF16), wr_ref[...], preferred_element_type=F32) + br_ref[...]
    tm = lg.shape[0]
    lane = lax.broadcasted_iota(I32, (tm, LANES), 1)
    neg = jnp.float32(-1e30)
    big = jnp.float32(LANES)
    is_g = lane < N_GROUPS
    lgg = jnp.where(is_g, lg, neg)
    mg = jnp.max(lgg, axis=1, keepdims=True)
    zg = jnp.sum(jnp.where(is_g, jnp.exp(lgg - mg), 0.0), axis=1, keepdims=True)
    p_g = 1.0 / zg
    grp = jnp.min(jnp.where(is_g & (lgg == mg), lane.astype(F32), big), axis=1, keepdims=True)
    e_id = lane - N_GROUPS
    e_grp = lax.shift_right_arithmetic(e_id, jnp.int32(EXP_PER_GROUP.bit_length() - 1)).astype(F32)
    e_idf = e_id.astype(F32)
    is_e = (e_id >= 0) & (e_id < N_EXPERTS) & (e_grp == grp)
    lge = jnp.where(is_e, lg, neg)
    m1 = jnp.max(lge, axis=1, keepdims=True)
    i1 = jnp.min(jnp.where(is_e & (lge == m1), e_idf, big), axis=1, keepdims=True)
    is_e2 = is_e & (e_idf != i1)
    lge2 = jnp.where(is_e2, lg, neg)
    m2 = jnp.max(lge2, axis=1, keepdims=True)
    i2 = jnp.min(jnp.where(is_e2 & (lge2 == m2), e_idf, big), axis=1, keepdims=True)
    ex2 = jnp.exp(m2 - m1)
    den = 1.0 + ex2
    g1 = p_g / den
    g2 = p_g * ex2 / den
    @pl.when(pl.program_id(0) == 0)
    def _():
        cnt_ref[...] = jnp.zeros(cnt_ref.shape, F32)

    lanef = lane.astype(F32)
    oh1 = lanef == i1
    oh2 = lanef == i2
    oh = jnp.where(oh1 | oh2, 1.0, 0.0)
    ti = lax.broadcasted_iota(I32, (tm, tm), 0)
    si = lax.broadcasted_iota(I32, (tm, tm), 1)
    tri = jnp.where(si < ti, 1.0, 0.0).astype(BF16)
    before = jnp.dot(tri, oh.astype(BF16), preferred_element_type=F32) + cnt_ref[...]
    rank1 = jnp.sum(jnp.where(oh1, before, 0.0), axis=1, keepdims=True)
    rank2 = jnp.sum(jnp.where(oh2, before, 0.0), axis=1, keepdims=True)
    cnt_ref[...] += jnp.sum(oh, axis=0, keepdims=True)

    r = jnp.where(lane == 0, i1, 0.0)
    r = jnp.where(lane == 1, i2, r)
    r = jnp.where(lane == 2, g1, r)
    r = jnp.where(lane == 3, g2, r)
    r = jnp.where(lane == 4, rank1, r)
    r = jnp.where(lane == 5, rank2, r)
    r_ref[...] = r


def _router(x1, g, w_r, b_r, tm):
    n, d = x1.shape
    return pl.pallas_call(
        _router_kernel,
        grid=(n // tm,),
        in_specs=[
            pl.BlockSpec((tm, d), lambda i: (i, 0)),
            pl.BlockSpec((1, d), lambda i: (0, 0)),
            pl.BlockSpec((d, LANES), lambda i: (0, 0)),
            pl.BlockSpec((1, LANES), lambda i: (0, 0)),
        ],
        out_specs=[pl.BlockSpec((tm, d // 2), lambda i: (i, 0)),
                   pl.BlockSpec((tm, LANES), lambda i: (i, 0)),
                   pl.BlockSpec((1, LANES), lambda i: (0, 0))],
        out_shape=[jax.ShapeDtypeStruct((n, d // 2), U32),
                   jax.ShapeDtypeStruct((n, LANES), F32),
                   jax.ShapeDtypeStruct((1, LANES), F32)],
        compiler_params=_cparams(("arbitrary",), 40),
        name="router",
    )(x1, g.reshape(1, d), w_r, b_r)


def _dispatch_kernel(prow_ref, h_ref, xs_hbm, sem, *, tm):
    base = 2 * tm * pl.program_id(0)

    def scatter_group(g, carry):
        r0 = pl.multiple_of(g * DMA_GROUP, DMA_GROUP)
        rows = h_ref.at[pl.ds(r0, DMA_GROUP), :]
        for u in range(DMA_GROUP):
            for s in range(2):
                dst_row = prow_ref[base + 2 * r0 + (2 * u + s)]
                pltpu.make_async_copy(rows.at[pl.ds(u, 1), :], xs_hbm.at[pl.ds(dst_row, 1), :], sem).start()
        return carry

    lax.fori_loop(0, tm // DMA_GROUP, scatter_group, 0)
    for _ in range(2):
        pltpu.make_async_copy(h_ref, xs_hbm.at[pl.ds(0, tm), :], sem).wait()


def _dispatch(h2, prow, tm):
    n, d = h2.shape
    grid_spec = pltpu.PrefetchScalarGridSpec(
        num_scalar_prefetch=1,
        grid=(n // tm,),
        in_specs=[pl.BlockSpec((tm, d), lambda i, pr: (i, 0))],
        out_specs=pl.BlockSpec(memory_space=pl.ANY),
        scratch_shapes=[pltpu.SemaphoreType.DMA(())],
    )
    return pl.pallas_call(
        functools.partial(_dispatch_kernel, tm=tm),
        grid_spec=grid_spec,
        out_shape=jax.ShapeDtypeStruct((2 * n, d), h2.dtype),
        compiler_params=_cparams(("arbitrary",), 32),
        name="dispatch",
    )(prow, h2)


def _moe_kernel(pt_ref, pe_ref, plo_ref, phi_ref, slot_ref, nxt_ref, np_ref, x_ref, wg_hbm, wu_hbm, wd_hbm,
                y_ref, wgf_ref, wuf_ref, wdf_ref, wgb_ref, wub_ref, wdb_ref, yacc_ref, sem):
    p = pl.program_id(0)
    active = p < np_ref[0]
    prev = jnp.maximum(p - 1, 0)
    new_expert = (p == 0) | (pe_ref[p] != pe_ref[prev])
    new_tile = (p == 0) | (pt_ref[p] != pt_ref[prev])
    slot = slot_ref[p]

    def weight_copies(e, s):
        return (pltpu.make_async_copy(wg_hbm.at[e], wgf_ref.at[s], sem.at[s]),
                pltpu.make_async_copy(wu_hbm.at[e], wuf_ref.at[s], sem.at[s]),
                pltpu.make_async_copy(wd_hbm.at[e], wdf_ref.at[s], sem.at[s]))

    @pl.when(p == 0)
    def _():
        for cp in weight_copies(pe_ref[0], 0):
            cp.start()

    @pl.when(active & new_expert)
    def _():
        for cp in weight_copies(pe_ref[p], slot):
            cp.wait()
        wgb_ref[...] = wgf_ref[slot].astype(BF16)
        wub_ref[...] = wuf_ref[slot].astype(BF16)
        wdb_ref[...] = wdf_ref[slot].astype(BF16)

        @pl.when(nxt_ref[p] >= 0)
        def _():
            for cp in weight_copies(nxt_ref[p], 1 - slot):
                cp.start()

    @pl.when(active)
    def _():
        tm, half = x_ref.shape
        lo, hi = _unpack_halves(x_ref[...])
        xl = lo.astype(BF16)
        xh = hi.astype(BF16)
        a = (jnp.dot(xl, wgb_ref[0:half, :], preferred_element_type=F32)
             + jnp.dot(xh, wgb_ref[half:, :], preferred_element_type=F32))
        u = (jnp.dot(xl, wub_ref[0:half, :], preferred_element_type=F32)
             + jnp.dot(xh, wub_ref[half:, :], preferred_element_type=F32))
        hmid = (a * _sigmoid(a) * u).astype(BF16)
        yv = jnp.dot(hmid, wdb_ref[...], preferred_element_type=F32)
        row = lax.broadcasted_iota(I32, (tm, 1), 0)
        mine = (row >= plo_ref[p]) & (row < phi_ref[p])
        yv = jnp.where(mine, yv, 0.0)

        @pl.when(new_tile)
        def _():
            yacc_ref[...] = yv

        @pl.when(jnp.logical_not(new_tile))
        def _():
            yacc_ref[...] += yv

        nxt_p = jnp.minimum(p + 1, pl.num_programs(0) - 1)
        last_of_tile = (p == np_ref[0] - 1) | (pt_ref[nxt_p] != pt_ref[p])

        @pl.when(last_of_tile)
        def _():
            y_ref[...] = _pack_halves(yacc_ref[...])


def _moe(xs, pairs, w_gate, w_up, w_down, tm):
    pair_tile, pair_expert, pair_lo, pair_hi, pair_slot, pair_next, n_pairs = pairs
    na, half = xs.shape
    d, de = w_gate.shape[1:]
    max_pairs = pair_tile.shape[0]
    grid_spec = pltpu.PrefetchScalarGridSpec(
        num_scalar_prefetch=7,
        grid=(max_pairs,),
        in_specs=[
            pl.BlockSpec((tm, half), lambda p, pt, *_: (pt[p], 0)),
            pl.BlockSpec(memory_space=pl.ANY),
            pl.BlockSpec(memory_space=pl.ANY),
            pl.BlockSpec(memory_space=pl.ANY),
        ],
        out_specs=pl.BlockSpec((tm, half), lambda p, pt, *_: (pt[p], 0)),
        scratch_shapes=[
            pltpu.VMEM((2, d, de), F32),
            pltpu.VMEM((2, d, de), F32),
            pltpu.VMEM((2, de, d), F32),
            pltpu.VMEM((d, de), BF16),
            pltpu.VMEM((d, de), BF16),
            pltpu.VMEM((de, d), BF16),
            pltpu.VMEM((tm, d), F32),
            pltpu.SemaphoreType.DMA((2,)),
        ],
    )
    return pl.pallas_call(
        _moe_kernel,
        grid_spec=grid_spec,
        out_shape=jax.ShapeDtypeStruct((na, half), U32),
        compiler_params=_cparams(("arbitrary",), 56),
        name="moe",
    )(pair_tile, pair_expert, pair_lo, pair_hi, pair_slot, pair_next, n_pairs, xs, w_gate, w_up, w_down)


def _moe_rows(r, cnt):
    sizes = cnt[0, :N_EXPERTS].astype(I32)
    starts = jnp.cumsum(sizes) - sizes
    e12 = r[:, 0:2].astype(I32)
    rank = r[:, 4:6].astype(I32)
    onehot = e12[:, :, None] == jnp.arange(N_EXPERTS, dtype=I32)[None, None, :]
    start_a = jnp.sum(jnp.where(onehot, starts[None, None, :], 0), axis=-1)
    return (start_a + rank).reshape(-1).astype(I32), sizes


def _moe_pairs(sizes, na, tm):
    ntiles = na // tm
    max_pairs = ntiles + N_EXPERTS - 1
    ends = jnp.cumsum(sizes)
    starts = ends - sizes
    t0 = jnp.arange(ntiles, dtype=I32) * tm
    e_lo = jnp.searchsorted(ends, t0, side="right").astype(I32)
    e_hi = jnp.searchsorted(ends, t0 + (tm - 1), side="right").astype(I32)
    per_tile = e_hi - e_lo + 1
    pend = jnp.cumsum(per_tile)
    pstart = pend - per_tile
    n_pairs = pend[-1]
    pc = jnp.minimum(jnp.arange(max_pairs, dtype=I32), n_pairs - 1)
    tile = jnp.searchsorted(pend, pc, side="right").astype(I32)
    expert = e_lo[tile] + (pc - pstart[tile])
    lo = jnp.clip(starts[expert] - tile * tm, 0, tm).astype(I32)
    hi = jnp.clip(ends[expert] - tile * tm, 0, tm).astype(I32)
    expert = expert.astype(I32)
    first = jnp.concatenate([jnp.ones((1,), bool), expert[1:] != expert[:-1]])
    slot = ((jnp.cumsum(first) - 1) % 2).astype(I32)
    idx = jnp.arange(max_pairs, dtype=I32)
    first_at = jnp.where(first, idx, max_pairs)
    nxt_first = jnp.concatenate([lax.cummin(first_at[::-1])[::-1][1:], jnp.full((1,), max_pairs, I32)])
    nxt = jnp.where(nxt_first < max_pairs, expert[jnp.minimum(nxt_first, max_pairs - 1)], -1).astype(I32)
    return tile, expert, lo, hi, slot, nxt, n_pairs.astype(I32).reshape(1)


def _final_kernel(prow_ref, x_ref, r_ref, g_ref, y_hbm, o_ref, ybuf, sem, *, tm):
    i = pl.program_id(0)
    cur = i % 2

    def issue(step, buf):
        base = 2 * tm * step

        def gather_group(g, carry):
            r0 = pl.multiple_of(g * DMA_GROUP, DMA_GROUP)
            for s in range(2):
                rows = ybuf.at[buf, s, pl.ds(r0, DMA_GROUP), :]
                for u in range(DMA_GROUP):
                    src_row = prow_ref[base + 2 * r0 + (2 * u + s)]
                    pltpu.make_async_copy(y_hbm.at[pl.ds(src_row, 1), :], rows.at[pl.ds(u, 1), :],
                                          sem.at[buf]).start()
            return carry

        lax.fori_loop(0, tm // DMA_GROUP, gather_group, 0)

    @pl.when(i == 0)
    def _():
        issue(0, 0)

    @pl.when(i + 1 < pl.num_programs(0))
    def _():
        issue(i + 1, 1 - cur)

    for s in range(2):
        pltpu.make_async_copy(y_hbm.at[pl.ds(0, tm), :], ybuf.at[cur, s], sem.at[cur]).wait()
    r = r_ref[...]
    g1 = r[:, 2:3]
    g2 = r[:, 3:4]
    half = ybuf.shape[-1]
    lo0, hi0 = _unpack_halves(ybuf[cur, 0])
    lo1, hi1 = _unpack_halves(ybuf[cur, 1])
    xa = x_ref[:, :half] + g1 * lo0 + g2 * lo1
    xb = x_ref[:, half:] + g1 * hi0 + g2 * hi1
    ms = (jnp.sum(xa * xa, axis=-1, keepdims=True) + jnp.sum(xb * xb, axis=-1, keepdims=True)) / (2 * half)
    rs = lax.rsqrt(ms + EPS)
    o_ref[:, :half] = xa * rs * g_ref[:, :half]
    o_ref[:, half:] = xb * rs * g_ref[:, half:]


def _final(x1, y, r, prow, g, tm):
    n, d = x1.shape
    grid_spec = pltpu.PrefetchScalarGridSpec(
        num_scalar_prefetch=1,
        grid=(n // tm,),
        in_specs=[
            pl.BlockSpec((tm, d), lambda i, pr: (i, 0)),
            pl.BlockSpec((tm, LANES), lambda i, pr: (i, 0)),
            pl.BlockSpec((1, d), lambda i, pr: (0, 0)),
            pl.BlockSpec(memory_space=pl.ANY),
        ],
        out_specs=pl.BlockSpec((tm, d), lambda i, pr: (i, 0)),
        scratch_shapes=[pltpu.VMEM((2, 2, tm, d // 2), U32), pltpu.SemaphoreType.DMA((2,))],
    )
    return pl.pallas_call(
        functools.partial(_final_kernel, tm=tm),
        grid_spec=grid_spec,
        out_shape=jax.ShapeDtypeStruct((n, d), F32),
        compiler_params=_cparams(("arbitrary",), 48),
        name="final",
    )(prow, x1, r, g.reshape(1, d), y)


def _pack_w_in(w):
    d, din = w.shape
    tc = 256
    wt = jnp.swapaxes(w, 0, 1)

    def pack_kernel(w_ref, o_ref):
        o_ref[0:912, :] = w_ref[0:912, :].astype(BF16)
        o_ref[912:1024, :] = jnp.zeros((LANES - IDX_HEADS, tc), BF16)
        o_ref[1024:2048, :] = w_ref[1936:2960, :].astype(BF16)
        o_ref[2048:3072, :] = w_ref[912:1936, :].astype(BF16)
        o_ref[3072:, :] = w_ref[2960:, :].astype(BF16)

    return pl.pallas_call(
        pack_kernel,
        grid=(d // tc,),
        in_specs=[pl.BlockSpec((din, tc), lambda i: (0, i))],
        out_specs=pl.BlockSpec((PH_COLS + PT_COLS, tc), lambda i: (0, i)),
        out_shape=jax.ShapeDtypeStruct((PH_COLS + PT_COLS, d), BF16),
        compiler_params=_cparams(("parallel",), 48),
        name="pack_w_in",
    )(wt)


def _rope_tables(seq):
    half = HEAD_DIM // 2
    inv = 1.0 / (ROPE_THETA ** (jnp.arange(0, HEAD_DIM, 2, dtype=F32) / HEAD_DIM))
    ang = jnp.arange(seq, dtype=F32)[:, None] * inv[None, :]
    c, s = jnp.cos(ang), jnp.sin(ang)
    del half
    return jnp.concatenate([c, c], axis=1), jnp.concatenate([-s, s], axis=1)


def _layer(x2, batch, seq, cos2, sin2, lb, g_norm1, w_in, g_cq, w_uq, w_qidx, g_hnorm, w_pa, w_pb, w_o,
           g_norm2, w_grp, b_grp, w_exp, b_exp, w_gate, w_up, w_down, g_out):
    n, d = x2.shape
    ph, pt = _in_proj(x2, g_norm1, _pack_w_in(w_in), tm=min(1024, n), tn=1024)
    tq, tk = min(256, seq), min(256, seq)
    qt, qit, wt, k, ki, vt = _dsa_prep(ph, cos2, sin2, g_cq, w_uq.astype(BF16), w_qidx.astype(BF16), seq, tm=tk, tq=tq)
    y_att = _dsa(qit, qt, wt, ki, k, vt, batch, seq, tq=tq, tk=tk)
    y_rec = _hgrn(ph, pt, lb, g_hnorm, batch, seq, chunk=64)
    x1 = _merge(y_att, y_rec, pt, x2, w_pa.astype(BF16), w_pb.astype(BF16), w_o.astype(BF16),
                tm=min(512, n), tc=512)
    w_r = jnp.concatenate([w_grp, w_exp, jnp.zeros((d, LANES - N_GROUPS - N_EXPERTS), F32)], axis=1).astype(BF16)
    b_r = jnp.concatenate([b_grp, b_exp, jnp.zeros((LANES - N_GROUPS - N_EXPERTS,), F32)]).reshape(1, LANES)
    h2, r, cnt = _router(x1, g_norm2, w_r, b_r, tm=min(512, n))
    prow, sizes = _moe_rows(r, cnt)
    xs = _dispatch(h2, prow, tm=min(512, n))
    tm_moe = 256
    assert (2 * n) % tm_moe == 0
    y = _moe(xs, _moe_pairs(sizes, 2 * n, tm_moe), w_gate, w_up, w_down, tm_moe)
    return _final(x1, y, r, prow, g_out, tm=min(512, n))


def kernel(x, g_norm1, w_in, g_cq, w_uq, w_qidx, lb_logits, g_hnorm, w_pa, w_pb, w_o, g_norm2, w_grp, b_grp,
           w_exp, b_exp, w_gate, w_up, w_down, g_final):
    batch, seq, d = x.shape
    depth = g_norm1.shape[0]
    cos2, sin2 = _rope_tables(seq)
    lb_all = jnp.cumsum(jax.nn.softmax(lb_logits.astype(F32), axis=0), axis=0)
    x2 = x.reshape(batch * seq, d)
    for l in range(depth):
        assert depth == 1
        x2 = _layer(x2, batch, seq, cos2, sin2, lb_all[l], g_norm1[l], w_in[l], g_cq[l], w_uq[l], w_qidx[l],
                    g_hnorm[l], w_pa[l], w_pb[l], w_o[l], g_norm2[l], w_grp[l], b_grp[l], w_exp[l], b_exp[l],
                    w_gate[l], w_up[l], w_down[l], g_final)
    return x2.reshape(batch, seq, d)
```

```python
import functools
import math

import jax
import jax.numpy as jnp
from jax import lax
from jax.experimental import pallas as pl
from jax.experimental.pallas import tpu as pltpu

F32 = jnp.float32
BF16 = jnp.bfloat16
I32 = jnp.int32
U32 = jnp.uint32

EPS = 1e-6
ROPE_THETA = 10000.0
LANES = 128
MXU_LANES = 256
DMA_GROUP = 32
VT_ROWS = 128 + 16

A_HEADS = 8
HEAD_DIM = 128
IDX_HEADS = 16
Q_RANK = 512
TOPK_MAX = 256
G_HEADS = 8
N_GROUPS = 4
EXP_PER_GROUP = 8
N_EXPERTS = N_GROUPS * EXP_PER_GROUP
D_EXPERT = 512

INT_MIN = -2147483648
EXP_CLAMP = 80.0


def _cparams(sem, vmem_mb):
    return pltpu.CompilerParams(dimension_semantics=sem, vmem_limit_bytes=vmem_mb << 20)


def _pack_halves(x):
    c = x.shape[1] // 2
    bits = lax.bitcast_convert_type(x.astype(BF16).astype(F32), U32)
    return lax.shift_right_logical(bits[:, :c], jnp.uint32(16)) | (bits[:, c:] & jnp.uint32(0xFFFF0000))


def _unpack_halves(w):
    lo = lax.bitcast_convert_type(lax.shift_left(w, jnp.uint32(16)), F32)
    hi = lax.bitcast_convert_type(w & jnp.uint32(0xFFFF0000), F32)
    return lo, hi


def _sigmoid(x):
    return 1.0 / (1.0 + jnp.exp(-x))


PH_COLS = 2048
PT_COLS = 7168


def _inproj_kernel(x_ref, g_ref, w_ref, oh_ref, ot_ref, xn_ref, *, nh):
    j = pl.program_id(1)

    @pl.when(j == 0)
    def _():
        x = x_ref[...]
        ms = jnp.mean(x * x, axis=-1, keepdims=True)
        xn_ref[...] = (x * lax.rsqrt(ms + EPS) * g_ref[...]).astype(BF16)

    y = _dot_nt(xn_ref[...], w_ref[...])

    @pl.when(j < nh)
    def _():
        oh_ref[...] = y

    @pl.when(j >= nh)
    def _():
        ot_ref[...] = y.astype(ot_ref.dtype)


def _in_proj(x2, g, w, tm, tn):
    n, d = x2.shape
    nh = PH_COLS // tn
    nt = PT_COLS // tn
    return pl.pallas_call(
        functools.partial(_inproj_kernel, nh=nh),
        grid=(n // tm, nh + nt),
        in_specs=[
            pl.BlockSpec((tm, d), lambda i, j: (i, 0)),
            pl.BlockSpec((1, d), lambda i, j: (0, 0)),
            pl.BlockSpec((tn, d), lambda i, j: (j, 0)),
        ],
        out_specs=[pl.BlockSpec((tm, tn), lambda i, j: (i, jnp.minimum(j, nh - 1))),
                   pl.BlockSpec((tm, tn), lambda i, j: (i, jnp.maximum(j - nh, 0)))],
        out_shape=[jax.ShapeDtypeStruct((n, PH_COLS), F32),
                   jax.ShapeDtypeStruct((n, PT_COLS), BF16)],
        scratch_shapes=[pltpu.VMEM((tm, d), BF16)],
        compiler_params=_cparams(("parallel", "arbitrary"), 56),
        name="in_proj",
    )(x2, g.reshape(1, d), w)


def _rope(x, c2, s2):
    return x * c2 + pltpu.roll(x, HEAD_DIM // 2, 1) * s2


def _prep_kernel(cq_ref, ka_ref, va_ref, kx_ref, wi_ref, cos_ref, sin_ref, g_ref, wuq_ref, wqi_ref,
                 qt_ref, qit_ref, wt_ref, k_ref, ki_ref, vt_ref, *, scale, wscale, tq):
    nqb = cq_ref.shape[0] // tq
    c2 = cos_ref[...]
    s2 = sin_ref[...]
    cq = cq_ref[...]
    ms = jnp.mean(cq * cq, axis=-1, keepdims=True)
    cqn = (cq * lax.rsqrt(ms + EPS) * g_ref[...]).astype(BF16)
    q = jnp.dot(cqn, wuq_ref[...], preferred_element_type=F32)
    for h in range(A_HEADS):
        sl = slice(h * HEAD_DIM, (h + 1) * HEAD_DIM)
        qh_t = (_rope(q[:, sl], c2, s2) * scale).T
        for b in range(nqb):
            qt_ref[b, :, h * tq:(h + 1) * tq] = qh_t[:, b * tq:(b + 1) * tq].astype(BF16)
    qi = jnp.dot(cqn, wqi_ref[...], preferred_element_type=F32)
    for h in range(IDX_HEADS):
        sl = slice(h * HEAD_DIM, (h + 1) * HEAD_DIM)
        qh_t = _rope(qi[:, sl], c2, s2).T
        for b in range(nqb):
            qit_ref[b, :, h * tq:(h + 1) * tq] = qh_t[:, b * tq:(b + 1) * tq].astype(BF16)
    w_t = (wi_ref[...] * wscale).T
    for h in range(IDX_HEADS):
        for b in range(nqb):
            wt_ref[b, :, h * tq:(h + 1) * tq] = w_t[h:h + 1, b * tq:(b + 1) * tq]
    k_ref[...] = _rope(ka_ref[...], c2, s2).astype(BF16)
    ki_ref[...] = _rope(kx_ref[...], c2, s2).astype(BF16)
    vt_ref[0, 0:HEAD_DIM, :] = va_ref[...].T.astype(BF16)
    ones_row = lax.broadcasted_iota(I32, (VT_ROWS - HEAD_DIM, va_ref.shape[0]), 0) == 0
    vt_ref[0, HEAD_DIM:VT_ROWS, :] = jnp.where(ones_row, 1.0, 0.0).astype(BF16)


def _dsa_prep(p, cos2, sin2, g_cq, w_uq, w_qidx, seq, tm, tq):
    n = p.shape[0]
    nsb = seq // tm
    nqb = tm // tq
    aw = A_HEADS * HEAD_DIM
    iw = IDX_HEADS * HEAD_DIM
    row = lambda i: (i, 0)
    blk3 = lambda i: (i, 0, 0)
    kern = functools.partial(_prep_kernel, scale=HEAD_DIM ** -0.5 * math.log2(math.e),
                             wscale=IDX_HEADS ** -0.5 * HEAD_DIM ** -0.5, tq=tq)
    return pl.pallas_call(
        kern,
        grid=(n // tm,),
        in_specs=[
            pl.BlockSpec((tm, Q_RANK), lambda i: (i, 0)),
            pl.BlockSpec((tm, LANES), lambda i: (i, 4)),
            pl.BlockSpec((tm, LANES), lambda i: (i, 5)),
            pl.BlockSpec((tm, LANES), lambda i: (i, 6)),
            pl.BlockSpec((tm, LANES), lambda i: (i, 7)),
            pl.BlockSpec((tm, LANES), lambda i: (i % nsb, 0)),
            pl.BlockSpec((tm, LANES), lambda i: (i % nsb, 0)),
            pl.BlockSpec((1, Q_RANK), lambda i: (0, 0)),
            pl.BlockSpec((Q_RANK, aw), lambda i: (0, 0)),
            pl.BlockSpec((Q_RANK, iw), lambda i: (0, 0)),
        ],
        out_specs=[
            pl.BlockSpec((nqb, HEAD_DIM, A_HEADS * tq), blk3),
            pl.BlockSpec((nqb, HEAD_DIM, IDX_HEADS * tq), blk3),
            pl.BlockSpec((nqb, 1, IDX_HEADS * tq), blk3),
            pl.BlockSpec((tm, LANES), row),
            pl.BlockSpec((tm, LANES), row),
            pl.BlockSpec((1, VT_ROWS, tm), blk3),
        ],
        out_shape=[
            jax.ShapeDtypeStruct((n // tq, HEAD_DIM, A_HEADS * tq), BF16),
            jax.ShapeDtypeStruct((n // tq, HEAD_DIM, IDX_HEADS * tq), BF16),
            jax.ShapeDtypeStruct((n // tq, 1, IDX_HEADS * tq), F32),
            jax.ShapeDtypeStruct((n, LANES), BF16),
            jax.ShapeDtypeStruct((n, LANES), BF16),
            jax.ShapeDtypeStruct((n // tm, VT_ROWS, tm), BF16),
        ],
        compiler_params=_cparams(("parallel",), 40),
        name="dsa_prep",
    )(p, p, p, p, p, cos2, sin2, g_cq.reshape(1, Q_RANK), w_uq, w_qidx)


def _dot_nt(a, b):
    return lax.dot_general(a, b, (((1,), (1,)), ((), ())), preferred_element_type=F32)


def _dsa_kernel(qit_ref, qt_ref, wt_ref, ki_ref, k_ref, vt_ref, o_ref,
                key_ref, sa_ref, sb_ref, pa_ref, pb_ref, acc_ref, *, tq, tk, topk, idx_bits):
    i = pl.program_id(1)
    nkc = ((i + 1) * tq + tk - 1) // tk
    q_pos = i * tq + lax.broadcasted_iota(I32, (tk, tq), 1)
    k_off = lax.broadcasted_iota(I32, (tk, tq), 0)

    def score_chunk(j, carry):
        kic = ki_ref[pl.ds(pl.multiple_of(j * tk, tk), tk), :]
        acc = jnp.zeros((tk, tq), F32)
        hpm = max(1, MXU_LANES // tq)
        for h0 in range(0, IDX_HEADS, hpm):
            sl = slice(h0 * tq, (h0 + hpm) * tq)
            lg = jnp.dot(kic, qit_ref[0, :, sl], preferred_element_type=F32)
            x = jnp.maximum(lg, 0.0) * wt_ref[0, :, sl]
            for u in range(hpm):
                acc = acc + x[:, u * tq:(u + 1) * tq]
        acc = jnp.where(acc == 0.0, 0.0, acc)
        bits = lax.bitcast_convert_type(acc, I32)
        key = jnp.where(bits < 0, bits ^ jnp.int32(0x7FFFFFFF), bits)
        causal = (j * tk + k_off) <= q_pos
        key_ref[j] = jnp.where(causal, key, jnp.int32(INT_MIN))
        return carry

    lax.fori_loop(0, nkc, score_chunk, 0)

    def count_rows(pred):
        def cnt_chunk(j, part):
            hit = pred(j, key_ref[j]).astype(I32)
            return part + jnp.sum(hit.reshape(tk // 8, 8, tq), axis=0)

        part = lax.fori_loop(0, nkc, cnt_chunk, jnp.zeros((8, tq), I32))
        return jnp.sum(part, axis=0, keepdims=True)

    def bit_step(b, c):
        t_u, cnt_t = c
        bit = jnp.left_shift(jnp.int32(1), 31 - b)
        cand_s = (t_u | bit) ^ jnp.int32(INT_MIN)
        cnt = count_rows(lambda j, key: key >= cand_s)
        take = cnt >= topk
        return jnp.where(take, t_u | bit, t_u), jnp.where(take, cnt, cnt_t)

    n_causal = i * tq + lax.broadcasted_iota(I32, (1, tq), 1) + 1
    t_u, cnt_t = lax.fori_loop(0, 32, bit_step, (jnp.zeros((1, tq), I32), n_causal))
    thr = jnp.maximum(t_u ^ jnp.int32(INT_MIN), jnp.int32(INT_MIN + 1))

    @pl.when(jnp.max(cnt_t) > topk)
    def _():
        need = topk - count_rows(lambda j, key: key > thr)

        def idx_step(b, cut):
            cand = cut | jnp.left_shift(jnp.int32(1), idx_bits - 1 - b)
            cnt = count_rows(lambda j, key: (key == thr) & ((j * tk + k_off) < cand))
            return jnp.where(cnt <= need, cand, cut)

        cut = lax.fori_loop(0, idx_bits, idx_step, jnp.zeros((1, tq), I32))

        def demote(j, carry):
            key = key_ref[j]
            drop = (key == thr) & ((j * tk + k_off) >= cut)
            key_ref[j] = jnp.where(drop, key - 1, key)
            return carry

        lax.fori_loop(0, nkc, demote, 0)

    aw = A_HEADS * tq
    acc_ref[...] = jnp.zeros(acc_ref.shape, F32)

    def scores(j, s_ref):
        kc = k_ref[pl.ds(pl.multiple_of(j * tk, tk), tk), :]
        sel = key_ref[j] >= thr
        cmax = []
        for h in range(A_HEADS):
            sl = slice(h * tq, (h + 1) * tq)
            s_h = jnp.dot(kc, qt_ref[0, :, sl], preferred_element_type=F32)
            s_h = jnp.where(sel, s_h, -jnp.inf)
            s_ref[:, sl] = s_h
            cmax.append(jnp.max(s_h, axis=0, keepdims=True))
        return jnp.concatenate(cmax, axis=1)

    def absorb(j, s_ref, p_ref, cmax, m_old):
        m_new = jnp.maximum(m_old, cmax)
        alpha = jnp.exp2(m_old - m_new)
        for h in range(A_HEADS):
            sl = slice(h * tq, (h + 1) * tq)
            p_ref[:, sl] = jnp.exp2(s_ref[:, sl] - m_new[:, sl]).astype(BF16)
        acc_ref[...] = alpha * acc_ref[...] + jnp.dot(vt_ref[j], p_ref[...], preferred_element_type=F32)
        return m_new

    def pair(t, carry):
        m, cm_a = carry
        a = 2 * t
        cm_b = scores(a + 1, sb_ref)
        m = absorb(a, sa_ref, pa_ref, cm_a, m)
        cm_next = scores(jnp.minimum(a + 2, nkc - 1), sa_ref)
        m = absorb(a + 1, sb_ref, pb_ref, cm_b, m)
        return m, cm_next

    m_fin, cm_last = lax.fori_loop(0, nkc // 2, pair, (jnp.full((1, aw), -1e30, F32), scores(0, sa_ref)))

    @pl.when(nkc % 2 == 1)
    def _():
        absorb(nkc - 1, sa_ref, pa_ref, cm_last, m_fin)

    out_t = acc_ref[0:HEAD_DIM, :] / acc_ref[HEAD_DIM:HEAD_DIM + 1, :]
    for h in range(A_HEADS):
        o_ref[:, h * HEAD_DIM:(h + 1) * HEAD_DIM] = out_t[:, h * tq:(h + 1) * tq].T.astype(o_ref.dtype)


def _dsa(qit, qt, wt, ki, k, vt, batch, seq, tq, tk):
    n = batch * seq
    nqb = seq // tq
    nkb = seq // tk
    topk = min(TOPK_MAX, seq // 4)
    aw = A_HEADS * HEAD_DIM
    kern = functools.partial(_dsa_kernel, tq=tq, tk=tk, topk=topk, idx_bits=seq.bit_length())
    qblk = lambda b, i: (b * nqb + i, 0, 0)
    return pl.pallas_call(
        kern,
        grid=(batch, nqb),
        in_specs=[
            pl.BlockSpec((1, HEAD_DIM, IDX_HEADS * tq), qblk),
            pl.BlockSpec((1, HEAD_DIM, A_HEADS * tq), qblk),
            pl.BlockSpec((1, 1, IDX_HEADS * tq), qblk),
            pl.BlockSpec((seq, LANES), lambda b, i: (b, 0)),
            pl.BlockSpec((seq, LANES), lambda b, i: (b, 0)),
            pl.BlockSpec((nkb, VT_ROWS, tk), lambda b, i: (b, 0, 0)),
        ],
        out_specs=pl.BlockSpec((tq, aw), lambda b, i: (b * nqb + i, 0)),
        out_shape=jax.ShapeDtypeStruct((n, aw), BF16),
        scratch_shapes=[
            pltpu.VMEM((nkb, tk, tq), I32),
            pltpu.VMEM((tk, A_HEADS * tq), F32),
            pltpu.VMEM((tk, A_HEADS * tq), F32),
            pltpu.VMEM((tk, A_HEADS * tq), BF16),
            pltpu.VMEM((tk, A_HEADS * tq), BF16),
            pltpu.VMEM((VT_ROWS, A_HEADS * tq), F32),
        ],
        compiler_params=_cparams(("parallel", "arbitrary"), 48),
        name="dsa",
    )(qit, qt, wt, ki, k, vt)


HG_SUB = 16


def _hgrn_diag_exact(q, kk, v, lc, rsub, nsub):
    chunk, gw = q.shape

    def bcast(x, s):
        return jnp.concatenate([jnp.broadcast_to(x[a * HG_SUB + s:a * HG_SUB + s + 1, :], (HG_SUB, gw))
                                for a in range(nsub)], axis=0)

    acc = [jnp.zeros((chunk, HEAD_DIM), F32) for _ in range(G_HEADS)]
    for s in range(HG_SUB):
        term = q * bcast(kk, s) * jnp.exp(jnp.minimum(lc - bcast(lc, s), 0.0))
        term = jnp.where(rsub >= s, term, 0.0)
        vrow = bcast(v, s)
        for h in range(G_HEADS):
            sl = slice(h * HEAD_DIM, (h + 1) * HEAD_DIM)
            acc[h] = acc[h] + jnp.sum(term[:, sl], axis=1, keepdims=True) * vrow[:, sl]
    return jnp.concatenate(acc, axis=1)


def _hgrn_gates(hq_ref, hf_ref, hi_ref, lb_ref):
    chunk, gw = hq_ref.shape
    lb = lb_ref[...]
    f = lb + (1.0 - lb) * _sigmoid(hf_ref[...])
    kk = 1.0 - f
    hq = hq_ref[...].astype(F32)
    q = hq * _sigmoid(hq)
    v = hi_ref[...].astype(F32)
    rsub = lax.broadcasted_iota(I32, (chunk, gw), 0) & (HG_SUB - 1)
    lc = jnp.log(f)
    sh = 1
    while sh < HG_SUB:
        lc = lc + jnp.where(rsub >= sh, pltpu.roll(lc, sh, 0), 0.0)
        sh *= 2
    return q, kk, v, lc, rsub


def _hgrn_diag_scores(q, kk, lc, h):
    chunk = q.shape[0]
    sl = slice(h * HEAD_DIM, (h + 1) * HEAD_DIM)
    qd = (q[:, sl] * jnp.exp(lc[:, sl])).astype(BF16)
    kd = (kk[:, sl] * jnp.exp(jnp.minimum(-lc[:, sl], EXP_CLAMP))).astype(BF16)
    ti = lax.broadcasted_iota(I32, (chunk, chunk), 0)
    si = lax.broadcasted_iota(I32, (chunk, chunk), 1)
    diag_ok = (ti // HG_SUB == si // HG_SUB) & (si <= ti)
    return jnp.where(diag_ok, _dot_nt(qd, kd), 0.0).astype(BF16)


def _hgrn_kernel(need_ref, hq_ref, hf_ref, hi_ref, hg_ref, lb_ref, gn_ref, o_ref, st_ref, oa_ref, *, chunk):
    nsub = chunk // HG_SUB
    gw = hq_ref.shape[1]

    @pl.when(pl.program_id(1) == 0)
    def _():
        st_ref[...] = jnp.zeros(st_ref.shape, F32)

    q, kk, v, lc, rsub = _hgrn_gates(hq_ref, hf_ref, hi_ref, lb_ref)
    def per_sub(vals):
        return jnp.concatenate([jnp.broadcast_to(x, (HG_SUB, gw)) for x in vals], axis=0)

    tots = [lc[(a + 1) * HG_SUB - 1:(a + 1) * HG_SUB, :] for a in range(nsub)]
    ends = []
    run = jnp.zeros((1, gw), F32)
    for a in range(nsub):
        run = run + tots[a]
        ends.append(run)
    cum_l = ends[-1]
    cum = lc + per_sub([jnp.zeros((1, gw), F32)] + ends[:-1])

    qs = (q * jnp.exp(cum)).astype(BF16)
    koff = kk * jnp.exp(per_sub(tots) - lc)
    ke = (koff * per_sub([jnp.exp(cum_l - e) for e in ends])).astype(BF16)
    koff_b = koff.astype(BF16)
    qoff = []
    kmask = []
    for b in range(nsub - 1):
        r0 = (b + 1) * HG_SUB
        qb = (q[r0:, :] * jnp.exp(cum[r0:, :] - ends[b])).astype(BF16)
        qoff.append(jnp.concatenate([jnp.zeros((r0, gw), BF16), qb], axis=0))
        parts = [koff_b[b * HG_SUB:r0, :], jnp.zeros((chunk - r0, gw), BF16)]
        if b > 0:
            parts.insert(0, jnp.zeros((b * HG_SUB, gw), BF16))
        kmask.append(jnp.concatenate(parts, axis=0))
    vb = v.astype(BF16)
    decay_l = jnp.exp(cum_l)

    for h in range(G_HEADS):
        sl = slice(h * HEAD_DIM, (h + 1) * HEAD_DIM)
        qcat = jnp.concatenate([qo[:, sl] for qo in qoff], axis=1)
        kcat = jnp.concatenate([km[:, sl] for km in kmask], axis=1)
        a = _hgrn_diag_scores(q, kk, lc, h) + _dot_nt(qcat, kcat).astype(BF16)
        st = st_ref[h]
        oa_ref[:, sl] = jnp.dot(a, vb[:, sl], preferred_element_type=F32) + _dot_nt(qs[:, sl], st.astype(BF16))
        upd = jnp.dot(v[:, sl].T.astype(BF16), ke[:, sl], preferred_element_type=F32)
        st_ref[h] = st * decay_l[:, sl] + upd

    @pl.when(need_ref[0] != 0)
    def _():
        q2, kk2, v2, lc2, rsub2 = _hgrn_gates(hq_ref, hf_ref, hi_ref, lb_ref)

        @pl.when(jnp.max(-lc2) > EXP_CLAMP)
        def _():
            v2b = v2.astype(BF16)
            fast = [jnp.dot(_hgrn_diag_scores(q2, kk2, lc2, h), v2b[:, h * HEAD_DIM:(h + 1) * HEAD_DIM],
                            preferred_element_type=F32) for h in range(G_HEADS)]
            oa_ref[...] += _hgrn_diag_exact(q2, kk2, v2, lc2, rsub2, nsub) - jnp.concatenate(fast, axis=1)

    o = oa_ref[...]
    ms = jnp.mean(o * o, axis=-1, keepdims=True)
    hg = hg_ref[...].astype(F32)
    y = o * lax.rsqrt(ms + EPS) * gn_ref[...] * (hg * _sigmoid(hg))
    o_ref[...] = y.astype(o_ref.dtype)


def _hgrn(ph, pt, lb, g_hnorm, batch, seq, chunk):
    n = batch * seq
    gw = G_HEADS * HEAD_DIM
    ncb = seq // chunk
    blk = lambda c: pl.BlockSpec((chunk, gw), lambda b, i, nd: (b * ncb + i, c))
    need = (HG_SUB * jnp.log(jnp.min(lb)) < -EXP_CLAMP).astype(I32).reshape(1)
    grid_spec = pltpu.PrefetchScalarGridSpec(
        num_scalar_prefetch=1,
        grid=(batch, ncb),
        in_specs=[blk(0), blk(1), blk(1), blk(2),
                  pl.BlockSpec((1, gw), lambda b, i, nd: (0, 0)),
                  pl.BlockSpec((1, gw), lambda b, i, nd: (0, 0))],
        out_specs=pl.BlockSpec((chunk, gw), lambda b, i, nd: (b * ncb + i, 0)),
        scratch_shapes=[pltpu.VMEM((G_HEADS, HEAD_DIM, HEAD_DIM), F32),
                        pltpu.VMEM((chunk, gw), F32)],
    )
    return pl.pallas_call(
        functools.partial(_hgrn_kernel, chunk=chunk),
        grid_spec=grid_spec,
        out_shape=jax.ShapeDtypeStruct((n, gw), BF16),
        compiler_params=_cparams(("parallel", "arbitrary"), 32),
        name="hgrn",
    )(need, pt, ph, pt, pt, lb.reshape(1, gw), g_hnorm.reshape(1, gw))


def _merge_kernel(ya_ref, yr_ref, ga_ref, gb_ref, x_ref, wpa_ref, wpb_ref, wo_ref, o_ref, mg_ref):
    c = pl.program_id(1)
    nc = pl.num_programs(1)
    tc = ga_ref.shape[1]

    ma = jnp.dot(ya_ref[...], wpa_ref[...], preferred_element_type=F32)
    mb = jnp.dot(yr_ref[...], wpb_ref[...], preferred_element_type=F32)
    merged = _sigmoid(ga_ref[...].astype(F32)) * ma + _sigmoid(gb_ref[...].astype(F32)) * mb
    merged = merged.astype(BF16)
    for k in range(mg_ref.shape[1] // tc):
        @pl.when(c == k)
        def _():
            mg_ref[:, k * tc:(k + 1) * tc] = merged

    @pl.when(c == nc - 1)
    def _():
        o_ref[...] = x_ref[...] + jnp.dot(mg_ref[...], wo_ref[...], preferred_element_type=F32)


def _merge(y_att, y_rec, pt, x2, w_pa, w_pb, w_o, tm, tc):
    n, d = x2.shape
    aw = y_att.shape[1]
    gw = y_rec.shape[1]
    ga0 = 3072 // tc
    gb0 = 5120 // tc
    return pl.pallas_call(
        _merge_kernel,
        grid=(n // tm, d // tc),
        in_specs=[
            pl.BlockSpec((tm, aw), lambda i, c: (i, 0)),
            pl.BlockSpec((tm, gw), lambda i, c: (i, 0)),
            pl.BlockSpec((tm, tc), lambda i, c: (i, ga0 + c)),
            pl.BlockSpec((tm, tc), lambda i, c: (i, gb0 + c)),
            pl.BlockSpec((tm, d), lambda i, c: (i, 0)),
            pl.BlockSpec((aw, tc), lambda i, c: (0, c)),
            pl.BlockSpec((gw, tc), lambda i, c: (0, c)),
            pl.BlockSpec((d, d), lambda i, c: (0, 0)),
        ],
        out_specs=pl.BlockSpec((tm, d), lambda i, c: (i, 0)),
        out_shape=jax.ShapeDtypeStruct((n, d), F32),
        scratch_shapes=[pltpu.VMEM((tm, d), BF16)],
        compiler_params=_cparams(("parallel", "arbitrary"), 56),
        name="merge",
    )(y_att, y_rec, pt, pt, x2, w_pa, w_pb, w_o)


def _router_kernel(x_ref, g_ref, wr_ref, br_ref, h_ref, r_ref, cnt_ref):
    x = x_ref[...]
    ms = jnp.mean(x * x, axis=-1, keepdims=True)
    h = x * lax.rsqrt(ms + EPS) * g_ref[...]
    h_ref[...] = _pack_halves(h)
    lg = jnp.dot(h.astype(BF16), wr_ref[...], preferred_element_type=F32) + br_ref[...]
    tm = lg.shape[0]
    lane = lax.broadcasted_iota(I32, (tm, LANES), 1)
    neg = jnp.float32(-1e30)
    big = jnp.float32(LANES)
    is_g = lane < N_GROUPS
    lgg = jnp.where(is_g, lg, neg)
    mg = jnp.max(lgg, axis=1, keepdims=True)
    zg = jnp.sum(jnp.where(is_g, jnp.exp(lgg - mg), 0.0), axis=1, keepdims=True)
    p_g = 1.0 / zg
    grp = jnp.min(jnp.where(is_g & (lgg == mg), lane.astype(F32), big), axis=1, keepdims=True)
    e_id = lane - N_GROUPS
    e_grp = lax.shift_right_arithmetic(e_id, jnp.int32(EXP_PER_GROUP.bit_length() - 1)).astype(F32)
    e_idf = e_id.astype(F32)
    is_e = (e_id >= 0) & (e_id < N_EXPERTS) & (e_grp == grp)
    lge = jnp.where(is_e, lg, neg)
    m1 = jnp.max(lge, axis=1, keepdims=True)
    i1 = jnp.min(jnp.where(is_e & (lge == m1), e_idf, big), axis=1, keepdims=True)
    is_e2 = is_e & (e_idf != i1)
    lge2 = jnp.where(is_e2, lg, neg)
    m2 = jnp.max(lge2, axis=1, keepdims=True)
    i2 = jnp.min(jnp.where(is_e2 & (lge2 == m2), e_idf, big), axis=1, keepdims=True)
    ex2 = jnp.exp(m2 - m1)
    den = 1.0 + ex2
    g1 = p_g / den
    g2 = p_g * ex2 / den
    @pl.when(pl.program_id(0) == 0)
    def _():
        cnt_ref[...] = jnp.zeros(cnt_ref.shape, F32)

    lanef = lane.astype(F32)
    oh1 = lanef == i1
    oh2 = lanef == i2
    oh = jnp.where(oh1 | oh2, 1.0, 0.0)
    ti = lax.broadcasted_iota(I32, (tm, tm), 0)
    si = lax.broadcasted_iota(I32, (tm, tm), 1)
    tri = jnp.where(si < ti, 1.0, 0.0).astype(BF16)
    before = jnp.dot(tri, oh.astype(BF16), preferred_element_type=F32) + cnt_ref[...]
    rank1 = jnp.sum(jnp.where(oh1, before, 0.0), axis=1, keepdims=True)
    rank2 = jnp.sum(jnp.where(oh2, before, 0.0), axis=1, keepdims=True)
    cnt_ref[...] += jnp.sum(oh, axis=0, keepdims=True)

    r = jnp.where(lane == 0, i1, 0.0)
    r = jnp.where(lane == 1, i2, r)
    r = jnp.where(lane == 2, g1, r)
    r = jnp.where(lane == 3, g2, r)
    r = jnp.where(lane == 4, rank1, r)
    r = jnp.where(lane == 5, rank2, r)
    r_ref[...] = r


def _router(x1, g, w_r, b_r, tm):
    n, d = x1.shape
    return pl.pallas_call(
        _router_kernel,
        grid=(n // tm,),
        in_specs=[
            pl.BlockSpec((tm, d), lambda i: (i, 0)),
            pl.BlockSpec((1, d), lambda i: (0, 0)),
            pl.BlockSpec((d, LANES), lambda i: (0, 0)),
            pl.BlockSpec((1, LANES), lambda i: (0, 0)),
        ],
        out_specs=[pl.BlockSpec((tm, d // 2), lambda i: (i, 0)),
                   pl.BlockSpec((tm, LANES), lambda i: (i, 0)),
                   pl.BlockSpec((1, LANES), lambda i: (0, 0))],
        out_shape=[jax.ShapeDtypeStruct((n, d // 2), U32),
                   jax.ShapeDtypeStruct((n, LANES), F32),
                   jax.ShapeDtypeStruct((1, LANES), F32)],
        compiler_params=_cparams(("arbitrary",), 40),
        name="router",
    )(x1, g.reshape(1, d), w_r, b_r)


def _dispatch_kernel(prow_ref, h_ref, xs_hbm, sem, *, tm):
    base = 2 * tm * pl.program_id(0)

    def scatter_group(g, carry):
        r0 = pl.multiple_of(g * DMA_GROUP, DMA_GROUP)
        rows = h_ref.at[pl.ds(r0, DMA_GROUP), :]
        for u in range(DMA_GROUP):
            for s in range(2):
                dst_row = prow_ref[base + 2 * r0 + (2 * u + s)]
                pltpu.make_async_copy(rows.at[pl.ds(u, 1), :], xs_hbm.at[pl.ds(dst_row, 1), :],
                                      sem).start(priority=s)
        return carry

    lax.fori_loop(0, tm // DMA_GROUP, scatter_group, 0)
    for _ in range(2):
        pltpu.make_async_copy(h_ref, xs_hbm.at[pl.ds(0, tm), :], sem).wait()


def _dispatch(h2, prow, tm):
    n, d = h2.shape
    grid_spec = pltpu.PrefetchScalarGridSpec(
        num_scalar_prefetch=1,
        grid=(n // tm,),
        in_specs=[pl.BlockSpec((tm, d), lambda i, pr: (i, 0))],
        out_specs=pl.BlockSpec(memory_space=pl.ANY),
        scratch_shapes=[pltpu.SemaphoreType.DMA(())],
    )
    return pl.pallas_call(
        functools.partial(_dispatch_kernel, tm=tm),
        grid_spec=grid_spec,
        out_shape=jax.ShapeDtypeStruct((2 * n, d), h2.dtype),
        compiler_params=_cparams(("arbitrary",), 32),
        name="dispatch",
    )(prow, h2)


def _moe_kernel(pt_ref, pe_ref, plo_ref, phi_ref, slot_ref, nxt_ref, np_ref, x_ref, wg_hbm, wu_hbm, wd_hbm,
                y_ref, wgf_ref, wuf_ref, wdf_ref, wgb_ref, wub_ref, wdb_ref, yacc_ref, sem):
    p = pl.program_id(0)
    active = p < np_ref[0]
    prev = jnp.maximum(p - 1, 0)
    new_expert = (p == 0) | (pe_ref[p] != pe_ref[prev])
    new_tile = (p == 0) | (pt_ref[p] != pt_ref[prev])
    slot = slot_ref[p]

    def weight_copies(e, s):
        return (pltpu.make_async_copy(wg_hbm.at[e], wgf_ref.at[s], sem.at[s]),
                pltpu.make_async_copy(wu_hbm.at[e], wuf_ref.at[s], sem.at[s]),
                pltpu.make_async_copy(wd_hbm.at[e], wdf_ref.at[s], sem.at[s]))

    @pl.when(p == 0)
    def _():
        for cp in weight_copies(pe_ref[0], 0):
            cp.start()

    @pl.when(active & new_expert)
    def _():
        for cp in weight_copies(pe_ref[p], slot):
            cp.wait()
        wgb_ref[...] = wgf_ref[slot].astype(BF16)
        wub_ref[...] = wuf_ref[slot].astype(BF16)
        wdb_ref[...] = wdf_ref[slot].astype(BF16)

        @pl.when(nxt_ref[p] >= 0)
        def _():
            for cp in weight_copies(nxt_ref[p], 1 - slot):
                cp.start()

    @pl.when(active)
    def _():
        tm, half = x_ref.shape
        lo, hi = _unpack_halves(x_ref[...])
        xl = lo.astype(BF16)
        xh = hi.astype(BF16)
        a = (jnp.dot(xl, wgb_ref[0:half, :], preferred_element_type=F32)
             + jnp.dot(xh, wgb_ref[half:, :], preferred_element_type=F32))
        u = (jnp.dot(xl, wub_ref[0:half, :], preferred_element_type=F32)
             + jnp.dot(xh, wub_ref[half:, :], preferred_element_type=F32))
        hmid = (a * _sigmoid(a) * u).astype(BF16)
        yv = jnp.dot(hmid, wdb_ref[...], preferred_element_type=F32)
        row = lax.broadcasted_iota(I32, (tm, 1), 0)
        mine = (row >= plo_ref[p]) & (row < phi_ref[p])
        yv = jnp.where(mine, yv, 0.0)

        @pl.when(new_tile)
        def _():
            yacc_ref[...] = yv

        @pl.when(jnp.logical_not(new_tile))
        def _():
            yacc_ref[...] += yv

        nxt_p = jnp.minimum(p + 1, pl.num_programs(0) - 1)
        last_of_tile = (p == np_ref[0] - 1) | (pt_ref[nxt_p] != pt_ref[p])

        @pl.when(last_of_tile)
        def _():
            y_ref[...] = _pack_halves(yacc_ref[...])


def _moe(xs, pairs, w_gate, w_up, w_down, tm):
    pair_tile, pair_expert, pair_lo, pair_hi, pair_slot, pair_next, n_pairs = pairs
    na, half = xs.shape
    d, de = w_gate.shape[1:]
    max_pairs = pair_tile.shape[0]
    grid_spec = pltpu.PrefetchScalarGridSpec(
        num_scalar_prefetch=7,
        grid=(max_pairs,),
        in_specs=[
            pl.BlockSpec((tm, half), lambda p, pt, *_: (pt[p], 0)),
            pl.BlockSpec(memory_space=pl.ANY),
            pl.BlockSpec(memory_space=pl.ANY),
            pl.BlockSpec(memory_space=pl.ANY),
        ],
        out_specs=pl.BlockSpec((tm, half), lambda p, pt, *_: (pt[p], 0)),
        scratch_shapes=[
            pltpu.VMEM((2, d, de), F32),
            pltpu.VMEM((2, d, de), F32),
            pltpu.VMEM((2, de, d), F32),
            pltpu.VMEM((d, de), BF16),
            pltpu.VMEM((d, de), BF16),
            pltpu.VMEM((de, d), BF16),
            pltpu.VMEM((tm, d), F32),
            pltpu.SemaphoreType.DMA((2,)),
        ],
    )
    return pl.pallas_call(
        _moe_kernel,
        grid_spec=grid_spec,
        out_shape=jax.ShapeDtypeStruct((na, half), U32),
        compiler_params=_cparams(("arbitrary",), 56),
        name="moe",
    )(pair_tile, pair_expert, pair_lo, pair_hi, pair_slot, pair_next, n_pairs, xs, w_gate, w_up, w_down)


def _moe_rows(r, cnt):
    sizes = cnt[0, :N_EXPERTS].astype(I32)
    starts = jnp.cumsum(sizes) - sizes
    e12 = r[:, 0:2].astype(I32)
    rank = r[:, 4:6].astype(I32)
    onehot = e12[:, :, None] == jnp.arange(N_EXPERTS, dtype=I32)[None, None, :]
    start_a = jnp.sum(jnp.where(onehot, starts[None, None, :], 0), axis=-1)
    return (start_a + rank).reshape(-1).astype(I32), sizes


def _count_le(sorted_vals, x):
    return jnp.sum((sorted_vals[None, :] <= x[:, None]).astype(I32), axis=1)


def _moe_pairs(sizes, na, tm):
    ntiles = na // tm
    max_pairs = ntiles + N_EXPERTS - 1
    ends = jnp.cumsum(sizes)
    starts = ends - sizes
    t0 = jnp.arange(ntiles, dtype=I32) * tm
    e_lo = _count_le(ends, t0)
    e_hi = _count_le(ends, t0 + (tm - 1))
    per_tile = e_hi - e_lo + 1
    pend = jnp.cumsum(per_tile)
    pstart = pend - per_tile
    n_pairs = pend[-1]
    pc = jnp.minimum(jnp.arange(max_pairs, dtype=I32), n_pairs - 1)
    tile = _count_le(pend, pc)
    expert = e_lo[tile] + (pc - pstart[tile])
    lo = jnp.clip(starts[expert] - tile * tm, 0, tm).astype(I32)
    hi = jnp.clip(ends[expert] - tile * tm, 0, tm).astype(I32)
    expert = expert.astype(I32)
    first = jnp.concatenate([jnp.ones((1,), bool), expert[1:] != expert[:-1]])
    slot = ((jnp.cumsum(first) - 1) % 2).astype(I32)
    idx = jnp.arange(max_pairs, dtype=I32)
    first_at = jnp.where(first, idx, max_pairs)
    nxt_first = jnp.concatenate([lax.cummin(first_at[::-1])[::-1][1:], jnp.full((1,), max_pairs, I32)])
    nxt = jnp.where(nxt_first < max_pairs, expert[jnp.minimum(nxt_first, max_pairs - 1)], -1).astype(I32)
    return tile, expert, lo, hi, slot, nxt, n_pairs.astype(I32).reshape(1)


def _final_kernel(prow_ref, x_ref, r_ref, g_ref, y_hbm, o_ref, ybuf, sem, *, tm):
    i = pl.program_id(0)
    cur = i % 2

    def issue(step, buf):
        base = 2 * tm * step

        def gather_group(g, carry):
            r0 = pl.multiple_of(g * DMA_GROUP, DMA_GROUP)
            for s in range(2):
                rows = ybuf.at[buf, s, pl.ds(r0, DMA_GROUP), :]
                for u in range(DMA_GROUP):
                    src_row = prow_ref[base + 2 * r0 + (2 * u + s)]
                    pltpu.make_async_copy(y_hbm.at[pl.ds(src_row, 1), :], rows.at[pl.ds(u, 1), :],
                                          sem.at[buf]).start(priority=u % 2)
            return carry

        lax.fori_loop(0, tm // DMA_GROUP, gather_group, 0)

    @pl.when(i == 0)
    def _():
        issue(0, 0)

    @pl.when(i + 1 < pl.num_programs(0))
    def _():
        issue(i + 1, 1 - cur)

    for s in range(2):
        pltpu.make_async_copy(y_hbm.at[pl.ds(0, tm), :], ybuf.at[cur, s], sem.at[cur]).wait()
    r = r_ref[...]
    g1 = r[:, 2:3]
    g2 = r[:, 3:4]
    half = ybuf.shape[-1]
    lo0, hi0 = _unpack_halves(ybuf[cur, 0])
    lo1, hi1 = _unpack_halves(ybuf[cur, 1])
    xa = x_ref[:, :half] + g1 * lo0 + g2 * lo1
    xb = x_ref[:, half:] + g1 * hi0 + g2 * hi1
    ms = (jnp.sum(xa * xa, axis=-1, keepdims=True) + jnp.sum(xb * xb, axis=-1, keepdims=True)) / (2 * half)
    rs = lax.rsqrt(ms + EPS)
    o_ref[:, :half] = xa * rs * g_ref[:, :half]
    o_ref[:, half:] = xb * rs * g_ref[:, half:]


def _final(x1, y, r, prow, g, tm):
    n, d = x1.shape
    grid_spec = pltpu.PrefetchScalarGridSpec(
        num_scalar_prefetch=1,
        grid=(n // tm,),
        in_specs=[
            pl.BlockSpec((tm, d), lambda i, pr: (i, 0)),
            pl.BlockSpec((tm, LANES), lambda i, pr: (i, 0)),
            pl.BlockSpec((1, d), lambda i, pr: (0, 0)),
            pl.BlockSpec(memory_space=pl.ANY),
        ],
        out_specs=pl.BlockSpec((tm, d), lambda i, pr: (i, 0)),
        scratch_shapes=[pltpu.VMEM((2, 2, tm, d // 2), U32), pltpu.SemaphoreType.DMA((2,))],
    )
    return pl.pallas_call(
        functools.partial(_final_kernel, tm=tm),
        grid_spec=grid_spec,
        out_shape=jax.ShapeDtypeStruct((n, d), F32),
        compiler_params=_cparams(("arbitrary",), 48),
        name="final",
    )(prow, x1, r, g.reshape(1, d), y)


def _pack_w_in(w):
    d, din = w.shape
    tc = 256
    wt = jnp.swapaxes(w, 0, 1)

    def pack_kernel(w_ref, o_ref):
        o_ref[0:912, :] = w_ref[0:912, :].astype(BF16)
        o_ref[912:1024, :] = jnp.zeros((LANES - IDX_HEADS, tc), BF16)
        o_ref[1024:2048, :] = w_ref[1936:2960, :].astype(BF16)
        o_ref[2048:3072, :] = w_ref[912:1936, :].astype(BF16)
        o_ref[3072:, :] = w_ref[2960:, :].astype(BF16)

    return pl.pallas_call(
        pack_kernel,
        grid=(d // tc,),
        in_specs=[pl.BlockSpec((din, tc), lambda i: (0, i))],
        out_specs=pl.BlockSpec((PH_COLS + PT_COLS, tc), lambda i: (0, i)),
        out_shape=jax.ShapeDtypeStruct((PH_COLS + PT_COLS, d), BF16),
        compiler_params=_cparams(("parallel",), 48),
        name="pack_w_in",
    )(wt)


def _rope_tables(seq):
    half = HEAD_DIM // 2
    inv = 1.0 / (ROPE_THETA ** (jnp.arange(0, HEAD_DIM, 2, dtype=F32) / HEAD_DIM))
    ang = jnp.arange(seq, dtype=F32)[:, None] * inv[None, :]
    c, s = jnp.cos(ang), jnp.sin(ang)
    del half
    return jnp.concatenate([c, c], axis=1), jnp.concatenate([-s, s], axis=1)


def _layer(x2, batch, seq, cos2, sin2, lb, g_norm1, w_in, g_cq, w_uq, w_qidx, g_hnorm, w_pa, w_pb, w_o,
           g_norm2, w_grp, b_grp, w_exp, b_exp, w_gate, w_up, w_down, g_out):
    n, d = x2.shape
    ph, pt = _in_proj(x2, g_norm1, _pack_w_in(w_in), tm=min(1024, n), tn=1024)
    tq, tk = min(256, seq), min(256, seq)
    qt, qit, wt, k, ki, vt = _dsa_prep(ph, cos2, sin2, g_cq, w_uq.astype(BF16), w_qidx.astype(BF16), seq, tm=tk, tq=tq)
    y_att = _dsa(qit, qt, wt, ki, k, vt, batch, seq, tq=tq, tk=tk)
    y_rec = _hgrn(ph, pt, lb, g_hnorm, batch, seq, chunk=64)
    x1 = _merge(y_att, y_rec, pt, x2, w_pa.astype(BF16), w_pb.astype(BF16), w_o.astype(BF16),
                tm=min(512, n), tc=1024)
    w_r = jnp.concatenate([w_grp, w_exp, jnp.zeros((d, LANES - N_GROUPS - N_EXPERTS), F32)], axis=1).astype(BF16)
    b_r = jnp.concatenate([b_grp, b_exp, jnp.zeros((LANES - N_GROUPS - N_EXPERTS,), F32)]).reshape(1, LANES)
    h2, r, cnt = _router(x1, g_norm2, w_r, b_r, tm=min(512, n))
    prow, sizes = _moe_rows(r, cnt)
    xs = _dispatch(h2, prow, tm=min(512, n))
    tm_moe = 256
    assert (2 * n) % tm_moe == 0
    y = _moe(xs, _moe_pairs(sizes, 2 * n, tm_moe), w_gate, w_up, w_down, tm_moe)
    return _final(x1, y, r, prow, g_out, tm=min(512, n))


def kernel(x, g_norm1, w_in, g_cq, w_uq, w_qidx, lb_logits, g_hnorm, w_pa, w_pb, w_o, g_norm2, w_grp, b_grp,
           w_exp, b_exp, w_gate, w_up, w_down, g_final):
    batch, seq, d = x.shape
    depth = g_norm1.shape[0]
    cos2, sin2 = _rope_tables(seq)
    lb_all = jnp.cumsum(jax.nn.softmax(lb_logits.astype(F32), axis=0), axis=0)
    x2 = x.reshape(batch * seq, d)
    for l in range(depth):
        assert depth == 1
        x2 = _layer(x2, batch, seq, cos2, sin2, lb_all[l], g_norm1[l], w_in[l], g_cq[l], w_uq[l], w_qidx[l],
                    g_hnorm[l], w_pa[l], w_pb[l], w_o[l], g_norm2[l], w_grp[l], b_grp[l], w_exp[l], b_exp[l],
                    w_gate[l], w_up[l], w_down[l], g_final)
    return x2.reshape(batch, seq, d)
```

```python
import functools
import math

import jax
import jax.numpy as jnp
from jax import lax
from jax.experimental import pallas as pl
from jax.experimental.pallas import tpu as pltpu

F32 = jnp.float32
BF16 = jnp.bfloat16
I32 = jnp.int32
U32 = jnp.uint32

EPS = 1e-6
ROPE_THETA = 10000.0
LANES = 128
MXU_LANES = 256
DMA_GROUP = 32
VT_ROWS = 128 + 16

A_HEADS = 8
HEAD_DIM = 128
IDX_HEADS = 16
Q_RANK = 512
TOPK_MAX = 256
G_HEADS = 8
N_GROUPS = 4
EXP_PER_GROUP = 8
N_EXPERTS = N_GROUPS * EXP_PER_GROUP
D_EXPERT = 512

INT_MIN = -2147483648
EXP_CLAMP = 80.0


def _cparams(sem, vmem_mb):
    return pltpu.CompilerParams(dimension_semantics=sem, vmem_limit_bytes=vmem_mb << 20)


def _pack_halves(x):
    c = x.shape[1] // 2
    bits = lax.bitcast_convert_type(x.astype(BF16).astype(F32), U32)
    return lax.shift_right_logical(bits[:, :c], jnp.uint32(16)) | (bits[:, c:] & jnp.uint32(0xFFFF0000))


def _unpack_halves(w):
    lo = lax.bitcast_convert_type(lax.shift_left(w, jnp.uint32(16)), F32)
    hi = lax.bitcast_convert_type(w & jnp.uint32(0xFFFF0000), F32)
    return lo, hi


def _sigmoid(x):
    return 1.0 / (1.0 + jnp.exp(-x))


PH_COLS = 2048
PT_COLS = 7168


def _inproj_kernel(x_ref, g_ref, w_ref, oh_ref, ot_ref, xn_ref, *, nh):
    j = pl.program_id(1)

    @pl.when(j == 0)
    def _():
        x = x_ref[...]
        ms = jnp.mean(x * x, axis=-1, keepdims=True)
        xn_ref[...] = (x * lax.rsqrt(ms + EPS) * g_ref[...]).astype(BF16)

    y = _dot_nt(xn_ref[...], w_ref[...])

    @pl.when(j < nh)
    def _():
        oh_ref[...] = y

    @pl.when(j >= nh)
    def _():
        ot_ref[...] = y.astype(ot_ref.dtype)


def _in_proj(x2, g, w, tm, tn):
    n, d = x2.shape
    nh = PH_COLS // tn
    nt = PT_COLS // tn
    return pl.pallas_call(
        functools.partial(_inproj_kernel, nh=nh),
        grid=(n // tm, nh + nt),
        in_specs=[
            pl.BlockSpec((tm, d), lambda i, j: (i, 0)),
            pl.BlockSpec((1, d), lambda i, j: (0, 0)),
            pl.BlockSpec((tn, d), lambda i, j: (j, 0)),
        ],
        out_specs=[pl.BlockSpec((tm, tn), lambda i, j: (i, jnp.minimum(j, nh - 1))),
                   pl.BlockSpec((tm, tn), lambda i, j: (i, jnp.maximum(j - nh, 0)))],
        out_shape=[jax.ShapeDtypeStruct((n, PH_COLS), F32),
                   jax.ShapeDtypeStruct((n, PT_COLS), BF16)],
        scratch_shapes=[pltpu.VMEM((tm, d), BF16)],
        compiler_params=_cparams(("parallel", "arbitrary"), 56),
        name="in_proj",
    )(x2, g.reshape(1, d), w)


def _rope(x, c2, s2):
    return x * c2 + pltpu.roll(x, HEAD_DIM // 2, 1) * s2


def _rope_t(x, c2t, s2t):
    half = HEAD_DIM // 2
    return x * c2t + jnp.concatenate([x[half:, :], x[:half, :]], axis=0) * s2t


def _prep_kernel(cq_ref, ka_ref, va_ref, kx_ref, wi_ref, cos_ref, sin_ref, cost_ref, sint_ref, g_ref, wuq_ref, wqi_ref,
                 qt_ref, qit_ref, wt_ref, k_ref, ki_ref, vt_ref, *, scale, wscale, tq):
    nqb = cq_ref.shape[0] // tq
    c2 = cos_ref[...]
    s2 = sin_ref[...]
    c2t = cost_ref[...]
    s2t = sint_ref[...]
    cq = cq_ref[...]
    ms = jnp.mean(cq * cq, axis=-1, keepdims=True)
    cqn = (cq * lax.rsqrt(ms + EPS) * g_ref[...]).astype(BF16)
    q_t = _dot_nt(wuq_ref[...], cqn)
    for h in range(A_HEADS):
        qh_t = _rope_t(q_t[h * HEAD_DIM:(h + 1) * HEAD_DIM, :], c2t, s2t) * scale
        for b in range(nqb):
            qt_ref[b, :, h * tq:(h + 1) * tq] = qh_t[:, b * tq:(b + 1) * tq].astype(BF16)
    qi_t = _dot_nt(wqi_ref[...], cqn)
    for h in range(IDX_HEADS):
        qh_t = _rope_t(qi_t[h * HEAD_DIM:(h + 1) * HEAD_DIM, :], c2t, s2t)
        for b in range(nqb):
            qit_ref[b, :, h * tq:(h + 1) * tq] = qh_t[:, b * tq:(b + 1) * tq].astype(BF16)
    w_t = (wi_ref[...] * wscale).T
    for h in range(IDX_HEADS):
        for b in range(nqb):
            wt_ref[b, :, h * tq:(h + 1) * tq] = w_t[h:h + 1, b * tq:(b + 1) * tq]
    k_ref[...] = _rope(ka_ref[...], c2, s2).astype(BF16)
    ki_ref[...] = _rope(kx_ref[...], c2, s2).astype(BF16)
    vt_ref[0, 0:HEAD_DIM, :] = va_ref[...].T.astype(BF16)
    ones_row = lax.broadcasted_iota(I32, (VT_ROWS - HEAD_DIM, va_ref.shape[0]), 0) == 0
    vt_ref[0, HEAD_DIM:VT_ROWS, :] = jnp.where(ones_row, 1.0, 0.0).astype(BF16)


def _dsa_prep(p, rope, g_cq, w_uq_t, w_qidx_t, seq, tm, tq):
    n = p.shape[0]
    nsb = seq // tm
    nqb = tm // tq
    aw = A_HEADS * HEAD_DIM
    iw = IDX_HEADS * HEAD_DIM
    row = lambda i: (i, 0)
    blk3 = lambda i: (i, 0, 0)
    kern = functools.partial(_prep_kernel, scale=HEAD_DIM ** -0.5 * math.log2(math.e),
                             wscale=IDX_HEADS ** -0.5 * HEAD_DIM ** -0.5, tq=tq)
    return pl.pallas_call(
        kern,
        grid=(n // tm,),
        in_specs=[
            pl.BlockSpec((tm, Q_RANK), lambda i: (i, 0)),
            pl.BlockSpec((tm, LANES), lambda i: (i, 4)),
            pl.BlockSpec((tm, LANES), lambda i: (i, 5)),
            pl.BlockSpec((tm, LANES), lambda i: (i, 6)),
            pl.BlockSpec((tm, LANES), lambda i: (i, 7)),
            pl.BlockSpec((tm, LANES), lambda i: (i % nsb, 0)),
            pl.BlockSpec((tm, LANES), lambda i: (i % nsb, 0)),
            pl.BlockSpec((HEAD_DIM, tm), lambda i: (0, i % nsb)),
            pl.BlockSpec((HEAD_DIM, tm), lambda i: (0, i % nsb)),
            pl.BlockSpec((1, Q_RANK), lambda i: (0, 0)),
            pl.BlockSpec((aw, Q_RANK), lambda i: (0, 0)),
            pl.BlockSpec((iw, Q_RANK), lambda i: (0, 0)),
        ],
        out_specs=[
            pl.BlockSpec((nqb, HEAD_DIM, A_HEADS * tq), blk3),
            pl.BlockSpec((nqb, HEAD_DIM, IDX_HEADS * tq), blk3),
            pl.BlockSpec((nqb, 1, IDX_HEADS * tq), blk3),
            pl.BlockSpec((tm, LANES), row),
            pl.BlockSpec((tm, LANES), row),
            pl.BlockSpec((1, VT_ROWS, tm), blk3),
        ],
        out_shape=[
            jax.ShapeDtypeStruct((n // tq, HEAD_DIM, A_HEADS * tq), BF16),
            jax.ShapeDtypeStruct((n // tq, HEAD_DIM, IDX_HEADS * tq), BF16),
            jax.ShapeDtypeStruct((n // tq, 1, IDX_HEADS * tq), F32),
            jax.ShapeDtypeStruct((n, LANES), BF16),
            jax.ShapeDtypeStruct((n, LANES), BF16),
            jax.ShapeDtypeStruct((n // tm, VT_ROWS, tm), BF16),
        ],
        compiler_params=_cparams(("parallel",), 40),
        name="dsa_prep",
    )(p, p, p, p, p, *rope, g_cq.reshape(1, Q_RANK), w_uq_t, w_qidx_t)


def _dot_nt(a, b):
    return lax.dot_general(a, b, (((1,), (1,)), ((), ())), preferred_element_type=F32)


def _dsa_kernel(qit_ref, qt_ref, wt_ref, ki_ref, k_ref, vt_ref, o_ref,
                key_ref, sa_ref, sb_ref, pa_ref, pb_ref, acc_ref, *, tq, tk, topk, idx_bits):
    i = pl.program_id(1)
    nkc = ((i + 1) * tq + tk - 1) // tk
    q_pos = i * tq + lax.broadcasted_iota(I32, (tk, tq), 1)
    k_off = lax.broadcasted_iota(I32, (tk, tq), 0)

    def score_chunk(j, carry):
        kic = ki_ref[pl.ds(pl.multiple_of(j * tk, tk), tk), :]
        acc = jnp.zeros((tk, tq), F32)
        hpm = max(1, MXU_LANES // tq)
        for h0 in range(0, IDX_HEADS, hpm):
            sl = slice(h0 * tq, (h0 + hpm) * tq)
            lg = jnp.dot(kic, qit_ref[0, :, sl], preferred_element_type=F32)
            x = jnp.maximum(lg, 0.0) * wt_ref[0, :, sl]
            for u in range(hpm):
                acc = acc + x[:, u * tq:(u + 1) * tq]
        acc = jnp.where(acc == 0.0, 0.0, acc)
        bits = lax.bitcast_convert_type(acc, I32)
        key = jnp.where(bits < 0, bits ^ jnp.int32(0x7FFFFFFF), bits)
        causal = (j * tk + k_off) <= q_pos
        key_ref[j] = jnp.where(causal, key, jnp.int32(INT_MIN))
        return carry

    lax.fori_loop(0, nkc, score_chunk, 0)

    def count_rows(pred):
        def cnt_chunk(j, part):
            hit = pred(j, key_ref[j]).astype(I32)
            return part + jnp.sum(hit.reshape(tk // 8, 8, tq), axis=0)

        part = lax.fori_loop(0, nkc, cnt_chunk, jnp.zeros((8, tq), I32))
        return jnp.sum(part, axis=0, keepdims=True)

    def bit_step(b, c):
        t_u, cnt_t = c
        bit = jnp.left_shift(jnp.int32(1), 31 - b)
        cand_s = (t_u | bit) ^ jnp.int32(INT_MIN)
        cnt = count_rows(lambda j, key: key >= cand_s)
        take = cnt >= topk
        return jnp.where(take, t_u | bit, t_u), jnp.where(take, cnt, cnt_t)

    n_causal = i * tq + lax.broadcasted_iota(I32, (1, tq), 1) + 1
    t_u, cnt_t = lax.fori_loop(0, 32, bit_step, (jnp.zeros((1, tq), I32), n_causal))
    thr = jnp.maximum(t_u ^ jnp.int32(INT_MIN), jnp.int32(INT_MIN + 1))

    @pl.when(jnp.max(cnt_t) > topk)
    def _():
        need = topk - count_rows(lambda j, key: key > thr)

        def idx_step(b, cut):
            cand = cut | jnp.left_shift(jnp.int32(1), idx_bits - 1 - b)
            cnt = count_rows(lambda j, key: (key == thr) & ((j * tk + k_off) < cand))
            return jnp.where(cnt <= need, cand, cut)

        cut = lax.fori_loop(0, idx_bits, idx_step, jnp.zeros((1, tq), I32))

        def demote(j, carry):
            key = key_ref[j]
            drop = (key == thr) & ((j * tk + k_off) >= cut)
            key_ref[j] = jnp.where(drop, key - 1, key)
            return carry

        lax.fori_loop(0, nkc, demote, 0)

    aw = A_HEADS * tq
    acc_ref[...] = jnp.zeros(acc_ref.shape, F32)

    def scores(j, s_ref):
        kc = k_ref[pl.ds(pl.multiple_of(j * tk, tk), tk), :]
        sel = key_ref[j] >= thr
        cmax = []
        for h in range(A_HEADS):
            sl = slice(h * tq, (h + 1) * tq)
            s_h = jnp.dot(kc, qt_ref[0, :, sl], preferred_element_type=F32)
            s_h = jnp.where(sel, s_h, -jnp.inf)
            s_ref[:, sl] = s_h
            cmax.append(jnp.max(s_h, axis=0, keepdims=True))
        return jnp.concatenate(cmax, axis=1)

    def absorb(j, s_ref, p_ref, cmax, m_old):
        m_new = jnp.maximum(m_old, cmax)
        alpha = jnp.exp2(m_old - m_new)
        for h in range(A_HEADS):
            sl = slice(h * tq, (h + 1) * tq)
            p_ref[:, sl] = jnp.exp2(s_ref[:, sl] - m_new[:, sl]).astype(BF16)
        acc_ref[...] = alpha * acc_ref[...] + jnp.dot(vt_ref[j], p_ref[...], preferred_element_type=F32)
        return m_new

    def pair(t, carry):
        m, cm_a = carry
        a = 2 * t
        cm_b = scores(a + 1, sb_ref)
        m = absorb(a, sa_ref, pa_ref, cm_a, m)
        cm_next = scores(jnp.minimum(a + 2, nkc - 1), sa_ref)
        m = absorb(a + 1, sb_ref, pb_ref, cm_b, m)
        return m, cm_next

    m_fin, cm_last = lax.fori_loop(0, nkc // 2, pair, (jnp.full((1, aw), -1e30, F32), scores(0, sa_ref)))

    @pl.when(nkc % 2 == 1)
    def _():
        absorb(nkc - 1, sa_ref, pa_ref, cm_last, m_fin)

    out_t = acc_ref[0:HEAD_DIM, :] / acc_ref[HEAD_DIM:HEAD_DIM + 1, :]
    for h in range(A_HEADS):
        o_ref[:, h * HEAD_DIM:(h + 1) * HEAD_DIM] = out_t[:, h * tq:(h + 1) * tq].T.astype(o_ref.dtype)


def _dsa(qit, qt, wt, ki, k, vt, batch, seq, tq, tk):
    n = batch * seq
    nqb = seq // tq
    nkb = seq // tk
    topk = min(TOPK_MAX, seq // 4)
    aw = A_HEADS * HEAD_DIM
    kern = functools.partial(_dsa_kernel, tq=tq, tk=tk, topk=topk, idx_bits=seq.bit_length())
    qblk = lambda b, i: (b * nqb + i, 0, 0)
    return pl.pallas_call(
        kern,
        grid=(batch, nqb),
        in_specs=[
            pl.BlockSpec((1, HEAD_DIM, IDX_HEADS * tq), qblk),
            pl.BlockSpec((1, HEAD_DIM, A_HEADS * tq), qblk),
            pl.BlockSpec((1, 1, IDX_HEADS * tq), qblk),
            pl.BlockSpec((seq, LANES), lambda b, i: (b, 0)),
            pl.BlockSpec((seq, LANES), lambda b, i: (b, 0)),
            pl.BlockSpec((nkb, VT_ROWS, tk), lambda b, i: (b, 0, 0)),
        ],
        out_specs=pl.BlockSpec((tq, aw), lambda b, i: (b * nqb + i, 0)),
        out_shape=jax.ShapeDtypeStruct((n, aw), BF16),
        scratch_shapes=[
            pltpu.VMEM((nkb, tk, tq), I32),
            pltpu.VMEM((tk, A_HEADS * tq), F32),
            pltpu.VMEM((tk, A_HEADS * tq), F32),
            pltpu.VMEM((tk, A_HEADS * tq), BF16),
            pltpu.VMEM((tk, A_HEADS * tq), BF16),
            pltpu.VMEM((VT_ROWS, A_HEADS * tq), F32),
        ],
        compiler_params=_cparams(("parallel", "arbitrary"), 48),
        name="dsa",
    )(qit, qt, wt, ki, k, vt)


HG_SUB = 16
HG_CHUNKS_PER_STEP = 2


def _hgrn_diag_exact(q, kk, v, lc, rsub, nsub):
    chunk, gw = q.shape

    def bcast(x, s):
        return jnp.concatenate([jnp.broadcast_to(x[a * HG_SUB + s:a * HG_SUB + s + 1, :], (HG_SUB, gw))
                                for a in range(nsub)], axis=0)

    acc = [jnp.zeros((chunk, HEAD_DIM), F32) for _ in range(G_HEADS)]
    for s in range(HG_SUB):
        term = q * bcast(kk, s) * jnp.exp(jnp.minimum(lc - bcast(lc, s), 0.0))
        term = jnp.where(rsub >= s, term, 0.0)
        vrow = bcast(v, s)
        for h in range(G_HEADS):
            sl = slice(h * HEAD_DIM, (h + 1) * HEAD_DIM)
            acc[h] = acc[h] + jnp.sum(term[:, sl], axis=1, keepdims=True) * vrow[:, sl]
    return jnp.concatenate(acc, axis=1)


def _hgrn_gates(hq_ref, hf_ref, hi_ref, lb_ref):
    chunk, gw = hq_ref.shape
    lb = lb_ref[...]
    f = lb + (1.0 - lb) * _sigmoid(hf_ref[...])
    kk = 1.0 - f
    hq = hq_ref[...].astype(F32)
    q = hq * _sigmoid(hq)
    v = hi_ref[...].astype(F32)
    rsub = lax.broadcasted_iota(I32, (chunk, gw), 0) & (HG_SUB - 1)
    lc = jnp.log(f)
    sh = 1
    while sh < HG_SUB:
        lc = lc + jnp.where(rsub >= sh, pltpu.roll(lc, sh, 0), 0.0)
        sh *= 2
    return q, kk, v, lc, rsub


def _hgrn_diag_scores(q, kk, lc, h):
    chunk = q.shape[0]
    sl = slice(h * HEAD_DIM, (h + 1) * HEAD_DIM)
    qd = (q[:, sl] * jnp.exp(lc[:, sl])).astype(BF16)
    kd = (kk[:, sl] * jnp.exp(jnp.minimum(-lc[:, sl], EXP_CLAMP))).astype(BF16)
    ti = lax.broadcasted_iota(I32, (chunk, chunk), 0)
    si = lax.broadcasted_iota(I32, (chunk, chunk), 1)
    diag_ok = (ti // HG_SUB == si // HG_SUB) & (si <= ti)
    return jnp.where(diag_ok, _dot_nt(qd, kd), 0.0).astype(BF16)


def _hgrn_kernel(need_ref, hq_ref, hf_ref, hi_ref, hg_ref, lb_ref, gn_ref, o_ref, st_ref, oa_ref, *, chunk):
    @pl.when(pl.program_id(1) == 0)
    def _():
        st_ref[...] = jnp.zeros(st_ref.shape, F32)

    for c in range(hq_ref.shape[0] // chunk):
        rows = pl.ds(c * chunk, chunk)
        _hgrn_chunk(need_ref, hq_ref.at[rows], hf_ref.at[rows], hi_ref.at[rows], hg_ref.at[rows], lb_ref, gn_ref,
                    o_ref.at[rows], st_ref, oa_ref)


def _hgrn_chunk(need_ref, hq_ref, hf_ref, hi_ref, hg_ref, lb_ref, gn_ref, o_ref, st_ref, oa_ref):
    chunk, gw = hq_ref.shape
    nsub = chunk // HG_SUB

    q, kk, v, lc, rsub = _hgrn_gates(hq_ref, hf_ref, hi_ref, lb_ref)
    def per_sub(vals):
        return jnp.concatenate([jnp.broadcast_to(x, (HG_SUB, gw)) for x in vals], axis=0)

    tots = [lc[(a + 1) * HG_SUB - 1:(a + 1) * HG_SUB, :] for a in range(nsub)]
    ends = []
    run = jnp.zeros((1, gw), F32)
    for a in range(nsub):
        run = run + tots[a]
        ends.append(run)
    cum_l = ends[-1]
    cum = lc + per_sub([jnp.zeros((1, gw), F32)] + ends[:-1])

    qs = (q * jnp.exp(cum)).astype(BF16)
    koff = kk * jnp.exp(per_sub(tots) - lc)
    ke = (koff * per_sub([jnp.exp(cum_l - e) for e in ends])).astype(BF16)
    koff_b = koff.astype(BF16)
    qoff = []
    kmask = []
    for b in range(nsub - 1):
        r0 = (b + 1) * HG_SUB
        qb = (q[r0:, :] * jnp.exp(cum[r0:, :] - ends[b])).astype(BF16)
        qoff.append(jnp.concatenate([jnp.zeros((r0, gw), BF16), qb], axis=0))
        parts = [koff_b[b * HG_SUB:r0, :], jnp.zeros((chunk - r0, gw), BF16)]
        if b > 0:
            parts.insert(0, jnp.zeros((b * HG_SUB, gw), BF16))
        kmask.append(jnp.concatenate(parts, axis=0))
    vb = v.astype(BF16)
    decay_l = jnp.exp(cum_l)

    for h in range(G_HEADS):
        sl = slice(h * HEAD_DIM, (h + 1) * HEAD_DIM)
        qcat = jnp.concatenate([qo[:, sl] for qo in qoff], axis=1)
        kcat = jnp.concatenate([km[:, sl] for km in kmask], axis=1)
        a = _hgrn_diag_scores(q, kk, lc, h) + _dot_nt(qcat, kcat).astype(BF16)
        st = st_ref[h]
        oa_ref[:, sl] = jnp.dot(a, vb[:, sl], preferred_element_type=F32) + _dot_nt(qs[:, sl], st.astype(BF16))
        upd = jnp.dot(v[:, sl].T.astype(BF16), ke[:, sl], preferred_element_type=F32)
        st_ref[h] = st * decay_l[:, sl] + upd

    @pl.when(need_ref[0] != 0)
    def _():
        q2, kk2, v2, lc2, rsub2 = _hgrn_gates(hq_ref, hf_ref, hi_ref, lb_ref)

        @pl.when(jnp.max(-lc2) > EXP_CLAMP)
        def _():
            v2b = v2.astype(BF16)
            fast = [jnp.dot(_hgrn_diag_scores(q2, kk2, lc2, h), v2b[:, h * HEAD_DIM:(h + 1) * HEAD_DIM],
                            preferred_element_type=F32) for h in range(G_HEADS)]
            oa_ref[...] += _hgrn_diag_exact(q2, kk2, v2, lc2, rsub2, nsub) - jnp.concatenate(fast, axis=1)

    o = oa_ref[...]
    ms = jnp.mean(o * o, axis=-1, keepdims=True)
    hg = hg_ref[...].astype(F32)
    y = o * lax.rsqrt(ms + EPS) * gn_ref[...] * (hg * _sigmoid(hg))
    o_ref[...] = y.astype(o_ref.dtype)


def _hgrn(ph, pt, lb, g_hnorm, batch, seq, chunk):
    n = batch * seq
    gw = G_HEADS * HEAD_DIM
    rows = HG_CHUNKS_PER_STEP * chunk
    ncb = seq // rows
    blk = lambda c: pl.BlockSpec((rows, gw), lambda b, i, nd: (b * ncb + i, c))
    need = (HG_SUB * jnp.log(jnp.min(lb)) < -EXP_CLAMP).astype(I32).reshape(1)
    grid_spec = pltpu.PrefetchScalarGridSpec(
        num_scalar_prefetch=1,
        grid=(batch, ncb),
        in_specs=[blk(0), blk(1), blk(1), blk(2),
                  pl.BlockSpec((1, gw), lambda b, i, nd: (0, 0)),
                  pl.BlockSpec((1, gw), lambda b, i, nd: (0, 0))],
        out_specs=pl.BlockSpec((rows, gw), lambda b, i, nd: (b * ncb + i, 0)),
        scratch_shapes=[pltpu.VMEM((G_HEADS, HEAD_DIM, HEAD_DIM), F32),
                        pltpu.VMEM((chunk, gw), F32)],
    )
    return pl.pallas_call(
        functools.partial(_hgrn_kernel, chunk=chunk),
        grid_spec=grid_spec,
        out_shape=jax.ShapeDtypeStruct((n, gw), BF16),
        compiler_params=_cparams(("parallel", "arbitrary"), 32),
        name="hgrn",
    )(need, pt, ph, pt, pt, lb.reshape(1, gw), g_hnorm.reshape(1, gw))


def _merge_kernel(ya_ref, yr_ref, ga_ref, gb_ref, x_ref, wpa_ref, wpb_ref, wo_ref, o_ref, mg_ref):
    c = pl.program_id(1)
    nc = pl.num_programs(1)
    tc = ga_ref.shape[1]

    ma = jnp.dot(ya_ref[...], wpa_ref[...], preferred_element_type=F32)
    mb = jnp.dot(yr_ref[...], wpb_ref[...], preferred_element_type=F32)
    merged = _sigmoid(ga_ref[...].astype(F32)) * ma + _sigmoid(gb_ref[...].astype(F32)) * mb
    merged = merged.astype(BF16)
    for k in range(mg_ref.shape[1] // tc):
        @pl.when(c == k)
        def _():
            mg_ref[:, k * tc:(k + 1) * tc] = merged

    @pl.when(c == nc - 1)
    def _():
        o_ref[...] = x_ref[...] + jnp.dot(mg_ref[...], wo_ref[...], preferred_element_type=F32)


def _merge(y_att, y_rec, pt, x2, w_pa, w_pb, w_o, tm, tc):
    n, d = x2.shape
    aw = y_att.shape[1]
    gw = y_rec.shape[1]
    ga0 = 3072 // tc
    gb0 = 5120 // tc
    return pl.pallas_call(
        _merge_kernel,
        grid=(n // tm, d // tc),
        in_specs=[
            pl.BlockSpec((tm, aw), lambda i, c: (i, 0)),
            pl.BlockSpec((tm, gw), lambda i, c: (i, 0)),
            pl.BlockSpec((tm, tc), lambda i, c: (i, ga0 + c)),
            pl.BlockSpec((tm, tc), lambda i, c: (i, gb0 + c)),
            pl.BlockSpec((tm, d), lambda i, c: (i, 0)),
            pl.BlockSpec((aw, tc), lambda i, c: (0, c)),
            pl.BlockSpec((gw, tc), lambda i, c: (0, c)),
            pl.BlockSpec((d, d), lambda i, c: (0, 0)),
        ],
        out_specs=pl.BlockSpec((tm, d), lambda i, c: (i, 0)),
        out_shape=jax.ShapeDtypeStruct((n, d), F32),
        scratch_shapes=[pltpu.VMEM((tm, d), BF16)],
        compiler_params=_cparams(("parallel", "arbitrary"), 56),
        name="merge",
    )(y_att, y_rec, pt, pt, x2, w_pa, w_pb, w_o)


def _router_kernel(x_ref, g_ref, wr_ref, br_ref, h_ref, r_ref, cnt_ref):
    x = x_ref[...]
    ms = jnp.mean(x * x, axis=-1, keepdims=True)
    h = x * lax.rsqrt(ms + EPS) * g_ref[...]
    h_ref[...] = _pack_halves(h)
    lg = jnp.dot(h.astype(BF16), wr_ref[...], preferred_element_type=F32) + br_ref[...]
    tm = lg.shape[0]
    lane = lax.broadcasted_iota(I32, (tm, LANES), 1)
    neg = jnp.float32(-1e30)
    big = jnp.float32(LANES)
    is_g = lane < N_GROUPS
    lgg = jnp.where(is_g, lg, neg)
    mg = jnp.max(lgg, axis=1, keepdims=True)
    zg = jnp.sum(jnp.where(is_g, jnp.exp(lgg - mg), 0.0), axis=1, keepdims=True)
    p_g = 1.0 / zg
    grp = jnp.min(jnp.where(is_g & (lgg == mg), lane.astype(F32), big), axis=1, keepdims=True)
    e_id = lane - N_GROUPS
    e_grp = lax.shift_right_arithmetic(e_id, jnp.int32(EXP_PER_GROUP.bit_length() - 1)).astype(F32)
    e_idf = e_id.astype(F32)
    is_e = (e_id >= 0) & (e_id < N_EXPERTS) & (e_grp == grp)
    lge = jnp.where(is_e, lg, neg)
    m1 = jnp.max(lge, axis=1, keepdims=True)
    i1 = jnp.min(jnp.where(is_e & (lge == m1), e_idf, big), axis=1, keepdims=True)
    is_e2 = is_e & (e_idf != i1)
    lge2 = jnp.where(is_e2, lg, neg)
    m2 = jnp.max(lge2, axis=1, keepdims=True)
    i2 = jnp.min(jnp.where(is_e2 & (lge2 == m2), e_idf, big), axis=1, keepdims=True)
    ex2 = jnp.exp(m2 - m1)
    den = 1.0 + ex2
    g1 = p_g / den
    g2 = p_g * ex2 / den
    @pl.when(pl.program_id(0) == 0)
    def _():
        cnt_ref[...] = jnp.zeros(cnt_ref.shape, F32)

    lanef = lane.astype(F32)
    oh1 = lanef == i1
    oh2 = lanef == i2
    oh = jnp.where(oh1 | oh2, 1.0, 0.0)
    ti = lax.broadcasted_iota(I32, (tm, tm), 0)
    si = lax.broadcasted_iota(I32, (tm, tm), 1)
    tri = jnp.where(si < ti, 1.0, 0.0).astype(BF16)
    before = jnp.dot(tri, oh.astype(BF16), preferred_element_type=F32) + cnt_ref[...]
    rank1 = jnp.sum(jnp.where(oh1, before, 0.0), axis=1, keepdims=True)
    rank2 = jnp.sum(jnp.where(oh2, before, 0.0), axis=1, keepdims=True)
    cnt_ref[...] += jnp.sum(oh, axis=0, keepdims=True)

    r = jnp.where(lane == 0, i1, 0.0)
    r = jnp.where(lane == 1, i2, r)
    r = jnp.where(lane == 2, g1, r)
    r = jnp.where(lane == 3, g2, r)
    r = jnp.where(lane == 4, rank1, r)
    r = jnp.where(lane == 5, rank2, r)
    r_ref[...] = r


def _router(x1, g, w_r, b_r, tm):
    n, d = x1.shape
    return pl.pallas_call(
        _router_kernel,
        grid=(n // tm,),
        in_specs=[
            pl.BlockSpec((tm, d), lambda i: (i, 0)),
            pl.BlockSpec((1, d), lambda i: (0, 0)),
            pl.BlockSpec((d, LANES), lambda i: (0, 0)),
            pl.BlockSpec((1, LANES), lambda i: (0, 0)),
        ],
        out_specs=[pl.BlockSpec((tm, d // 2), lambda i: (i, 0)),
                   pl.BlockSpec((tm, LANES), lambda i: (i, 0)),
                   pl.BlockSpec((1, LANES), lambda i: (0, 0))],
        out_shape=[jax.ShapeDtypeStruct((n, d // 2), U32),
                   jax.ShapeDtypeStruct((n, LANES), F32),
                   jax.ShapeDtypeStruct((1, LANES), F32)],
        compiler_params=_cparams(("arbitrary",), 40),
        name="router",
    )(x1, g.reshape(1, d), w_r, b_r)


def _dispatch_kernel(prow_ref, h_ref, xs_hbm, sem, *, tm):
    base = 2 * tm * pl.program_id(0)

    def scatter_group(g, carry):
        r0 = pl.multiple_of(g * DMA_GROUP, DMA_GROUP)
        rows = h_ref.at[pl.ds(r0, DMA_GROUP), :]
        for u in range(DMA_GROUP):
            for s in range(2):
                dst_row = prow_ref[base + 2 * r0 + (2 * u + s)]
                pltpu.make_async_copy(rows.at[pl.ds(u, 1), :], xs_hbm.at[pl.ds(dst_row, 1), :],
                                      sem).start(priority=s)
        return carry

    lax.fori_loop(0, tm // DMA_GROUP, scatter_group, 0)
    for _ in range(2):
        pltpu.make_async_copy(h_ref, xs_hbm.at[pl.ds(0, tm), :], sem).wait()


def _dispatch(h2, prow, tm):
    n, d = h2.shape
    grid_spec = pltpu.PrefetchScalarGridSpec(
        num_scalar_prefetch=1,
        grid=(n // tm,),
        in_specs=[pl.BlockSpec((tm, d), lambda i, pr: (i, 0))],
        out_specs=pl.BlockSpec(memory_space=pl.ANY),
        scratch_shapes=[pltpu.SemaphoreType.DMA(())],
    )
    return pl.pallas_call(
        functools.partial(_dispatch_kernel, tm=tm),
        grid_spec=grid_spec,
        out_shape=jax.ShapeDtypeStruct((2 * n, d), h2.dtype),
        compiler_params=_cparams(("arbitrary",), 32),
        name="dispatch",
    )(prow, h2)


def _moe_kernel(pt_ref, pe_ref, plo_ref, phi_ref, slot_ref, nxt_ref, np_ref, x_ref, wg_hbm, wu_hbm, wd_hbm,
                y_ref, wgf_ref, wuf_ref, wdf_ref, wgb_ref, wub_ref, wdb_ref, yacc_ref, sem):
    p = pl.program_id(0)
    active = p < np_ref[0]
    prev = jnp.maximum(p - 1, 0)
    new_expert = (p == 0) | (pe_ref[p] != pe_ref[prev])
    new_tile = (p == 0) | (pt_ref[p] != pt_ref[prev])
    slot = slot_ref[p]

    def weight_copies(e, s):
        return (pltpu.make_async_copy(wg_hbm.at[e], wgf_ref.at[s], sem.at[s]),
                pltpu.make_async_copy(wu_hbm.at[e], wuf_ref.at[s], sem.at[s]),
                pltpu.make_async_copy(wd_hbm.at[e], wdf_ref.at[s], sem.at[s]))

    @pl.when(p == 0)
    def _():
        for cp in weight_copies(pe_ref[0], 0):
            cp.start()

    @pl.when(active & new_expert)
    def _():
        for cp in weight_copies(pe_ref[p], slot):
            cp.wait()
        wgb_ref[...] = wgf_ref[slot].astype(BF16)
        wub_ref[...] = wuf_ref[slot].astype(BF16)
        wdb_ref[...] = wdf_ref[slot].astype(BF16)

        @pl.when(nxt_ref[p] >= 0)
        def _():
            for cp in weight_copies(nxt_ref[p], 1 - slot):
                cp.start()

    @pl.when(active)
    def _():
        tm, half = x_ref.shape
        lo, hi = _unpack_halves(x_ref[...])
        xl = lo.astype(BF16)
        xh = hi.astype(BF16)
        a = (jnp.dot(xl, wgb_ref[0:half, :], preferred_element_type=F32)
             + jnp.dot(xh, wgb_ref[half:, :], preferred_element_type=F32))
        u = (jnp.dot(xl, wub_ref[0:half, :], preferred_element_type=F32)
             + jnp.dot(xh, wub_ref[half:, :], preferred_element_type=F32))
        hmid = (a * _sigmoid(a) * u).astype(BF16)
        yv = jnp.dot(hmid, wdb_ref[...], preferred_element_type=F32)
        row = lax.broadcasted_iota(I32, (tm, 1), 0)
        mine = (row >= plo_ref[p]) & (row < phi_ref[p])
        yv = jnp.where(mine, yv, 0.0)

        @pl.when(new_tile)
        def _():
            yacc_ref[...] = yv

        @pl.when(jnp.logical_not(new_tile))
        def _():
            yacc_ref[...] += yv

        nxt_p = jnp.minimum(p + 1, pl.num_programs(0) - 1)
        last_of_tile = (p == np_ref[0] - 1) | (pt_ref[nxt_p] != pt_ref[p])

        @pl.when(last_of_tile)
        def _():
            y_ref[...] = _pack_halves(yacc_ref[...])


def _moe(xs, pairs, w_gate, w_up, w_down, tm):
    pair_tile, pair_expert, pair_lo, pair_hi, pair_slot, pair_next, n_pairs = pairs
    na, half = xs.shape
    d, de = w_gate.shape[1:]
    max_pairs = pair_tile.shape[0]
    grid_spec = pltpu.PrefetchScalarGridSpec(
        num_scalar_prefetch=7,
        grid=(max_pairs,),
        in_specs=[
            pl.BlockSpec((tm, half), lambda p, pt, *_: (pt[p], 0)),
            pl.BlockSpec(memory_space=pl.ANY),
            pl.BlockSpec(memory_space=pl.ANY),
            pl.BlockSpec(memory_space=pl.ANY),
        ],
        out_specs=pl.BlockSpec((tm, half), lambda p, pt, *_: (pt[p], 0)),
        scratch_shapes=[
            pltpu.VMEM((2, d, de), F32),
            pltpu.VMEM((2, d, de), F32),
            pltpu.VMEM((2, de, d), F32),
            pltpu.VMEM((d, de), BF16),
            pltpu.VMEM((d, de), BF16),
            pltpu.VMEM((de, d), BF16),
            pltpu.VMEM((tm, d), F32),
            pltpu.SemaphoreType.DMA((2,)),
        ],
    )
    return pl.pallas_call(
        _moe_kernel,
        grid_spec=grid_spec,
        out_shape=jax.ShapeDtypeStruct((na, half), U32),
        compiler_params=_cparams(("arbitrary",), 56),
        name="moe",
    )(pair_tile, pair_expert, pair_lo, pair_hi, pair_slot, pair_next, n_pairs, xs, w_gate, w_up, w_down)


def _moe_rows(r, cnt):
    sizes = cnt[0, :N_EXPERTS].astype(I32)
    starts = jnp.cumsum(sizes) - sizes
    e12 = r[:, 0:2].astype(I32)
    rank = r[:, 4:6].astype(I32)
    onehot = e12[:, :, None] == jnp.arange(N_EXPERTS, dtype=I32)[None, None, :]
    start_a = jnp.sum(jnp.where(onehot, starts[None, None, :], 0), axis=-1)
    return (start_a + rank).reshape(-1).astype(I32), sizes


def _count_le(sorted_vals, x):
    return jnp.sum((sorted_vals[None, :] <= x[:, None]).astype(I32), axis=1)


def _moe_pairs(sizes, na, tm):
    ntiles = na // tm
    max_pairs = ntiles + N_EXPERTS - 1
    ends = jnp.cumsum(sizes)
    starts = ends - sizes
    t0 = jnp.arange(ntiles, dtype=I32) * tm
    e_lo = _count_le(ends, t0)
    e_hi = _count_le(ends, t0 + (tm - 1))
    per_tile = e_hi - e_lo + 1
    pend = jnp.cumsum(per_tile)
    pstart = pend - per_tile
    n_pairs = pend[-1]
    pc = jnp.minimum(jnp.arange(max_pairs, dtype=I32), n_pairs - 1)
    tile = _count_le(pend, pc)
    expert = e_lo[tile] + (pc - pstart[tile])
    lo = jnp.clip(starts[expert] - tile * tm, 0, tm).astype(I32)
    hi = jnp.clip(ends[expert] - tile * tm, 0, tm).astype(I32)
    expert = expert.astype(I32)
    first = jnp.concatenate([jnp.ones((1,), bool), expert[1:] != expert[:-1]])
    slot = ((jnp.cumsum(first) - 1) % 2).astype(I32)
    idx = jnp.arange(max_pairs, dtype=I32)
    first_at = jnp.where(first, idx, max_pairs)
    nxt_first = jnp.concatenate([lax.cummin(first_at[::-1])[::-1][1:], jnp.full((1,), max_pairs, I32)])
    nxt = jnp.where(nxt_first < max_pairs, expert[jnp.minimum(nxt_first, max_pairs - 1)], -1).astype(I32)
    return tile, expert, lo, hi, slot, nxt, n_pairs.astype(I32).reshape(1)


def _final_kernel(prow_ref, x_ref, r_ref, g_ref, y_hbm, o_ref, ybuf, sem, *, tm):
    i = pl.program_id(0)
    cur = i % 2

    def issue(step, buf):
        base = 2 * tm * step

        def gather_group(g, carry):
            r0 = pl.multiple_of(g * DMA_GROUP, DMA_GROUP)
            for s in range(2):
                rows = ybuf.at[buf, s, pl.ds(r0, DMA_GROUP), :]
                for u in range(DMA_GROUP):
                    src_row = prow_ref[base + 2 * r0 + (2 * u + s)]
                    pltpu.make_async_copy(y_hbm.at[pl.ds(src_row, 1), :], rows.at[pl.ds(u, 1), :],
                                          sem.at[buf]).start(priority=u % 2)
            return carry

        lax.fori_loop(0, tm // DMA_GROUP, gather_group, 0)

    @pl.when(i == 0)
    def _():
        issue(0, 0)

    @pl.when(i + 1 < pl.num_programs(0))
    def _():
        issue(i + 1, 1 - cur)

    for s in range(2):
        pltpu.make_async_copy(y_hbm.at[pl.ds(0, tm), :], ybuf.at[cur, s], sem.at[cur]).wait()
    r = r_ref[...]
    g1 = r[:, 2:3]
    g2 = r[:, 3:4]
    half = ybuf.shape[-1]
    lo0, hi0 = _unpack_halves(ybuf[cur, 0])
    lo1, hi1 = _unpack_halves(ybuf[cur, 1])
    xa = x_ref[:, :half] + g1 * lo0 + g2 * lo1
    xb = x_ref[:, half:] + g1 * hi0 + g2 * hi1
    ms = (jnp.sum(xa * xa, axis=-1, keepdims=True) + jnp.sum(xb * xb, axis=-1, keepdims=True)) / (2 * half)
    rs = lax.rsqrt(ms + EPS)
    o_ref[:, :half] = xa * rs * g_ref[:, :half]
    o_ref[:, half:] = xb * rs * g_ref[:, half:]


def _final(x1, y, r, prow, g, tm):
    n, d = x1.shape
    grid_spec = pltpu.PrefetchScalarGridSpec(
        num_scalar_prefetch=1,
        grid=(n // tm,),
        in_specs=[
            pl.BlockSpec((tm, d), lambda i, pr: (i, 0)),
            pl.BlockSpec((tm, LANES), lambda i, pr: (i, 0)),
            pl.BlockSpec((1, d), lambda i, pr: (0, 0)),
            pl.BlockSpec(memory_space=pl.ANY),
        ],
        out_specs=pl.BlockSpec((tm, d), lambda i, pr: (i, 0)),
        scratch_shapes=[pltpu.VMEM((2, 2, tm, d // 2), U32), pltpu.SemaphoreType.DMA((2,))],
    )
    return pl.pallas_call(
        functools.partial(_final_kernel, tm=tm),
        grid_spec=grid_spec,
        out_shape=jax.ShapeDtypeStruct((n, d), F32),
        compiler_params=_cparams(("arbitrary",), 48),
        name="final",
    )(prow, x1, r, g.reshape(1, d), y)


def _pack_w_in(w):
    d, din = w.shape
    tc = 256
    wt = jnp.swapaxes(w, 0, 1)

    def pack_kernel(w_ref, o_ref):
        o_ref[0:912, :] = w_ref[0:912, :].astype(BF16)
        o_ref[912:1024, :] = jnp.zeros((LANES - IDX_HEADS, tc), BF16)
        o_ref[1024:2048, :] = w_ref[1936:2960, :].astype(BF16)
        o_ref[2048:3072, :] = w_ref[912:1936, :].astype(BF16)
        o_ref[3072:, :] = w_ref[2960:, :].astype(BF16)

    return pl.pallas_call(
        pack_kernel,
        grid=(d // tc,),
        in_specs=[pl.BlockSpec((din, tc), lambda i: (0, i))],
        out_specs=pl.BlockSpec((PH_COLS + PT_COLS, tc), lambda i: (0, i)),
        out_shape=jax.ShapeDtypeStruct((PH_COLS + PT_COLS, d), BF16),
        compiler_params=_cparams(("parallel",), 48),
        name="pack_w_in",
    )(wt)


def _rope_tables(seq):
    half = HEAD_DIM // 2
    inv = 1.0 / (ROPE_THETA ** (jnp.arange(0, HEAD_DIM, 2, dtype=F32) / HEAD_DIM))
    ang = jnp.arange(seq, dtype=F32)[:, None] * inv[None, :]
    c, s = jnp.cos(ang), jnp.sin(ang)
    del half
    c2, s2 = jnp.concatenate([c, c], axis=1), jnp.concatenate([-s, s], axis=1)
    return c2, s2, c2.T, s2.T


def _layer(x2, batch, seq, rope, lb, g_norm1, w_in, g_cq, w_uq, w_qidx, g_hnorm, w_pa, w_pb, w_o,
           g_norm2, w_grp, b_grp, w_exp, b_exp, w_gate, w_up, w_down, g_out):
    n, d = x2.shape
    ph, pt = _in_proj(x2, g_norm1, _pack_w_in(w_in), tm=min(1024, n), tn=1024)
    tq, tk = min(256, seq), min(256, seq)
    qt, qit, wt, k, ki, vt = _dsa_prep(ph, rope, g_cq, w_uq.T.astype(BF16), w_qidx.T.astype(BF16), seq, tm=tk, tq=tq)
    y_att = _dsa(qit, qt, wt, ki, k, vt, batch, seq, tq=tq, tk=tk)
    y_rec = _hgrn(ph, pt, lb, g_hnorm, batch, seq, chunk=64)
    x1 = _merge(y_att, y_rec, pt, x2, w_pa.astype(BF16), w_pb.astype(BF16), w_o.astype(BF16),
                tm=min(512, n), tc=1024)
    w_r = jnp.concatenate([w_grp, w_exp, jnp.zeros((d, LANES - N_GROUPS - N_EXPERTS), F32)], axis=1).astype(BF16)
    b_r = jnp.concatenate([b_grp, b_exp, jnp.zeros((LANES - N_GROUPS - N_EXPERTS,), F32)]).reshape(1, LANES)
    h2, r, cnt = _router(x1, g_norm2, w_r, b_r, tm=min(512, n))
    prow, sizes = _moe_rows(r, cnt)
    xs = _dispatch(h2, prow, tm=min(512, n))
    tm_moe = 256
    assert (2 * n) % tm_moe == 0
    y = _moe(xs, _moe_pairs(sizes, 2 * n, tm_moe), w_gate, w_up, w_down, tm_moe)
    return _final(x1, y, r, prow, g_out, tm=min(512, n))


def kernel(x, g_norm1, w_in, g_cq, w_uq, w_qidx, lb_logits, g_hnorm, w_pa, w_pb, w_o, g_norm2, w_grp, b_grp,
           w_exp, b_exp, w_gate, w_up, w_down, g_final):
    batch, seq, d = x.shape
    depth = g_norm1.shape[0]
    rope = _rope_tables(seq)
    lb_all = jnp.cumsum(jax.nn.softmax(lb_logits.astype(F32), axis=0), axis=0)
    x2 = x.reshape(batch * seq, d)
    for l in range(depth):
        assert depth == 1
        x2 = _layer(x2, batch, seq, rope, lb_all[l], g_norm1[l], w_in[l], g_cq[l], w_uq[l], w_qidx[l],
                    g_hnorm[l], w_pa[l], w_pb[l], w_o[l], g_norm2[l], w_grp[l], b_grp[l], w_exp[l], b_exp[l],
                    w_gate[l], w_up[l], w_down[l], g_final)
    return x2.reshape(batch, seq, d)
```

```python
import functools
import math

import jax
import jax.numpy as jnp
from jax import lax
from jax.experimental import pallas as pl
from jax.experimental.pallas import tpu as pltpu

F32 = jnp.float32
BF16 = jnp.bfloat16
I32 = jnp.int32
U32 = jnp.uint32

EPS = 1e-6
ROPE_THETA = 10000.0
LANES = 128
MXU_LANES = 256
DMA_GROUP = 32
VT_ROWS = 128 + 16

A_HEADS = 8
HEAD_DIM = 128
IDX_HEADS = 16
Q_RANK = 512
TOPK_MAX = 256
G_HEADS = 8
N_GROUPS = 4
EXP_PER_GROUP = 8
N_EXPERTS = N_GROUPS * EXP_PER_GROUP
D_EXPERT = 512

INT_MIN = -2147483648
EXP_CLAMP = 80.0


def _cparams(sem, vmem_mb):
    return pltpu.CompilerParams(dimension_semantics=sem, vmem_limit_bytes=vmem_mb << 20)


def _pack_halves(x):
    c = x.shape[1] // 2
    bits = lax.bitcast_convert_type(x.astype(BF16).astype(F32), U32)
    return lax.shift_right_logical(bits[:, :c], jnp.uint32(16)) | (bits[:, c:] & jnp.uint32(0xFFFF0000))


def _unpack_halves(w):
    lo = lax.bitcast_convert_type(lax.shift_left(w, jnp.uint32(16)), F32)
    hi = lax.bitcast_convert_type(w & jnp.uint32(0xFFFF0000), F32)
    return lo, hi


def _sigmoid(x):
    return 0.5 * jnp.tanh(0.5 * x) + 0.5


PH_COLS = 2048
PT_COLS = 7168


def _inproj_kernel(x_ref, g_ref, w_ref, oh_ref, ot_ref, xn_ref, *, nh):
    j = pl.program_id(1)

    @pl.when(j == 0)
    def _():
        x = x_ref[...]
        ms = jnp.mean(x * x, axis=-1, keepdims=True)
        xn_ref[...] = (x * lax.rsqrt(ms + EPS) * g_ref[...]).astype(BF16)

    y = _dot_nt(xn_ref[...], w_ref[...])

    @pl.when(j < nh)
    def _():
        oh_ref[...] = y

    @pl.when(j >= nh)
    def _():
        ot_ref[...] = y.astype(ot_ref.dtype)


def _in_proj(x2, g, w, tm, tn):
    n, d = x2.shape
    nh = PH_COLS // tn
    nt = PT_COLS // tn
    return pl.pallas_call(
        functools.partial(_inproj_kernel, nh=nh),
        grid=(n // tm, nh + nt),
        in_specs=[
            pl.BlockSpec((tm, d), lambda i, j: (i, 0)),
            pl.BlockSpec((1, d), lambda i, j: (0, 0)),
            pl.BlockSpec((tn, d), lambda i, j: (j, 0)),
        ],
        out_specs=[pl.BlockSpec((tm, tn), lambda i, j: (i, jnp.minimum(j, nh - 1))),
                   pl.BlockSpec((tm, tn), lambda i, j: (i, jnp.maximum(j - nh, 0)))],
        out_shape=[jax.ShapeDtypeStruct((n, PH_COLS), F32),
                   jax.ShapeDtypeStruct((n, PT_COLS), BF16)],
        scratch_shapes=[pltpu.VMEM((tm, d), BF16)],
        compiler_params=_cparams(("parallel", "arbitrary"), 56),
        name="in_proj",
    )(x2, g.reshape(1, d), w)


def _rope(x, c2, s2):
    return x * c2 + pltpu.roll(x, HEAD_DIM // 2, 1) * s2


def _rope_t(x, c2t, s2t):
    half = HEAD_DIM // 2
    return x * c2t + jnp.concatenate([x[half:, :], x[:half, :]], axis=0) * s2t


def _prep_kernel(cq_ref, ka_ref, va_ref, kx_ref, wi_ref, cos_ref, sin_ref, cost_ref, sint_ref, g_ref, wuq_ref, wqi_ref,
                 qt_ref, qit_ref, wt_ref, k_ref, ki_ref, vt_ref, *, scale, wscale, tq):
    nqb = cq_ref.shape[0] // tq
    c2 = cos_ref[...]
    s2 = sin_ref[...]
    c2t = cost_ref[...]
    s2t = sint_ref[...]
    cq = cq_ref[...]
    ms = jnp.mean(cq * cq, axis=-1, keepdims=True)
    cqn = (cq * lax.rsqrt(ms + EPS) * g_ref[...]).astype(BF16)
    q_t = _dot_nt(wuq_ref[...], cqn)
    for h in range(A_HEADS):
        qh_t = _rope_t(q_t[h * HEAD_DIM:(h + 1) * HEAD_DIM, :], c2t, s2t) * scale
        for b in range(nqb):
            qt_ref[b, :, h * tq:(h + 1) * tq] = qh_t[:, b * tq:(b + 1) * tq].astype(BF16)
    qi_t = _dot_nt(wqi_ref[...], cqn)
    for h in range(IDX_HEADS):
        qh_t = _rope_t(qi_t[h * HEAD_DIM:(h + 1) * HEAD_DIM, :], c2t, s2t)
        for b in range(nqb):
            qit_ref[b, :, h * tq:(h + 1) * tq] = qh_t[:, b * tq:(b + 1) * tq].astype(BF16)
    w_t = (wi_ref[...] * wscale).T
    for h in range(IDX_HEADS):
        for b in range(nqb):
            wt_ref[b, :, h * tq:(h + 1) * tq] = w_t[h:h + 1, b * tq:(b + 1) * tq]
    k_ref[...] = _rope(ka_ref[...], c2, s2).astype(BF16)
    ki_ref[...] = _rope(kx_ref[...], c2, s2).astype(BF16)
    vt_ref[0, 0:HEAD_DIM, :] = va_ref[...].T.astype(BF16)
    ones_row = lax.broadcasted_iota(I32, (VT_ROWS - HEAD_DIM, va_ref.shape[0]), 0) == 0
    vt_ref[0, HEAD_DIM:VT_ROWS, :] = jnp.where(ones_row, 1.0, 0.0).astype(BF16)


def _dsa_prep(p, rope, g_cq, w_uq_t, w_qidx_t, seq, tm, tq):
    n = p.shape[0]
    nsb = seq // tm
    nqb = tm // tq
    aw = A_HEADS * HEAD_DIM
    iw = IDX_HEADS * HEAD_DIM
    row = lambda i: (i, 0)
    blk3 = lambda i: (i, 0, 0)
    kern = functools.partial(_prep_kernel, scale=HEAD_DIM ** -0.5 * math.log2(math.e),
                             wscale=IDX_HEADS ** -0.5 * HEAD_DIM ** -0.5, tq=tq)
    return pl.pallas_call(
        kern,
        grid=(n // tm,),
        in_specs=[
            pl.BlockSpec((tm, Q_RANK), lambda i: (i, 0)),
            pl.BlockSpec((tm, LANES), lambda i: (i, 4)),
            pl.BlockSpec((tm, LANES), lambda i: (i, 5)),
            pl.BlockSpec((tm, LANES), lambda i: (i, 6)),
            pl.BlockSpec((tm, LANES), lambda i: (i, 7)),
            pl.BlockSpec((tm, LANES), lambda i: (i % nsb, 0)),
            pl.BlockSpec((tm, LANES), lambda i: (i % nsb, 0)),
            pl.BlockSpec((HEAD_DIM, tm), lambda i: (0, i % nsb)),
            pl.BlockSpec((HEAD_DIM, tm), lambda i: (0, i % nsb)),
            pl.BlockSpec((1, Q_RANK), lambda i: (0, 0)),
            pl.BlockSpec((aw, Q_RANK), lambda i: (0, 0)),
            pl.BlockSpec((iw, Q_RANK), lambda i: (0, 0)),
        ],
        out_specs=[
            pl.BlockSpec((nqb, HEAD_DIM, A_HEADS * tq), blk3),
            pl.BlockSpec((nqb, HEAD_DIM, IDX_HEADS * tq), blk3),
            pl.BlockSpec((nqb, 1, IDX_HEADS * tq), blk3),
            pl.BlockSpec((tm, LANES), row),
            pl.BlockSpec((tm, LANES), row),
            pl.BlockSpec((1, VT_ROWS, tm), blk3),
        ],
        out_shape=[
            jax.ShapeDtypeStruct((n // tq, HEAD_DIM, A_HEADS * tq), BF16),
            jax.ShapeDtypeStruct((n // tq, HEAD_DIM, IDX_HEADS * tq), BF16),
            jax.ShapeDtypeStruct((n // tq, 1, IDX_HEADS * tq), F32),
            jax.ShapeDtypeStruct((n, LANES), BF16),
            jax.ShapeDtypeStruct((n, LANES), BF16),
            jax.ShapeDtypeStruct((n // tm, VT_ROWS, tm), BF16),
        ],
        compiler_params=_cparams(("parallel",), 40),
        name="dsa_prep",
    )(p, p, p, p, p, *rope, g_cq.reshape(1, Q_RANK), w_uq_t, w_qidx_t)


def _dot_nt(a, b):
    return lax.dot_general(a, b, (((1,), (1,)), ((), ())), preferred_element_type=F32)


def _dsa_kernel(qit_ref, qt_ref, wt_ref, ki_ref, k_ref, vt_ref, o_ref,
                key_ref, sa_ref, sb_ref, pa_ref, pb_ref, acc_ref, *, tq, tk, topk, idx_bits):
    i = pl.program_id(1)
    nkc = ((i + 1) * tq + tk - 1) // tk
    q_pos = i * tq + lax.broadcasted_iota(I32, (tk, tq), 1)
    k_off = lax.broadcasted_iota(I32, (tk, tq), 0)

    def score_chunk(j, carry):
        kic = ki_ref[pl.ds(pl.multiple_of(j * tk, tk), tk), :]
        acc = jnp.zeros((tk, tq), F32)
        hpm = max(1, MXU_LANES // tq)
        for h0 in range(0, IDX_HEADS, hpm):
            sl = slice(h0 * tq, (h0 + hpm) * tq)
            lg = jnp.dot(kic, qit_ref[0, :, sl], preferred_element_type=F32)
            x = jnp.maximum(lg, 0.0) * wt_ref[0, :, sl]
            for u in range(hpm):
                acc = acc + x[:, u * tq:(u + 1) * tq]
        acc = jnp.where(acc == 0.0, 0.0, acc)
        bits = lax.bitcast_convert_type(acc, I32)
        key = jnp.where(bits < 0, bits ^ jnp.int32(0x7FFFFFFF), bits)
        causal = (j * tk + k_off) <= q_pos
        key_ref[j] = jnp.where(causal, key, jnp.int32(INT_MIN))
        return carry

    lax.fori_loop(0, nkc, score_chunk, 0)

    def count_rows(pred):
        def cnt_chunk(j, part):
            hit = pred(j, key_ref[j]).astype(I32)
            return part + jnp.sum(hit.reshape(tk // 8, 8, tq), axis=0)

        part = lax.fori_loop(0, nkc, cnt_chunk, jnp.zeros((8, tq), I32))
        return jnp.sum(part, axis=0, keepdims=True)

    def bit_step(b, c):
        t_u, cnt_t = c
        bit = jnp.left_shift(jnp.int32(1), 31 - b)
        cand_s = (t_u | bit) ^ jnp.int32(INT_MIN)
        cnt = count_rows(lambda j, key: key >= cand_s)
        take = cnt >= topk
        return jnp.where(take, t_u | bit, t_u), jnp.where(take, cnt, cnt_t)

    n_causal = i * tq + lax.broadcasted_iota(I32, (1, tq), 1) + 1
    t_u, cnt_t = lax.fori_loop(0, 32, bit_step, (jnp.zeros((1, tq), I32), n_causal))
    thr = jnp.maximum(t_u ^ jnp.int32(INT_MIN), jnp.int32(INT_MIN + 1))

    @pl.when(jnp.max(cnt_t) > topk)
    def _():
        need = topk - count_rows(lambda j, key: key > thr)

        def idx_step(b, cut):
            cand = cut | jnp.left_shift(jnp.int32(1), idx_bits - 1 - b)
            cnt = count_rows(lambda j, key: (key == thr) & ((j * tk + k_off) < cand))
            return jnp.where(cnt <= need, cand, cut)

        cut = lax.fori_loop(0, idx_bits, idx_step, jnp.zeros((1, tq), I32))

        def demote(j, carry):
            key = key_ref[j]
            drop = (key == thr) & ((j * tk + k_off) >= cut)
            key_ref[j] = jnp.where(drop, key - 1, key)
            return carry

        lax.fori_loop(0, nkc, demote, 0)

    aw = A_HEADS * tq
    acc_ref[...] = jnp.zeros(acc_ref.shape, F32)

    def scores(j, s_ref):
        kc = k_ref[pl.ds(pl.multiple_of(j * tk, tk), tk), :]
        sel = key_ref[j] >= thr
        cmax = []
        for h in range(A_HEADS):
            sl = slice(h * tq, (h + 1) * tq)
            s_h = jnp.dot(kc, qt_ref[0, :, sl], preferred_element_type=F32)
            s_h = jnp.where(sel, s_h, -jnp.inf)
            s_ref[:, sl] = s_h
            cmax.append(jnp.max(s_h, axis=0, keepdims=True))
        return jnp.concatenate(cmax, axis=1)

    def absorb(j, s_ref, p_ref, cmax, m_old):
        m_new = jnp.maximum(m_old, cmax)
        alpha = jnp.exp2(m_old - m_new)
        for h in range(A_HEADS):
            sl = slice(h * tq, (h + 1) * tq)
            p_ref[:, sl] = jnp.exp2(s_ref[:, sl] - m_new[:, sl]).astype(BF16)
        acc_ref[...] = alpha * acc_ref[...] + jnp.dot(vt_ref[j], p_ref[...], preferred_element_type=F32)
        return m_new

    def pair(t, carry):
        m, cm_a = carry
        a = 2 * t
        cm_b = scores(a + 1, sb_ref)
        m = absorb(a, sa_ref, pa_ref, cm_a, m)
        cm_next = scores(jnp.minimum(a + 2, nkc - 1), sa_ref)
        m = absorb(a + 1, sb_ref, pb_ref, cm_b, m)
        return m, cm_next

    m_fin, cm_last = lax.fori_loop(0, nkc // 2, pair, (jnp.full((1, aw), -1e30, F32), scores(0, sa_ref)))

    @pl.when(nkc % 2 == 1)
    def _():
        absorb(nkc - 1, sa_ref, pa_ref, cm_last, m_fin)

    out_t = acc_ref[0:HEAD_DIM, :] / acc_ref[HEAD_DIM:HEAD_DIM + 1, :]
    for h in range(A_HEADS):
        o_ref[:, h * HEAD_DIM:(h + 1) * HEAD_DIM] = out_t[:, h * tq:(h + 1) * tq].T.astype(o_ref.dtype)


def _dsa(qit, qt, wt, ki, k, vt, batch, seq, tq, tk):
    n = batch * seq
    nqb = seq // tq
    nkb = seq // tk
    topk = min(TOPK_MAX, seq // 4)
    aw = A_HEADS * HEAD_DIM
    kern = functools.partial(_dsa_kernel, tq=tq, tk=tk, topk=topk, idx_bits=seq.bit_length())
    qblk = lambda b, i: (b * nqb + i, 0, 0)
    return pl.pallas_call(
        kern,
        grid=(batch, nqb),
        in_specs=[
            pl.BlockSpec((1, HEAD_DIM, IDX_HEADS * tq), qblk),
            pl.BlockSpec((1, HEAD_DIM, A_HEADS * tq), qblk),
            pl.BlockSpec((1, 1, IDX_HEADS * tq), qblk),
            pl.BlockSpec((seq, LANES), lambda b, i: (b, 0)),
            pl.BlockSpec((seq, LANES), lambda b, i: (b, 0)),
            pl.BlockSpec((nkb, VT_ROWS, tk), lambda b, i: (b, 0, 0)),
        ],
        out_specs=pl.BlockSpec((tq, aw), lambda b, i: (b * nqb + i, 0)),
        out_shape=jax.ShapeDtypeStruct((n, aw), BF16),
        scratch_shapes=[
            pltpu.VMEM((nkb, tk, tq), I32),
            pltpu.VMEM((tk, A_HEADS * tq), F32),
            pltpu.VMEM((tk, A_HEADS * tq), F32),
            pltpu.VMEM((tk, A_HEADS * tq), BF16),
            pltpu.VMEM((tk, A_HEADS * tq), BF16),
            pltpu.VMEM((VT_ROWS, A_HEADS * tq), F32),
        ],
        compiler_params=_cparams(("parallel", "arbitrary"), 48),
        name="dsa",
    )(qit, qt, wt, ki, k, vt)


HG_SUB = 16
HG_CHUNKS_PER_STEP = 4


def _hgrn_diag_exact(q, kk, v, lc, rsub, nsub):
    chunk, gw = q.shape

    def bcast(x, s):
        return jnp.concatenate([jnp.broadcast_to(x[a * HG_SUB + s:a * HG_SUB + s + 1, :], (HG_SUB, gw))
                                for a in range(nsub)], axis=0)

    acc = [jnp.zeros((chunk, HEAD_DIM), F32) for _ in range(G_HEADS)]
    for s in range(HG_SUB):
        term = q * bcast(kk, s) * jnp.exp(jnp.minimum(lc - bcast(lc, s), 0.0))
        term = jnp.where(rsub >= s, term, 0.0)
        vrow = bcast(v, s)
        for h in range(G_HEADS):
            sl = slice(h * HEAD_DIM, (h + 1) * HEAD_DIM)
            acc[h] = acc[h] + jnp.sum(term[:, sl], axis=1, keepdims=True) * vrow[:, sl]
    return jnp.concatenate(acc, axis=1)


def _hgrn_gates(hq_ref, hf_ref, hi_ref, lb_ref):
    chunk, gw = hq_ref.shape
    lb = lb_ref[...]
    f = lb + (1.0 - lb) * _sigmoid(hf_ref[...])
    kk = 1.0 - f
    hq = hq_ref[...].astype(F32)
    q = hq * _sigmoid(hq)
    v = hi_ref[...].astype(F32)
    rsub = lax.broadcasted_iota(I32, (chunk, gw), 0) & (HG_SUB - 1)
    lc = jnp.log(f)
    sh = 1
    while sh < HG_SUB:
        lc = lc + jnp.where(rsub >= sh, pltpu.roll(lc, sh, 0), 0.0)
        sh *= 2
    return q, kk, v, lc, rsub


def _hgrn_diag_scores(q, kk, lc, h):
    chunk = q.shape[0]
    sl = slice(h * HEAD_DIM, (h + 1) * HEAD_DIM)
    qd = (q[:, sl] * jnp.exp(lc[:, sl])).astype(BF16)
    kd = (kk[:, sl] * jnp.exp(jnp.minimum(-lc[:, sl], EXP_CLAMP))).astype(BF16)
    ti = lax.broadcasted_iota(I32, (chunk, chunk), 0)
    si = lax.broadcasted_iota(I32, (chunk, chunk), 1)
    diag_ok = (ti // HG_SUB == si // HG_SUB) & (si <= ti)
    return jnp.where(diag_ok, _dot_nt(qd, kd), 0.0).astype(BF16)


def _hgrn_kernel(need_ref, hq_ref, hf_ref, hi_ref, hg_ref, lb_ref, gn_ref, o_ref, st_ref, oa_ref, *, chunk):
    @pl.when(pl.program_id(1) == 0)
    def _():
        st_ref[...] = jnp.zeros(st_ref.shape, F32)

    for c in range(hq_ref.shape[0] // chunk):
        rows = pl.ds(c * chunk, chunk)
        _hgrn_chunk(need_ref, hq_ref.at[rows], hf_ref.at[rows], hi_ref.at[rows], hg_ref.at[rows], lb_ref, gn_ref,
                    o_ref.at[rows], st_ref, oa_ref)


def _hgrn_chunk(need_ref, hq_ref, hf_ref, hi_ref, hg_ref, lb_ref, gn_ref, o_ref, st_ref, oa_ref):
    chunk, gw = hq_ref.shape
    nsub = chunk // HG_SUB

    q, kk, v, lc, rsub = _hgrn_gates(hq_ref, hf_ref, hi_ref, lb_ref)
    def per_sub(vals):
        return jnp.concatenate([jnp.broadcast_to(x, (HG_SUB, gw)) for x in vals], axis=0)

    tots = [lc[(a + 1) * HG_SUB - 1:(a + 1) * HG_SUB, :] for a in range(nsub)]
    ends = []
    run = jnp.zeros((1, gw), F32)
    for a in range(nsub):
        run = run + tots[a]
        ends.append(run)
    cum_l = ends[-1]
    cum = lc + per_sub([jnp.zeros((1, gw), F32)] + ends[:-1])

    qs = (q * jnp.exp(cum)).astype(BF16)
    koff = kk * jnp.exp(per_sub(tots) - lc)
    ke = (koff * per_sub([jnp.exp(cum_l - e) for e in ends])).astype(BF16)
    koff_b = koff.astype(BF16)
    qoff = []
    kmask = []
    for b in range(nsub - 1):
        r0 = (b + 1) * HG_SUB
        qb = (q[r0:, :] * jnp.exp(cum[r0:, :] - ends[b])).astype(BF16)
        qoff.append(jnp.concatenate([jnp.zeros((r0, gw), BF16), qb], axis=0))
        parts = [koff_b[b * HG_SUB:r0, :], jnp.zeros((chunk - r0, gw), BF16)]
        if b > 0:
            parts.insert(0, jnp.zeros((b * HG_SUB, gw), BF16))
        kmask.append(jnp.concatenate(parts, axis=0))
    vb = v.astype(BF16)
    decay_l = jnp.exp(cum_l)

    for h in range(G_HEADS):
        sl = slice(h * HEAD_DIM, (h + 1) * HEAD_DIM)
        qcat = jnp.concatenate([qo[:, sl] for qo in qoff], axis=1)
        kcat = jnp.concatenate([km[:, sl] for km in kmask], axis=1)
        a = _hgrn_diag_scores(q, kk, lc, h) + _dot_nt(qcat, kcat).astype(BF16)
        st = st_ref[h]
        oa_ref[:, sl] = jnp.dot(a, vb[:, sl], preferred_element_type=F32) + _dot_nt(qs[:, sl], st.astype(BF16))
        upd = jnp.dot(v[:, sl].T.astype(BF16), ke[:, sl], preferred_element_type=F32)
        st_ref[h] = st * decay_l[:, sl] + upd

    @pl.when(need_ref[0] != 0)
    def _():
        q2, kk2, v2, lc2, rsub2 = _hgrn_gates(hq_ref, hf_ref, hi_ref, lb_ref)

        @pl.when(jnp.max(-lc2) > EXP_CLAMP)
        def _():
            v2b = v2.astype(BF16)
            fast = [jnp.dot(_hgrn_diag_scores(q2, kk2, lc2, h), v2b[:, h * HEAD_DIM:(h + 1) * HEAD_DIM],
                            preferred_element_type=F32) for h in range(G_HEADS)]
            oa_ref[...] += _hgrn_diag_exact(q2, kk2, v2, lc2, rsub2, nsub) - jnp.concatenate(fast, axis=1)

    o = oa_ref[...]
    ms = jnp.mean(o * o, axis=-1, keepdims=True)
    hg = hg_ref[...].astype(F32)
    y = o * lax.rsqrt(ms + EPS) * gn_ref[...] * (hg * _sigmoid(hg))
    o_ref[...] = y.astype(o_ref.dtype)


def _hgrn(ph, pt, lb, g_hnorm, batch, seq, chunk):
    n = batch * seq
    gw = G_HEADS * HEAD_DIM
    rows = HG_CHUNKS_PER_STEP * chunk
    ncb = seq // rows
    blk = lambda c: pl.BlockSpec((rows, gw), lambda b, i, nd: (b * ncb + i, c))
    need = (HG_SUB * jnp.log(jnp.min(lb)) < -EXP_CLAMP).astype(I32).reshape(1)
    grid_spec = pltpu.PrefetchScalarGridSpec(
        num_scalar_prefetch=1,
        grid=(batch, ncb),
        in_specs=[blk(0), blk(1), blk(1), blk(2),
                  pl.BlockSpec((1, gw), lambda b, i, nd: (0, 0)),
                  pl.BlockSpec((1, gw), lambda b, i, nd: (0, 0))],
        out_specs=pl.BlockSpec((rows, gw), lambda b, i, nd: (b * ncb + i, 0)),
        scratch_shapes=[pltpu.VMEM((G_HEADS, HEAD_DIM, HEAD_DIM), F32),
                        pltpu.VMEM((chunk, gw), F32)],
    )
    return pl.pallas_call(
        functools.partial(_hgrn_kernel, chunk=chunk),
        grid_spec=grid_spec,
        out_shape=jax.ShapeDtypeStruct((n, gw), BF16),
        compiler_params=_cparams(("parallel", "arbitrary"), 32),
        name="hgrn",
    )(need, pt, ph, pt, pt, lb.reshape(1, gw), g_hnorm.reshape(1, gw))


def _merge_kernel(ya_ref, yr_ref, ga_ref, gb_ref, x_ref, wpa_ref, wpb_ref, wo_ref, g2_ref, wr_ref, br_ref,
                  o_ref, h_ref, r_ref, cnt_ref, mg_ref):
    c = pl.program_id(1)
    nc = pl.num_programs(1)
    tc = ga_ref.shape[1]

    ma = jnp.dot(ya_ref[...], wpa_ref[...], preferred_element_type=F32)
    mb = jnp.dot(yr_ref[...], wpb_ref[...], preferred_element_type=F32)
    merged = _sigmoid(ga_ref[...].astype(F32)) * ma + _sigmoid(gb_ref[...].astype(F32)) * mb
    merged = merged.astype(BF16)
    for k in range(mg_ref.shape[1] // tc):
        @pl.when(c == k)
        def _():
            mg_ref[:, k * tc:(k + 1) * tc] = merged

    @pl.when(c == nc - 1)
    def _():
        x1 = x_ref[...] + jnp.dot(mg_ref[...], wo_ref[...], preferred_element_type=F32)
        o_ref[...] = x1
        _route_tile(x1, g2_ref, wr_ref, br_ref, h_ref, r_ref, cnt_ref)


def _merge(y_att, y_rec, pt, x2, w_pa, w_pb, w_o, g_norm2, w_r, b_r, tm, tc):
    n, d = x2.shape
    aw = y_att.shape[1]
    gw = y_rec.shape[1]
    ga0 = 3072 // tc
    gb0 = 5120 // tc
    return pl.pallas_call(
        _merge_kernel,
        grid=(n // tm, d // tc),
        in_specs=[
            pl.BlockSpec((tm, aw), lambda i, c: (i, 0)),
            pl.BlockSpec((tm, gw), lambda i, c: (i, 0)),
            pl.BlockSpec((tm, tc), lambda i, c: (i, ga0 + c)),
            pl.BlockSpec((tm, tc), lambda i, c: (i, gb0 + c)),
            pl.BlockSpec((tm, d), lambda i, c: (i, 0)),
            pl.BlockSpec((aw, tc), lambda i, c: (0, c)),
            pl.BlockSpec((gw, tc), lambda i, c: (0, c)),
            pl.BlockSpec((d, d), lambda i, c: (0, 0)),
            pl.BlockSpec((1, d), lambda i, c: (0, 0)),
            pl.BlockSpec((d, LANES), lambda i, c: (0, 0)),
            pl.BlockSpec((1, LANES), lambda i, c: (0, 0)),
        ],
        out_specs=[pl.BlockSpec((tm, d), lambda i, c: (i, 0)),
                   pl.BlockSpec((tm, d // 2), lambda i, c: (i, 0)),
                   pl.BlockSpec((tm, LANES), lambda i, c: (i, 0)),
                   pl.BlockSpec((1, LANES), lambda i, c: (0, 0))],
        out_shape=[jax.ShapeDtypeStruct((n, d), F32),
                   jax.ShapeDtypeStruct((n, d // 2), U32),
                   jax.ShapeDtypeStruct((n, LANES), F32),
                   jax.ShapeDtypeStruct((1, LANES), F32)],
        scratch_shapes=[pltpu.VMEM((tm, d), BF16)],
        compiler_params=_cparams(("arbitrary", "arbitrary"), 56),
        name="merge",
    )(y_att, y_rec, pt, pt, x2, w_pa, w_pb, w_o, g_norm2.reshape(1, d), w_r, b_r)


def _route_tile(x, g_ref, wr_ref, br_ref, h_ref, r_ref, cnt_ref):
    ms = jnp.mean(x * x, axis=-1, keepdims=True)
    h = x * lax.rsqrt(ms + EPS) * g_ref[...]
    h_ref[...] = _pack_halves(h)
    lg = jnp.dot(h.astype(BF16), wr_ref[...], preferred_element_type=F32) + br_ref[...]
    tm = lg.shape[0]
    lane = lax.broadcasted_iota(I32, (tm, LANES), 1)
    neg = jnp.float32(-1e30)
    big = jnp.float32(LANES)
    is_g = lane < N_GROUPS
    lgg = jnp.where(is_g, lg, neg)
    mg = jnp.max(lgg, axis=1, keepdims=True)
    zg = jnp.sum(jnp.where(is_g, jnp.exp(lgg - mg), 0.0), axis=1, keepdims=True)
    p_g = 1.0 / zg
    grp = jnp.min(jnp.where(is_g & (lgg == mg), lane.astype(F32), big), axis=1, keepdims=True)
    e_id = lane - N_GROUPS
    e_grp = lax.shift_right_arithmetic(e_id, jnp.int32(EXP_PER_GROUP.bit_length() - 1)).astype(F32)
    e_idf = e_id.astype(F32)
    is_e = (e_id >= 0) & (e_id < N_EXPERTS) & (e_grp == grp)
    lge = jnp.where(is_e, lg, neg)
    m1 = jnp.max(lge, axis=1, keepdims=True)
    i1 = jnp.min(jnp.where(is_e & (lge == m1), e_idf, big), axis=1, keepdims=True)
    is_e2 = is_e & (e_idf != i1)
    lge2 = jnp.where(is_e2, lg, neg)
    m2 = jnp.max(lge2, axis=1, keepdims=True)
    i2 = jnp.min(jnp.where(is_e2 & (lge2 == m2), e_idf, big), axis=1, keepdims=True)
    ex2 = jnp.exp(m2 - m1)
    den = 1.0 + ex2
    g1 = p_g / den
    g2 = p_g * ex2 / den
    @pl.when(pl.program_id(0) == 0)
    def _():
        cnt_ref[...] = jnp.zeros(cnt_ref.shape, F32)

    lanef = lane.astype(F32)
    oh1 = lanef == i1
    oh2 = lanef == i2
    oh = jnp.where(oh1 | oh2, 1.0, 0.0)
    ti = lax.broadcasted_iota(I32, (tm, tm), 0)
    si = lax.broadcasted_iota(I32, (tm, tm), 1)
    tri = jnp.where(si < ti, 1.0, 0.0).astype(BF16)
    before = jnp.dot(tri, oh.astype(BF16), preferred_element_type=F32) + cnt_ref[...]
    rank1 = jnp.sum(jnp.where(oh1, before, 0.0), axis=1, keepdims=True)
    rank2 = jnp.sum(jnp.where(oh2, before, 0.0), axis=1, keepdims=True)
    cnt_ref[...] += jnp.sum(oh, axis=0, keepdims=True)

    r = jnp.where(lane == 0, i1, 0.0)
    r = jnp.where(lane == 1, i2, r)
    r = jnp.where(lane == 2, g1, r)
    r = jnp.where(lane == 3, g2, r)
    r = jnp.where(lane == 4, rank1, r)
    r = jnp.where(lane == 5, rank2, r)
    r_ref[...] = r


def _dispatch_kernel(prow_ref, h_ref, xs_hbm, sem, *, tm):
    base = 2 * tm * pl.program_id(0)

    def scatter_group(g, carry):
        r0 = pl.multiple_of(g * DMA_GROUP, DMA_GROUP)
        rows = h_ref.at[pl.ds(r0, DMA_GROUP), :]
        for u in range(DMA_GROUP):
            for s in range(2):
                dst_row = prow_ref[base + 2 * r0 + (2 * u + s)]
                pltpu.make_async_copy(rows.at[pl.ds(u, 1), :], xs_hbm.at[pl.ds(dst_row, 1), :],
                                      sem).start(priority=s)
        return carry

    lax.fori_loop(0, tm // DMA_GROUP, scatter_group, 0)
    for _ in range(2):
        pltpu.make_async_copy(h_ref, xs_hbm.at[pl.ds(0, tm), :], sem).wait()


def _dispatch(h2, prow, tm):
    n, d = h2.shape
    grid_spec = pltpu.PrefetchScalarGridSpec(
        num_scalar_prefetch=1,
        grid=(n // tm,),
        in_specs=[pl.BlockSpec((tm, d), lambda i, pr: (i, 0))],
        out_specs=pl.BlockSpec(memory_space=pl.ANY),
        scratch_shapes=[pltpu.SemaphoreType.DMA(())],
    )
    return pl.pallas_call(
        functools.partial(_dispatch_kernel, tm=tm),
        grid_spec=grid_spec,
        out_shape=jax.ShapeDtypeStruct((2 * n, d), h2.dtype),
        compiler_params=_cparams(("arbitrary",), 32),
        name="dispatch",
    )(prow, h2)


def _moe_kernel(pt_ref, pe_ref, plo_ref, phi_ref, slot_ref, nxt_ref, np_ref, x_ref, wg_hbm, wu_hbm, wd_hbm,
                y_ref, wgf_ref, wuf_ref, wdf_ref, wgb_ref, wub_ref, wdb_ref, yacc_ref, sem):
    p = pl.program_id(0)
    active = p < np_ref[0]
    prev = jnp.maximum(p - 1, 0)
    new_expert = (p == 0) | (pe_ref[p] != pe_ref[prev])
    new_tile = (p == 0) | (pt_ref[p] != pt_ref[prev])
    slot = slot_ref[p]

    def weight_copies(e, s):
        return (pltpu.make_async_copy(wg_hbm.at[e], wgf_ref.at[s], sem.at[s]),
                pltpu.make_async_copy(wu_hbm.at[e], wuf_ref.at[s], sem.at[s]),
                pltpu.make_async_copy(wd_hbm.at[e], wdf_ref.at[s], sem.at[s]))

    @pl.when(p == 0)
    def _():
        for cp in weight_copies(pe_ref[0], 0):
            cp.start()

    @pl.when(active & new_expert)
    def _():
        for cp in weight_copies(pe_ref[p], slot):
            cp.wait()
        wgb_ref[...] = wgf_ref[slot].astype(BF16)
        wub_ref[...] = wuf_ref[slot].astype(BF16)
        wdb_ref[...] = wdf_ref[slot].astype(BF16)

        @pl.when(nxt_ref[p] >= 0)
        def _():
            for cp in weight_copies(nxt_ref[p], 1 - slot):
                cp.start()

    @pl.when(active)
    def _():
        tm, half = x_ref.shape
        lo, hi = _unpack_halves(x_ref[...])
        xl = lo.astype(BF16)
        xh = hi.astype(BF16)
        a = (jnp.dot(xl, wgb_ref[0:half, :], preferred_element_type=F32)
             + jnp.dot(xh, wgb_ref[half:, :], preferred_element_type=F32))
        u = (jnp.dot(xl, wub_ref[0:half, :], preferred_element_type=F32)
             + jnp.dot(xh, wub_ref[half:, :], preferred_element_type=F32))
        hmid = (a * _sigmoid(a) * u).astype(BF16)
        yv = jnp.dot(hmid, wdb_ref[...], preferred_element_type=F32)
        row = lax.broadcasted_iota(I32, (tm, 1), 0)
        mine = (row >= plo_ref[p]) & (row < phi_ref[p])
        yv = jnp.where(mine, yv, 0.0)

        @pl.when(new_tile)
        def _():
            yacc_ref[...] = yv

        @pl.when(jnp.logical_not(new_tile))
        def _():
            yacc_ref[...] += yv

        nxt_p = jnp.minimum(p + 1, pl.num_programs(0) - 1)
        last_of_tile = (p == np_ref[0] - 1) | (pt_ref[nxt_p] != pt_ref[p])

        @pl.when(last_of_tile)
        def _():
            y_ref[...] = _pack_halves(yacc_ref[...])


def _moe(xs, pairs, w_gate, w_up, w_down, tm):
    pair_tile, pair_expert, pair_lo, pair_hi, pair_slot, pair_next, n_pairs = pairs
    na, half = xs.shape
    d, de = w_gate.shape[1:]
    max_pairs = pair_tile.shape[0]
    grid_spec = pltpu.PrefetchScalarGridSpec(
        num_scalar_prefetch=7,
        grid=(max_pairs,),
        in_specs=[
            pl.BlockSpec((tm, half), lambda p, pt, *_: (pt[p], 0)),
            pl.BlockSpec(memory_space=pl.ANY),
            pl.BlockSpec(memory_space=pl.ANY),
            pl.BlockSpec(memory_space=pl.ANY),
        ],
        out_specs=pl.BlockSpec((tm, half), lambda p, pt, *_: (pt[p], 0)),
        scratch_shapes=[
            pltpu.VMEM((2, d, de), F32),
            pltpu.VMEM((2, d, de), F32),
            pltpu.VMEM((2, de, d), F32),
            pltpu.VMEM((d, de), BF16),
            pltpu.VMEM((d, de), BF16),
            pltpu.VMEM((de, d), BF16),
            pltpu.VMEM((tm, d), F32),
            pltpu.SemaphoreType.DMA((2,)),
        ],
    )
    return pl.pallas_call(
        _moe_kernel,
        grid_spec=grid_spec,
        out_shape=jax.ShapeDtypeStruct((na, half), U32),
        compiler_params=_cparams(("arbitrary",), 56),
        name="moe",
    )(pair_tile, pair_expert, pair_lo, pair_hi, pair_slot, pair_next, n_pairs, xs, w_gate, w_up, w_down)


def _moe_rows(r, cnt):
    sizes = cnt[0, :N_EXPERTS].astype(I32)
    starts = jnp.cumsum(sizes) - sizes
    e12 = r[:, 0:2].astype(I32)
    rank = r[:, 4:6].astype(I32)
    onehot = e12[:, :, None] == jnp.arange(N_EXPERTS, dtype=I32)[None, None, :]
    start_a = jnp.sum(jnp.where(onehot, starts[None, None, :], 0), axis=-1)
    return (start_a + rank).reshape(-1).astype(I32), sizes


def _count_le(sorted_vals, x):
    return jnp.sum((sorted_vals[None, :] <= x[:, None]).astype(I32), axis=1)


def _moe_pairs(sizes, na, tm):
    ntiles = na // tm
    max_pairs = ntiles + N_EXPERTS - 1
    ends = jnp.cumsum(sizes)
    starts = ends - sizes
    t0 = jnp.arange(ntiles, dtype=I32) * tm
    e_lo = _count_le(ends, t0)
    e_hi = _count_le(ends, t0 + (tm - 1))
    per_tile = e_hi - e_lo + 1
    pend = jnp.cumsum(per_tile)
    pstart = pend - per_tile
    n_pairs = pend[-1]
    pc = jnp.minimum(jnp.arange(max_pairs, dtype=I32), n_pairs - 1)
    tile = _count_le(pend, pc)
    expert = e_lo[tile] + (pc - pstart[tile])
    lo = jnp.clip(starts[expert] - tile * tm, 0, tm).astype(I32)
    hi = jnp.clip(ends[expert] - tile * tm, 0, tm).astype(I32)
    expert = expert.astype(I32)
    first = jnp.concatenate([jnp.ones((1,), bool), expert[1:] != expert[:-1]])
    slot = ((jnp.cumsum(first) - 1) % 2).astype(I32)
    idx = jnp.arange(max_pairs, dtype=I32)
    first_at = jnp.where(first, idx, max_pairs)
    nxt_first = jnp.concatenate([lax.cummin(first_at[::-1])[::-1][1:], jnp.full((1,), max_pairs, I32)])
    nxt = jnp.where(nxt_first < max_pairs, expert[jnp.minimum(nxt_first, max_pairs - 1)], -1).astype(I32)
    return tile, expert, lo, hi, slot, nxt, n_pairs.astype(I32).reshape(1)


def _final_kernel(prow_ref, x_ref, r_ref, g_ref, y_hbm, o_ref, ybuf, sem, *, tm):
    i = pl.program_id(0)
    cur = i % 2

    def issue(step, buf):
        base = 2 * tm * step

        def gather_group(g, carry):
            r0 = pl.multiple_of(g * DMA_GROUP, DMA_GROUP)
            for s in range(2):
                rows = ybuf.at[buf, s, pl.ds(r0, DMA_GROUP), :]
                for u in range(DMA_GROUP):
                    src_row = prow_ref[base + 2 * r0 + (2 * u + s)]
                    pltpu.make_async_copy(y_hbm.at[pl.ds(src_row, 1), :], rows.at[pl.ds(u, 1), :],
                                          sem.at[buf]).start(priority=u % 2)
            return carry

        lax.fori_loop(0, tm // DMA_GROUP, gather_group, 0)

    @pl.when(i == 0)
    def _():
        issue(0, 0)

    @pl.when(i + 1 < pl.num_programs(0))
    def _():
        issue(i + 1, 1 - cur)

    for s in range(2):
        pltpu.make_async_copy(y_hbm.at[pl.ds(0, tm), :], ybuf.at[cur, s], sem.at[cur]).wait()
    r = r_ref[...]
    g1 = r[:, 2:3]
    g2 = r[:, 3:4]
    half = ybuf.shape[-1]
    lo0, hi0 = _unpack_halves(ybuf[cur, 0])
    lo1, hi1 = _unpack_halves(ybuf[cur, 1])
    xa = x_ref[:, :half] + g1 * lo0 + g2 * lo1
    xb = x_ref[:, half:] + g1 * hi0 + g2 * hi1
    ms = (jnp.sum(xa * xa, axis=-1, keepdims=True) + jnp.sum(xb * xb, axis=-1, keepdims=True)) / (2 * half)
    rs = lax.rsqrt(ms + EPS)
    o_ref[:, :half] = xa * rs * g_ref[:, :half]
    o_ref[:, half:] = xb * rs * g_ref[:, half:]


def _final(x1, y, r, prow, g, tm):
    n, d = x1.shape
    grid_spec = pltpu.PrefetchScalarGridSpec(
        num_scalar_prefetch=1,
        grid=(n // tm,),
        in_specs=[
            pl.BlockSpec((tm, d), lambda i, pr: (i, 0)),
            pl.BlockSpec((tm, LANES), lambda i, pr: (i, 0)),
            pl.BlockSpec((1, d), lambda i, pr: (0, 0)),
            pl.BlockSpec(memory_space=pl.ANY),
        ],
        out_specs=pl.BlockSpec((tm, d), lambda i, pr: (i, 0)),
        scratch_shapes=[pltpu.VMEM((2, 2, tm, d // 2), U32), pltpu.SemaphoreType.DMA((2,))],
    )
    return pl.pallas_call(
        functools.partial(_final_kernel, tm=tm),
        grid_spec=grid_spec,
        out_shape=jax.ShapeDtypeStruct((n, d), F32),
        compiler_params=_cparams(("arbitrary",), 48),
        name="final",
    )(prow, x1, r, g.reshape(1, d), y)


def _pack_w_in(w):
    d, din = w.shape
    tc = 256
    wt = jnp.swapaxes(w, 0, 1)

    def pack_kernel(w_ref, o_ref):
        o_ref[0:912, :] = w_ref[0:912, :].astype(BF16)
        o_ref[912:1024, :] = jnp.zeros((LANES - IDX_HEADS, tc), BF16)
        o_ref[1024:2048, :] = w_ref[1936:2960, :].astype(BF16)
        o_ref[2048:3072, :] = w_ref[912:1936, :].astype(BF16)
        o_ref[3072:, :] = w_ref[2960:, :].astype(BF16)

    return pl.pallas_call(
        pack_kernel,
        grid=(d // tc,),
        in_specs=[pl.BlockSpec((din, tc), lambda i: (0, i))],
        out_specs=pl.BlockSpec((PH_COLS + PT_COLS, tc), lambda i: (0, i)),
        out_shape=jax.ShapeDtypeStruct((PH_COLS + PT_COLS, d), BF16),
        compiler_params=_cparams(("parallel",), 48),
        name="pack_w_in",
    )(wt)


def _rope_tables(seq):
    half = HEAD_DIM // 2
    inv = 1.0 / (ROPE_THETA ** (jnp.arange(0, HEAD_DIM, 2, dtype=F32) / HEAD_DIM))
    ang = jnp.arange(seq, dtype=F32)[:, None] * inv[None, :]
    c, s = jnp.cos(ang), jnp.sin(ang)
    del half
    c2, s2 = jnp.concatenate([c, c], axis=1), jnp.concatenate([-s, s], axis=1)
    return c2, s2, c2.T, s2.T


def _layer(x2, batch, seq, rope, lb, g_norm1, w_in, g_cq, w_uq, w_qidx, g_hnorm, w_pa, w_pb, w_o,
           g_norm2, w_grp, b_grp, w_exp, b_exp, w_gate, w_up, w_down, g_out):
    n, d = x2.shape
    ph, pt = _in_proj(x2, g_norm1, _pack_w_in(w_in), tm=min(1024, n), tn=1024)
    tq, tk = min(256, seq), min(256, seq)
    qt, qit, wt, k, ki, vt = _dsa_prep(ph, rope, g_cq, w_uq.T.astype(BF16), w_qidx.T.astype(BF16), seq, tm=tk, tq=tq)
    y_att = _dsa(qit, qt, wt, ki, k, vt, batch, seq, tq=tq, tk=tk)
    y_rec = _hgrn(ph, pt, lb, g_hnorm, batch, seq, chunk=64)
    w_r = jnp.concatenate([w_grp, w_exp, jnp.zeros((d, LANES - N_GROUPS - N_EXPERTS), F32)], axis=1).astype(BF16)
    b_r = jnp.concatenate([b_grp, b_exp, jnp.zeros((LANES - N_GROUPS - N_EXPERTS,), F32)]).reshape(1, LANES)
    x1, h2, r, cnt = _merge(y_att, y_rec, pt, x2, w_pa.astype(BF16), w_pb.astype(BF16), w_o.astype(BF16),
                            g_norm2, w_r, b_r, tm=min(512, n), tc=1024)
    prow, sizes = _moe_rows(r, cnt)
    xs = _dispatch(h2, prow, tm=min(512, n))
    tm_moe = 256
    assert (2 * n) % tm_moe == 0
    y = _moe(xs, _moe_pairs(sizes, 2 * n, tm_moe), w_gate, w_up, w_down, tm_moe)
    return _final(x1, y, r, prow, g_out, tm=min(512, n))


def kernel(x, g_norm1, w_in, g_cq, w_uq, w_qidx, lb_logits, g_hnorm, w_pa, w_pb, w_o, g_norm2, w_grp, b_grp,
           w_exp, b_exp, w_gate, w_up, w_down, g_final):
    batch, seq, d = x.shape
    depth = g_norm1.shape[0]
    rope = _rope_tables(seq)
    lb_all = jnp.cumsum(jax.nn.softmax(lb_logits.astype(F32), axis=0), axis=0)
    x2 = x.reshape(batch * seq, d)
    for l in range(depth):
        assert depth == 1
        x2 = _layer(x2, batch, seq, rope, lb_all[l], g_norm1[l], w_in[l], g_cq[l], w_uq[l], w_qidx[l],
                    g_hnorm[l], w_pa[l], w_pb[l], w_o[l], g_norm2[l], w_grp[l], b_grp[l], w_exp[l], b_exp[l],
                    w_gate[l], w_up[l], w_down[l], g_final)
    return x2.reshape(batch, seq, d)
```

```python
import functools
import math

import jax
import jax.numpy as jnp
from jax import lax
from jax.experimental import pallas as pl
from jax.experimental.pallas import tpu as pltpu

F32 = jnp.float32
BF16 = jnp.bfloat16
I32 = jnp.int32
U32 = jnp.uint32

EPS = 1e-6
ROPE_THETA = 10000.0
LANES = 128
MXU_LANES = 256
DMA_GROUP = 32
VT_ROWS = 128 + 16

A_HEADS = 8
HEAD_DIM = 128
IDX_HEADS = 16
Q_RANK = 512
TOPK_MAX = 256
G_HEADS = 8
N_GROUPS = 4
EXP_PER_GROUP = 8
N_EXPERTS = N_GROUPS * EXP_PER_GROUP
D_EXPERT = 512

INT_MIN = -2147483648
EXP_CLAMP = 80.0


def _cparams(sem, vmem_mb):
    return pltpu.CompilerParams(dimension_semantics=sem, vmem_limit_bytes=vmem_mb << 20)


def _pack_halves(x):
    c = x.shape[1] // 2
    bits = lax.bitcast_convert_type(x.astype(BF16).astype(F32), U32)
    return lax.shift_right_logical(bits[:, :c], jnp.uint32(16)) | (bits[:, c:] & jnp.uint32(0xFFFF0000))


def _unpack_halves(w):
    lo = lax.bitcast_convert_type(lax.shift_left(w, jnp.uint32(16)), F32)
    hi = lax.bitcast_convert_type(w & jnp.uint32(0xFFFF0000), F32)
    return lo, hi


def _sigmoid(x):
    return 0.5 * jnp.tanh(0.5 * x) + 0.5


PH_COLS = 2048
PT_COLS = 7168


def _inproj_kernel(x_ref, g_ref, w_ref, oh_ref, ot_ref, xn_ref, *, nh):
    j = pl.program_id(1)

    @pl.when(j == 0)
    def _():
        x = x_ref[...]
        ms = jnp.mean(x * x, axis=-1, keepdims=True)
        xn_ref[...] = (x * lax.rsqrt(ms + EPS) * g_ref[...]).astype(BF16)

    y = _dot_nt(xn_ref[...], w_ref[...])

    @pl.when(j < nh)
    def _():
        oh_ref[...] = y

    @pl.when(j >= nh)
    def _():
        ot_ref[...] = y.astype(ot_ref.dtype)


def _in_proj(x2, g, w, tm, tn):
    n, d = x2.shape
    nh = PH_COLS // tn
    nt = PT_COLS // tn
    return pl.pallas_call(
        functools.partial(_inproj_kernel, nh=nh),
        grid=(n // tm, nh + nt),
        in_specs=[
            pl.BlockSpec((tm, d), lambda i, j: (i, 0)),
            pl.BlockSpec((1, d), lambda i, j: (0, 0)),
            pl.BlockSpec((tn, d), lambda i, j: (j, 0)),
        ],
        out_specs=[pl.BlockSpec((tm, tn), lambda i, j: (i, jnp.minimum(j, nh - 1))),
                   pl.BlockSpec((tm, tn), lambda i, j: (i, jnp.maximum(j - nh, 0)))],
        out_shape=[jax.ShapeDtypeStruct((n, PH_COLS), F32),
                   jax.ShapeDtypeStruct((n, PT_COLS), BF16)],
        scratch_shapes=[pltpu.VMEM((tm, d), BF16)],
        compiler_params=_cparams(("parallel", "arbitrary"), 56),
        name="in_proj",
    )(x2, g.reshape(1, d), w)


def _rope(x, c2, s2):
    return x * c2 + pltpu.roll(x, HEAD_DIM // 2, 1) * s2


def _rope_t(x, c2t, s2t):
    half = HEAD_DIM // 2
    return x * c2t + jnp.concatenate([x[half:, :], x[:half, :]], axis=0) * s2t


def _prep_kernel(cq_ref, ka_ref, va_ref, kx_ref, wi_ref, cos_ref, sin_ref, cost_ref, sint_ref, g_ref, wuq_ref, wqi_ref,
                 qt_ref, qit_ref, wt_ref, k_ref, ki_ref, vt_ref, *, scale, wscale, tq):
    nqb = cq_ref.shape[0] // tq
    c2 = cos_ref[...]
    s2 = sin_ref[...]
    c2t = cost_ref[...]
    s2t = sint_ref[...]
    cq = cq_ref[...]
    ms = jnp.mean(cq * cq, axis=-1, keepdims=True)
    cqn = (cq * lax.rsqrt(ms + EPS) * g_ref[...]).astype(BF16)
    q_t = _dot_nt(wuq_ref[...], cqn)
    for h in range(A_HEADS):
        qh_t = _rope_t(q_t[h * HEAD_DIM:(h + 1) * HEAD_DIM, :], c2t, s2t) * scale
        for b in range(nqb):
            qt_ref[b, :, h * tq:(h + 1) * tq] = qh_t[:, b * tq:(b + 1) * tq].astype(BF16)
    qi_t = _dot_nt(wqi_ref[...], cqn)
    for h in range(IDX_HEADS):
        qh_t = _rope_t(qi_t[h * HEAD_DIM:(h + 1) * HEAD_DIM, :], c2t, s2t)
        for b in range(nqb):
            qit_ref[b, :, h * tq:(h + 1) * tq] = qh_t[:, b * tq:(b + 1) * tq].astype(BF16)
    w_t = (wi_ref[...] * wscale).T
    for h in range(IDX_HEADS):
        for b in range(nqb):
            wt_ref[b, :, h * tq:(h + 1) * tq] = w_t[h:h + 1, b * tq:(b + 1) * tq]
    k_ref[...] = _rope(ka_ref[...], c2, s2).astype(BF16)
    ki_ref[...] = _rope(kx_ref[...], c2, s2).astype(BF16)
    vt_ref[0, 0:HEAD_DIM, :] = va_ref[...].T.astype(BF16)
    ones_row = lax.broadcasted_iota(I32, (VT_ROWS - HEAD_DIM, va_ref.shape[0]), 0) == 0
    vt_ref[0, HEAD_DIM:VT_ROWS, :] = jnp.where(ones_row, 1.0, 0.0).astype(BF16)


def _dsa_prep(p, rope, g_cq, w_uq_t, w_qidx_t, seq, tm, tq):
    n = p.shape[0]
    nsb = seq // tm
    nqb = tm // tq
    aw = A_HEADS * HEAD_DIM
    iw = IDX_HEADS * HEAD_DIM
    row = lambda i: (i, 0)
    blk3 = lambda i: (i, 0, 0)
    kern = functools.partial(_prep_kernel, scale=HEAD_DIM ** -0.5 * math.log2(math.e),
                             wscale=IDX_HEADS ** -0.5 * HEAD_DIM ** -0.5, tq=tq)
    return pl.pallas_call(
        kern,
        grid=(n // tm,),
        in_specs=[
            pl.BlockSpec((tm, Q_RANK), lambda i: (i, 0)),
            pl.BlockSpec((tm, LANES), lambda i: (i, 4)),
            pl.BlockSpec((tm, LANES), lambda i: (i, 5)),
            pl.BlockSpec((tm, LANES), lambda i: (i, 6)),
            pl.BlockSpec((tm, LANES), lambda i: (i, 7)),
            pl.BlockSpec((tm, LANES), lambda i: (i % nsb, 0)),
            pl.BlockSpec((tm, LANES), lambda i: (i % nsb, 0)),
            pl.BlockSpec((HEAD_DIM, tm), lambda i: (0, i % nsb)),
            pl.BlockSpec((HEAD_DIM, tm), lambda i: (0, i % nsb)),
            pl.BlockSpec((1, Q_RANK), lambda i: (0, 0)),
            pl.BlockSpec((aw, Q_RANK), lambda i: (0, 0)),
            pl.BlockSpec((iw, Q_RANK), lambda i: (0, 0)),
        ],
        out_specs=[
            pl.BlockSpec((nqb, HEAD_DIM, A_HEADS * tq), blk3),
            pl.BlockSpec((nqb, HEAD_DIM, IDX_HEADS * tq), blk3),
            pl.BlockSpec((nqb, 1, IDX_HEADS * tq), blk3),
            pl.BlockSpec((tm, LANES), row),
            pl.BlockSpec((tm, LANES), row),
            pl.BlockSpec((1, VT_ROWS, tm), blk3),
        ],
        out_shape=[
            jax.ShapeDtypeStruct((n // tq, HEAD_DIM, A_HEADS * tq), BF16),
            jax.ShapeDtypeStruct((n // tq, HEAD_DIM, IDX_HEADS * tq), BF16),
            jax.ShapeDtypeStruct((n // tq, 1, IDX_HEADS * tq), F32),
            jax.ShapeDtypeStruct((n, LANES), BF16),
            jax.ShapeDtypeStruct((n, LANES), BF16),
            jax.ShapeDtypeStruct((n // tm, VT_ROWS, tm), BF16),
        ],
        compiler_params=_cparams(("parallel",), 40),
        name="dsa_prep",
    )(p, p, p, p, p, *rope, g_cq.reshape(1, Q_RANK), w_uq_t, w_qidx_t)


def _dot_nt(a, b):
    return lax.dot_general(a, b, (((1,), (1,)), ((), ())), preferred_element_type=F32)


def _dsa_kernel(qit_ref, qt_ref, wt_ref, ki_ref, k_ref, vt_ref, o_ref,
                key_ref, sa_ref, sb_ref, pa_ref, pb_ref, acc_ref, *, tq, tk, topk, idx_bits):
    i = pl.program_id(1)
    nkc = ((i + 1) * tq + tk - 1) // tk
    q_pos = i * tq + lax.broadcasted_iota(I32, (tk, tq), 1)
    k_off = lax.broadcasted_iota(I32, (tk, tq), 0)

    def score_chunks(j0, nch):
        kic = ki_ref[pl.ds(pl.multiple_of(j0 * tk, tk), nch * tk), :]
        acc = jnp.zeros((nch * tk, tq), F32)
        hpm = max(1, MXU_LANES // tq)
        for h0 in range(0, IDX_HEADS, hpm):
            sl = slice(h0 * tq, (h0 + hpm) * tq)
            lg = jnp.dot(kic, qit_ref[0, :, sl], preferred_element_type=F32)
            x = jnp.maximum(lg, 0.0) * wt_ref[0, :, sl]
            for u in range(hpm):
                acc = acc + x[:, u * tq:(u + 1) * tq]
        acc = jnp.where(acc == 0.0, 0.0, acc)
        bits = lax.bitcast_convert_type(acc, I32)
        key = jnp.where(bits < 0, bits ^ jnp.int32(0x7FFFFFFF), bits)
        for c in range(nch):
            causal = ((j0 + c) * tk + k_off) <= q_pos
            key_ref[j0 + c] = jnp.where(causal, key[c * tk:(c + 1) * tk, :], jnp.int32(INT_MIN))

    lax.fori_loop(0, nkc // 2, lambda t, c: (score_chunks(2 * t, 2), c)[1], 0)

    @pl.when(nkc % 2 == 1)
    def _():
        score_chunks(nkc - 1, 1)

    def count_rows(pred):
        def cnt_chunk(j, part):
            hit = pred(j, key_ref[j]).astype(I32)
            return part + jnp.sum(hit.reshape(tk // 8, 8, tq), axis=0)

        part = lax.fori_loop(0, nkc, cnt_chunk, jnp.zeros((8, tq), I32))
        return jnp.sum(part, axis=0, keepdims=True)

    def bit_step(b, c):
        t_u, cnt_t = c
        bit = jnp.left_shift(jnp.int32(1), 31 - b)
        cand_s = (t_u | bit) ^ jnp.int32(INT_MIN)
        cnt = count_rows(lambda j, key: key >= cand_s)
        take = cnt >= topk
        return jnp.where(take, t_u | bit, t_u), jnp.where(take, cnt, cnt_t)

    n_causal = i * tq + lax.broadcasted_iota(I32, (1, tq), 1) + 1
    t_u, cnt_t = lax.fori_loop(0, 32, bit_step, (jnp.zeros((1, tq), I32), n_causal))
    thr = jnp.maximum(t_u ^ jnp.int32(INT_MIN), jnp.int32(INT_MIN + 1))

    @pl.when(jnp.max(cnt_t) > topk)
    def _():
        need = topk - count_rows(lambda j, key: key > thr)

        def idx_step(b, cut):
            cand = cut | jnp.left_shift(jnp.int32(1), idx_bits - 1 - b)
            cnt = count_rows(lambda j, key: (key == thr) & ((j * tk + k_off) < cand))
            return jnp.where(cnt <= need, cand, cut)

        cut = lax.fori_loop(0, idx_bits, idx_step, jnp.zeros((1, tq), I32))

        def demote(j, carry):
            key = key_ref[j]
            drop = (key == thr) & ((j * tk + k_off) >= cut)
            key_ref[j] = jnp.where(drop, key - 1, key)
            return carry

        lax.fori_loop(0, nkc, demote, 0)

    aw = A_HEADS * tq
    acc_ref[...] = jnp.zeros(acc_ref.shape, F32)

    def scores(j, s_ref):
        kc = k_ref[pl.ds(pl.multiple_of(j * tk, tk), tk), :]
        sel = key_ref[j] >= thr
        cmax = []
        for h in range(A_HEADS):
            sl = slice(h * tq, (h + 1) * tq)
            s_h = jnp.dot(kc, qt_ref[0, :, sl], preferred_element_type=F32)
            s_h = jnp.where(sel, s_h, -jnp.inf)
            s_ref[:, sl] = s_h
            cmax.append(jnp.max(s_h, axis=0, keepdims=True))
        return jnp.concatenate(cmax, axis=1)

    def absorb(j, s_ref, p_ref, cmax, m_old):
        m_new = jnp.maximum(m_old, cmax)
        alpha = jnp.exp2(m_old - m_new)
        for h in range(A_HEADS):
            sl = slice(h * tq, (h + 1) * tq)
            p_ref[:, sl] = jnp.exp2(s_ref[:, sl] - m_new[:, sl]).astype(BF16)
        acc_ref[...] = alpha * acc_ref[...] + jnp.dot(vt_ref[j], p_ref[...], preferred_element_type=F32)
        return m_new

    def pair(t, carry):
        m, cm_a = carry
        a = 2 * t
        cm_b = scores(a + 1, sb_ref)
        m = absorb(a, sa_ref, pa_ref, cm_a, m)
        cm_next = scores(jnp.minimum(a + 2, nkc - 1), sa_ref)
        m = absorb(a + 1, sb_ref, pb_ref, cm_b, m)
        return m, cm_next

    m_fin, cm_last = lax.fori_loop(0, nkc // 2, pair, (jnp.full((1, aw), -1e30, F32), scores(0, sa_ref)))

    @pl.when(nkc % 2 == 1)
    def _():
        absorb(nkc - 1, sa_ref, pa_ref, cm_last, m_fin)

    out_t = acc_ref[0:HEAD_DIM, :] / acc_ref[HEAD_DIM:HEAD_DIM + 1, :]
    for h in range(A_HEADS):
        o_ref[:, h * HEAD_DIM:(h + 1) * HEAD_DIM] = out_t[:, h * tq:(h + 1) * tq].T.astype(o_ref.dtype)


def _dsa(qit, qt, wt, ki, k, vt, batch, seq, tq, tk):
    n = batch * seq
    nqb = seq // tq
    nkb = seq // tk
    topk = min(TOPK_MAX, seq // 4)
    aw = A_HEADS * HEAD_DIM
    kern = functools.partial(_dsa_kernel, tq=tq, tk=tk, topk=topk, idx_bits=seq.bit_length())
    qblk = lambda b, i: (b * nqb + i, 0, 0)
    return pl.pallas_call(
        kern,
        grid=(batch, nqb),
        in_specs=[
            pl.BlockSpec((1, HEAD_DIM, IDX_HEADS * tq), qblk),
            pl.BlockSpec((1, HEAD_DIM, A_HEADS * tq), qblk),
            pl.BlockSpec((1, 1, IDX_HEADS * tq), qblk),
            pl.BlockSpec((seq, LANES), lambda b, i: (b, 0)),
            pl.BlockSpec((seq, LANES), lambda b, i: (b, 0)),
            pl.BlockSpec((nkb, VT_ROWS, tk), lambda b, i: (b, 0, 0)),
        ],
        out_specs=pl.BlockSpec((tq, aw), lambda b, i: (b * nqb + i, 0)),
        out_shape=jax.ShapeDtypeStruct((n, aw), BF16),
        scratch_shapes=[
            pltpu.VMEM((nkb, tk, tq), I32),
            pltpu.VMEM((tk, A_HEADS * tq), F32),
            pltpu.VMEM((tk, A_HEADS * tq), F32),
            pltpu.VMEM((tk, A_HEADS * tq), BF16),
            pltpu.VMEM((tk, A_HEADS * tq), BF16),
            pltpu.VMEM((VT_ROWS, A_HEADS * tq), F32),
        ],
        compiler_params=_cparams(("parallel", "arbitrary"), 48),
        name="dsa",
    )(qit, qt, wt, ki, k, vt)


HG_SUB = 16
HG_CHUNKS_PER_STEP = 4


def _hgrn_diag_exact(q, kk, v, lc, rsub, nsub):
    chunk, gw = q.shape

    def bcast(x, s):
        return jnp.concatenate([jnp.broadcast_to(x[a * HG_SUB + s:a * HG_SUB + s + 1, :], (HG_SUB, gw))
                                for a in range(nsub)], axis=0)

    acc = [jnp.zeros((chunk, HEAD_DIM), F32) for _ in range(G_HEADS)]
    for s in range(HG_SUB):
        term = q * bcast(kk, s) * jnp.exp(jnp.minimum(lc - bcast(lc, s), 0.0))
        term = jnp.where(rsub >= s, term, 0.0)
        vrow = bcast(v, s)
        for h in range(G_HEADS):
            sl = slice(h * HEAD_DIM, (h + 1) * HEAD_DIM)
            acc[h] = acc[h] + jnp.sum(term[:, sl], axis=1, keepdims=True) * vrow[:, sl]
    return jnp.concatenate(acc, axis=1)


def _hgrn_gates(hq_ref, hf_ref, hi_ref, lb_ref):
    chunk, gw = hq_ref.shape
    lb = lb_ref[...]
    f = lb + (1.0 - lb) * _sigmoid(hf_ref[...])
    kk = 1.0 - f
    hq = hq_ref[...].astype(F32)
    q = hq * _sigmoid(hq)
    v = hi_ref[...].astype(F32)
    rsub = lax.broadcasted_iota(I32, (chunk, gw), 0) & (HG_SUB - 1)
    lc = jnp.log(f)
    sh = 1
    while sh < HG_SUB:
        lc = lc + jnp.where(rsub >= sh, pltpu.roll(lc, sh, 0), 0.0)
        sh *= 2
    return q, kk, v, lc, rsub


def _hgrn_diag_scores(q, kk, lc, h):
    chunk = q.shape[0]
    sl = slice(h * HEAD_DIM, (h + 1) * HEAD_DIM)
    qd = (q[:, sl] * jnp.exp(lc[:, sl])).astype(BF16)
    kd = (kk[:, sl] * jnp.exp(jnp.minimum(-lc[:, sl], EXP_CLAMP))).astype(BF16)
    ti = lax.broadcasted_iota(I32, (chunk, chunk), 0)
    si = lax.broadcasted_iota(I32, (chunk, chunk), 1)
    diag_ok = (ti // HG_SUB == si // HG_SUB) & (si <= ti)
    return jnp.where(diag_ok, _dot_nt(qd, kd), 0.0).astype(BF16)


def _hgrn_kernel(need_ref, hq_ref, hf_ref, hi_ref, hg_ref, lb_ref, gn_ref, o_ref, st_ref, oa_ref, *, chunk):
    @pl.when(pl.program_id(1) == 0)
    def _():
        st_ref[...] = jnp.zeros(st_ref.shape, F32)

    for c in range(hq_ref.shape[0] // chunk):
        rows = pl.ds(c * chunk, chunk)
        _hgrn_chunk(need_ref, hq_ref.at[rows], hf_ref.at[rows], hi_ref.at[rows], hg_ref.at[rows], lb_ref, gn_ref,
                    o_ref.at[rows], st_ref, oa_ref)


def _hgrn_chunk(need_ref, hq_ref, hf_ref, hi_ref, hg_ref, lb_ref, gn_ref, o_ref, st_ref, oa_ref):
    chunk, gw = hq_ref.shape
    nsub = chunk // HG_SUB

    q, kk, v, lc, rsub = _hgrn_gates(hq_ref, hf_ref, hi_ref, lb_ref)
    def per_sub(vals):
        return jnp.concatenate([jnp.broadcast_to(x, (HG_SUB, gw)) for x in vals], axis=0)

    tots = [lc[(a + 1) * HG_SUB - 1:(a + 1) * HG_SUB, :] for a in range(nsub)]
    ends = []
    run = jnp.zeros((1, gw), F32)
    for a in range(nsub):
        run = run + tots[a]
        ends.append(run)
    cum_l = ends[-1]
    cum = lc + per_sub([jnp.zeros((1, gw), F32)] + ends[:-1])

    qs = (q * jnp.exp(cum)).astype(BF16)
    koff = kk * jnp.exp(per_sub(tots) - lc)
    ke = (koff * per_sub([jnp.exp(cum_l - e) for e in ends])).astype(BF16)
    koff_b = koff.astype(BF16)
    qoff = []
    kmask = []
    for b in range(nsub - 1):
        r0 = (b + 1) * HG_SUB
        qb = (q[r0:, :] * jnp.exp(cum[r0:, :] - ends[b])).astype(BF16)
        qoff.append(jnp.concatenate([jnp.zeros((r0, gw), BF16), qb], axis=0))
        parts = [koff_b[b * HG_SUB:r0, :], jnp.zeros((chunk - r0, gw), BF16)]
        if b > 0:
            parts.insert(0, jnp.zeros((b * HG_SUB, gw), BF16))
        kmask.append(jnp.concatenate(parts, axis=0))
    vb = v.astype(BF16)
    decay_l = jnp.exp(cum_l)

    for h in range(G_HEADS):
        sl = slice(h * HEAD_DIM, (h + 1) * HEAD_DIM)
        qcat = jnp.concatenate([qo[:, sl] for qo in qoff], axis=1)
        kcat = jnp.concatenate([km[:, sl] for km in kmask], axis=1)
        a = _hgrn_diag_scores(q, kk, lc, h) + _dot_nt(qcat, kcat).astype(BF16)
        st = st_ref[h]
        oa_ref[:, sl] = jnp.dot(a, vb[:, sl], preferred_element_type=F32) + _dot_nt(qs[:, sl], st.astype(BF16))
        upd = jnp.dot(v[:, sl].T.astype(BF16), ke[:, sl], preferred_element_type=F32)
        st_ref[h] = st * decay_l[:, sl] + upd

    @pl.when(need_ref[0] != 0)
    def _():
        q2, kk2, v2, lc2, rsub2 = _hgrn_gates(hq_ref, hf_ref, hi_ref, lb_ref)

        @pl.when(jnp.max(-lc2) > EXP_CLAMP)
        def _():
            v2b = v2.astype(BF16)
            fast = [jnp.dot(_hgrn_diag_scores(q2, kk2, lc2, h), v2b[:, h * HEAD_DIM:(h + 1) * HEAD_DIM],
                            preferred_element_type=F32) for h in range(G_HEADS)]
            oa_ref[...] += _hgrn_diag_exact(q2, kk2, v2, lc2, rsub2, nsub) - jnp.concatenate(fast, axis=1)

    o = oa_ref[...]
    ms = jnp.mean(o * o, axis=-1, keepdims=True)
    hg = hg_ref[...].astype(F32)
    y = o * lax.rsqrt(ms + EPS) * gn_ref[...] * (hg * _sigmoid(hg))
    o_ref[...] = y.astype(o_ref.dtype)


def _hgrn(ph, pt, lb, g_hnorm, batch, seq, chunk):
    n = batch * seq
    gw = G_HEADS * HEAD_DIM
    rows = HG_CHUNKS_PER_STEP * chunk
    ncb = seq // rows
    blk = lambda c: pl.BlockSpec((rows, gw), lambda b, i, nd: (b * ncb + i, c))
    need = (HG_SUB * jnp.log(jnp.min(lb)) < -EXP_CLAMP).astype(I32).reshape(1)
    grid_spec = pltpu.PrefetchScalarGridSpec(
        num_scalar_prefetch=1,
        grid=(batch, ncb),
        in_specs=[blk(0), blk(1), blk(1), blk(2),
                  pl.BlockSpec((1, gw), lambda b, i, nd: (0, 0)),
                  pl.BlockSpec((1, gw), lambda b, i, nd: (0, 0))],
        out_specs=pl.BlockSpec((rows, gw), lambda b, i, nd: (b * ncb + i, 0)),
        scratch_shapes=[pltpu.VMEM((G_HEADS, HEAD_DIM, HEAD_DIM), F32),
                        pltpu.VMEM((chunk, gw), F32)],
    )
    return pl.pallas_call(
        functools.partial(_hgrn_kernel, chunk=chunk),
        grid_spec=grid_spec,
        out_shape=jax.ShapeDtypeStruct((n, gw), BF16),
        compiler_params=_cparams(("parallel", "arbitrary"), 32),
        name="hgrn",
    )(need, pt, ph, pt, pt, lb.reshape(1, gw), g_hnorm.reshape(1, gw))


def _merge_kernel(ya_ref, yr_ref, ga_ref, gb_ref, x_ref, wpa_ref, wpb_ref, wo_ref, o_ref, mg_ref):
    c = pl.program_id(1)
    nc = pl.num_programs(1)
    tc = ga_ref.shape[1]

    ma = jnp.dot(ya_ref[...], wpa_ref[...], preferred_element_type=F32)
    mb = jnp.dot(yr_ref[...], wpb_ref[...], preferred_element_type=F32)
    merged = _sigmoid(ga_ref[...].astype(F32)) * ma + _sigmoid(gb_ref[...].astype(F32)) * mb
    merged = merged.astype(BF16)
    for k in range(mg_ref.shape[1] // tc):
        @pl.when(c == k)
        def _():
            mg_ref[:, k * tc:(k + 1) * tc] = merged

    @pl.when(c == nc - 1)
    def _():
        o_ref[...] = x_ref[...] + jnp.dot(mg_ref[...], wo_ref[...], preferred_element_type=F32)


def _merge(y_att, y_rec, pt, x2, w_pa, w_pb, w_o, tm, tc):
    n, d = x2.shape
    aw = y_att.shape[1]
    gw = y_rec.shape[1]
    ga0 = 3072 // tc
    gb0 = 5120 // tc
    return pl.pallas_call(
        _merge_kernel,
        grid=(n // tm, d // tc),
        in_specs=[
            pl.BlockSpec((tm, aw), lambda i, c: (i, 0)),
            pl.BlockSpec((tm, gw), lambda i, c: (i, 0)),
            pl.BlockSpec((tm, tc), lambda i, c: (i, ga0 + c)),
            pl.BlockSpec((tm, tc), lambda i, c: (i, gb0 + c)),
            pl.BlockSpec((tm, d), lambda i, c: (i, 0)),
            pl.BlockSpec((aw, tc), lambda i, c: (0, c)),
            pl.BlockSpec((gw, tc), lambda i, c: (0, c)),
            pl.BlockSpec((d, d), lambda i, c: (0, 0)),
        ],
        out_specs=pl.BlockSpec((tm, d), lambda i, c: (i, 0)),
        out_shape=jax.ShapeDtypeStruct((n, d), F32),
        scratch_shapes=[pltpu.VMEM((tm, d), BF16)],
        compiler_params=_cparams(("parallel", "arbitrary"), 56),
        name="merge",
    )(y_att, y_rec, pt, pt, x2, w_pa, w_pb, w_o)


def _router_kernel(x_ref, g_ref, wr_ref, br_ref, h_ref, r_ref, cnt_ref):
    x = x_ref[...]
    ms = jnp.mean(x * x, axis=-1, keepdims=True)
    h = x * lax.rsqrt(ms + EPS) * g_ref[...]
    h_ref[...] = _pack_halves(h)
    lg = jnp.dot(h.astype(BF16), wr_ref[...], preferred_element_type=F32) + br_ref[...]
    tm = lg.shape[0]
    lane = lax.broadcasted_iota(I32, (tm, LANES), 1)
    neg = jnp.float32(-1e30)
    big = jnp.float32(LANES)
    is_g = lane < N_GROUPS
    lgg = jnp.where(is_g, lg, neg)
    mg = jnp.max(lgg, axis=1, keepdims=True)
    zg = jnp.sum(jnp.where(is_g, jnp.exp(lgg - mg), 0.0), axis=1, keepdims=True)
    p_g = 1.0 / zg
    grp = jnp.min(jnp.where(is_g & (lgg == mg), lane.astype(F32), big), axis=1, keepdims=True)
    e_id = lane - N_GROUPS
    e_grp = lax.shift_right_arithmetic(e_id, jnp.int32(EXP_PER_GROUP.bit_length() - 1)).astype(F32)
    e_idf = e_id.astype(F32)
    is_e = (e_id >= 0) & (e_id < N_EXPERTS) & (e_grp == grp)
    lge = jnp.where(is_e, lg, neg)
    m1 = jnp.max(lge, axis=1, keepdims=True)
    i1 = jnp.min(jnp.where(is_e & (lge == m1), e_idf, big), axis=1, keepdims=True)
    is_e2 = is_e & (e_idf != i1)
    lge2 = jnp.where(is_e2, lg, neg)
    m2 = jnp.max(lge2, axis=1, keepdims=True)
    i2 = jnp.min(jnp.where(is_e2 & (lge2 == m2), e_idf, big), axis=1, keepdims=True)
    ex2 = jnp.exp(m2 - m1)
    den = 1.0 + ex2
    g1 = p_g / den
    g2 = p_g * ex2 / den
    @pl.when(pl.program_id(0) == 0)
    def _():
        cnt_ref[...] = jnp.zeros(cnt_ref.shape, F32)

    lanef = lane.astype(F32)
    oh1 = lanef == i1
    oh2 = lanef == i2
    oh = jnp.where(oh1 | oh2, 1.0, 0.0)
    ti = lax.broadcasted_iota(I32, (tm, tm), 0)
    si = lax.broadcasted_iota(I32, (tm, tm), 1)
    tri = jnp.where(si < ti, 1.0, 0.0).astype(BF16)
    before = jnp.dot(tri, oh.astype(BF16), preferred_element_type=F32) + cnt_ref[...]
    rank1 = jnp.sum(jnp.where(oh1, before, 0.0), axis=1, keepdims=True)
    rank2 = jnp.sum(jnp.where(oh2, before, 0.0), axis=1, keepdims=True)
    cnt_ref[...] += jnp.sum(oh, axis=0, keepdims=True)

    r = jnp.where(lane == 0, i1, 0.0)
    r = jnp.where(lane == 1, i2, r)
    r = jnp.where(lane == 2, g1, r)
    r = jnp.where(lane == 3, g2, r)
    r = jnp.where(lane == 4, rank1, r)
    r = jnp.where(lane == 5, rank2, r)
    r_ref[...] = r


def _router(x1, g, w_r, b_r, tm):
    n, d = x1.shape
    return pl.pallas_call(
        _router_kernel,
        grid=(n // tm,),
        in_specs=[
            pl.BlockSpec((tm, d), lambda i: (i, 0)),
            pl.BlockSpec((1, d), lambda i: (0, 0)),
            pl.BlockSpec((d, LANES), lambda i: (0, 0)),
            pl.BlockSpec((1, LANES), lambda i: (0, 0)),
        ],
        out_specs=[pl.BlockSpec((tm, d // 2), lambda i: (i, 0)),
                   pl.BlockSpec((tm, LANES), lambda i: (i, 0)),
                   pl.BlockSpec((1, LANES), lambda i: (0, 0))],
        out_shape=[jax.ShapeDtypeStruct((n, d // 2), U32),
                   jax.ShapeDtypeStruct((n, LANES), F32),
                   jax.ShapeDtypeStruct((1, LANES), F32)],
        compiler_params=_cparams(("arbitrary",), 40),
        name="router",
    )(x1, g.reshape(1, d), w_r, b_r)


def _dispatch_kernel(prow_ref, h_ref, xs_hbm, sem, *, tm):
    base = 2 * tm * pl.program_id(0)

    def scatter_group(g, carry):
        r0 = pl.multiple_of(g * DMA_GROUP, DMA_GROUP)
        rows = h_ref.at[pl.ds(r0, DMA_GROUP), :]
        for u in range(DMA_GROUP):
            for s in range(2):
                dst_row = prow_ref[base + 2 * r0 + (2 * u + s)]
                pltpu.make_async_copy(rows.at[pl.ds(u, 1), :], xs_hbm.at[pl.ds(dst_row, 1), :],
                                      sem).start(priority=s)
        return carry

    lax.fori_loop(0, tm // DMA_GROUP, scatter_group, 0)
    for _ in range(2):
        pltpu.make_async_copy(h_ref, xs_hbm.at[pl.ds(0, tm), :], sem).wait()


def _dispatch(h2, prow, tm):
    n, d = h2.shape
    grid_spec = pltpu.PrefetchScalarGridSpec(
        num_scalar_prefetch=1,
        grid=(n // tm,),
        in_specs=[pl.BlockSpec((tm, d), lambda i, pr: (i, 0))],
        out_specs=pl.BlockSpec(memory_space=pl.ANY),
        scratch_shapes=[pltpu.SemaphoreType.DMA(())],
    )
    return pl.pallas_call(
        functools.partial(_dispatch_kernel, tm=tm),
        grid_spec=grid_spec,
        out_shape=jax.ShapeDtypeStruct((2 * n, d), h2.dtype),
        compiler_params=_cparams(("arbitrary",), 32),
        name="dispatch",
    )(prow, h2)


def _moe_kernel(pt_ref, pe_ref, plo_ref, phi_ref, slot_ref, nxt_ref, np_ref, x_ref, wg_hbm, wu_hbm, wd_hbm,
                y_ref, wgf_ref, wuf_ref, wdf_ref, wgb_ref, wub_ref, wdb_ref, yacc_ref, sem):
    p = pl.program_id(0)
    active = p < np_ref[0]
    prev = jnp.maximum(p - 1, 0)
    new_expert = (p == 0) | (pe_ref[p] != pe_ref[prev])
    new_tile = (p == 0) | (pt_ref[p] != pt_ref[prev])
    slot = slot_ref[p]

    def weight_copies(e, s):
        return (pltpu.make_async_copy(wg_hbm.at[e], wgf_ref.at[s], sem.at[s]),
                pltpu.make_async_copy(wu_hbm.at[e], wuf_ref.at[s], sem.at[s]),
                pltpu.make_async_copy(wd_hbm.at[e], wdf_ref.at[s], sem.at[s]))

    @pl.when(p == 0)
    def _():
        for cp in weight_copies(pe_ref[0], 0):
            cp.start()

    @pl.when(active & new_expert)
    def _():
        for cp in weight_copies(pe_ref[p], slot):
            cp.wait()
        wgb_ref[...] = wgf_ref[slot].astype(BF16)
        wub_ref[...] = wuf_ref[slot].astype(BF16)
        wdb_ref[...] = wdf_ref[slot].astype(BF16)

        @pl.when(nxt_ref[p] >= 0)
        def _():
            for cp in weight_copies(nxt_ref[p], 1 - slot):
                cp.start()

    @pl.when(active)
    def _():
        tm, half = x_ref.shape
        lo, hi = _unpack_halves(x_ref[...])
        xl = lo.astype(BF16)
        xh = hi.astype(BF16)
        a = (jnp.dot(xl, wgb_ref[0:half, :], preferred_element_type=F32)
             + jnp.dot(xh, wgb_ref[half:, :], preferred_element_type=F32))
        u = (jnp.dot(xl, wub_ref[0:half, :], preferred_element_type=F32)
             + jnp.dot(xh, wub_ref[half:, :], preferred_element_type=F32))
        hmid = (a * _sigmoid(a) * u).astype(BF16)
        yv = jnp.dot(hmid, wdb_ref[...], preferred_element_type=F32)
        row = lax.broadcasted_iota(I32, (tm, 1), 0)
        mine = (row >= plo_ref[p]) & (row < phi_ref[p])
        yv = jnp.where(mine, yv, 0.0)

        @pl.when(new_tile)
        def _():
            yacc_ref[...] = yv

        @pl.when(jnp.logical_not(new_tile))
        def _():
            yacc_ref[...] += yv

        nxt_p = jnp.minimum(p + 1, pl.num_programs(0) - 1)
        last_of_tile = (p == np_ref[0] - 1) | (pt_ref[nxt_p] != pt_ref[p])

        @pl.when(last_of_tile)
        def _():
            y_ref[...] = _pack_halves(yacc_ref[...])


def _moe(xs, pairs, w_gate, w_up, w_down, tm):
    pair_tile, pair_expert, pair_lo, pair_hi, pair_slot, pair_next, n_pairs = pairs
    na, half = xs.shape
    d, de = w_gate.shape[1:]
    max_pairs = pair_tile.shape[0]
    grid_spec = pltpu.PrefetchScalarGridSpec(
        num_scalar_prefetch=7,
        grid=(max_pairs,),
        in_specs=[
            pl.BlockSpec((tm, half), lambda p, pt, *_: (pt[p], 0)),
            pl.BlockSpec(memory_space=pl.ANY),
            pl.BlockSpec(memory_space=pl.ANY),
            pl.BlockSpec(memory_space=pl.ANY),
        ],
        out_specs=pl.BlockSpec((tm, half), lambda p, pt, *_: (pt[p], 0)),
        scratch_shapes=[
            pltpu.VMEM((2, d, de), F32),
            pltpu.VMEM((2, d, de), F32),
            pltpu.VMEM((2, de, d), F32),
            pltpu.VMEM((d, de), BF16),
            pltpu.VMEM((d, de), BF16),
            pltpu.VMEM((de, d), BF16),
            pltpu.VMEM((tm, d), F32),
            pltpu.SemaphoreType.DMA((2,)),
        ],
    )
    return pl.pallas_call(
        _moe_kernel,
        grid_spec=grid_spec,
        out_shape=jax.ShapeDtypeStruct((na, half), U32),
        compiler_params=_cparams(("arbitrary",), 56),
        name="moe",
    )(pair_tile, pair_expert, pair_lo, pair_hi, pair_slot, pair_next, n_pairs, xs, w_gate, w_up, w_down)


def _moe_rows(r, cnt):
    sizes = cnt[0, :N_EXPERTS].astype(I32)
    starts = jnp.cumsum(sizes) - sizes
    e12 = r[:, 0:2].astype(I32)
    rank = r[:, 4:6].astype(I32)
    onehot = e12[:, :, None] == jnp.arange(N_EXPERTS, dtype=I32)[None, None, :]
    start_a = jnp.sum(jnp.where(onehot, starts[None, None, :], 0), axis=-1)
    return (start_a + rank).reshape(-1).astype(I32), sizes


def _count_le(sorted_vals, x):
    return jnp.sum((sorted_vals[None, :] <= x[:, None]).astype(I32), axis=1)


def _moe_pairs(sizes, na, tm):
    ntiles = na // tm
    max_pairs = ntiles + N_EXPERTS - 1
    ends = jnp.cumsum(sizes)
    starts = ends - sizes
    t0 = jnp.arange(ntiles, dtype=I32) * tm
    e_lo = _count_le(ends, t0)
    e_hi = _count_le(ends, t0 + (tm - 1))
    per_tile = e_hi - e_lo + 1
    pend = jnp.cumsum(per_tile)
    pstart = pend - per_tile
    n_pairs = pend[-1]
    pc = jnp.minimum(jnp.arange(max_pairs, dtype=I32), n_pairs - 1)
    tile = _count_le(pend, pc)
    expert = e_lo[tile] + (pc - pstart[tile])
    lo = jnp.clip(starts[expert] - tile * tm, 0, tm).astype(I32)
    hi = jnp.clip(ends[expert] - tile * tm, 0, tm).astype(I32)
    expert = expert.astype(I32)
    first = jnp.concatenate([jnp.ones((1,), bool), expert[1:] != expert[:-1]])
    slot = ((jnp.cumsum(first) - 1) % 2).astype(I32)
    idx = jnp.arange(max_pairs, dtype=I32)
    first_at = jnp.where(first, idx, max_pairs)
    nxt_first = jnp.concatenate([lax.cummin(first_at[::-1])[::-1][1:], jnp.full((1,), max_pairs, I32)])
    nxt = jnp.where(nxt_first < max_pairs, expert[jnp.minimum(nxt_first, max_pairs - 1)], -1).astype(I32)
    return tile, expert, lo, hi, slot, nxt, n_pairs.astype(I32).reshape(1)


def _final_kernel(prow_ref, x_ref, r_ref, g_ref, y_hbm, o_ref, ybuf, sem, *, tm):
    i = pl.program_id(0)
    cur = i % 2

    def issue(step, buf):
        base = 2 * tm * step

        def gather_group(g, carry):
            r0 = pl.multiple_of(g * DMA_GROUP, DMA_GROUP)
            for s in range(2):
                rows = ybuf.at[buf, s, pl.ds(r0, DMA_GROUP), :]
                for u in range(DMA_GROUP):
                    src_row = prow_ref[base + 2 * r0 + (2 * u + s)]
                    pltpu.make_async_copy(y_hbm.at[pl.ds(src_row, 1), :], rows.at[pl.ds(u, 1), :],
                                          sem.at[buf]).start(priority=u % 2)
            return carry

        lax.fori_loop(0, tm // DMA_GROUP, gather_group, 0)

    @pl.when(i == 0)
    def _():
        issue(0, 0)

    @pl.when(i + 1 < pl.num_programs(0))
    def _():
        issue(i + 1, 1 - cur)

    for s in range(2):
        pltpu.make_async_copy(y_hbm.at[pl.ds(0, tm), :], ybuf.at[cur, s], sem.at[cur]).wait()
    r = r_ref[...]
    g1 = r[:, 2:3]
    g2 = r[:, 3:4]
    half = ybuf.shape[-1]
    lo0, hi0 = _unpack_halves(ybuf[cur, 0])
    lo1, hi1 = _unpack_halves(ybuf[cur, 1])
    xa = x_ref[:, :half] + g1 * lo0 + g2 * lo1
    xb = x_ref[:, half:] + g1 * hi0 + g2 * hi1
    ms = (jnp.sum(xa * xa, axis=-1, keepdims=True) + jnp.sum(xb * xb, axis=-1, keepdims=True)) / (2 * half)
    rs = lax.rsqrt(ms + EPS)
    o_ref[:, :half] = xa * rs * g_ref[:, :half]
    o_ref[:, half:] = xb * rs * g_ref[:, half:]


def _final(x1, y, r, prow, g, tm):
    n, d = x1.shape
    grid_spec = pltpu.PrefetchScalarGridSpec(
        num_scalar_prefetch=1,
        grid=(n // tm,),
        in_specs=[
            pl.BlockSpec((tm, d), lambda i, pr: (i, 0)),
            pl.BlockSpec((tm, LANES), lambda i, pr: (i, 0)),
            pl.BlockSpec((1, d), lambda i, pr: (0, 0)),
            pl.BlockSpec(memory_space=pl.ANY),
        ],
        out_specs=pl.BlockSpec((tm, d), lambda i, pr: (i, 0)),
        scratch_shapes=[pltpu.VMEM((2, 2, tm, d // 2), U32), pltpu.SemaphoreType.DMA((2,))],
    )
    return pl.pallas_call(
        functools.partial(_final_kernel, tm=tm),
        grid_spec=grid_spec,
        out_shape=jax.ShapeDtypeStruct((n, d), F32),
        compiler_params=_cparams(("arbitrary",), 48),
        name="final",
    )(prow, x1, r, g.reshape(1, d), y)


def _pack_w_in(w):
    d, din = w.shape
    tc = 256
    wt = jnp.swapaxes(w, 0, 1)

    def pack_kernel(w_ref, o_ref):
        o_ref[0:912, :] = w_ref[0:912, :].astype(BF16)
        o_ref[912:1024, :] = jnp.zeros((LANES - IDX_HEADS, tc), BF16)
        o_ref[1024:2048, :] = w_ref[1936:2960, :].astype(BF16)
        o_ref[2048:3072, :] = w_ref[912:1936, :].astype(BF16)
        o_ref[3072:, :] = w_ref[2960:, :].astype(BF16)

    return pl.pallas_call(
        pack_kernel,
        grid=(d // tc,),
        in_specs=[pl.BlockSpec((din, tc), lambda i: (0, i))],
        out_specs=pl.BlockSpec((PH_COLS + PT_COLS, tc), lambda i: (0, i)),
        out_shape=jax.ShapeDtypeStruct((PH_COLS + PT_COLS, d), BF16),
        compiler_params=_cparams(("parallel",), 48),
        name="pack_w_in",
    )(wt)


def _rope_tables(seq):
    half = HEAD_DIM // 2
    inv = 1.0 / (ROPE_THETA ** (jnp.arange(0, HEAD_DIM, 2, dtype=F32) / HEAD_DIM))
    ang = jnp.arange(seq, dtype=F32)[:, None] * inv[None, :]
    c, s = jnp.cos(ang), jnp.sin(ang)
    del half
    c2, s2 = jnp.concatenate([c, c], axis=1), jnp.concatenate([-s, s], axis=1)
    return c2, s2, c2.T, s2.T


def _layer(x2, batch, seq, rope, lb, g_norm1, w_in, g_cq, w_uq, w_qidx, g_hnorm, w_pa, w_pb, w_o,
           g_norm2, w_grp, b_grp, w_exp, b_exp, w_gate, w_up, w_down, g_out):
    n, d = x2.shape
    ph, pt = _in_proj(x2, g_norm1, _pack_w_in(w_in), tm=min(1024, n), tn=1024)
    tq, tk = min(256, seq), min(256, seq)
    qt, qit, wt, k, ki, vt = _dsa_prep(ph, rope, g_cq, w_uq.T.astype(BF16), w_qidx.T.astype(BF16), seq, tm=tk, tq=tq)
    y_att = _dsa(qit, qt, wt, ki, k, vt, batch, seq, tq=tq, tk=tk)
    y_rec = _hgrn(ph, pt, lb, g_hnorm, batch, seq, chunk=64)
    x1 = _merge(y_att, y_rec, pt, x2, w_pa.astype(BF16), w_pb.astype(BF16), w_o.astype(BF16),
                tm=min(512, n), tc=1024)
    w_r = jnp.concatenate([w_grp, w_exp, jnp.zeros((d, LANES - N_GROUPS - N_EXPERTS), F32)], axis=1).astype(BF16)
    b_r = jnp.concatenate([b_grp, b_exp, jnp.zeros((LANES - N_GROUPS - N_EXPERTS,), F32)]).reshape(1, LANES)
    h2, r, cnt = _router(x1, g_norm2, w_r, b_r, tm=min(512, n))
    prow, sizes = _moe_rows(r, cnt)
    xs = _dispatch(h2, prow, tm=min(512, n))
    tm_moe = 256
    assert (2 * n) % tm_moe == 0
    y = _moe(xs, _moe_pairs(sizes, 2 * n, tm_moe), w_gate, w_up, w_down, tm_moe)
    return _final(x1, y, r, prow, g_out, tm=min(512, n))


def kernel(x, g_norm1, w_in, g_cq, w_uq, w_qidx, lb_logits, g_hnorm, w_pa, w_pb, w_o, g_norm2, w_grp, b_grp,
           w_exp, b_exp, w_gate, w_up, w_down, g_final):
    batch, seq, d = x.shape
    depth = g_norm1.shape[0]
    rope = _rope_tables(seq)
    lb_all = jnp.cumsum(jax.nn.softmax(lb_logits.astype(F32), axis=0), axis=0)
    x2 = x.reshape(batch * seq, d)
    for l in range(depth):
        assert depth == 1
        x2 = _layer(x2, batch, seq, rope, lb_all[l], g_norm1[l], w_in[l], g_cq[l], w_uq[l], w_qidx[l],
                    g_hnorm[l], w_pa[l], w_pb[l], w_o[l], g_norm2[l], w_grp[l], b_grp[l], w_exp[l], b_exp[l],
                    w_gate[l], w_up[l], w_down[l], g_final)
    return x2.reshape(batch, seq, d)
```

```python
import functools
import math
from typing import NamedTuple

import jax
import jax.numpy as jnp
from jax import lax
from jax.experimental import pallas as pl
from jax.experimental.pallas import tpu as pltpu

F32 = jnp.float32
BF16 = jnp.bfloat16
I32 = jnp.int32
U32 = jnp.uint32

EPS = 1e-6
ROPE_THETA = 10000.0
LANES = 128
MXU_LANES = 256
DMA_GROUP = 32
VT_ROWS = 128 + 16

A_HEADS = 8
HEAD_DIM = 128
IDX_HEADS = 16
Q_RANK = 512
TOPK_MAX = 256
G_HEADS = 8
N_GROUPS = 4
EXP_PER_GROUP = 8
N_EXPERTS = N_GROUPS * EXP_PER_GROUP

INT_MIN = -2147483648
EXP_CLAMP = 80.0


def _cparams(sem, vmem_mb):
    return pltpu.CompilerParams(dimension_semantics=sem, vmem_limit_bytes=vmem_mb << 20)


def _pack_halves(x):
    c = x.shape[1] // 2
    bits = lax.bitcast_convert_type(x.astype(BF16).astype(F32), U32)
    return lax.shift_right_logical(bits[:, :c], jnp.uint32(16)) | (bits[:, c:] & jnp.uint32(0xFFFF0000))


def _unpack_halves(w):
    lo = lax.bitcast_convert_type(lax.shift_left(w, jnp.uint32(16)), F32)
    hi = lax.bitcast_convert_type(w & jnp.uint32(0xFFFF0000), F32)
    return lo, hi


def _sigmoid(x):
    return 0.5 * jnp.tanh(0.5 * x) + 0.5


PH_COLS = 2048
PT_COLS = 7168


def _inproj_kernel(x_ref, g_ref, w_ref, oh_ref, ot_ref, xn_ref, *, nh):
    j = pl.program_id(1)

    @pl.when(j == 0)
    def _():
        x = x_ref[...]
        ms = jnp.mean(x * x, axis=-1, keepdims=True)
        xn_ref[...] = (x * lax.rsqrt(ms + EPS) * g_ref[...]).astype(BF16)

    y = _dot_nt(xn_ref[...], w_ref[...])

    @pl.when(j < nh)
    def _():
        oh_ref[...] = y

    @pl.when(j >= nh)
    def _():
        ot_ref[...] = y.astype(ot_ref.dtype)


def _in_proj(x2, g, w, tm, tn):
    n, d = x2.shape
    nh = PH_COLS // tn
    nt = PT_COLS // tn
    return pl.pallas_call(
        functools.partial(_inproj_kernel, nh=nh),
        grid=(n // tm, nh + nt),
        in_specs=[
            pl.BlockSpec((tm, d), lambda i, j: (i, 0)),
            pl.BlockSpec((1, d), lambda i, j: (0, 0)),
            pl.BlockSpec((tn, d), lambda i, j: (j, 0)),
        ],
        out_specs=[pl.BlockSpec((tm, tn), lambda i, j: (i, jnp.minimum(j, nh - 1))),
                   pl.BlockSpec((tm, tn), lambda i, j: (i, jnp.maximum(j - nh, 0)))],
        out_shape=[jax.ShapeDtypeStruct((n, PH_COLS), F32),
                   jax.ShapeDtypeStruct((n, PT_COLS), BF16)],
        scratch_shapes=[pltpu.VMEM((tm, d), BF16)],
        compiler_params=_cparams(("parallel", "arbitrary"), 56),
        name="in_proj",
    )(x2, g.reshape(1, d), w)


def _rope(x, c2, s2):
    return x * c2 + pltpu.roll(x, HEAD_DIM // 2, 1) * s2


def _rope_t(x, c2t, s2t):
    half = HEAD_DIM // 2
    return x * c2t + jnp.concatenate([x[half:, :], x[:half, :]], axis=0) * s2t


def _prep_kernel(cq_ref, ka_ref, va_ref, kx_ref, wi_ref, cos_ref, sin_ref, cost_ref, sint_ref, g_ref, wuq_ref, wqi_ref,
                 qt_ref, qit_ref, wt_ref, k_ref, ki_ref, vt_ref, *, scale, wscale, tq):
    nqb = cq_ref.shape[0] // tq
    c2 = cos_ref[...]
    s2 = sin_ref[...]
    c2t = cost_ref[...]
    s2t = sint_ref[...]
    cq = cq_ref[...]
    ms = jnp.mean(cq * cq, axis=-1, keepdims=True)
    cqn = (cq * lax.rsqrt(ms + EPS) * g_ref[...]).astype(BF16)
    q_t = _dot_nt(wuq_ref[...], cqn)
    for h in range(A_HEADS):
        qh_t = _rope_t(q_t[h * HEAD_DIM:(h + 1) * HEAD_DIM, :], c2t, s2t) * scale
        for b in range(nqb):
            qt_ref[b, :, h * tq:(h + 1) * tq] = qh_t[:, b * tq:(b + 1) * tq].astype(BF16)
    qi_t = _dot_nt(wqi_ref[...], cqn)
    for h in range(IDX_HEADS):
        qh_t = _rope_t(qi_t[h * HEAD_DIM:(h + 1) * HEAD_DIM, :], c2t, s2t)
        for b in range(nqb):
            qit_ref[b, :, h * tq:(h + 1) * tq] = qh_t[:, b * tq:(b + 1) * tq].astype(BF16)
    w_t = (wi_ref[...] * wscale).T
    for h in range(IDX_HEADS):
        for b in range(nqb):
            wt_ref[b, :, h * tq:(h + 1) * tq] = w_t[h:h + 1, b * tq:(b + 1) * tq]
    k_ref[...] = _rope(ka_ref[...], c2, s2).astype(BF16)
    ki_ref[...] = _rope(kx_ref[...], c2, s2).astype(BF16)
    vt_ref[0, 0:HEAD_DIM, :] = va_ref[...].T.astype(BF16)
    ones_row = lax.broadcasted_iota(I32, (VT_ROWS - HEAD_DIM, va_ref.shape[0]), 0) == 0
    vt_ref[0, HEAD_DIM:VT_ROWS, :] = jnp.where(ones_row, 1.0, 0.0).astype(BF16)


def _dsa_prep(p, rope, g_cq, w_uq_t, w_qidx_t, seq, tm, tq):
    n = p.shape[0]
    nsb = seq // tm
    nqb = tm // tq
    aw = A_HEADS * HEAD_DIM
    iw = IDX_HEADS * HEAD_DIM
    row = lambda i: (i, 0)
    blk3 = lambda i: (i, 0, 0)
    kern = functools.partial(_prep_kernel, scale=HEAD_DIM ** -0.5 * math.log2(math.e),
                             wscale=IDX_HEADS ** -0.5 * HEAD_DIM ** -0.5, tq=tq)
    return pl.pallas_call(
        kern,
        grid=(n // tm,),
        in_specs=[
            pl.BlockSpec((tm, Q_RANK), lambda i: (i, 0)),
            pl.BlockSpec((tm, LANES), lambda i: (i, 4)),
            pl.BlockSpec((tm, LANES), lambda i: (i, 5)),
            pl.BlockSpec((tm, LANES), lambda i: (i, 6)),
            pl.BlockSpec((tm, LANES), lambda i: (i, 7)),
            pl.BlockSpec((tm, LANES), lambda i: (i % nsb, 0)),
            pl.BlockSpec((tm, LANES), lambda i: (i % nsb, 0)),
            pl.BlockSpec((HEAD_DIM, tm), lambda i: (0, i % nsb)),
            pl.BlockSpec((HEAD_DIM, tm), lambda i: (0, i % nsb)),
            pl.BlockSpec((1, Q_RANK), lambda i: (0, 0)),
            pl.BlockSpec((aw, Q_RANK), lambda i: (0, 0)),
            pl.BlockSpec((iw, Q_RANK), lambda i: (0, 0)),
        ],
        out_specs=[
            pl.BlockSpec((nqb, HEAD_DIM, A_HEADS * tq), blk3),
            pl.BlockSpec((nqb, HEAD_DIM, IDX_HEADS * tq), blk3),
            pl.BlockSpec((nqb, 1, IDX_HEADS * tq), blk3),
            pl.BlockSpec((tm, LANES), row),
            pl.BlockSpec((tm, LANES), row),
            pl.BlockSpec((1, VT_ROWS, tm), blk3),
        ],
        out_shape=[
            jax.ShapeDtypeStruct((n // tq, HEAD_DIM, A_HEADS * tq), BF16),
            jax.ShapeDtypeStruct((n // tq, HEAD_DIM, IDX_HEADS * tq), BF16),
            jax.ShapeDtypeStruct((n // tq, 1, IDX_HEADS * tq), F32),
            jax.ShapeDtypeStruct((n, LANES), BF16),
            jax.ShapeDtypeStruct((n, LANES), BF16),
            jax.ShapeDtypeStruct((n // tm, VT_ROWS, tm), BF16),
        ],
        compiler_params=_cparams(("parallel",), 40),
        name="dsa_prep",
    )(p, p, p, p, p, *rope, g_cq.reshape(1, Q_RANK), w_uq_t, w_qidx_t)


def _dot_nt(a, b):
    return lax.dot_general(a, b, (((1,), (1,)), ((), ())), preferred_element_type=F32)


def _dsa_kernel(qit_ref, qt_ref, wt_ref, ki_ref, k_ref, vt_ref, o_ref,
                key_ref, sa_ref, sb_ref, pa_ref, pb_ref, acc_ref, *, tq, tk, topk, idx_bits):
    i = pl.program_id(1)
    nkc = ((i + 1) * tq + tk - 1) // tk
    q_pos = i * tq + lax.broadcasted_iota(I32, (tk, tq), 1)
    k_off = lax.broadcasted_iota(I32, (tk, tq), 0)

    def score_chunks(j0, nch):
        kic = ki_ref[pl.ds(pl.multiple_of(j0 * tk, tk), nch * tk), :]
        acc = jnp.zeros((nch * tk, tq), F32)
        hpm = max(1, MXU_LANES // tq)
        for h0 in range(0, IDX_HEADS, hpm):
            sl = slice(h0 * tq, (h0 + hpm) * tq)
            lg = jnp.dot(kic, qit_ref[0, :, sl], preferred_element_type=F32)
            x = jnp.maximum(lg, 0.0) * wt_ref[0, :, sl]
            for u in range(hpm):
                acc = acc + x[:, u * tq:(u + 1) * tq]
        acc = jnp.where(acc == 0.0, 0.0, acc)
        bits = lax.bitcast_convert_type(acc, I32)
        key = jnp.where(bits < 0, bits ^ jnp.int32(0x7FFFFFFF), bits)
        for c in range(nch):
            causal = ((j0 + c) * tk + k_off) <= q_pos
            key_ref[j0 + c] = jnp.where(causal, key[c * tk:(c + 1) * tk, :], jnp.int32(INT_MIN))

    lax.fori_loop(0, nkc // 2, lambda t, c: (score_chunks(2 * t, 2), c)[1], 0)

    @pl.when(nkc % 2 == 1)
    def _():
        score_chunks(nkc - 1, 1)

    def count_rows(pred):
        def cnt_chunk(j, part):
            hit = pred(j, key_ref[j]).astype(I32)
            return part + jnp.sum(hit.reshape(tk // 8, 8, tq), axis=0)

        part = lax.fori_loop(0, nkc, cnt_chunk, jnp.zeros((8, tq), I32))
        return jnp.sum(part, axis=0, keepdims=True)

    def bit_step(b, c):
        t_u, cnt_t = c
        bit = jnp.left_shift(jnp.int32(1), 31 - b)
        cand_s = (t_u | bit) ^ jnp.int32(INT_MIN)
        cnt = count_rows(lambda j, key: key >= cand_s)
        take = cnt >= topk
        return jnp.where(take, t_u | bit, t_u), jnp.where(take, cnt, cnt_t)

    n_causal = i * tq + lax.broadcasted_iota(I32, (1, tq), 1) + 1
    t_u, cnt_t = lax.fori_loop(0, 32, bit_step, (jnp.zeros((1, tq), I32), n_causal))
    thr = jnp.maximum(t_u ^ jnp.int32(INT_MIN), jnp.int32(INT_MIN + 1))

    @pl.when(jnp.max(cnt_t) > topk)
    def _():
        need = topk - count_rows(lambda j, key: key > thr)

        def idx_step(b, cut):
            cand = cut | jnp.left_shift(jnp.int32(1), idx_bits - 1 - b)
            cnt = count_rows(lambda j, key: (key == thr) & ((j * tk + k_off) < cand))
            return jnp.where(cnt <= need, cand, cut)

        cut = lax.fori_loop(0, idx_bits, idx_step, jnp.zeros((1, tq), I32))

        def demote(j, carry):
            key = key_ref[j]
            drop = (key == thr) & ((j * tk + k_off) >= cut)
            key_ref[j] = jnp.where(drop, key - 1, key)
            return carry

        lax.fori_loop(0, nkc, demote, 0)

    aw = A_HEADS * tq
    acc_ref[...] = jnp.zeros(acc_ref.shape, F32)

    def scores(j, s_ref):
        kc = k_ref[pl.ds(pl.multiple_of(j * tk, tk), tk), :]
        sel = key_ref[j] >= thr
        cmax = []
        for h in range(A_HEADS):
            sl = slice(h * tq, (h + 1) * tq)
            s_h = jnp.dot(kc, qt_ref[0, :, sl], preferred_element_type=F32)
            s_h = jnp.where(sel, s_h, -jnp.inf)
            s_ref[:, sl] = s_h
            cmax.append(jnp.max(s_h, axis=0, keepdims=True))
        return jnp.concatenate(cmax, axis=1)

    def absorb(j, s_ref, p_ref, cmax, m_old):
        m_new = jnp.maximum(m_old, cmax)
        alpha = jnp.exp2(m_old - m_new)
        for h in range(A_HEADS):
            sl = slice(h * tq, (h + 1) * tq)
            p_ref[:, sl] = jnp.exp2(s_ref[:, sl] - m_new[:, sl]).astype(BF16)
        acc_ref[...] = alpha * acc_ref[...] + jnp.dot(vt_ref[j], p_ref[...], preferred_element_type=F32)
        return m_new

    def pair(t, carry):
        m, cm_a = carry
        a = 2 * t
        cm_b = scores(a + 1, sb_ref)
        m = absorb(a, sa_ref, pa_ref, cm_a, m)
        cm_next = scores(jnp.minimum(a + 2, nkc - 1), sa_ref)
        m = absorb(a + 1, sb_ref, pb_ref, cm_b, m)
        return m, cm_next

    m_fin, cm_last = lax.fori_loop(0, nkc // 2, pair, (jnp.full((1, aw), -1e30, F32), scores(0, sa_ref)))

    @pl.when(nkc % 2 == 1)
    def _():
        absorb(nkc - 1, sa_ref, pa_ref, cm_last, m_fin)

    out_t = acc_ref[0:HEAD_DIM, :] / acc_ref[HEAD_DIM:HEAD_DIM + 1, :]
    for h in range(A_HEADS):
        o_ref[:, h * HEAD_DIM:(h + 1) * HEAD_DIM] = out_t[:, h * tq:(h + 1) * tq].T.astype(o_ref.dtype)


def _dsa(qit, qt, wt, ki, k, vt, batch, seq, tq, tk):
    n = batch * seq
    nqb = seq // tq
    nkb = seq // tk
    topk = min(TOPK_MAX, seq // 4)
    aw = A_HEADS * HEAD_DIM
    kern = functools.partial(_dsa_kernel, tq=tq, tk=tk, topk=topk, idx_bits=seq.bit_length())
    qblk = lambda b, i: (b * nqb + i, 0, 0)
    return pl.pallas_call(
        kern,
        grid=(batch, nqb),
        in_specs=[
            pl.BlockSpec((1, HEAD_DIM, IDX_HEADS * tq), qblk),
            pl.BlockSpec((1, HEAD_DIM, A_HEADS * tq), qblk),
            pl.BlockSpec((1, 1, IDX_HEADS * tq), qblk),
            pl.BlockSpec((seq, LANES), lambda b, i: (b, 0)),
            pl.BlockSpec((seq, LANES), lambda b, i: (b, 0)),
            pl.BlockSpec((nkb, VT_ROWS, tk), lambda b, i: (b, 0, 0)),
        ],
        out_specs=pl.BlockSpec((tq, aw), lambda b, i: (b * nqb + i, 0)),
        out_shape=jax.ShapeDtypeStruct((n, aw), BF16),
        scratch_shapes=[
            pltpu.VMEM((nkb, tk, tq), I32),
            pltpu.VMEM((tk, A_HEADS * tq), F32),
            pltpu.VMEM((tk, A_HEADS * tq), F32),
            pltpu.VMEM((tk, A_HEADS * tq), BF16),
            pltpu.VMEM((tk, A_HEADS * tq), BF16),
            pltpu.VMEM((VT_ROWS, A_HEADS * tq), F32),
        ],
        compiler_params=_cparams(("parallel", "arbitrary"), 48),
        name="dsa",
    )(qit, qt, wt, ki, k, vt)


HG_SUB = 16
HG_CHUNKS_PER_STEP = 4


def _hgrn_diag_exact(q, kk, v, lc, rsub, nsub):
    chunk, gw = q.shape

    def bcast(x, s):
        return jnp.concatenate([jnp.broadcast_to(x[a * HG_SUB + s:a * HG_SUB + s + 1, :], (HG_SUB, gw))
                                for a in range(nsub)], axis=0)

    acc = [jnp.zeros((chunk, HEAD_DIM), F32) for _ in range(G_HEADS)]
    for s in range(HG_SUB):
        term = q * bcast(kk, s) * jnp.exp(jnp.minimum(lc - bcast(lc, s), 0.0))
        term = jnp.where(rsub >= s, term, 0.0)
        vrow = bcast(v, s)
        for h in range(G_HEADS):
            sl = slice(h * HEAD_DIM, (h + 1) * HEAD_DIM)
            acc[h] = acc[h] + jnp.sum(term[:, sl], axis=1, keepdims=True) * vrow[:, sl]
    return jnp.concatenate(acc, axis=1)


def _hgrn_gates(hq_ref, hf_ref, hi_ref, lb_ref):
    chunk, gw = hq_ref.shape
    lb = lb_ref[...]
    f = lb + (1.0 - lb) * _sigmoid(hf_ref[...])
    kk = 1.0 - f
    hq = hq_ref[...].astype(F32)
    q = hq * _sigmoid(hq)
    v = hi_ref[...].astype(F32)
    rsub = lax.broadcasted_iota(I32, (chunk, gw), 0) & (HG_SUB - 1)
    lc = jnp.log(f)
    sh = 1
    while sh < HG_SUB:
        lc = lc + jnp.where(rsub >= sh, pltpu.roll(lc, sh, 0), 0.0)
        sh *= 2
    return q, kk, v, lc, rsub


def _hgrn_diag_scores(q, kk, lc, h):
    chunk = q.shape[0]
    sl = slice(h * HEAD_DIM, (h + 1) * HEAD_DIM)
    qd = (q[:, sl] * jnp.exp(lc[:, sl])).astype(BF16)
    kd = (kk[:, sl] * jnp.exp(jnp.minimum(-lc[:, sl], EXP_CLAMP))).astype(BF16)
    ti = lax.broadcasted_iota(I32, (chunk, chunk), 0)
    si = lax.broadcasted_iota(I32, (chunk, chunk), 1)
    diag_ok = (ti // HG_SUB == si // HG_SUB) & (si <= ti)
    return jnp.where(diag_ok, _dot_nt(qd, kd), 0.0).astype(BF16)


def _hgrn_kernel(need_ref, hq_ref, hf_ref, hi_ref, hg_ref, lb_ref, gn_ref, o_ref, st_ref, oa_ref, *, chunk):
    @pl.when(pl.program_id(1) == 0)
    def _():
        st_ref[...] = jnp.zeros(st_ref.shape, F32)

    for c in range(hq_ref.shape[0] // chunk):
        rows = pl.ds(c * chunk, chunk)
        _hgrn_chunk(need_ref, hq_ref.at[rows], hf_ref.at[rows], hi_ref.at[rows], hg_ref.at[rows], lb_ref, gn_ref,
                    o_ref.at[rows], st_ref, oa_ref)


def _hgrn_chunk(need_ref, hq_ref, hf_ref, hi_ref, hg_ref, lb_ref, gn_ref, o_ref, st_ref, oa_ref):
    chunk, gw = hq_ref.shape
    nsub = chunk // HG_SUB

    q, kk, v, lc, rsub = _hgrn_gates(hq_ref, hf_ref, hi_ref, lb_ref)
    def per_sub(vals):
        return jnp.concatenate([jnp.broadcast_to(x, (HG_SUB, gw)) for x in vals], axis=0)

    tots = [lc[(a + 1) * HG_SUB - 1:(a + 1) * HG_SUB, :] for a in range(nsub)]
    ends = []
    run = jnp.zeros((1, gw), F32)
    for a in range(nsub):
        run = run + tots[a]
        ends.append(run)
    cum_l = ends[-1]
    cum = lc + per_sub([jnp.zeros((1, gw), F32)] + ends[:-1])

    qs = (q * jnp.exp(cum)).astype(BF16)
    koff = kk * jnp.exp(per_sub(tots) - lc)
    ke = (koff * per_sub([jnp.exp(cum_l - e) for e in ends])).astype(BF16)
    koff_b = koff.astype(BF16)
    qoff = []
    kmask = []
    for b in range(nsub - 1):
        r0 = (b + 1) * HG_SUB
        qb = (q[r0:, :] * jnp.exp(cum[r0:, :] - ends[b])).astype(BF16)
        qoff.append(jnp.concatenate([jnp.zeros((r0, gw), BF16), qb], axis=0))
        parts = [koff_b[b * HG_SUB:r0, :], jnp.zeros((chunk - r0, gw), BF16)]
        if b > 0:
            parts.insert(0, jnp.zeros((b * HG_SUB, gw), BF16))
        kmask.append(jnp.concatenate(parts, axis=0))
    vb = v.astype(BF16)
    decay_l = jnp.exp(cum_l)

    for h in range(G_HEADS):
        sl = slice(h * HEAD_DIM, (h + 1) * HEAD_DIM)
        qcat = jnp.concatenate([qo[:, sl] for qo in qoff], axis=1)
        kcat = jnp.concatenate([km[:, sl] for km in kmask], axis=1)
        a = _hgrn_diag_scores(q, kk, lc, h) + _dot_nt(qcat, kcat).astype(BF16)
        st = st_ref[h]
        oa_ref[:, sl] = jnp.dot(a, vb[:, sl], preferred_element_type=F32) + _dot_nt(qs[:, sl], st.astype(BF16))
        upd = jnp.dot(v[:, sl].T.astype(BF16), ke[:, sl], preferred_element_type=F32)
        st_ref[h] = st * decay_l[:, sl] + upd

    @pl.when(need_ref[0] != 0)
    def _():
        q2, kk2, v2, lc2, rsub2 = _hgrn_gates(hq_ref, hf_ref, hi_ref, lb_ref)

        @pl.when(jnp.max(-lc2) > EXP_CLAMP)
        def _():
            v2b = v2.astype(BF16)
            fast = [jnp.dot(_hgrn_diag_scores(q2, kk2, lc2, h), v2b[:, h * HEAD_DIM:(h + 1) * HEAD_DIM],
                            preferred_element_type=F32) for h in range(G_HEADS)]
            oa_ref[...] += _hgrn_diag_exact(q2, kk2, v2, lc2, rsub2, nsub) - jnp.concatenate(fast, axis=1)

    o = oa_ref[...]
    ms = jnp.mean(o * o, axis=-1, keepdims=True)
    hg = hg_ref[...].astype(F32)
    y = o * lax.rsqrt(ms + EPS) * gn_ref[...] * (hg * _sigmoid(hg))
    o_ref[...] = y.astype(o_ref.dtype)


def _hgrn(ph, pt, lb, g_hnorm, batch, seq, chunk):
    n = batch * seq
    gw = G_HEADS * HEAD_DIM
    rows = HG_CHUNKS_PER_STEP * chunk
    ncb = seq // rows
    blk = lambda c: pl.BlockSpec((rows, gw), lambda b, i, nd: (b * ncb + i, c))
    need = (HG_SUB * jnp.log(jnp.min(lb)) < -EXP_CLAMP).astype(I32).reshape(1)
    grid_spec = pltpu.PrefetchScalarGridSpec(
        num_scalar_prefetch=1,
        grid=(batch, ncb),
        in_specs=[blk(0), blk(1), blk(1), blk(2),
                  pl.BlockSpec((1, gw), lambda b, i, nd: (0, 0)),
                  pl.BlockSpec((1, gw), lambda b, i, nd: (0, 0))],
        out_specs=pl.BlockSpec((rows, gw), lambda b, i, nd: (b * ncb + i, 0)),
        scratch_shapes=[pltpu.VMEM((G_HEADS, HEAD_DIM, HEAD_DIM), F32),
                        pltpu.VMEM((chunk, gw), F32)],
    )
    return pl.pallas_call(
        functools.partial(_hgrn_kernel, chunk=chunk),
        grid_spec=grid_spec,
        out_shape=jax.ShapeDtypeStruct((n, gw), BF16),
        compiler_params=_cparams(("parallel", "arbitrary"), 32),
        name="hgrn",
    )(need, pt, ph, pt, pt, lb.reshape(1, gw), g_hnorm.reshape(1, gw))


def _merge_kernel(ya_ref, yr_ref, ga_ref, gb_ref, x_ref, wpa_ref, wpb_ref, wo_ref, o_ref, mg_ref):
    c = pl.program_id(1)
    nc = pl.num_programs(1)
    tc = ga_ref.shape[1]

    ma = jnp.dot(ya_ref[...], wpa_ref[...], preferred_element_type=F32)
    mb = jnp.dot(yr_ref[...], wpb_ref[...], preferred_element_type=F32)
    merged = _sigmoid(ga_ref[...].astype(F32)) * ma + _sigmoid(gb_ref[...].astype(F32)) * mb
    merged = merged.astype(BF16)
    for k in range(mg_ref.shape[1] // tc):
        @pl.when(c == k)
        def _():
            mg_ref[:, k * tc:(k + 1) * tc] = merged

    @pl.when(c == nc - 1)
    def _():
        o_ref[...] = x_ref[...] + jnp.dot(mg_ref[...], wo_ref[...], preferred_element_type=F32)


def _merge(y_att, y_rec, pt, x2, w_pa, w_pb, w_o, tm, tc):
    n, d = x2.shape
    aw = y_att.shape[1]
    gw = y_rec.shape[1]
    ga0 = 3072 // tc
    gb0 = 5120 // tc
    return pl.pallas_call(
        _merge_kernel,
        grid=(n // tm, d // tc),
        in_specs=[
            pl.BlockSpec((tm, aw), lambda i, c: (i, 0)),
            pl.BlockSpec((tm, gw), lambda i, c: (i, 0)),
            pl.BlockSpec((tm, tc), lambda i, c: (i, ga0 + c)),
            pl.BlockSpec((tm, tc), lambda i, c: (i, gb0 + c)),
            pl.BlockSpec((tm, d), lambda i, c: (i, 0)),
            pl.BlockSpec((aw, tc), lambda i, c: (0, c)),
            pl.BlockSpec((gw, tc), lambda i, c: (0, c)),
            pl.BlockSpec((d, d), lambda i, c: (0, 0)),
        ],
        out_specs=pl.BlockSpec((tm, d), lambda i, c: (i, 0)),
        out_shape=jax.ShapeDtypeStruct((n, d), F32),
        scratch_shapes=[pltpu.VMEM((tm, d), BF16)],
        compiler_params=_cparams(("parallel", "arbitrary"), 56),
        name="merge",
    )(y_att, y_rec, pt, pt, x2, w_pa, w_pb, w_o)


def _router_kernel(x_ref, g_ref, wr_ref, br_ref, h_ref, r_ref, cnt_ref):
    x = x_ref[...]
    ms = jnp.mean(x * x, axis=-1, keepdims=True)
    h = x * lax.rsqrt(ms + EPS) * g_ref[...]
    h_ref[...] = _pack_halves(h)
    lg = jnp.dot(h.astype(BF16), wr_ref[...], preferred_element_type=F32) + br_ref[...]
    tm = lg.shape[0]
    lane = lax.broadcasted_iota(I32, (tm, LANES), 1)
    neg = jnp.float32(-1e30)
    big = jnp.float32(LANES)
    is_g = lane < N_GROUPS
    lgg = jnp.where(is_g, lg, neg)
    mg = jnp.max(lgg, axis=1, keepdims=True)
    zg = jnp.sum(jnp.where(is_g, jnp.exp(lgg - mg), 0.0), axis=1, keepdims=True)
    p_g = 1.0 / zg
    grp = jnp.min(jnp.where(is_g & (lgg == mg), lane.astype(F32), big), axis=1, keepdims=True)
    e_id = lane - N_GROUPS
    e_grp = lax.shift_right_arithmetic(e_id, jnp.int32(EXP_PER_GROUP.bit_length() - 1)).astype(F32)
    e_idf = e_id.astype(F32)
    is_e = (e_id >= 0) & (e_id < N_EXPERTS) & (e_grp == grp)
    lge = jnp.where(is_e, lg, neg)
    m1 = jnp.max(lge, axis=1, keepdims=True)
    i1 = jnp.min(jnp.where(is_e & (lge == m1), e_idf, big), axis=1, keepdims=True)
    is_e2 = is_e & (e_idf != i1)
    lge2 = jnp.where(is_e2, lg, neg)
    m2 = jnp.max(lge2, axis=1, keepdims=True)
    i2 = jnp.min(jnp.where(is_e2 & (lge2 == m2), e_idf, big), axis=1, keepdims=True)
    ex2 = jnp.exp(m2 - m1)
    den = 1.0 + ex2
    g1 = p_g / den
    g2 = p_g * ex2 / den
    @pl.when(pl.program_id(0) == 0)
    def _():
        cnt_ref[...] = jnp.zeros(cnt_ref.shape, F32)

    lanef = lane.astype(F32)
    oh1 = lanef == i1
    oh2 = lanef == i2
    oh = jnp.where(oh1 | oh2, 1.0, 0.0)
    ti = lax.broadcasted_iota(I32, (tm, tm), 0)
    si = lax.broadcasted_iota(I32, (tm, tm), 1)
    tri = jnp.where(si < ti, 1.0, 0.0).astype(BF16)
    before = jnp.dot(tri, oh.astype(BF16), preferred_element_type=F32) + cnt_ref[...]
    rank1 = jnp.sum(jnp.where(oh1, before, 0.0), axis=1, keepdims=True)
    rank2 = jnp.sum(jnp.where(oh2, before, 0.0), axis=1, keepdims=True)
    cnt_ref[...] += jnp.sum(oh, axis=0, keepdims=True)

    r = jnp.where(lane == 0, i1, 0.0)
    r = jnp.where(lane == 1, i2, r)
    r = jnp.where(lane == 2, g1, r)
    r = jnp.where(lane == 3, g2, r)
    r = jnp.where(lane == 4, rank1, r)
    r = jnp.where(lane == 5, rank2, r)
    r_ref[...] = r


def _router(x1, g, w_r, b_r, tm):
    n, d = x1.shape
    return pl.pallas_call(
        _router_kernel,
        grid=(n // tm,),
        in_specs=[
            pl.BlockSpec((tm, d), lambda i: (i, 0)),
            pl.BlockSpec((1, d), lambda i: (0, 0)),
            pl.BlockSpec((d, LANES), lambda i: (0, 0)),
            pl.BlockSpec((1, LANES), lambda i: (0, 0)),
        ],
        out_specs=[pl.BlockSpec((tm, d // 2), lambda i: (i, 0)),
                   pl.BlockSpec((tm, LANES), lambda i: (i, 0)),
                   pl.BlockSpec((1, LANES), lambda i: (0, 0))],
        out_shape=[jax.ShapeDtypeStruct((n, d // 2), U32),
                   jax.ShapeDtypeStruct((n, LANES), F32),
                   jax.ShapeDtypeStruct((1, LANES), F32)],
        compiler_params=_cparams(("arbitrary",), 40),
        name="router",
    )(x1, g.reshape(1, d), w_r, b_r)


def _dispatch_kernel(prow_ref, h_ref, xs_hbm, sem, *, tm):
    base = 2 * tm * pl.program_id(0)

    def scatter_group(g, carry):
        r0 = pl.multiple_of(g * DMA_GROUP, DMA_GROUP)
        rows = h_ref.at[pl.ds(r0, DMA_GROUP), :]
        for u in range(DMA_GROUP):
            for s in range(2):
                dst_row = prow_ref[base + 2 * r0 + (2 * u + s)]
                pltpu.make_async_copy(rows.at[pl.ds(u, 1), :], xs_hbm.at[pl.ds(dst_row, 1), :],
                                      sem).start(priority=s)
        return carry

    lax.fori_loop(0, tm // DMA_GROUP, scatter_group, 0)
    for _ in range(2):
        pltpu.make_async_copy(h_ref, xs_hbm.at[pl.ds(0, tm), :], sem).wait()


def _dispatch(h2, prow, tm):
    n, d = h2.shape
    grid_spec = pltpu.PrefetchScalarGridSpec(
        num_scalar_prefetch=1,
        grid=(n // tm,),
        in_specs=[pl.BlockSpec((tm, d), lambda i, pr: (i, 0))],
        out_specs=pl.BlockSpec(memory_space=pl.ANY),
        scratch_shapes=[pltpu.SemaphoreType.DMA(())],
    )
    return pl.pallas_call(
        functools.partial(_dispatch_kernel, tm=tm),
        grid_spec=grid_spec,
        out_shape=jax.ShapeDtypeStruct((2 * n, d), h2.dtype),
        compiler_params=_cparams(("arbitrary",), 32),
        name="dispatch",
    )(prow, h2)


def _moe_kernel(pt_ref, pe_ref, plo_ref, phi_ref, slot_ref, nxt_ref, np_ref, x_ref, wg_hbm, wu_hbm, wd_hbm,
                y_ref, wgf_ref, wuf_ref, wdf_ref, wgb_ref, wub_ref, wdb_ref, yacc_ref, sem):
    p = pl.program_id(0)
    active = p < np_ref[0]
    prev = jnp.maximum(p - 1, 0)
    new_expert = (p == 0) | (pe_ref[p] != pe_ref[prev])
    new_tile = (p == 0) | (pt_ref[p] != pt_ref[prev])
    slot = slot_ref[p]

    def weight_copies(e, s):
        return (pltpu.make_async_copy(wg_hbm.at[e], wgf_ref.at[s], sem.at[s]),
                pltpu.make_async_copy(wu_hbm.at[e], wuf_ref.at[s], sem.at[s]),
                pltpu.make_async_copy(wd_hbm.at[e], wdf_ref.at[s], sem.at[s]))

    @pl.when(p == 0)
    def _():
        for cp in weight_copies(pe_ref[0], 0):
            cp.start()

    @pl.when(active & new_expert)
    def _():
        for cp in weight_copies(pe_ref[p], slot):
            cp.wait()
        wgb_ref[...] = wgf_ref[slot].astype(BF16)
        wub_ref[...] = wuf_ref[slot].astype(BF16)
        wdb_ref[...] = wdf_ref[slot].astype(BF16)

        @pl.when(nxt_ref[p] >= 0)
        def _():
            for cp in weight_copies(nxt_ref[p], 1 - slot):
                cp.start()

    @pl.when(active)
    def _():
        tm, half = x_ref.shape
        lo, hi = _unpack_halves(x_ref[...])
        xl = lo.astype(BF16)
        xh = hi.astype(BF16)
        a = (jnp.dot(xl, wgb_ref[0:half, :], preferred_element_type=F32)
             + jnp.dot(xh, wgb_ref[half:, :], preferred_element_type=F32))
        u = (jnp.dot(xl, wub_ref[0:half, :], preferred_element_type=F32)
             + jnp.dot(xh, wub_ref[half:, :], preferred_element_type=F32))
        hmid = (a * _sigmoid(a) * u).astype(BF16)
        yv = jnp.dot(hmid, wdb_ref[...], preferred_element_type=F32)
        row = lax.broadcasted_iota(I32, (tm, 1), 0)
        mine = (row >= plo_ref[p]) & (row < phi_ref[p])
        yv = jnp.where(mine, yv, 0.0)

        @pl.when(new_tile)
        def _():
            yacc_ref[...] = yv

        @pl.when(jnp.logical_not(new_tile))
        def _():
            yacc_ref[...] += yv

        nxt_p = jnp.minimum(p + 1, pl.num_programs(0) - 1)
        last_of_tile = (p == np_ref[0] - 1) | (pt_ref[nxt_p] != pt_ref[p])

        @pl.when(last_of_tile)
        def _():
            y_ref[...] = _pack_halves(yacc_ref[...])


def _moe(xs, pairs, w_gate, w_up, w_down, tm):
    pair_tile, pair_expert, pair_lo, pair_hi, pair_slot, pair_next, n_pairs = pairs
    na, half = xs.shape
    d, de = w_gate.shape[1:]
    max_pairs = pair_tile.shape[0]
    grid_spec = pltpu.PrefetchScalarGridSpec(
        num_scalar_prefetch=7,
        grid=(max_pairs,),
        in_specs=[
            pl.BlockSpec((tm, half), lambda p, pt, *_: (pt[p], 0)),
            pl.BlockSpec(memory_space=pl.ANY),
            pl.BlockSpec(memory_space=pl.ANY),
            pl.BlockSpec(memory_space=pl.ANY),
        ],
        out_specs=pl.BlockSpec((tm, half), lambda p, pt, *_: (pt[p], 0)),
        scratch_shapes=[
            pltpu.VMEM((2, d, de), F32),
            pltpu.VMEM((2, d, de), F32),
            pltpu.VMEM((2, de, d), F32),
            pltpu.VMEM((d, de), BF16),
            pltpu.VMEM((d, de), BF16),
            pltpu.VMEM((de, d), BF16),
            pltpu.VMEM((tm, d), F32),
            pltpu.SemaphoreType.DMA((2,)),
        ],
    )
    return pl.pallas_call(
        _moe_kernel,
        grid_spec=grid_spec,
        out_shape=jax.ShapeDtypeStruct((na, half), U32),
        compiler_params=_cparams(("arbitrary",), 56),
        name="moe",
    )(pair_tile, pair_expert, pair_lo, pair_hi, pair_slot, pair_next, n_pairs, xs, w_gate, w_up, w_down)


def _moe_rows(r, cnt):
    sizes = cnt[0, :N_EXPERTS].astype(I32)
    starts = jnp.cumsum(sizes) - sizes
    e12 = r[:, 0:2].astype(I32)
    rank = r[:, 4:6].astype(I32)
    onehot = e12[:, :, None] == jnp.arange(N_EXPERTS, dtype=I32)[None, None, :]
    start_a = jnp.sum(jnp.where(onehot, starts[None, None, :], 0), axis=-1)
    return (start_a + rank).reshape(-1).astype(I32), sizes


def _count_le(sorted_vals, x):
    return jnp.sum((sorted_vals[None, :] <= x[:, None]).astype(I32), axis=1)


def _moe_pairs(sizes, na, tm):
    ntiles = na // tm
    max_pairs = ntiles + N_EXPERTS - 1
    ends = jnp.cumsum(sizes)
    starts = ends - sizes
    t0 = jnp.arange(ntiles, dtype=I32) * tm
    e_lo = _count_le(ends, t0)
    e_hi = _count_le(ends, t0 + (tm - 1))
    per_tile = e_hi - e_lo + 1
    pend = jnp.cumsum(per_tile)
    pstart = pend - per_tile
    n_pairs = pend[-1]
    pc = jnp.minimum(jnp.arange(max_pairs, dtype=I32), n_pairs - 1)
    tile = _count_le(pend, pc)
    expert = e_lo[tile] + (pc - pstart[tile])
    lo = jnp.clip(starts[expert] - tile * tm, 0, tm).astype(I32)
    hi = jnp.clip(ends[expert] - tile * tm, 0, tm).astype(I32)
    expert = expert.astype(I32)
    first = jnp.concatenate([jnp.ones((1,), bool), expert[1:] != expert[:-1]])
    slot = ((jnp.cumsum(first) - 1) % 2).astype(I32)
    idx = jnp.arange(max_pairs, dtype=I32)
    first_at = jnp.where(first, idx, max_pairs)
    nxt_first = jnp.concatenate([lax.cummin(first_at[::-1])[::-1][1:], jnp.full((1,), max_pairs, I32)])
    nxt = jnp.where(nxt_first < max_pairs, expert[jnp.minimum(nxt_first, max_pairs - 1)], -1).astype(I32)
    return tile, expert, lo, hi, slot, nxt, n_pairs.astype(I32).reshape(1)


def _final_kernel(prow_ref, x_ref, r_ref, g_ref, y_hbm, o_ref, ybuf, sem, *, tm):
    i = pl.program_id(0)
    cur = i % 2

    def issue(step, buf):
        base = 2 * tm * step

        def gather_group(g, carry):
            r0 = pl.multiple_of(g * DMA_GROUP, DMA_GROUP)
            for s in range(2):
                rows = ybuf.at[buf, s, pl.ds(r0, DMA_GROUP), :]
                for u in range(DMA_GROUP):
                    src_row = prow_ref[base + 2 * r0 + (2 * u + s)]
                    pltpu.make_async_copy(y_hbm.at[pl.ds(src_row, 1), :], rows.at[pl.ds(u, 1), :],
                                          sem.at[buf]).start(priority=u % 2)
            return carry

        lax.fori_loop(0, tm // DMA_GROUP, gather_group, 0)

    @pl.when(i == 0)
    def _():
        issue(0, 0)

    @pl.when(i + 1 < pl.num_programs(0))
    def _():
        issue(i + 1, 1 - cur)

    for s in range(2):
        pltpu.make_async_copy(y_hbm.at[pl.ds(0, tm), :], ybuf.at[cur, s], sem.at[cur]).wait()
    r = r_ref[...]
    g1 = r[:, 2:3]
    g2 = r[:, 3:4]
    half = ybuf.shape[-1]
    lo0, hi0 = _unpack_halves(ybuf[cur, 0])
    lo1, hi1 = _unpack_halves(ybuf[cur, 1])
    xa = x_ref[:, :half] + g1 * lo0 + g2 * lo1
    xb = x_ref[:, half:] + g1 * hi0 + g2 * hi1
    ms = (jnp.sum(xa * xa, axis=-1, keepdims=True) + jnp.sum(xb * xb, axis=-1, keepdims=True)) / (2 * half)
    rs = lax.rsqrt(ms + EPS)
    o_ref[:, :half] = xa * rs * g_ref[:, :half]
    o_ref[:, half:] = xb * rs * g_ref[:, half:]


def _final(x1, y, r, prow, g, tm):
    n, d = x1.shape
    grid_spec = pltpu.PrefetchScalarGridSpec(
        num_scalar_prefetch=1,
        grid=(n // tm,),
        in_specs=[
            pl.BlockSpec((tm, d), lambda i, pr: (i, 0)),
            pl.BlockSpec((tm, LANES), lambda i, pr: (i, 0)),
            pl.BlockSpec((1, d), lambda i, pr: (0, 0)),
            pl.BlockSpec(memory_space=pl.ANY),
        ],
        out_specs=pl.BlockSpec((tm, d), lambda i, pr: (i, 0)),
        scratch_shapes=[pltpu.VMEM((2, 2, tm, d // 2), U32), pltpu.SemaphoreType.DMA((2,))],
    )
    return pl.pallas_call(
        functools.partial(_final_kernel, tm=tm),
        grid_spec=grid_spec,
        out_shape=jax.ShapeDtypeStruct((n, d), F32),
        compiler_params=_cparams(("arbitrary",), 48),
        name="final",
    )(prow, x1, r, g.reshape(1, d), y)


def _pack_w_in(w):
    d, din = w.shape
    tc = 256
    wt = jnp.swapaxes(w, 0, 1)

    def pack_kernel(w_ref, o_ref):
        o_ref[0:912, :] = w_ref[0:912, :].astype(BF16)
        o_ref[912:1024, :] = jnp.zeros((LANES - IDX_HEADS, tc), BF16)
        o_ref[1024:2048, :] = w_ref[1936:2960, :].astype(BF16)
        o_ref[2048:3072, :] = w_ref[912:1936, :].astype(BF16)
        o_ref[3072:, :] = w_ref[2960:, :].astype(BF16)

    return pl.pallas_call(
        pack_kernel,
        grid=(d // tc,),
        in_specs=[pl.BlockSpec((din, tc), lambda i: (0, i))],
        out_specs=pl.BlockSpec((PH_COLS + PT_COLS, tc), lambda i: (0, i)),
        out_shape=jax.ShapeDtypeStruct((PH_COLS + PT_COLS, d), BF16),
        compiler_params=_cparams(("parallel",), 48),
        name="pack_w_in",
    )(wt)


def _rope_tables(seq):
    inv = 1.0 / (ROPE_THETA ** (jnp.arange(0, HEAD_DIM, 2, dtype=F32) / HEAD_DIM))
    ang = jnp.arange(seq, dtype=F32)[:, None] * inv[None, :]
    c, s = jnp.cos(ang), jnp.sin(ang)
    c2, s2 = jnp.concatenate([c, c], axis=1), jnp.concatenate([-s, s], axis=1)
    return c2, s2, c2.T, s2.T


class _Tiles(NamedTuple):
    proj_rows: int
    proj_cols: int
    att_q: int
    att_k: int
    rec_chunk: int
    merge_rows: int
    merge_cols: int
    tok_rows: int
    moe_rows: int


def _tiles(n, seq):
    return _Tiles(proj_rows=min(1024, n), proj_cols=1024, att_q=min(256, seq), att_k=min(256, seq), rec_chunk=64,
                  merge_rows=min(512, n), merge_cols=1024, tok_rows=min(512, n), moe_rows=256)


def _layer(x2, batch, seq, rope, lb, g_norm1, w_in, g_cq, w_uq, w_qidx, g_hnorm, w_pa, w_pb, w_o,
           g_norm2, w_grp, b_grp, w_exp, b_exp, w_gate, w_up, w_down, g_out):
    n, d = x2.shape
    t = _tiles(n, seq)
    ph, pt = _in_proj(x2, g_norm1, _pack_w_in(w_in), tm=t.proj_rows, tn=t.proj_cols)
    qt, qit, wt, k, ki, vt = _dsa_prep(ph, rope, g_cq, w_uq.T.astype(BF16), w_qidx.T.astype(BF16), seq,
                                       tm=t.att_k, tq=t.att_q)
    y_att = _dsa(qit, qt, wt, ki, k, vt, batch, seq, tq=t.att_q, tk=t.att_k)
    y_rec = _hgrn(ph, pt, lb, g_hnorm, batch, seq, chunk=t.rec_chunk)
    x1 = _merge(y_att, y_rec, pt, x2, w_pa.astype(BF16), w_pb.astype(BF16), w_o.astype(BF16),
                tm=t.merge_rows, tc=t.merge_cols)
    w_r = jnp.concatenate([w_grp, w_exp, jnp.zeros((d, LANES - N_GROUPS - N_EXPERTS), F32)], axis=1).astype(BF16)
    b_r = jnp.concatenate([b_grp, b_exp, jnp.zeros((LANES - N_GROUPS - N_EXPERTS,), F32)]).reshape(1, LANES)
    h2, r, cnt = _router(x1, g_norm2, w_r, b_r, tm=t.tok_rows)
    prow, sizes = _moe_rows(r, cnt)
    xs = _dispatch(h2, prow, tm=t.tok_rows)
    assert (2 * n) % t.moe_rows == 0
    y = _moe(xs, _moe_pairs(sizes, 2 * n, t.moe_rows), w_gate, w_up, w_down, t.moe_rows)
    return _final(x1, y, r, prow, g_out, tm=t.tok_rows)


def kernel(x, g_norm1, w_in, g_cq, w_uq, w_qidx, lb_logits, g_hnorm, w_pa, w_pb, w_o, g_norm2, w_grp, b_grp,
           w_exp, b_exp, w_gate, w_up, w_down, g_final):
    batch, seq, d = x.shape
    depth = g_norm1.shape[0]
    rope = _rope_tables(seq)
    lb_all = jnp.cumsum(jax.nn.softmax(lb_logits.astype(F32), axis=0), axis=0)
    x2 = x.reshape(batch * seq, d)
    for l in range(depth):
        assert depth == 1
        x2 = _layer(x2, batch, seq, rope, lb_all[l], g_norm1[l], w_in[l], g_cq[l], w_uq[l], w_qidx[l],
                    g_hnorm[l], w_pa[l], w_pb[l], w_o[l], g_norm2[l], w_grp[l], b_grp[l], w_exp[l], b_exp[l],
                    w_gate[l], w_up[l], w_down[l], g_final)
    return x2.reshape(batch, seq, d)
```

```python
import functools
import math
from typing import NamedTuple

import jax
import jax.numpy as jnp
from jax import lax
from jax.experimental import pallas as pl
from jax.experimental.pallas import tpu as pltpu

F32 = jnp.float32
BF16 = jnp.bfloat16
I32 = jnp.int32
U32 = jnp.uint32

EPS = 1e-6
ROPE_THETA = 10000.0
LANES = 128
MXU_LANES = 256
DMA_GROUP = 32
VT_ROWS = 128 + 16

A_HEADS = 8
HEAD_DIM = 128
IDX_HEADS = 16
Q_RANK = 512
TOPK_MAX = 256
G_HEADS = 8
N_GROUPS = 4
EXP_PER_GROUP = 8
N_EXPERTS = N_GROUPS * EXP_PER_GROUP

INT_MIN = -2147483648
EXP_CLAMP = 80.0


def _cparams(sem, vmem_mb):
    return pltpu.CompilerParams(dimension_semantics=sem, vmem_limit_bytes=vmem_mb << 20)


def _pack_halves(x):
    c = x.shape[1] // 2
    bits = lax.bitcast_convert_type(x.astype(BF16).astype(F32), U32)
    return lax.shift_right_logical(bits[:, :c], jnp.uint32(16)) | (bits[:, c:] & jnp.uint32(0xFFFF0000))


def _unpack_halves(w):
    lo = lax.bitcast_convert_type(lax.shift_left(w, jnp.uint32(16)), F32)
    hi = lax.bitcast_convert_type(w & jnp.uint32(0xFFFF0000), F32)
    return lo, hi


def _sigmoid(x):
    return 0.5 * jnp.tanh(0.5 * x) + 0.5


PH_COLS = 2048
PT_COLS = 7168


def _inproj_kernel(x_ref, g_ref, w_ref, oh_ref, ot_ref, xn_ref, *, nh):
    j = pl.program_id(1)

    @pl.when(j == 0)
    def _():
        x = x_ref[...]
        ms = jnp.mean(x * x, axis=-1, keepdims=True)
        xn_ref[...] = (x * lax.rsqrt(ms + EPS) * g_ref[...]).astype(BF16)

    y = _dot_nt(xn_ref[...], w_ref[...])

    @pl.when(j < nh)
    def _():
        oh_ref[...] = y

    @pl.when(j >= nh)
    def _():
        ot_ref[...] = y.astype(ot_ref.dtype)


def _in_proj(x2, g, w, tm, tn):
    n, d = x2.shape
    nh = PH_COLS // tn
    nt = PT_COLS // tn
    return pl.pallas_call(
        functools.partial(_inproj_kernel, nh=nh),
        grid=(n // tm, nh + nt),
        in_specs=[
            pl.BlockSpec((tm, d), lambda i, j: (i, 0)),
            pl.BlockSpec((1, d), lambda i, j: (0, 0)),
            pl.BlockSpec((tn, d), lambda i, j: (j, 0)),
        ],
        out_specs=[pl.BlockSpec((tm, tn), lambda i, j: (i, jnp.minimum(j, nh - 1))),
                   pl.BlockSpec((tm, tn), lambda i, j: (i, jnp.maximum(j - nh, 0)))],
        out_shape=[jax.ShapeDtypeStruct((n, PH_COLS), F32),
                   jax.ShapeDtypeStruct((n, PT_COLS), BF16)],
        scratch_shapes=[pltpu.VMEM((tm, d), BF16)],
        compiler_params=_cparams(("parallel", "arbitrary"), 56),
        name="in_proj",
    )(x2, g.reshape(1, d), w)


def _rope(x, c2, s2):
    return x * c2 + pltpu.roll(x, HEAD_DIM // 2, 1) * s2


def _rope_t(x, c2t, s2t):
    half = HEAD_DIM // 2
    return x * c2t + jnp.concatenate([x[half:, :], x[:half, :]], axis=0) * s2t


def _prep_kernel(cq_ref, ka_ref, va_ref, kx_ref, wi_ref, cos_ref, sin_ref, cost_ref, sint_ref, g_ref, wuq_ref, wqi_ref,
                 qt_ref, qit_ref, wt_ref, k_ref, ki_ref, vt_ref, *, scale, wscale, tq):
    nqb = cq_ref.shape[0] // tq
    c2 = cos_ref[...]
    s2 = sin_ref[...]
    c2t = cost_ref[...]
    s2t = sint_ref[...]
    cq = cq_ref[...]
    ms = jnp.mean(cq * cq, axis=-1, keepdims=True)
    cqn = (cq * lax.rsqrt(ms + EPS) * g_ref[...]).astype(BF16)
    q_t = _dot_nt(wuq_ref[...], cqn)
    for h in range(A_HEADS):
        qh_t = _rope_t(q_t[h * HEAD_DIM:(h + 1) * HEAD_DIM, :], c2t, s2t) * scale
        for b in range(nqb):
            qt_ref[b, :, h * tq:(h + 1) * tq] = qh_t[:, b * tq:(b + 1) * tq].astype(BF16)
    qi_t = _dot_nt(wqi_ref[...], cqn)
    for h in range(IDX_HEADS):
        qh_t = _rope_t(qi_t[h * HEAD_DIM:(h + 1) * HEAD_DIM, :], c2t, s2t)
        for b in range(nqb):
            qit_ref[b, :, h * tq:(h + 1) * tq] = qh_t[:, b * tq:(b + 1) * tq].astype(BF16)
    w_t = (wi_ref[...] * wscale).T
    for h in range(IDX_HEADS):
        for b in range(nqb):
            wt_ref[b, :, h * tq:(h + 1) * tq] = w_t[h:h + 1, b * tq:(b + 1) * tq]
    k_ref[...] = _rope(ka_ref[...], c2, s2).astype(BF16)
    ki_ref[...] = _rope(kx_ref[...], c2, s2).astype(BF16)
    vt_ref[0, 0:HEAD_DIM, :] = va_ref[...].T.astype(BF16)
    ones_row = lax.broadcasted_iota(I32, (VT_ROWS - HEAD_DIM, va_ref.shape[0]), 0) == 0
    vt_ref[0, HEAD_DIM:VT_ROWS, :] = jnp.where(ones_row, 1.0, 0.0).astype(BF16)


def _dsa_prep(p, rope, g_cq, w_uq_t, w_qidx_t, seq, tm, tq):
    n = p.shape[0]
    nsb = seq // tm
    nqb = tm // tq
    aw = A_HEADS * HEAD_DIM
    iw = IDX_HEADS * HEAD_DIM
    row = lambda i: (i, 0)
    blk3 = lambda i: (i, 0, 0)
    kern = functools.partial(_prep_kernel, scale=HEAD_DIM ** -0.5 * math.log2(math.e),
                             wscale=IDX_HEADS ** -0.5 * HEAD_DIM ** -0.5, tq=tq)
    return pl.pallas_call(
        kern,
        grid=(n // tm,),
        in_specs=[
            pl.BlockSpec((tm, Q_RANK), lambda i: (i, 0)),
            pl.BlockSpec((tm, LANES), lambda i: (i, 4)),
            pl.BlockSpec((tm, LANES), lambda i: (i, 5)),
            pl.BlockSpec((tm, LANES), lambda i: (i, 6)),
            pl.BlockSpec((tm, LANES), lambda i: (i, 7)),
            pl.BlockSpec((tm, LANES), lambda i: (i % nsb, 0)),
            pl.BlockSpec((tm, LANES), lambda i: (i % nsb, 0)),
            pl.BlockSpec((HEAD_DIM, tm), lambda i: (0, i % nsb)),
            pl.BlockSpec((HEAD_DIM, tm), lambda i: (0, i % nsb)),
            pl.BlockSpec((1, Q_RANK), lambda i: (0, 0)),
            pl.BlockSpec((aw, Q_RANK), lambda i: (0, 0)),
            pl.BlockSpec((iw, Q_RANK), lambda i: (0, 0)),
        ],
        out_specs=[
            pl.BlockSpec((nqb, HEAD_DIM, A_HEADS * tq), blk3),
            pl.BlockSpec((nqb, HEAD_DIM, IDX_HEADS * tq), blk3),
            pl.BlockSpec((nqb, 1, IDX_HEADS * tq), blk3),
            pl.BlockSpec((tm, LANES), row),
            pl.BlockSpec((tm, LANES), row),
            pl.BlockSpec((1, VT_ROWS, tm), blk3),
        ],
        out_shape=[
            jax.ShapeDtypeStruct((n // tq, HEAD_DIM, A_HEADS * tq), BF16),
            jax.ShapeDtypeStruct((n // tq, HEAD_DIM, IDX_HEADS * tq), BF16),
            jax.ShapeDtypeStruct((n // tq, 1, IDX_HEADS * tq), F32),
            jax.ShapeDtypeStruct((n, LANES), BF16),
            jax.ShapeDtypeStruct((n, LANES), BF16),
            jax.ShapeDtypeStruct((n // tm, VT_ROWS, tm), BF16),
        ],
        compiler_params=_cparams(("parallel",), 40),
        name="dsa_prep",
    )(p, p, p, p, p, *rope, g_cq.reshape(1, Q_RANK), w_uq_t, w_qidx_t)


def _dot_nt(a, b):
    return lax.dot_general(a, b, (((1,), (1,)), ((), ())), preferred_element_type=F32)


def _dsa_kernel(qit_ref, qt_ref, wt_ref, ki_ref, k_ref, vt_ref, o_ref,
                key_ref, sa_ref, sb_ref, pa_ref, pb_ref, acc_ref, *, tq, tk, topk, idx_bits):
    i = pl.program_id(1)
    nkc = ((i + 1) * tq + tk - 1) // tk
    q_pos = i * tq + lax.broadcasted_iota(I32, (tk, tq), 1)
    k_off = lax.broadcasted_iota(I32, (tk, tq), 0)

    def score_chunks(j0, nch):
        kic = ki_ref[pl.ds(pl.multiple_of(j0 * tk, tk), nch * tk), :]
        acc = jnp.zeros((nch * tk, tq), F32)
        hpm = max(1, MXU_LANES // tq)
        for h0 in range(0, IDX_HEADS, hpm):
            sl = slice(h0 * tq, (h0 + hpm) * tq)
            lg = jnp.dot(kic, qit_ref[0, :, sl], preferred_element_type=F32)
            x = jnp.maximum(lg, 0.0) * wt_ref[0, :, sl]
            for u in range(hpm):
                acc = acc + x[:, u * tq:(u + 1) * tq]
        acc = jnp.where(acc == 0.0, 0.0, acc)
        bits = lax.bitcast_convert_type(acc, I32)
        key = jnp.where(bits < 0, bits ^ jnp.int32(0x7FFFFFFF), bits)
        for c in range(nch):
            causal = ((j0 + c) * tk + k_off) <= q_pos
            key_ref[j0 + c] = jnp.where(causal, key[c * tk:(c + 1) * tk, :], jnp.int32(INT_MIN))

    lax.fori_loop(0, nkc // 2, lambda t, c: (score_chunks(2 * t, 2), c)[1], 0)

    @pl.when(nkc % 2 == 1)
    def _():
        score_chunks(nkc - 1, 1)

    def count_rows(pred):
        def cnt_chunk(j, part):
            hit = pred(j, key_ref[j]).astype(I32)
            return part + jnp.sum(hit.reshape(tk // 8, 8, tq), axis=0)

        part = lax.fori_loop(0, nkc, cnt_chunk, jnp.zeros((8, tq), I32))
        return jnp.sum(part, axis=0, keepdims=True)

    def bit_step(b, c):
        t_u, cnt_t = c
        bit = jnp.left_shift(jnp.int32(1), 31 - b)
        cand_s = (t_u | bit) ^ jnp.int32(INT_MIN)
        cnt = count_rows(lambda j, key: key >= cand_s)
        take = cnt >= topk
        return jnp.where(take, t_u | bit, t_u), jnp.where(take, cnt, cnt_t)

    n_causal = i * tq + lax.broadcasted_iota(I32, (1, tq), 1) + 1
    t_u, cnt_t = lax.fori_loop(0, 32, bit_step, (jnp.zeros((1, tq), I32), n_causal))
    thr = jnp.maximum(t_u ^ jnp.int32(INT_MIN), jnp.int32(INT_MIN + 1))

    @pl.when(jnp.max(cnt_t) > topk)
    def _():
        need = topk - count_rows(lambda j, key: key > thr)

        def idx_step(b, cut):
            cand = cut | jnp.left_shift(jnp.int32(1), idx_bits - 1 - b)
            cnt = count_rows(lambda j, key: (key == thr) & ((j * tk + k_off) < cand))
            return jnp.where(cnt <= need, cand, cut)

        cut = lax.fori_loop(0, idx_bits, idx_step, jnp.zeros((1, tq), I32))

        def demote(j, carry):
            key = key_ref[j]
            drop = (key == thr) & ((j * tk + k_off) >= cut)
            key_ref[j] = jnp.where(drop, key - 1, key)
            return carry

        lax.fori_loop(0, nkc, demote, 0)

    aw = A_HEADS * tq
    acc_ref[...] = jnp.zeros(acc_ref.shape, F32)

    def scores(j, s_ref):
        kc = k_ref[pl.ds(pl.multiple_of(j * tk, tk), tk), :]
        sel = key_ref[j] >= thr
        cmax = []
        for h in range(A_HEADS):
            sl = slice(h * tq, (h + 1) * tq)
            s_h = jnp.dot(kc, qt_ref[0, :, sl], preferred_element_type=F32)
            s_h = jnp.where(sel, s_h, -jnp.inf)
            s_ref[:, sl] = s_h
            cmax.append(jnp.max(s_h, axis=0, keepdims=True))
        return jnp.concatenate(cmax, axis=1)

    def absorb(j, s_ref, p_ref, cmax, m_old):
        m_new = jnp.maximum(m_old, cmax)
        alpha = jnp.exp2(m_old - m_new)
        for h in range(A_HEADS):
            sl = slice(h * tq, (h + 1) * tq)
            p_ref[:, sl] = jnp.exp2(s_ref[:, sl] - m_new[:, sl]).astype(BF16)
        acc_ref[...] = alpha * acc_ref[...] + jnp.dot(vt_ref[j], p_ref[...], preferred_element_type=F32)
        return m_new

    def pair(t, carry):
        m, cm_a = carry
        a = 2 * t
        cm_b = scores(a + 1, sb_ref)
        m = absorb(a, sa_ref, pa_ref, cm_a, m)
        cm_next = scores(jnp.minimum(a + 2, nkc - 1), sa_ref)
        m = absorb(a + 1, sb_ref, pb_ref, cm_b, m)
        return m, cm_next

    m_fin, cm_last = lax.fori_loop(0, nkc // 2, pair, (jnp.full((1, aw), -1e30, F32), scores(0, sa_ref)))

    @pl.when(nkc % 2 == 1)
    def _():
        absorb(nkc - 1, sa_ref, pa_ref, cm_last, m_fin)

    out_t = acc_ref[0:HEAD_DIM, :] / acc_ref[HEAD_DIM:HEAD_DIM + 1, :]
    for h in range(A_HEADS):
        o_ref[:, h * HEAD_DIM:(h + 1) * HEAD_DIM] = out_t[:, h * tq:(h + 1) * tq].T.astype(o_ref.dtype)


def _dsa(qit, qt, wt, ki, k, vt, batch, seq, tq, tk):
    n = batch * seq
    nqb = seq // tq
    nkb = seq // tk
    topk = min(TOPK_MAX, seq // 4)
    aw = A_HEADS * HEAD_DIM
    kern = functools.partial(_dsa_kernel, tq=tq, tk=tk, topk=topk, idx_bits=seq.bit_length())
    qblk = lambda b, i: (b * nqb + i, 0, 0)
    return pl.pallas_call(
        kern,
        grid=(batch, nqb),
        in_specs=[
            pl.BlockSpec((1, HEAD_DIM, IDX_HEADS * tq), qblk),
            pl.BlockSpec((1, HEAD_DIM, A_HEADS * tq), qblk),
            pl.BlockSpec((1, 1, IDX_HEADS * tq), qblk),
            pl.BlockSpec((seq, LANES), lambda b, i: (b, 0)),
            pl.BlockSpec((seq, LANES), lambda b, i: (b, 0)),
            pl.BlockSpec((nkb, VT_ROWS, tk), lambda b, i: (b, 0, 0)),
        ],
        out_specs=pl.BlockSpec((tq, aw), lambda b, i: (b * nqb + i, 0)),
        out_shape=jax.ShapeDtypeStruct((n, aw), BF16),
        scratch_shapes=[
            pltpu.VMEM((nkb, tk, tq), I32),
            pltpu.VMEM((tk, A_HEADS * tq), F32),
            pltpu.VMEM((tk, A_HEADS * tq), F32),
            pltpu.VMEM((tk, A_HEADS * tq), BF16),
            pltpu.VMEM((tk, A_HEADS * tq), BF16),
            pltpu.VMEM((VT_ROWS, A_HEADS * tq), F32),
        ],
        compiler_params=_cparams(("parallel", "arbitrary"), 48),
        name="dsa",
    )(qit, qt, wt, ki, k, vt)


HG_SUB = 16
HG_CHUNKS_PER_STEP = 4


def _hgrn_diag_exact(q, kk, v, lc, rsub, nsub):
    chunk, gw = q.shape

    def bcast(x, s):
        return jnp.concatenate([jnp.broadcast_to(x[a * HG_SUB + s:a * HG_SUB + s + 1, :], (HG_SUB, gw))
                                for a in range(nsub)], axis=0)

    acc = [jnp.zeros((chunk, HEAD_DIM), F32) for _ in range(G_HEADS)]
    for s in range(HG_SUB):
        term = q * bcast(kk, s) * jnp.exp(jnp.minimum(lc - bcast(lc, s), 0.0))
        term = jnp.where(rsub >= s, term, 0.0)
        vrow = bcast(v, s)
        for h in range(G_HEADS):
            sl = slice(h * HEAD_DIM, (h + 1) * HEAD_DIM)
            acc[h] = acc[h] + jnp.sum(term[:, sl], axis=1, keepdims=True) * vrow[:, sl]
    return jnp.concatenate(acc, axis=1)


def _hgrn_gates(hq_ref, hf_ref, hi_ref, lb_ref):
    chunk, gw = hq_ref.shape
    lb = lb_ref[...]
    f = lb + (1.0 - lb) * _sigmoid(hf_ref[...])
    kk = 1.0 - f
    hq = hq_ref[...].astype(F32)
    q = hq * _sigmoid(hq)
    v = hi_ref[...].astype(F32)
    rsub = lax.broadcasted_iota(I32, (chunk, gw), 0) & (HG_SUB - 1)
    lc = jnp.log(f)
    sh = 1
    while sh < HG_SUB:
        lc = lc + jnp.where(rsub >= sh, pltpu.roll(lc, sh, 0), 0.0)
        sh *= 2
    return q, kk, v, lc, rsub


def _hgrn_diag_scores(q, kk, lc, h):
    chunk = q.shape[0]
    sl = slice(h * HEAD_DIM, (h + 1) * HEAD_DIM)
    qd = (q[:, sl] * jnp.exp(lc[:, sl])).astype(BF16)
    kd = (kk[:, sl] * jnp.exp(jnp.minimum(-lc[:, sl], EXP_CLAMP))).astype(BF16)
    ti = lax.broadcasted_iota(I32, (chunk, chunk), 0)
    si = lax.broadcasted_iota(I32, (chunk, chunk), 1)
    diag_ok = (ti // HG_SUB == si // HG_SUB) & (si <= ti)
    return jnp.where(diag_ok, _dot_nt(qd, kd), 0.0).astype(BF16)


def _hgrn_kernel(need_ref, hq_ref, hf_ref, hi_ref, hg_ref, lb_ref, gn_ref, o_ref, st_ref, oa_ref, *, chunk):
    @pl.when(pl.program_id(1) == 0)
    def _():
        st_ref[...] = jnp.zeros(st_ref.shape, F32)

    for c in range(hq_ref.shape[0] // chunk):
        rows = pl.ds(c * chunk, chunk)
        _hgrn_chunk(need_ref, hq_ref.at[rows], hf_ref.at[rows], hi_ref.at[rows], hg_ref.at[rows], lb_ref, gn_ref,
                    o_ref.at[rows], st_ref, oa_ref)


def _hgrn_chunk(need_ref, hq_ref, hf_ref, hi_ref, hg_ref, lb_ref, gn_ref, o_ref, st_ref, oa_ref):
    chunk, gw = hq_ref.shape
    nsub = chunk // HG_SUB

    q, kk, v, lc, rsub = _hgrn_gates(hq_ref, hf_ref, hi_ref, lb_ref)
    def per_sub(vals):
        return jnp.concatenate([jnp.broadcast_to(x, (HG_SUB, gw)) for x in vals], axis=0)

    tots = [lc[(a + 1) * HG_SUB - 1:(a + 1) * HG_SUB, :] for a in range(nsub)]
    ends = []
    run = jnp.zeros((1, gw), F32)
    for a in range(nsub):
        run = run + tots[a]
        ends.append(run)
    cum_l = ends[-1]
    cum = lc + per_sub([jnp.zeros((1, gw), F32)] + ends[:-1])

    qs = (q * jnp.exp(cum)).astype(BF16)
    koff = kk * jnp.exp(per_sub(tots) - lc)
    ke = (koff * per_sub([jnp.exp(cum_l - e) for e in ends])).astype(BF16)
    koff_b = koff.astype(BF16)
    qoff = []
    kmask = []
    for b in range(nsub - 1):
        r0 = (b + 1) * HG_SUB
        qb = (q[r0:, :] * jnp.exp(cum[r0:, :] - ends[b])).astype(BF16)
        qoff.append(jnp.concatenate([jnp.zeros((r0, gw), BF16), qb], axis=0))
        parts = [koff_b[b * HG_SUB:r0, :], jnp.zeros((chunk - r0, gw), BF16)]
        if b > 0:
            parts.insert(0, jnp.zeros((b * HG_SUB, gw), BF16))
        kmask.append(jnp.concatenate(parts, axis=0))
    vb = v.astype(BF16)
    decay_l = jnp.exp(cum_l)

    for h in range(G_HEADS):
        sl = slice(h * HEAD_DIM, (h + 1) * HEAD_DIM)
        qcat = jnp.concatenate([qo[:, sl] for qo in qoff], axis=1)
        kcat = jnp.concatenate([km[:, sl] for km in kmask], axis=1)
        a = _hgrn_diag_scores(q, kk, lc, h) + _dot_nt(qcat, kcat).astype(BF16)
        st = st_ref[h]
        oa_ref[:, sl] = jnp.dot(a, vb[:, sl], preferred_element_type=F32) + _dot_nt(qs[:, sl], st.astype(BF16))
        upd = jnp.dot(v[:, sl].T.astype(BF16), ke[:, sl], preferred_element_type=F32)
        st_ref[h] = st * decay_l[:, sl] + upd

    @pl.when(need_ref[0] != 0)
    def _():
        q2, kk2, v2, lc2, rsub2 = _hgrn_gates(hq_ref, hf_ref, hi_ref, lb_ref)

        @pl.when(jnp.max(-lc2) > EXP_CLAMP)
        def _():
            v2b = v2.astype(BF16)
            fast = [jnp.dot(_hgrn_diag_scores(q2, kk2, lc2, h), v2b[:, h * HEAD_DIM:(h + 1) * HEAD_DIM],
                            preferred_element_type=F32) for h in range(G_HEADS)]
            oa_ref[...] += _hgrn_diag_exact(q2, kk2, v2, lc2, rsub2, nsub) - jnp.concatenate(fast, axis=1)

    o = oa_ref[...]
    ms = jnp.mean(o * o, axis=-1, keepdims=True)
    hg = hg_ref[...].astype(F32)
    y = o * lax.rsqrt(ms + EPS) * gn_ref[...] * (hg * _sigmoid(hg))
    o_ref[...] = y.astype(o_ref.dtype)


def _hgrn(ph, pt, lb, g_hnorm, batch, seq, chunk):
    n = batch * seq
    gw = G_HEADS * HEAD_DIM
    rows = HG_CHUNKS_PER_STEP * chunk
    ncb = seq // rows
    blk = lambda c: pl.BlockSpec((rows, gw), lambda b, i, nd: (b * ncb + i, c))
    need = (HG_SUB * jnp.log(jnp.min(lb)) < -EXP_CLAMP).astype(I32).reshape(1)
    grid_spec = pltpu.PrefetchScalarGridSpec(
        num_scalar_prefetch=1,
        grid=(batch, ncb),
        in_specs=[blk(0), blk(1), blk(1), blk(2),
                  pl.BlockSpec((1, gw), lambda b, i, nd: (0, 0)),
                  pl.BlockSpec((1, gw), lambda b, i, nd: (0, 0))],
        out_specs=pl.BlockSpec((rows, gw), lambda b, i, nd: (b * ncb + i, 0)),
        scratch_shapes=[pltpu.VMEM((G_HEADS, HEAD_DIM, HEAD_DIM), F32),
                        pltpu.VMEM((chunk, gw), F32)],
    )
    return pl.pallas_call(
        functools.partial(_hgrn_kernel, chunk=chunk),
        grid_spec=grid_spec,
        out_shape=jax.ShapeDtypeStruct((n, gw), BF16),
        compiler_params=_cparams(("parallel", "arbitrary"), 32),
        name="hgrn",
    )(need, pt, ph, pt, pt, lb.reshape(1, gw), g_hnorm.reshape(1, gw))


def _merge_kernel(ya_ref, yr_ref, ga_ref, gb_ref, x_ref, wpa_ref, wpb_ref, wo_ref, o_ref, mg_ref):
    c = pl.program_id(1)
    nc = pl.num_programs(1)
    tc = ga_ref.shape[1]

    ma = jnp.dot(ya_ref[...], wpa_ref[c], preferred_element_type=F32)
    mb = jnp.dot(yr_ref[...], wpb_ref[c], preferred_element_type=F32)
    merged = _sigmoid(ga_ref[...].astype(F32)) * ma + _sigmoid(gb_ref[...].astype(F32)) * mb
    merged = merged.astype(BF16)
    for k in range(mg_ref.shape[1] // tc):
        @pl.when(c == k)
        def _():
            mg_ref[:, k * tc:(k + 1) * tc] = merged

    @pl.when(c == nc - 1)
    def _():
        o_ref[...] = x_ref[...] + jnp.dot(mg_ref[...], wo_ref[...], preferred_element_type=F32)


def _merge(y_att, y_rec, pt, x2, w_pa, w_pb, w_o, tm, tc):
    n, d = x2.shape
    aw = y_att.shape[1]
    gw = y_rec.shape[1]
    ga0 = 3072 // tc
    gb0 = 5120 // tc

    def col_blocks(w):
        return jnp.swapaxes(w.reshape(w.shape[0], d // tc, tc), 0, 1)

    return pl.pallas_call(
        _merge_kernel,
        grid=(n // tm, d // tc),
        in_specs=[
            pl.BlockSpec((tm, aw), lambda i, c: (i, 0)),
            pl.BlockSpec((tm, gw), lambda i, c: (i, 0)),
            pl.BlockSpec((tm, tc), lambda i, c: (i, ga0 + c)),
            pl.BlockSpec((tm, tc), lambda i, c: (i, gb0 + c)),
            pl.BlockSpec((tm, d), lambda i, c: (i, 0)),
            pl.BlockSpec((d // tc, aw, tc), lambda i, c: (0, 0, 0)),
            pl.BlockSpec((d // tc, gw, tc), lambda i, c: (0, 0, 0)),
            pl.BlockSpec((d, d), lambda i, c: (0, 0)),
        ],
        out_specs=pl.BlockSpec((tm, d), lambda i, c: (i, 0)),
        out_shape=jax.ShapeDtypeStruct((n, d), F32),
        scratch_shapes=[pltpu.VMEM((tm, d), BF16)],
        compiler_params=_cparams(("parallel", "arbitrary"), 56),
        name="merge",
    )(y_att, y_rec, pt, pt, x2, col_blocks(w_pa), col_blocks(w_pb), w_o)


def _router_kernel(x_ref, g_ref, wr_ref, br_ref, h_ref, r_ref, cnt_ref):
    x = x_ref[...]
    ms = jnp.mean(x * x, axis=-1, keepdims=True)
    h = x * lax.rsqrt(ms + EPS) * g_ref[...]
    h_ref[...] = _pack_halves(h)
    lg = jnp.dot(h.astype(BF16), wr_ref[...], preferred_element_type=F32) + br_ref[...]
    tm = lg.shape[0]
    lane = lax.broadcasted_iota(I32, (tm, LANES), 1)
    neg = jnp.float32(-1e30)
    big = jnp.float32(LANES)
    is_g = lane < N_GROUPS
    lgg = jnp.where(is_g, lg, neg)
    mg = jnp.max(lgg, axis=1, keepdims=True)
    zg = jnp.sum(jnp.where(is_g, jnp.exp(lgg - mg), 0.0), axis=1, keepdims=True)
    p_g = 1.0 / zg
    grp = jnp.min(jnp.where(is_g & (lgg == mg), lane.astype(F32), big), axis=1, keepdims=True)
    e_id = lane - N_GROUPS
    e_grp = lax.shift_right_arithmetic(e_id, jnp.int32(EXP_PER_GROUP.bit_length() - 1)).astype(F32)
    e_idf = e_id.astype(F32)
    is_e = (e_id >= 0) & (e_id < N_EXPERTS) & (e_grp == grp)
    lge = jnp.where(is_e, lg, neg)
    m1 = jnp.max(lge, axis=1, keepdims=True)
    i1 = jnp.min(jnp.where(is_e & (lge == m1), e_idf, big), axis=1, keepdims=True)
    is_e2 = is_e & (e_idf != i1)
    lge2 = jnp.where(is_e2, lg, neg)
    m2 = jnp.max(lge2, axis=1, keepdims=True)
    i2 = jnp.min(jnp.where(is_e2 & (lge2 == m2), e_idf, big), axis=1, keepdims=True)
    ex2 = jnp.exp(m2 - m1)
    den = 1.0 + ex2
    g1 = p_g / den
    g2 = p_g * ex2 / den
    @pl.when(pl.program_id(0) == 0)
    def _():
        cnt_ref[...] = jnp.zeros(cnt_ref.shape, F32)

    lanef = lane.astype(F32)
    oh1 = lanef == i1
    oh2 = lanef == i2
    oh = jnp.where(oh1 | oh2, 1.0, 0.0)
    ti = lax.broadcasted_iota(I32, (tm, tm), 0)
    si = lax.broadcasted_iota(I32, (tm, tm), 1)
    tri = jnp.where(si < ti, 1.0, 0.0).astype(BF16)
    before = jnp.dot(tri, oh.astype(BF16), preferred_element_type=F32) + cnt_ref[...]
    rank1 = jnp.sum(jnp.where(oh1, before, 0.0), axis=1, keepdims=True)
    rank2 = jnp.sum(jnp.where(oh2, before, 0.0), axis=1, keepdims=True)
    cnt_ref[...] += jnp.sum(oh, axis=0, keepdims=True)

    r = jnp.where(lane == 0, i1, 0.0)
    r = jnp.where(lane == 1, i2, r)
    r = jnp.where(lane == 2, g1, r)
    r = jnp.where(lane == 3, g2, r)
    r = jnp.where(lane == 4, rank1, r)
    r = jnp.where(lane == 5, rank2, r)
    r_ref[...] = r


def _router(x1, g, w_r, b_r, tm):
    n, d = x1.shape
    return pl.pallas_call(
        _router_kernel,
        grid=(n // tm,),
        in_specs=[
            pl.BlockSpec((tm, d), lambda i: (i, 0)),
            pl.BlockSpec((1, d), lambda i: (0, 0)),
            pl.BlockSpec((d, LANES), lambda i: (0, 0)),
            pl.BlockSpec((1, LANES), lambda i: (0, 0)),
        ],
        out_specs=[pl.BlockSpec((tm, d // 2), lambda i: (i, 0)),
                   pl.BlockSpec((tm, LANES), lambda i: (i, 0)),
                   pl.BlockSpec((1, LANES), lambda i: (0, 0))],
        out_shape=[jax.ShapeDtypeStruct((n, d // 2), U32),
                   jax.ShapeDtypeStruct((n, LANES), F32),
                   jax.ShapeDtypeStruct((1, LANES), F32)],
        compiler_params=_cparams(("arbitrary",), 40),
        name="router",
    )(x1, g.reshape(1, d), w_r, b_r)


def _dispatch_kernel(prow_ref, h_ref, xs_hbm, sem, *, tm):
    base = 2 * tm * pl.program_id(0)

    def scatter_group(g, carry):
        r0 = pl.multiple_of(g * DMA_GROUP, DMA_GROUP)
        rows = h_ref.at[pl.ds(r0, DMA_GROUP), :]
        for u in range(DMA_GROUP):
            for s in range(2):
                dst_row = prow_ref[base + 2 * r0 + (2 * u + s)]
                pltpu.make_async_copy(rows.at[pl.ds(u, 1), :], xs_hbm.at[pl.ds(dst_row, 1), :],
                                      sem).start(priority=s)
        return carry

    lax.fori_loop(0, tm // DMA_GROUP, scatter_group, 0)
    for _ in range(2):
        pltpu.make_async_copy(h_ref, xs_hbm.at[pl.ds(0, tm), :], sem).wait()


def _dispatch(h2, prow, tm):
    n, d = h2.shape
    grid_spec = pltpu.PrefetchScalarGridSpec(
        num_scalar_prefetch=1,
        grid=(n // tm,),
        in_specs=[pl.BlockSpec((tm, d), lambda i, pr: (i, 0))],
        out_specs=pl.BlockSpec(memory_space=pl.ANY),
        scratch_shapes=[pltpu.SemaphoreType.DMA(())],
    )
    return pl.pallas_call(
        functools.partial(_dispatch_kernel, tm=tm),
        grid_spec=grid_spec,
        out_shape=jax.ShapeDtypeStruct((2 * n, d), h2.dtype),
        compiler_params=_cparams(("arbitrary",), 32),
        name="dispatch",
    )(prow, h2)


def _moe_kernel(pt_ref, pe_ref, plo_ref, phi_ref, slot_ref, nxt_ref, np_ref, x_ref, wg_hbm, wu_hbm, wd_hbm,
                y_ref, wgf_ref, wuf_ref, wdf_ref, wgb_ref, wub_ref, wdb_ref, yacc_ref, sem):
    p = pl.program_id(0)
    active = p < np_ref[0]
    prev = jnp.maximum(p - 1, 0)
    new_expert = (p == 0) | (pe_ref[p] != pe_ref[prev])
    new_tile = (p == 0) | (pt_ref[p] != pt_ref[prev])
    slot = slot_ref[p]

    def weight_copies(e, s):
        return (pltpu.make_async_copy(wg_hbm.at[e], wgf_ref.at[s], sem.at[s]),
                pltpu.make_async_copy(wu_hbm.at[e], wuf_ref.at[s], sem.at[s]),
                pltpu.make_async_copy(wd_hbm.at[e], wdf_ref.at[s], sem.at[s]))

    @pl.when(p == 0)
    def _():
        for cp in weight_copies(pe_ref[0], 0):
            cp.start()

    @pl.when(active & new_expert)
    def _():
        for cp in weight_copies(pe_ref[p], slot):
            cp.wait()
        wgb_ref[...] = wgf_ref[slot].astype(BF16)
        wub_ref[...] = wuf_ref[slot].astype(BF16)
        wdb_ref[...] = wdf_ref[slot].astype(BF16)

        @pl.when(nxt_ref[p] >= 0)
        def _():
            for cp in weight_copies(nxt_ref[p], 1 - slot):
                cp.start()

    @pl.when(active)
    def _():
        tm, half = x_ref.shape
        lo, hi = _unpack_halves(x_ref[...])
        xl = lo.astype(BF16)
        xh = hi.astype(BF16)
        a = (jnp.dot(xl, wgb_ref[0:half, :], preferred_element_type=F32)
             + jnp.dot(xh, wgb_ref[half:, :], preferred_element_type=F32))
        u = (jnp.dot(xl, wub_ref[0:half, :], preferred_element_type=F32)
             + jnp.dot(xh, wub_ref[half:, :], preferred_element_type=F32))
        hmid = (a * _sigmoid(a) * u).astype(BF16)
        yv = jnp.dot(hmid, wdb_ref[...], preferred_element_type=F32)
        row = lax.broadcasted_iota(I32, (tm, 1), 0)
        mine = (row >= plo_ref[p]) & (row < phi_ref[p])
        yv = jnp.where(mine, yv, 0.0)

        @pl.when(new_tile)
        def _():
            yacc_ref[...] = yv

        @pl.when(jnp.logical_not(new_tile))
        def _():
            yacc_ref[...] += yv

        nxt_p = jnp.minimum(p + 1, pl.num_programs(0) - 1)
        last_of_tile = (p == np_ref[0] - 1) | (pt_ref[nxt_p] != pt_ref[p])

        @pl.when(last_of_tile)
        def _():
            y_ref[...] = _pack_halves(yacc_ref[...])


def _moe(xs, pairs, w_gate, w_up, w_down, tm):
    pair_tile, pair_expert, pair_lo, pair_hi, pair_slot, pair_next, n_pairs = pairs
    na, half = xs.shape
    d, de = w_gate.shape[1:]
    max_pairs = pair_tile.shape[0]
    grid_spec = pltpu.PrefetchScalarGridSpec(
        num_scalar_prefetch=7,
        grid=(max_pairs,),
        in_specs=[
            pl.BlockSpec((tm, half), lambda p, pt, *_: (pt[p], 0)),
            pl.BlockSpec(memory_space=pl.ANY),
            pl.BlockSpec(memory_space=pl.ANY),
            pl.BlockSpec(memory_space=pl.ANY),
        ],
        out_specs=pl.BlockSpec((tm, half), lambda p, pt, *_: (pt[p], 0)),
        scratch_shapes=[
            pltpu.VMEM((2, d, de), F32),
            pltpu.VMEM((2, d, de), F32),
            pltpu.VMEM((2, de, d), F32),
            pltpu.VMEM((d, de), BF16),
            pltpu.VMEM((d, de), BF16),
            pltpu.VMEM((de, d), BF16),
            pltpu.VMEM((tm, d), F32),
            pltpu.SemaphoreType.DMA((2,)),
        ],
    )
    return pl.pallas_call(
        _moe_kernel,
        grid_spec=grid_spec,
        out_shape=jax.ShapeDtypeStruct((na, half), U32),
        compiler_params=_cparams(("arbitrary",), 56),
        name="moe",
    )(pair_tile, pair_expert, pair_lo, pair_hi, pair_slot, pair_next, n_pairs, xs, w_gate, w_up, w_down)


def _moe_rows(r, cnt):
    sizes = cnt[0, :N_EXPERTS].astype(I32)
    starts = jnp.cumsum(sizes) - sizes
    e12 = r[:, 0:2].astype(I32)
    rank = r[:, 4:6].astype(I32)
    onehot = e12[:, :, None] == jnp.arange(N_EXPERTS, dtype=I32)[None, None, :]
    start_a = jnp.sum(jnp.where(onehot, starts[None, None, :], 0), axis=-1)
    return (start_a + rank).reshape(-1).astype(I32), sizes


def _count_le(sorted_vals, x):
    return jnp.sum((sorted_vals[None, :] <= x[:, None]).astype(I32), axis=1)


def _moe_pairs(sizes, na, tm):
    ntiles = na // tm
    max_pairs = ntiles + N_EXPERTS - 1
    ends = jnp.cumsum(sizes)
    starts = ends - sizes
    t0 = jnp.arange(ntiles, dtype=I32) * tm
    e_lo = _count_le(ends, t0)
    e_hi = _count_le(ends, t0 + (tm - 1))
    per_tile = e_hi - e_lo + 1
    pend = jnp.cumsum(per_tile)
    pstart = pend - per_tile
    n_pairs = pend[-1]
    pc = jnp.minimum(jnp.arange(max_pairs, dtype=I32), n_pairs - 1)
    tile = _count_le(pend, pc)
    expert = e_lo[tile] + (pc - pstart[tile])
    lo = jnp.clip(starts[expert] - tile * tm, 0, tm).astype(I32)
    hi = jnp.clip(ends[expert] - tile * tm, 0, tm).astype(I32)
    expert = expert.astype(I32)
    first = jnp.concatenate([jnp.ones((1,), bool), expert[1:] != expert[:-1]])
    slot = ((jnp.cumsum(first) - 1) % 2).astype(I32)
    idx = jnp.arange(max_pairs, dtype=I32)
    first_at = jnp.where(first, idx, max_pairs)
    nxt_first = jnp.concatenate([lax.cummin(first_at[::-1])[::-1][1:], jnp.full((1,), max_pairs, I32)])
    nxt = jnp.where(nxt_first < max_pairs, expert[jnp.minimum(nxt_first, max_pairs - 1)], -1).astype(I32)
    return tile, expert, lo, hi, slot, nxt, n_pairs.astype(I32).reshape(1)


def _final_kernel(prow_ref, x_ref, r_ref, g_ref, y_hbm, o_ref, ybuf, sem, *, tm):
    i = pl.program_id(0)
    cur = i % 2

    def issue(step, buf):
        base = 2 * tm * step

        def gather_group(g, carry):
            r0 = pl.multiple_of(g * DMA_GROUP, DMA_GROUP)
            for s in range(2):
                rows = ybuf.at[buf, s, pl.ds(r0, DMA_GROUP), :]
                for u in range(DMA_GROUP):
                    src_row = prow_ref[base + 2 * r0 + (2 * u + s)]
                    pltpu.make_async_copy(y_hbm.at[pl.ds(src_row, 1), :], rows.at[pl.ds(u, 1), :],
                                          sem.at[buf]).start(priority=u % 2)
            return carry

        lax.fori_loop(0, tm // DMA_GROUP, gather_group, 0)

    @pl.when(i == 0)
    def _():
        issue(0, 0)

    @pl.when(i + 1 < pl.num_programs(0))
    def _():
        issue(i + 1, 1 - cur)

    for s in range(2):
        pltpu.make_async_copy(y_hbm.at[pl.ds(0, tm), :], ybuf.at[cur, s], sem.at[cur]).wait()
    r = r_ref[...]
    g1 = r[:, 2:3]
    g2 = r[:, 3:4]
    half = ybuf.shape[-1]
    lo0, hi0 = _unpack_halves(ybuf[cur, 0])
    lo1, hi1 = _unpack_halves(ybuf[cur, 1])
    xa = x_ref[:, :half] + g1 * lo0 + g2 * lo1
    xb = x_ref[:, half:] + g1 * hi0 + g2 * hi1
    ms = (jnp.sum(xa * xa, axis=-1, keepdims=True) + jnp.sum(xb * xb, axis=-1, keepdims=True)) / (2 * half)
    rs = lax.rsqrt(ms + EPS)
    o_ref[:, :half] = xa * rs * g_ref[:, :half]
    o_ref[:, half:] = xb * rs * g_ref[:, half:]


def _final(x1, y, r, prow, g, tm):
    n, d = x1.shape
    grid_spec = pltpu.PrefetchScalarGridSpec(
        num_scalar_prefetch=1,
        grid=(n // tm,),
        in_specs=[
            pl.BlockSpec((tm, d), lambda i, pr: (i, 0)),
            pl.BlockSpec((tm, LANES), lambda i, pr: (i, 0)),
            pl.BlockSpec((1, d), lambda i, pr: (0, 0)),
            pl.BlockSpec(memory_space=pl.ANY),
        ],
        out_specs=pl.BlockSpec((tm, d), lambda i, pr: (i, 0)),
        scratch_shapes=[pltpu.VMEM((2, 2, tm, d // 2), U32), pltpu.SemaphoreType.DMA((2,))],
    )
    return pl.pallas_call(
        functools.partial(_final_kernel, tm=tm),
        grid_spec=grid_spec,
        out_shape=jax.ShapeDtypeStruct((n, d), F32),
        compiler_params=_cparams(("arbitrary",), 48),
        name="final",
    )(prow, x1, r, g.reshape(1, d), y)


def _pack_w_in(w):
    d, din = w.shape
    tc = 256
    wt = jnp.swapaxes(w, 0, 1)

    def pack_kernel(w_ref, o_ref):
        o_ref[0:912, :] = w_ref[0:912, :].astype(BF16)
        o_ref[912:1024, :] = jnp.zeros((LANES - IDX_HEADS, tc), BF16)
        o_ref[1024:2048, :] = w_ref[1936:2960, :].astype(BF16)
        o_ref[2048:3072, :] = w_ref[912:1936, :].astype(BF16)
        o_ref[3072:, :] = w_ref[2960:, :].astype(BF16)

    return pl.pallas_call(
        pack_kernel,
        grid=(d // tc,),
        in_specs=[pl.BlockSpec((din, tc), lambda i: (0, i))],
        out_specs=pl.BlockSpec((PH_COLS + PT_COLS, tc), lambda i: (0, i)),
        out_shape=jax.ShapeDtypeStruct((PH_COLS + PT_COLS, d), BF16),
        compiler_params=_cparams(("parallel",), 48),
        name="pack_w_in",
    )(wt)


def _rope_tables(seq):
    inv = 1.0 / (ROPE_THETA ** (jnp.arange(0, HEAD_DIM, 2, dtype=F32) / HEAD_DIM))
    ang = jnp.arange(seq, dtype=F32)[:, None] * inv[None, :]
    c, s = jnp.cos(ang), jnp.sin(ang)
    c2, s2 = jnp.concatenate([c, c], axis=1), jnp.concatenate([-s, s], axis=1)
    return c2, s2, c2.T, s2.T


class _Tiles(NamedTuple):
    proj_rows: int
    proj_cols: int
    att_q: int
    att_k: int
    rec_chunk: int
    merge_rows: int
    merge_cols: int
    tok_rows: int
    moe_rows: int


def _tiles(n, seq):
    return _Tiles(proj_rows=min(1024, n), proj_cols=1024, att_q=min(256, seq), att_k=min(256, seq), rec_chunk=64,
                  merge_rows=min(512, n), merge_cols=1024, tok_rows=min(512, n), moe_rows=256)


def _layer(x2, batch, seq, rope, lb, g_norm1, w_in, g_cq, w_uq, w_qidx, g_hnorm, w_pa, w_pb, w_o,
           g_norm2, w_grp, b_grp, w_exp, b_exp, w_gate, w_up, w_down, g_out):
    n, d = x2.shape
    t = _tiles(n, seq)
    ph, pt = _in_proj(x2, g_norm1, _pack_w_in(w_in), tm=t.proj_rows, tn=t.proj_cols)
    qt, qit, wt, k, ki, vt = _dsa_prep(ph, rope, g_cq, w_uq.T.astype(BF16), w_qidx.T.astype(BF16), seq,
                                       tm=t.att_k, tq=t.att_q)
    y_att = _dsa(qit, qt, wt, ki, k, vt, batch, seq, tq=t.att_q, tk=t.att_k)
    y_rec = _hgrn(ph, pt, lb, g_hnorm, batch, seq, chunk=t.rec_chunk)
    x1 = _merge(y_att, y_rec, pt, x2, w_pa.astype(BF16), w_pb.astype(BF16), w_o.astype(BF16),
                tm=t.merge_rows, tc=t.merge_cols)
    w_r = jnp.concatenate([w_grp, w_exp, jnp.zeros((d, LANES - N_GROUPS - N_EXPERTS), F32)], axis=1).astype(BF16)
    b_r = jnp.concatenate([b_grp, b_exp, jnp.zeros((LANES - N_GROUPS - N_EXPERTS,), F32)]).reshape(1, LANES)
    h2, r, cnt = _router(x1, g_norm2, w_r, b_r, tm=t.tok_rows)
    prow, sizes = _moe_rows(r, cnt)
    xs = _dispatch(h2, prow, tm=t.tok_rows)
    assert (2 * n) % t.moe_rows == 0
    y = _moe(xs, _moe_pairs(sizes, 2 * n, t.moe_rows), w_gate, w_up, w_down, t.moe_rows)
    return _final(x1, y, r, prow, g_out, tm=t.tok_rows)


def kernel(x, g_norm1, w_in, g_cq, w_uq, w_qidx, lb_logits, g_hnorm, w_pa, w_pb, w_o, g_norm2, w_grp, b_grp,
           w_exp, b_exp, w_gate, w_up, w_down, g_final):
    batch, seq, d = x.shape
    depth = g_norm1.shape[0]
    rope = _rope_tables(seq)
    lb_all = jnp.cumsum(jax.nn.softmax(lb_logits.astype(F32), axis=0), axis=0)
    x2 = x.reshape(batch * seq, d)
    for l in range(depth):
        assert depth == 1
        x2 = _layer(x2, batch, seq, rope, lb_all[l], g_norm1[l], w_in[l], g_cq[l], w_uq[l], w_qidx[l],
                    g_hnorm[l], w_pa[l], w_pb[l], w_o[l], g_norm2[l], w_grp[l], b_grp[l], w_exp[l], b_exp[l],
                    w_gate[l], w_up[l], w_down[l], g_final)
    return x2.reshape(batch, seq, d)
```

```python
import functools
import math
from typing import NamedTuple

import jax
import jax.numpy as jnp
from jax import lax
from jax.experimental import pallas as pl
from jax.experimental.pallas import tpu as pltpu

F32 = jnp.float32
BF16 = jnp.bfloat16
I32 = jnp.int32
U32 = jnp.uint32

EPS = 1e-6
ROPE_THETA = 10000.0
LANES = 128
MXU_LANES = 256
DMA_GROUP = 32
VT_ROWS = 128 + 16

A_HEADS = 8
HEAD_DIM = 128
IDX_HEADS = 16
Q_RANK = 512
TOPK_MAX = 256
G_HEADS = 8
N_GROUPS = 4
EXP_PER_GROUP = 8
N_EXPERTS = N_GROUPS * EXP_PER_GROUP

INT_MIN = -2147483648
EXP_CLAMP = 80.0


def _cparams(sem, vmem_mb):
    return pltpu.CompilerParams(dimension_semantics=sem, vmem_limit_bytes=vmem_mb << 20)


def _pack_halves(x):
    c = x.shape[1] // 2
    bits = lax.bitcast_convert_type(x.astype(BF16).astype(F32), U32)
    return lax.shift_right_logical(bits[:, :c], jnp.uint32(16)) | (bits[:, c:] & jnp.uint32(0xFFFF0000))


def _unpack_halves(w):
    lo = lax.bitcast_convert_type(lax.shift_left(w, jnp.uint32(16)), F32)
    hi = lax.bitcast_convert_type(w & jnp.uint32(0xFFFF0000), F32)
    return lo, hi


def _sigmoid(x):
    return 0.5 * jnp.tanh(0.5 * x) + 0.5


PH_COLS = 2048
PT_COLS = 7168


def _inproj_kernel(x_ref, g_ref, w_ref, oh_ref, ot_ref, xn_ref, *, nh):
    j = pl.program_id(1)

    @pl.when(j == 0)
    def _():
        x = x_ref[...]
        ms = jnp.mean(x * x, axis=-1, keepdims=True)
        xn_ref[...] = (x * lax.rsqrt(ms + EPS) * g_ref[...]).astype(BF16)

    y = _dot_nt(xn_ref[...], w_ref[...])

    @pl.when(j < nh)
    def _():
        oh_ref[...] = y

    @pl.when(j >= nh)
    def _():
        ot_ref[...] = y.astype(ot_ref.dtype)


def _in_proj(x2, g, w, tm, tn):
    n, d = x2.shape
    nh = PH_COLS // tn
    nt = PT_COLS // tn
    return pl.pallas_call(
        functools.partial(_inproj_kernel, nh=nh),
        grid=(n // tm, nh + nt),
        in_specs=[
            pl.BlockSpec((tm, d), lambda i, j: (i, 0)),
            pl.BlockSpec((1, d), lambda i, j: (0, 0)),
            pl.BlockSpec((tn, d), lambda i, j: (j, 0)),
        ],
        out_specs=[pl.BlockSpec((tm, tn), lambda i, j: (i, jnp.minimum(j, nh - 1))),
                   pl.BlockSpec((tm, tn), lambda i, j: (i, jnp.maximum(j - nh, 0)))],
        out_shape=[jax.ShapeDtypeStruct((n, PH_COLS), F32),
                   jax.ShapeDtypeStruct((n, PT_COLS), BF16)],
        scratch_shapes=[pltpu.VMEM((tm, d), BF16)],
        compiler_params=_cparams(("parallel", "arbitrary"), 56),
        name="in_proj",
    )(x2, g.reshape(1, d), w)


def _rope(x, c2, s2):
    return x * c2 + pltpu.roll(x, HEAD_DIM // 2, 1) * s2


def _rope_t(x, c2t, s2t):
    half = HEAD_DIM // 2
    return x * c2t + jnp.concatenate([x[half:, :], x[:half, :]], axis=0) * s2t


def _prep_kernel(cq_ref, ka_ref, va_ref, kx_ref, wi_ref, cos_ref, sin_ref, cost_ref, sint_ref, g_ref, wuq_ref, wqi_ref,
                 qt_ref, qit_ref, wt_ref, k_ref, ki_ref, vt_ref, *, scale, wscale, tq):
    nqb = cq_ref.shape[0] // tq
    c2 = cos_ref[...]
    s2 = sin_ref[...]
    c2t = cost_ref[...]
    s2t = sint_ref[...]
    cq = cq_ref[...]
    ms = jnp.mean(cq * cq, axis=-1, keepdims=True)
    cqn = (cq * lax.rsqrt(ms + EPS) * g_ref[...]).astype(BF16)
    q_t = _dot_nt(wuq_ref[...], cqn)
    for h in range(A_HEADS):
        qh_t = _rope_t(q_t[h * HEAD_DIM:(h + 1) * HEAD_DIM, :], c2t, s2t) * scale
        for b in range(nqb):
            qt_ref[b, :, h * tq:(h + 1) * tq] = qh_t[:, b * tq:(b + 1) * tq].astype(BF16)
    qi_t = _dot_nt(wqi_ref[...], cqn)
    for h in range(IDX_HEADS):
        qh_t = _rope_t(qi_t[h * HEAD_DIM:(h + 1) * HEAD_DIM, :], c2t, s2t)
        for b in range(nqb):
            qit_ref[b, :, h * tq:(h + 1) * tq] = qh_t[:, b * tq:(b + 1) * tq].astype(BF16)
    w_t = (wi_ref[...] * wscale).T
    for h in range(IDX_HEADS):
        for b in range(nqb):
            wt_ref[b, :, h * tq:(h + 1) * tq] = w_t[h:h + 1, b * tq:(b + 1) * tq]
    k_ref[...] = _rope(ka_ref[...], c2, s2).astype(BF16)
    ki_ref[...] = _rope(kx_ref[...], c2, s2).astype(BF16)
    vt_ref[0, 0:HEAD_DIM, :] = va_ref[...].T.astype(BF16)
    ones_row = lax.broadcasted_iota(I32, (VT_ROWS - HEAD_DIM, va_ref.shape[0]), 0) == 0
    vt_ref[0, HEAD_DIM:VT_ROWS, :] = jnp.where(ones_row, 1.0, 0.0).astype(BF16)


def _dsa_prep(p, rope, g_cq, w_uq_t, w_qidx_t, seq, tm, tq):
    n = p.shape[0]
    nsb = seq // tm
    nqb = tm // tq
    aw = A_HEADS * HEAD_DIM
    iw = IDX_HEADS * HEAD_DIM
    row = lambda i: (i, 0)
    blk3 = lambda i: (i, 0, 0)
    kern = functools.partial(_prep_kernel, scale=HEAD_DIM ** -0.5 * math.log2(math.e),
                             wscale=IDX_HEADS ** -0.5 * HEAD_DIM ** -0.5, tq=tq)
    return pl.pallas_call(
        kern,
        grid=(n // tm,),
        in_specs=[
            pl.BlockSpec((tm, Q_RANK), lambda i: (i, 0)),
            pl.BlockSpec((tm, LANES), lambda i: (i, 4)),
            pl.BlockSpec((tm, LANES), lambda i: (i, 5)),
            pl.BlockSpec((tm, LANES), lambda i: (i, 6)),
            pl.BlockSpec((tm, LANES), lambda i: (i, 7)),
            pl.BlockSpec((tm, LANES), lambda i: (i % nsb, 0)),
            pl.BlockSpec((tm, LANES), lambda i: (i % nsb, 0)),
            pl.BlockSpec((HEAD_DIM, tm), lambda i: (0, i % nsb)),
            pl.BlockSpec((HEAD_DIM, tm), lambda i: (0, i % nsb)),
            pl.BlockSpec((1, Q_RANK), lambda i: (0, 0)),
            pl.BlockSpec((aw, Q_RANK), lambda i: (0, 0)),
            pl.BlockSpec((iw, Q_RANK), lambda i: (0, 0)),
        ],
        out_specs=[
            pl.BlockSpec((nqb, HEAD_DIM, A_HEADS * tq), blk3),
            pl.BlockSpec((nqb, HEAD_DIM, IDX_HEADS * tq), blk3),
            pl.BlockSpec((nqb, 1, IDX_HEADS * tq), blk3),
            pl.BlockSpec((tm, LANES), row),
            pl.BlockSpec((tm, LANES), row),
            pl.BlockSpec((1, VT_ROWS, tm), blk3),
        ],
        out_shape=[
            jax.ShapeDtypeStruct((n // tq, HEAD_DIM, A_HEADS * tq), BF16),
            jax.ShapeDtypeStruct((n // tq, HEAD_DIM, IDX_HEADS * tq), BF16),
            jax.ShapeDtypeStruct((n // tq, 1, IDX_HEADS * tq), F32),
            jax.ShapeDtypeStruct((n, LANES), BF16),
            jax.ShapeDtypeStruct((n, LANES), BF16),
            jax.ShapeDtypeStruct((n // tm, VT_ROWS, tm), BF16),
        ],
        compiler_params=_cparams(("parallel",), 40),
        name="dsa_prep",
    )(p, p, p, p, p, *rope, g_cq.reshape(1, Q_RANK), w_uq_t, w_qidx_t)


def _dot_nt(a, b):
    return lax.dot_general(a, b, (((1,), (1,)), ((), ())), preferred_element_type=F32)


def _dsa_kernel(qit_ref, qt_ref, wt_ref, ki_ref, k_ref, vt_ref, o_ref,
                key_ref, sa_ref, sb_ref, pa_ref, pb_ref, acc_ref, *, tq, tk, topk, idx_bits):
    i = pl.program_id(1)
    nkc = ((i + 1) * tq + tk - 1) // tk
    q_pos = i * tq + lax.broadcasted_iota(I32, (tk, tq), 1)
    k_off = lax.broadcasted_iota(I32, (tk, tq), 0)

    def score_chunks(j0, nch):
        kic = ki_ref[pl.ds(pl.multiple_of(j0 * tk, tk), nch * tk), :]
        acc = jnp.zeros((nch * tk, tq), F32)
        hpm = max(1, MXU_LANES // tq)
        for h0 in range(0, IDX_HEADS, hpm):
            sl = slice(h0 * tq, (h0 + hpm) * tq)
            lg = jnp.dot(kic, qit_ref[0, :, sl], preferred_element_type=F32)
            x = jnp.maximum(lg, 0.0) * wt_ref[0, :, sl]
            for u in range(hpm):
                acc = acc + x[:, u * tq:(u + 1) * tq]
        acc = jnp.where(acc == 0.0, 0.0, acc)
        bits = lax.bitcast_convert_type(acc, I32)
        key = jnp.where(bits < 0, bits ^ jnp.int32(0x7FFFFFFF), bits)
        for c in range(nch):
            causal = ((j0 + c) * tk + k_off) <= q_pos
            key_ref[j0 + c] = jnp.where(causal, key[c * tk:(c + 1) * tk, :], jnp.int32(INT_MIN))

    lax.fori_loop(0, nkc // 2, lambda t, c: (score_chunks(2 * t, 2), c)[1], 0)

    @pl.when(nkc % 2 == 1)
    def _():
        score_chunks(nkc - 1, 1)

    def count_rows(pred):
        def cnt_chunk(j, part):
            hit = pred(j, key_ref[j]).astype(I32)
            return part + jnp.sum(hit.reshape(tk // 8, 8, tq), axis=0)

        part = lax.fori_loop(0, nkc, cnt_chunk, jnp.zeros((8, tq), I32))
        return jnp.sum(part, axis=0, keepdims=True)

    def bit_step(b, c):
        t_u, cnt_t = c
        bit = jnp.left_shift(jnp.int32(1), 31 - b)
        cand_s = (t_u | bit) ^ jnp.int32(INT_MIN)
        cnt = count_rows(lambda j, key: key >= cand_s)
        take = cnt >= topk
        return jnp.where(take, t_u | bit, t_u), jnp.where(take, cnt, cnt_t)

    n_causal = i * tq + lax.broadcasted_iota(I32, (1, tq), 1) + 1
    t_u, cnt_t = lax.fori_loop(0, 32, bit_step, (jnp.zeros((1, tq), I32), n_causal))
    thr = jnp.maximum(t_u ^ jnp.int32(INT_MIN), jnp.int32(INT_MIN + 1))

    @pl.when(jnp.max(cnt_t) > topk)
    def _():
        need = topk - count_rows(lambda j, key: key > thr)

        def idx_step(b, cut):
            cand = cut | jnp.left_shift(jnp.int32(1), idx_bits - 1 - b)
            cnt = count_rows(lambda j, key: (key == thr) & ((j * tk + k_off) < cand))
            return jnp.where(cnt <= need, cand, cut)

        cut = lax.fori_loop(0, idx_bits, idx_step, jnp.zeros((1, tq), I32))

        def demote(j, carry):
            key = key_ref[j]
            drop = (key == thr) & ((j * tk + k_off) >= cut)
            key_ref[j] = jnp.where(drop, key - 1, key)
            return carry

        lax.fori_loop(0, nkc, demote, 0)

    aw = A_HEADS * tq
    acc_ref[...] = jnp.zeros(acc_ref.shape, F32)

    def scores(j, s_ref):
        kc = k_ref[pl.ds(pl.multiple_of(j * tk, tk), tk), :]
        sel = key_ref[j] >= thr
        cmax = []
        for h in range(A_HEADS):
            sl = slice(h * tq, (h + 1) * tq)
            s_h = jnp.dot(kc, qt_ref[0, :, sl], preferred_element_type=F32)
            s_h = jnp.where(sel, s_h, -jnp.inf)
            s_ref[:, sl] = s_h
            cmax.append(jnp.max(s_h, axis=0, keepdims=True))
        return jnp.concatenate(cmax, axis=1)

    def absorb(j, s_ref, p_ref, cmax, m_old):
        m_new = jnp.maximum(m_old, cmax)
        alpha = jnp.exp2(m_old - m_new)
        for h in range(A_HEADS):
            sl = slice(h * tq, (h + 1) * tq)
            p_ref[:, sl] = jnp.exp2(s_ref[:, sl] - m_new[:, sl]).astype(BF16)
        acc_ref[...] = alpha * acc_ref[...] + jnp.dot(vt_ref[j], p_ref[...], preferred_element_type=F32)
        return m_new

    def pair(t, carry):
        m, cm_a = carry
        a = 2 * t
        cm_b = scores(a + 1, sb_ref)
        m = absorb(a, sa_ref, pa_ref, cm_a, m)
        cm_next = scores(jnp.minimum(a + 2, nkc - 1), sa_ref)
        m = absorb(a + 1, sb_ref, pb_ref, cm_b, m)
        return m, cm_next

    m_fin, cm_last = lax.fori_loop(0, nkc // 2, pair, (jnp.full((1, aw), -1e30, F32), scores(0, sa_ref)))

    @pl.when(nkc % 2 == 1)
    def _():
        absorb(nkc - 1, sa_ref, pa_ref, cm_last, m_fin)

    out_t = acc_ref[0:HEAD_DIM, :] / acc_ref[HEAD_DIM:HEAD_DIM + 1, :]
    for h in range(A_HEADS):
        o_ref[:, h * HEAD_DIM:(h + 1) * HEAD_DIM] = out_t[:, h * tq:(h + 1) * tq].T.astype(o_ref.dtype)


def _dsa(qit, qt, wt, ki, k, vt, batch, seq, tq, tk):
    n = batch * seq
    nqb = seq // tq
    nkb = seq // tk
    topk = min(TOPK_MAX, seq // 4)
    aw = A_HEADS * HEAD_DIM
    kern = functools.partial(_dsa_kernel, tq=tq, tk=tk, topk=topk, idx_bits=seq.bit_length())
    qblk = lambda b, i: (b * nqb + i, 0, 0)
    return pl.pallas_call(
        kern,
        grid=(batch, nqb),
        in_specs=[
            pl.BlockSpec((1, HEAD_DIM, IDX_HEADS * tq), qblk),
            pl.BlockSpec((1, HEAD_DIM, A_HEADS * tq), qblk),
            pl.BlockSpec((1, 1, IDX_HEADS * tq), qblk),
            pl.BlockSpec((seq, LANES), lambda b, i: (b, 0)),
            pl.BlockSpec((seq, LANES), lambda b, i: (b, 0)),
            pl.BlockSpec((nkb, VT_ROWS, tk), lambda b, i: (b, 0, 0)),
        ],
        out_specs=pl.BlockSpec((tq, aw), lambda b, i: (b * nqb + i, 0)),
        out_shape=jax.ShapeDtypeStruct((n, aw), BF16),
        scratch_shapes=[
            pltpu.VMEM((nkb, tk, tq), I32),
            pltpu.VMEM((tk, A_HEADS * tq), F32),
            pltpu.VMEM((tk, A_HEADS * tq), F32),
            pltpu.VMEM((tk, A_HEADS * tq), BF16),
            pltpu.VMEM((tk, A_HEADS * tq), BF16),
            pltpu.VMEM((VT_ROWS, A_HEADS * tq), F32),
        ],
        compiler_params=_cparams(("parallel", "arbitrary"), 48),
        name="dsa",
    )(qit, qt, wt, ki, k, vt)


HG_SUB = 16
HG_CHUNKS_PER_STEP = 4


def _hgrn_diag_exact(q, kk, v, lc, rsub, nsub):
    chunk, gw = q.shape

    def bcast(x, s):
        return jnp.concatenate([jnp.broadcast_to(x[a * HG_SUB + s:a * HG_SUB + s + 1, :], (HG_SUB, gw))
                                for a in range(nsub)], axis=0)

    acc = [jnp.zeros((chunk, HEAD_DIM), F32) for _ in range(G_HEADS)]
    for s in range(HG_SUB):
        term = q * bcast(kk, s) * jnp.exp(jnp.minimum(lc - bcast(lc, s), 0.0))
        term = jnp.where(rsub >= s, term, 0.0)
        vrow = bcast(v, s)
        for h in range(G_HEADS):
            sl = slice(h * HEAD_DIM, (h + 1) * HEAD_DIM)
            acc[h] = acc[h] + jnp.sum(term[:, sl], axis=1, keepdims=True) * vrow[:, sl]
    return jnp.concatenate(acc, axis=1)


def _hgrn_gates(hq_ref, hf_ref, hi_ref, lb_ref):
    chunk, gw = hq_ref.shape
    lb = lb_ref[...]
    f = lb + (1.0 - lb) * _sigmoid(hf_ref[...])
    kk = 1.0 - f
    hq = hq_ref[...].astype(F32)
    q = hq * _sigmoid(hq)
    v = hi_ref[...].astype(F32)
    rsub = lax.broadcasted_iota(I32, (chunk, gw), 0) & (HG_SUB - 1)
    lc = jnp.log(f)
    sh = 1
    while sh < HG_SUB:
        lc = lc + jnp.where(rsub >= sh, pltpu.roll(lc, sh, 0), 0.0)
        sh *= 2
    return q, kk, v, lc, rsub


def _hgrn_diag_scores(q, kk, lc, h):
    chunk = q.shape[0]
    sl = slice(h * HEAD_DIM, (h + 1) * HEAD_DIM)
    qd = (q[:, sl] * jnp.exp(lc[:, sl])).astype(BF16)
    kd = (kk[:, sl] * jnp.exp(jnp.minimum(-lc[:, sl], EXP_CLAMP))).astype(BF16)
    ti = lax.broadcasted_iota(I32, (chunk, chunk), 0)
    si = lax.broadcasted_iota(I32, (chunk, chunk), 1)
    diag_ok = (ti // HG_SUB == si // HG_SUB) & (si <= ti)
    return jnp.where(diag_ok, _dot_nt(qd, kd), 0.0).astype(BF16)


def _hgrn_kernel(need_ref, hq_ref, hf_ref, hi_ref, hg_ref, lb_ref, gn_ref, o_ref, st_ref, oa_ref, *, chunk):
    @pl.when(pl.program_id(1) == 0)
    def _():
        st_ref[...] = jnp.zeros(st_ref.shape, F32)

    for c in range(hq_ref.shape[0] // chunk):
        rows = pl.ds(c * chunk, chunk)
        _hgrn_chunk(need_ref, hq_ref.at[rows], hf_ref.at[rows], hi_ref.at[rows], hg_ref.at[rows], lb_ref, gn_ref,
                    o_ref.at[rows], st_ref, oa_ref)


def _hgrn_chunk(need_ref, hq_ref, hf_ref, hi_ref, hg_ref, lb_ref, gn_ref, o_ref, st_ref, oa_ref):
    chunk, gw = hq_ref.shape
    nsub = chunk // HG_SUB

    q, kk, v, lc, rsub = _hgrn_gates(hq_ref, hf_ref, hi_ref, lb_ref)
    def per_sub(vals):
        return jnp.concatenate([jnp.broadcast_to(x, (HG_SUB, gw)) for x in vals], axis=0)

    tots = [lc[(a + 1) * HG_SUB - 1:(a + 1) * HG_SUB, :] for a in range(nsub)]
    ends = []
    run = jnp.zeros((1, gw), F32)
    for a in range(nsub):
        run = run + tots[a]
        ends.append(run)
    cum_l = ends[-1]
    cum = lc + per_sub([jnp.zeros((1, gw), F32)] + ends[:-1])

    qs = (q * jnp.exp(cum)).astype(BF16)
    koff = kk * jnp.exp(per_sub(tots) - lc)
    ke = (koff * per_sub([jnp.exp(cum_l - e) for e in ends])).astype(BF16)
    koff_b = koff.astype(BF16)
    qoff = []
    kmask = []
    for b in range(nsub - 1):
        r0 = (b + 1) * HG_SUB
        qb = (q[r0:, :] * jnp.exp(cum[r0:, :] - ends[b])).astype(BF16)
        qoff.append(jnp.concatenate([jnp.zeros((r0, gw), BF16), qb], axis=0))
        parts = [koff_b[b * HG_SUB:r0, :], jnp.zeros((chunk - r0, gw), BF16)]
        if b > 0:
            parts.insert(0, jnp.zeros((b * HG_SUB, gw), BF16))
        kmask.append(jnp.concatenate(parts, axis=0))
    vb = v.astype(BF16)
    decay_l = jnp.exp(cum_l)

    for h in range(G_HEADS):
        sl = slice(h * HEAD_DIM, (h + 1) * HEAD_DIM)
        qcat = jnp.concatenate([qo[:, sl] for qo in qoff], axis=1)
        kcat = jnp.concatenate([km[:, sl] for km in kmask], axis=1)
        a = _hgrn_diag_scores(q, kk, lc, h) + _dot_nt(qcat, kcat).astype(BF16)
        st = st_ref[h]
        oa_ref[:, sl] = jnp.dot(a, vb[:, sl], preferred_element_type=F32) + _dot_nt(qs[:, sl], st.astype(BF16))
        upd = jnp.dot(v[:, sl].T.astype(BF16), ke[:, sl], preferred_element_type=F32)
        st_ref[h] = st * decay_l[:, sl] + upd

    @pl.when(need_ref[0] != 0)
    def _():
        q2, kk2, v2, lc2, rsub2 = _hgrn_gates(hq_ref, hf_ref, hi_ref, lb_ref)

        @pl.when(jnp.max(-lc2) > EXP_CLAMP)
        def _():
            v2b = v2.astype(BF16)
            fast = [jnp.dot(_hgrn_diag_scores(q2, kk2, lc2, h), v2b[:, h * HEAD_DIM:(h + 1) * HEAD_DIM],
                            preferred_element_type=F32) for h in range(G_HEADS)]
            oa_ref[...] += _hgrn_diag_exact(q2, kk2, v2, lc2, rsub2, nsub) - jnp.concatenate(fast, axis=1)

    o = oa_ref[...]
    ms = jnp.mean(o * o, axis=-1, keepdims=True)
    hg = hg_ref[...].astype(F32)
    y = o * lax.rsqrt(ms + EPS) * gn_ref[...] * (hg * _sigmoid(hg))
    o_ref[...] = y.astype(o_ref.dtype)


def _hgrn(ph, pt, lb, g_hnorm, batch, seq, chunk):
    n = batch * seq
    gw = G_HEADS * HEAD_DIM
    rows = HG_CHUNKS_PER_STEP * chunk
    ncb = seq // rows
    blk = lambda c: pl.BlockSpec((rows, gw), lambda b, i, nd: (b * ncb + i, c))
    need = (HG_SUB * jnp.log(jnp.min(lb)) < -EXP_CLAMP).astype(I32).reshape(1)
    grid_spec = pltpu.PrefetchScalarGridSpec(
        num_scalar_prefetch=1,
        grid=(batch, ncb),
        in_specs=[blk(0), blk(1), blk(1), blk(2),
                  pl.BlockSpec((1, gw), lambda b, i, nd: (0, 0)),
                  pl.BlockSpec((1, gw), lambda b, i, nd: (0, 0))],
        out_specs=pl.BlockSpec((rows, gw), lambda b, i, nd: (b * ncb + i, 0)),
        scratch_shapes=[pltpu.VMEM((G_HEADS, HEAD_DIM, HEAD_DIM), F32),
                        pltpu.VMEM((chunk, gw), F32)],
    )
    return pl.pallas_call(
        functools.partial(_hgrn_kernel, chunk=chunk),
        grid_spec=grid_spec,
        out_shape=jax.ShapeDtypeStruct((n, gw), BF16),
        compiler_params=_cparams(("parallel", "arbitrary"), 32),
        name="hgrn",
    )(need, pt, ph, pt, pt, lb.reshape(1, gw), g_hnorm.reshape(1, gw))


def _merge_kernel(ya_ref, yr_ref, ga_ref, gb_ref, x_ref, wpa_ref, wpb_ref, wo_ref, o_ref, mg_ref):
    c = pl.program_id(1)
    nc = pl.num_programs(1)
    tc = ga_ref.shape[1]

    for k in range(mg_ref.shape[1] // tc):
        @pl.when(c == k)
        def _():
            sl = slice(k * tc, (k + 1) * tc)
            ma = jnp.dot(ya_ref[...], wpa_ref[:, sl], preferred_element_type=F32)
            mb = jnp.dot(yr_ref[...], wpb_ref[:, sl], preferred_element_type=F32)
            merged = _sigmoid(ga_ref[...].astype(F32)) * ma + _sigmoid(gb_ref[...].astype(F32)) * mb
            mg_ref[:, sl] = merged.astype(BF16)

    @pl.when(c == nc - 1)
    def _():
        o_ref[...] = x_ref[...] + jnp.dot(mg_ref[...], wo_ref[...], preferred_element_type=F32)


def _merge(y_att, y_rec, pt, x2, w_pa, w_pb, w_o, tm, tc):
    n, d = x2.shape
    aw = y_att.shape[1]
    gw = y_rec.shape[1]
    ga0 = 3072 // tc
    gb0 = 5120 // tc
    return pl.pallas_call(
        _merge_kernel,
        grid=(n // tm, d // tc),
        in_specs=[
            pl.BlockSpec((tm, aw), lambda i, c: (i, 0)),
            pl.BlockSpec((tm, gw), lambda i, c: (i, 0)),
            pl.BlockSpec((tm, tc), lambda i, c: (i, ga0 + c)),
            pl.BlockSpec((tm, tc), lambda i, c: (i, gb0 + c)),
            pl.BlockSpec((tm, d), lambda i, c: (i, 0)),
            pl.BlockSpec((aw, d), lambda i, c: (0, 0)),
            pl.BlockSpec((gw, d), lambda i, c: (0, 0)),
            pl.BlockSpec((d, d), lambda i, c: (0, 0)),
        ],
        out_specs=pl.BlockSpec((tm, d), lambda i, c: (i, 0)),
        out_shape=jax.ShapeDtypeStruct((n, d), F32),
        scratch_shapes=[pltpu.VMEM((tm, d), BF16)],
        compiler_params=_cparams(("parallel", "arbitrary"), 56),
        name="merge",
    )(y_att, y_rec, pt, pt, x2, w_pa, w_pb, w_o)


def _router_kernel(x_ref, g_ref, wr_ref, br_ref, h_ref, r_ref, cnt_ref):
    x = x_ref[...]
    ms = jnp.mean(x * x, axis=-1, keepdims=True)
    h = x * lax.rsqrt(ms + EPS) * g_ref[...]
    h_ref[...] = _pack_halves(h)
    lg = jnp.dot(h.astype(BF16), wr_ref[...], preferred_element_type=F32) + br_ref[...]
    tm = lg.shape[0]
    lane = lax.broadcasted_iota(I32, (tm, LANES), 1)
    neg = jnp.float32(-1e30)
    big = jnp.float32(LANES)
    is_g = lane < N_GROUPS
    lgg = jnp.where(is_g, lg, neg)
    mg = jnp.max(lgg, axis=1, keepdims=True)
    zg = jnp.sum(jnp.where(is_g, jnp.exp(lgg - mg), 0.0), axis=1, keepdims=True)
    p_g = 1.0 / zg
    grp = jnp.min(jnp.where(is_g & (lgg == mg), lane.astype(F32), big), axis=1, keepdims=True)
    e_id = lane - N_GROUPS
    e_grp = lax.shift_right_arithmetic(e_id, jnp.int32(EXP_PER_GROUP.bit_length() - 1)).astype(F32)
    e_idf = e_id.astype(F32)
    is_e = (e_id >= 0) & (e_id < N_EXPERTS) & (e_grp == grp)
    lge = jnp.where(is_e, lg, neg)
    m1 = jnp.max(lge, axis=1, keepdims=True)
    i1 = jnp.min(jnp.where(is_e & (lge == m1), e_idf, big), axis=1, keepdims=True)
    is_e2 = is_e & (e_idf != i1)
    lge2 = jnp.where(is_e2, lg, neg)
    m2 = jnp.max(lge2, axis=1, keepdims=True)
    i2 = jnp.min(jnp.where(is_e2 & (lge2 == m2), e_idf, big), axis=1, keepdims=True)
    ex2 = jnp.exp(m2 - m1)
    den = 1.0 + ex2
    g1 = p_g / den
    g2 = p_g * ex2 / den
    @pl.when(pl.program_id(0) == 0)
    def _():
        cnt_ref[...] = jnp.zeros(cnt_ref.shape, F32)

    lanef = lane.astype(F32)
    oh1 = lanef == i1
    oh2 = lanef == i2
    oh = jnp.where(oh1 | oh2, 1.0, 0.0)
    ti = lax.broadcasted_iota(I32, (tm, tm), 0)
    si = lax.broadcasted_iota(I32, (tm, tm), 1)
    tri = jnp.where(si < ti, 1.0, 0.0).astype(BF16)
    before = jnp.dot(tri, oh.astype(BF16), preferred_element_type=F32) + cnt_ref[...]
    rank1 = jnp.sum(jnp.where(oh1, before, 0.0), axis=1, keepdims=True)
    rank2 = jnp.sum(jnp.where(oh2, before, 0.0), axis=1, keepdims=True)
    cnt_ref[...] += jnp.sum(oh, axis=0, keepdims=True)

    r = jnp.where(lane == 0, i1, 0.0)
    r = jnp.where(lane == 1, i2, r)
    r = jnp.where(lane == 2, g1, r)
    r = jnp.where(lane == 3, g2, r)
    r = jnp.where(lane == 4, rank1, r)
    r = jnp.where(lane == 5, rank2, r)
    r_ref[...] = r


def _router(x1, g, w_r, b_r, tm):
    n, d = x1.shape
    return pl.pallas_call(
        _router_kernel,
        grid=(n // tm,),
        in_specs=[
            pl.BlockSpec((tm, d), lambda i: (i, 0)),
            pl.BlockSpec((1, d), lambda i: (0, 0)),
            pl.BlockSpec((d, LANES), lambda i: (0, 0)),
            pl.BlockSpec((1, LANES), lambda i: (0, 0)),
        ],
        out_specs=[pl.BlockSpec((tm, d // 2), lambda i: (i, 0)),
                   pl.BlockSpec((tm, LANES), lambda i: (i, 0)),
                   pl.BlockSpec((1, LANES), lambda i: (0, 0))],
        out_shape=[jax.ShapeDtypeStruct((n, d // 2), U32),
                   jax.ShapeDtypeStruct((n, LANES), F32),
                   jax.ShapeDtypeStruct((1, LANES), F32)],
        compiler_params=_cparams(("arbitrary",), 40),
        name="router",
    )(x1, g.reshape(1, d), w_r, b_r)


def _dispatch_kernel(prow_ref, h_ref, xs_hbm, sem, *, tm):
    base = 2 * tm * pl.program_id(0)

    def scatter_group(g, carry):
        r0 = pl.multiple_of(g * DMA_GROUP, DMA_GROUP)
        rows = h_ref.at[pl.ds(r0, DMA_GROUP), :]
        for u in range(DMA_GROUP):
            for s in range(2):
                dst_row = prow_ref[base + 2 * r0 + (2 * u + s)]
                pltpu.make_async_copy(rows.at[pl.ds(u, 1), :], xs_hbm.at[pl.ds(dst_row, 1), :],
                                      sem).start(priority=s)
        return carry

    lax.fori_loop(0, tm // DMA_GROUP, scatter_group, 0)
    for _ in range(2):
        pltpu.make_async_copy(h_ref, xs_hbm.at[pl.ds(0, tm), :], sem).wait()


def _dispatch(h2, prow, tm):
    n, d = h2.shape
    grid_spec = pltpu.PrefetchScalarGridSpec(
        num_scalar_prefetch=1,
        grid=(n // tm,),
        in_specs=[pl.BlockSpec((tm, d), lambda i, pr: (i, 0))],
        out_specs=pl.BlockSpec(memory_space=pl.ANY),
        scratch_shapes=[pltpu.SemaphoreType.DMA(())],
    )
    return pl.pallas_call(
        functools.partial(_dispatch_kernel, tm=tm),
        grid_spec=grid_spec,
        out_shape=jax.ShapeDtypeStruct((2 * n, d), h2.dtype),
        compiler_params=_cparams(("arbitrary",), 32),
        name="dispatch",
    )(prow, h2)


def _moe_kernel(pt_ref, pe_ref, plo_ref, phi_ref, slot_ref, nxt_ref, np_ref, x_ref, wg_hbm, wu_hbm, wd_hbm,
                y_ref, wgf_ref, wuf_ref, wdf_ref, wgb_ref, wub_ref, wdb_ref, yacc_ref, sem):
    p = pl.program_id(0)
    active = p < np_ref[0]
    prev = jnp.maximum(p - 1, 0)
    new_expert = (p == 0) | (pe_ref[p] != pe_ref[prev])
    new_tile = (p == 0) | (pt_ref[p] != pt_ref[prev])
    slot = slot_ref[p]

    def weight_copies(e, s):
        return (pltpu.make_async_copy(wg_hbm.at[e], wgf_ref.at[s], sem.at[s]),
                pltpu.make_async_copy(wu_hbm.at[e], wuf_ref.at[s], sem.at[s]),
                pltpu.make_async_copy(wd_hbm.at[e], wdf_ref.at[s], sem.at[s]))

    @pl.when(p == 0)
    def _():
        for cp in weight_copies(pe_ref[0], 0):
            cp.start()

    @pl.when(active & new_expert)
    def _():
        for cp in weight_copies(pe_ref[p], slot):
            cp.wait()
        wgb_ref[...] = wgf_ref[slot].astype(BF16)
        wub_ref[...] = wuf_ref[slot].astype(BF16)
        wdb_ref[...] = wdf_ref[slot].astype(BF16)

        @pl.when(nxt_ref[p] >= 0)
        def _():
            for cp in weight_copies(nxt_ref[p], 1 - slot):
                cp.start()

    @pl.when(active)
    def _():
        tm, half = x_ref.shape
        lo, hi = _unpack_halves(x_ref[...])
        xl = lo.astype(BF16)
        xh = hi.astype(BF16)
        a = (jnp.dot(xl, wgb_ref[0:half, :], preferred_element_type=F32)
             + jnp.dot(xh, wgb_ref[half:, :], preferred_element_type=F32))
        u = (jnp.dot(xl, wub_ref[0:half, :], preferred_element_type=F32)
             + jnp.dot(xh, wub_ref[half:, :], preferred_element_type=F32))
        hmid = (a * _sigmoid(a) * u).astype(BF16)
        yv = jnp.dot(hmid, wdb_ref[...], preferred_element_type=F32)
        row = lax.broadcasted_iota(I32, (tm, 1), 0)
        mine = (row >= plo_ref[p]) & (row < phi_ref[p])
        yv = jnp.where(mine, yv, 0.0)

        @pl.when(new_tile)
        def _():
            yacc_ref[...] = yv

        @pl.when(jnp.logical_not(new_tile))
        def _():
            yacc_ref[...] += yv

        nxt_p = jnp.minimum(p + 1, pl.num_programs(0) - 1)
        last_of_tile = (p == np_ref[0] - 1) | (pt_ref[nxt_p] != pt_ref[p])

        @pl.when(last_of_tile)
        def _():
            y_ref[...] = _pack_halves(yacc_ref[...])


def _moe(xs, pairs, w_gate, w_up, w_down, tm):
    pair_tile, pair_expert, pair_lo, pair_hi, pair_slot, pair_next, n_pairs = pairs
    na, half = xs.shape
    d, de = w_gate.shape[1:]
    max_pairs = pair_tile.shape[0]
    grid_spec = pltpu.PrefetchScalarGridSpec(
        num_scalar_prefetch=7,
        grid=(max_pairs,),
        in_specs=[
            pl.BlockSpec((tm, half), lambda p, pt, *_: (pt[p], 0)),
            pl.BlockSpec(memory_space=pl.ANY),
            pl.BlockSpec(memory_space=pl.ANY),
            pl.BlockSpec(memory_space=pl.ANY),
        ],
        out_specs=pl.BlockSpec((tm, half), lambda p, pt, *_: (pt[p], 0)),
        scratch_shapes=[
            pltpu.VMEM((2, d, de), F32),
            pltpu.VMEM((2, d, de), F32),
            pltpu.VMEM((2, de, d), F32),
            pltpu.VMEM((d, de), BF16),
            pltpu.VMEM((d, de), BF16),
            pltpu.VMEM((de, d), BF16),
            pltpu.VMEM((tm, d), F32),
            pltpu.SemaphoreType.DMA((2,)),
        ],
    )
    return pl.pallas_call(
        _moe_kernel,
        grid_spec=grid_spec,
        out_shape=jax.ShapeDtypeStruct((na, half), U32),
        compiler_params=_cparams(("arbitrary",), 56),
        name="moe",
    )(pair_tile, pair_expert, pair_lo, pair_hi, pair_slot, pair_next, n_pairs, xs, w_gate, w_up, w_down)


def _moe_rows(r, cnt):
    sizes = cnt[0, :N_EXPERTS].astype(I32)
    starts = jnp.cumsum(sizes) - sizes
    e12 = r[:, 0:2].astype(I32)
    rank = r[:, 4:6].astype(I32)
    onehot = e12[:, :, None] == jnp.arange(N_EXPERTS, dtype=I32)[None, None, :]
    start_a = jnp.sum(jnp.where(onehot, starts[None, None, :], 0), axis=-1)
    return (start_a + rank).reshape(-1).astype(I32), sizes


def _count_le(sorted_vals, x):
    return jnp.sum((sorted_vals[None, :] <= x[:, None]).astype(I32), axis=1)


def _moe_pairs(sizes, na, tm):
    ntiles = na // tm
    max_pairs = ntiles + N_EXPERTS - 1
    ends = jnp.cumsum(sizes)
    starts = ends - sizes
    t0 = jnp.arange(ntiles, dtype=I32) * tm
    e_lo = _count_le(ends, t0)
    e_hi = _count_le(ends, t0 + (tm - 1))
    per_tile = e_hi - e_lo + 1
    pend = jnp.cumsum(per_tile)
    pstart = pend - per_tile
    n_pairs = pend[-1]
    pc = jnp.minimum(jnp.arange(max_pairs, dtype=I32), n_pairs - 1)
    tile = _count_le(pend, pc)
    expert = e_lo[tile] + (pc - pstart[tile])
    lo = jnp.clip(starts[expert] - tile * tm, 0, tm).astype(I32)
    hi = jnp.clip(ends[expert] - tile * tm, 0, tm).astype(I32)
    expert = expert.astype(I32)
    first = jnp.concatenate([jnp.ones((1,), bool), expert[1:] != expert[:-1]])
    slot = ((jnp.cumsum(first) - 1) % 2).astype(I32)
    idx = jnp.arange(max_pairs, dtype=I32)
    first_at = jnp.where(first, idx, max_pairs)
    nxt_first = jnp.concatenate([lax.cummin(first_at[::-1])[::-1][1:], jnp.full((1,), max_pairs, I32)])
    nxt = jnp.where(nxt_first < max_pairs, expert[jnp.minimum(nxt_first, max_pairs - 1)], -1).astype(I32)
    return tile, expert, lo, hi, slot, nxt, n_pairs.astype(I32).reshape(1)


def _final_kernel(prow_ref, x_ref, r_ref, g_ref, y_hbm, o_ref, ybuf, sem, *, tm):
    i = pl.program_id(0)
    cur = i % 2

    def issue(step, buf):
        base = 2 * tm * step

        def gather_group(g, carry):
            r0 = pl.multiple_of(g * DMA_GROUP, DMA_GROUP)
            for s in range(2):
                rows = ybuf.at[buf, s, pl.ds(r0, DMA_GROUP), :]
                for u in range(DMA_GROUP):
                    src_row = prow_ref[base + 2 * r0 + (2 * u + s)]
                    pltpu.make_async_copy(y_hbm.at[pl.ds(src_row, 1), :], rows.at[pl.ds(u, 1), :],
                                          sem.at[buf]).start(priority=u % 2)
            return carry

        lax.fori_loop(0, tm // DMA_GROUP, gather_group, 0)

    @pl.when(i == 0)
    def _():
        issue(0, 0)

    @pl.when(i + 1 < pl.num_programs(0))
    def _():
        issue(i + 1, 1 - cur)

    for s in range(2):
        pltpu.make_async_copy(y_hbm.at[pl.ds(0, tm), :], ybuf.at[cur, s], sem.at[cur]).wait()
    r = r_ref[...]
    g1 = r[:, 2:3]
    g2 = r[:, 3:4]
    half = ybuf.shape[-1]
    lo0, hi0 = _unpack_halves(ybuf[cur, 0])
    lo1, hi1 = _unpack_halves(ybuf[cur, 1])
    xa = x_ref[:, :half] + g1 * lo0 + g2 * lo1
    xb = x_ref[:, half:] + g1 * hi0 + g2 * hi1
    ms = (jnp.sum(xa * xa, axis=-1, keepdims=True) + jnp.sum(xb * xb, axis=-1, keepdims=True)) / (2 * half)
    rs = lax.rsqrt(ms + EPS)
    o_ref[:, :half] = xa * rs * g_ref[:, :half]
    o_ref[:, half:] = xb * rs * g_ref[:, half:]


def _final(x1, y, r, prow, g, tm):
    n, d = x1.shape
    grid_spec = pltpu.PrefetchScalarGridSpec(
        num_scalar_prefetch=1,
        grid=(n // tm,),
        in_specs=[
            pl.BlockSpec((tm, d), lambda i, pr: (i, 0)),
            pl.BlockSpec((tm, LANES), lambda i, pr: (i, 0)),
            pl.BlockSpec((1, d), lambda i, pr: (0, 0)),
            pl.BlockSpec(memory_space=pl.ANY),
        ],
        out_specs=pl.BlockSpec((tm, d), lambda i, pr: (i, 0)),
        scratch_shapes=[pltpu.VMEM((2, 2, tm, d // 2), U32), pltpu.SemaphoreType.DMA((2,))],
    )
    return pl.pallas_call(
        functools.partial(_final_kernel, tm=tm),
        grid_spec=grid_spec,
        out_shape=jax.ShapeDtypeStruct((n, d), F32),
        compiler_params=_cparams(("arbitrary",), 48),
        name="final",
    )(prow, x1, r, g.reshape(1, d), y)


def _pack_w_in(w):
    d, din = w.shape
    tc = 256
    wt = jnp.swapaxes(w, 0, 1)

    def pack_kernel(w_ref, o_ref):
        o_ref[0:912, :] = w_ref[0:912, :].astype(BF16)
        o_ref[912:1024, :] = jnp.zeros((LANES - IDX_HEADS, tc), BF16)
        o_ref[1024:2048, :] = w_ref[1936:2960, :].astype(BF16)
        o_ref[2048:3072, :] = w_ref[912:1936, :].astype(BF16)
        o_ref[3072:, :] = w_ref[2960:, :].astype(BF16)

    return pl.pallas_call(
        pack_kernel,
        grid=(d // tc,),
        in_specs=[pl.BlockSpec((din, tc), lambda i: (0, i))],
        out_specs=pl.BlockSpec((PH_COLS + PT_COLS, tc), lambda i: (0, i)),
        out_shape=jax.ShapeDtypeStruct((PH_COLS + PT_COLS, d), BF16),
        compiler_params=_cparams(("parallel",), 48),
        name="pack_w_in",
    )(wt)


def _rope_tables(seq):
    inv = 1.0 / (ROPE_THETA ** (jnp.arange(0, HEAD_DIM, 2, dtype=F32) / HEAD_DIM))
    ang = jnp.arange(seq, dtype=F32)[:, None] * inv[None, :]
    c, s = jnp.cos(ang), jnp.sin(ang)
    c2, s2 = jnp.concatenate([c, c], axis=1), jnp.concatenate([-s, s], axis=1)
    return c2, s2, c2.T, s2.T


class _Tiles(NamedTuple):
    proj_rows: int
    proj_cols: int
    att_q: int
    att_k: int
    rec_chunk: int
    merge_rows: int
    merge_cols: int
    tok_rows: int
    moe_rows: int


def _tiles(n, seq):
    return _Tiles(proj_rows=min(1024, n), proj_cols=1024, att_q=min(256, seq), att_k=min(256, seq), rec_chunk=64,
                  merge_rows=min(512, n), merge_cols=1024, tok_rows=min(512, n), moe_rows=256)


def _layer(x2, batch, seq, rope, lb, g_norm1, w_in, g_cq, w_uq, w_qidx, g_hnorm, w_pa, w_pb, w_o,
           g_norm2, w_grp, b_grp, w_exp, b_exp, w_gate, w_up, w_down, g_out):
    n, d = x2.shape
    t = _tiles(n, seq)
    ph, pt = _in_proj(x2, g_norm1, _pack_w_in(w_in), tm=t.proj_rows, tn=t.proj_cols)
    qt, qit, wt, k, ki, vt = _dsa_prep(ph, rope, g_cq, w_uq.T.astype(BF16), w_qidx.T.astype(BF16), seq,
                                       tm=t.att_k, tq=t.att_q)
    y_att = _dsa(qit, qt, wt, ki, k, vt, batch, seq, tq=t.att_q, tk=t.att_k)
    y_rec = _hgrn(ph, pt, lb, g_hnorm, batch, seq, chunk=t.rec_chunk)
    x1 = _merge(y_att, y_rec, pt, x2, w_pa.astype(BF16), w_pb.astype(BF16), w_o.astype(BF16),
                tm=t.merge_rows, tc=t.merge_cols)
    w_r = jnp.concatenate([w_grp, w_exp, jnp.zeros((d, LANES - N_GROUPS - N_EXPERTS), F32)], axis=1).astype(BF16)
    b_r = jnp.concatenate([b_grp, b_exp, jnp.zeros((LANES - N_GROUPS - N_EXPERTS,), F32)]).reshape(1, LANES)
    h2, r, cnt = _router(x1, g_norm2, w_r, b_r, tm=t.tok_rows)
    prow, sizes = _moe_rows(r, cnt)
    xs = _dispatch(h2, prow, tm=t.tok_rows)
    assert (2 * n) % t.moe_rows == 0
    y = _moe(xs, _moe_pairs(sizes, 2 * n, t.moe_rows), w_gate, w_up, w_down, t.moe_rows)
    return _final(x1, y, r, prow, g_out, tm=t.tok_rows)


def kernel(x, g_norm1, w_in, g_cq, w_uq, w_qidx, lb_logits, g_hnorm, w_pa, w_pb, w_o, g_norm2, w_grp, b_grp,
           w_exp, b_exp, w_gate, w_up, w_down, g_final):
    batch, seq, d = x.shape
    depth = g_norm1.shape[0]
    rope = _rope_tables(seq)
    lb_all = jnp.cumsum(jax.nn.softmax(lb_logits.astype(F32), axis=0), axis=0)
    x2 = x.reshape(batch * seq, d)
    for l in range(depth):
        assert depth == 1
        x2 = _layer(x2, batch, seq, rope, lb_all[l], g_norm1[l], w_in[l], g_cq[l], w_uq[l], w_qidx[l],
                    g_hnorm[l], w_pa[l], w_pb[l], w_o[l], g_norm2[l], w_grp[l], b_grp[l], w_exp[l], b_exp[l],
                    w_gate[l], w_up[l], w_down[l], g_final)
    return x2.reshape(batch, seq, d)
```
